```python
import math
import jax
import jax.numpy as jnp
from jax import lax
import numpy as np

D_MODEL = 1024
BATCH = 32
SEQ = 2048
DEPTH = 1

N_HEADS = 8
HEAD_DIM = 128
ATTN_WIDTH = N_HEADS * HEAD_DIM
ROPE_THETA = 500000.0
ROPE_DIM = HEAD_DIM // 4
MOBA_BLOCK = 256
MOBA_TOPK = 3
Q_CHUNK = 128
CONV_CH = D_MODEL
CONV_WIDTH = 31
N_EXPERTS = 256
TOP_K = 8
N_GROUPS = 8
TOPK_GROUPS = 4
EXPERT_HIDDEN = 256
SHARED_HIDDEN = 256
ROUTED_SCALE = 2.5
MOE_BLOCK = 128
LN_EPS = 1e-5
DEEPNORM_ALPHA = (2 * DEPTH) ** 0.25
DEEPNORM_BETA = (8 * DEPTH) ** -0.25
IN_SPLITS = [ATTN_WIDTH, 2 * ATTN_WIDTH, 3 * ATTN_WIDTH,
             3 * ATTN_WIDTH + CONV_CH, 3 * ATTN_WIDTH + 2 * CONV_CH,
             3 * ATTN_WIDTH + 2 * CONV_CH + D_MODEL]
IN_COLS = 3 * ATTN_WIDTH + 2 * CONV_CH + 2 * D_MODEL

kernel_name = "hybrid_moba_conformer_moe_deepnorm"


def layer_norm(x, g, b):
    xf = x.astype(jnp.float32)
    mu = jnp.mean(xf, axis=-1, keepdims=True)
    var = jnp.mean(jnp.square(xf - mu), axis=-1, keepdims=True)
    y = (xf - mu) * lax.rsqrt(var + LN_EPS)
    return (y * g.astype(jnp.float32) + b.astype(jnp.float32)).astype(x.dtype)


def partial_rope(t, pos):
    half = ROPE_DIM // 2
    inv_freq = ROPE_THETA ** (-jnp.arange(0, ROPE_DIM, 2, dtype=jnp.float32) / ROPE_DIM)
    ang = pos.astype(jnp.float32)[:, None] * inv_freq[None, :]
    cos = jnp.cos(ang).astype(t.dtype)[None, :, None, :]
    sin = jnp.sin(ang).astype(t.dtype)[None, :, None, :]
    t1, t2, rest = t[..., :half], t[..., half:ROPE_DIM], t[..., ROPE_DIM:]
    return jnp.concatenate([t1 * cos - t2 * sin, t2 * cos + t1 * sin, rest], axis=-1)


def moba_attention(q, k, v):
    B, S, H, Dh = q.shape
    n_blk = -(-S // MOBA_BLOCK)
    S_pad = n_blk * MOBA_BLOCK
    cfg = ((0, 0), (0, S_pad - S), (0, 0), (0, 0))
    q, k, v = [jnp.pad(t, cfg).transpose(0, 2, 1, 3) for t in (q, k, v)]
    kb = k.reshape(B, H, n_blk, MOBA_BLOCK, Dh)
    vb = v.reshape(B, H, n_blk, MOBA_BLOCK, Dh)
    k_mean = jnp.mean(kb.astype(jnp.float32), axis=3)
    gate = jnp.einsum('bhsd,bhnd->bhsn', q.astype(jnp.float32), k_mean)
    q_blk = jnp.arange(S_pad) // MOBA_BLOCK
    past = jnp.arange(n_blk)[None, :] < q_blk[:, None]
    gate = jnp.where(past, gate, -jnp.inf)
    k_sel = min(MOBA_TOPK, n_blk)
    _, sel = lax.top_k(gate, k_sel)
    sel_ok = sel < q_blk[:, None]

    n_chunk = S_pad // Q_CHUNK

    def to_chunks(t):
        t = t.reshape(B, H, n_chunk, Q_CHUNK, *t.shape[3:])
        t = jnp.moveaxis(t, 2, 1)
        return t.reshape(B * n_chunk, H, Q_CHUNK, *t.shape[4:])

    qc, selc, okc = to_chunks(q), to_chunks(sel), to_chunks(sel_ok)
    b_id = jnp.repeat(jnp.arange(B), n_chunk)
    c_id = jnp.tile(jnp.arange(n_chunk), B)
    scale = Dh ** -0.5
    h_ar = jnp.arange(H)[:, None, None]
    n_past = k_sel * MOBA_BLOCK

    def chunk_attn(args):
        qh, s_idx, s_ok, b, c = args
        kb_b, vb_b = kb[b], vb[b]
        k_g = kb_b[h_ar, s_idx]
        v_g = vb_b[h_ar, s_idx]
        s_past = jnp.einsum('hqd,hqkld->hqkl', qh, k_g).astype(jnp.float32) * scale
        s_past = jnp.where(s_ok[..., None], s_past, -jnp.inf).reshape(H, Q_CHUNK, n_past)
        own = (c * Q_CHUNK) // MOBA_BLOCK
        k_o = lax.dynamic_index_in_dim(kb_b, own, axis=1, keepdims=False)
        v_o = lax.dynamic_index_in_dim(vb_b, own, axis=1, keepdims=False)
        s_own = jnp.einsum('hqd,hld->hql', qh, k_o).astype(jnp.float32) * scale
        q_pos = c * Q_CHUNK + jnp.arange(Q_CHUNK)
        k_pos = own * MOBA_BLOCK + jnp.arange(MOBA_BLOCK)
        s_own = jnp.where(k_pos[None, None, :] <= q_pos[None, :, None], s_own, -jnp.inf)
        p = jax.nn.softmax(jnp.concatenate([s_past, s_own], axis=-1), axis=-1).astype(v_g.dtype)
        p_past = p[..., :n_past].reshape(H, Q_CHUNK, k_sel, MOBA_BLOCK)
        p_own = p[..., n_past:]
        return (jnp.einsum('hqkl,hqkld->hqd', p_past, v_g)
                + jnp.einsum('hql,hld->hqd', p_own, v_o))

    out = lax.map(chunk_attn, (qc, selc, okc, b_id, c_id))
    out = out.reshape(B, n_chunk, H, Q_CHUNK, Dh).transpose(0, 1, 3, 2, 4)
    return out.reshape(B, S_pad, H * Dh)[:, :S]


def conformer_conv(glu_a, glu_g, w_dw, b_dw, ln_g, ln_b, w_o):
    u = glu_a * jax.nn.sigmoid(glu_g)
    u = lax.conv_general_dilated(u, w_dw, window_strides=(1,),
                                 padding=[(CONV_WIDTH - 1, 0)],
                                 dimension_numbers=('NWC', 'WIO', 'NWC'),
                                 feature_group_count=CONV_CH) + b_dw
    u = layer_norm(u, ln_g, ln_b)
    u = u * jax.nn.sigmoid(u)
    return jnp.einsum('bsc,cd->bsd', u, w_o)


def mixer_sublayer(x, pos, w_in, b_gate, w_attn_o, w_dw, b_dw, conv_ln_g, conv_ln_b, w_conv_o, w_out):
    B, S, _ = x.shape
    proj = jnp.einsum('bsd,dn->bsn', x, w_in)
    q, k, v, glu_a, glu_g, g_a, g_c = jnp.split(proj, IN_SPLITS, axis=-1)
    q = partial_rope(q.reshape(B, S, N_HEADS, HEAD_DIM), pos)
    k = partial_rope(k.reshape(B, S, N_HEADS, HEAD_DIM), pos)
    v = v.reshape(B, S, N_HEADS, HEAD_DIM)
    attn_branch = jnp.einsum('bsa,ad->bsd', moba_attention(q, k, v), w_attn_o)
    conv_branch = conformer_conv(glu_a, glu_g, w_dw, b_dw, conv_ln_g, conv_ln_b, w_conv_o)
    gate_a = jax.nn.sigmoid(g_a + b_gate[:D_MODEL])
    gate_c = jax.nn.sigmoid(g_c + b_gate[D_MODEL:])
    merged = gate_a * attn_branch + gate_c * conv_branch
    return jnp.einsum('bsd,de->bse', merged, w_out)


def swiglu(h, wg, wu, wd):
    return (jax.nn.silu(h @ wg) * (h @ wu)) @ wd


def routed_experts(h, top_idx, top_w, wg, wu, wd):
    N = h.shape[0]
    A = N * TOP_K
    e_flat = top_idx.reshape(A)
    t_flat = jnp.repeat(jnp.arange(N, dtype=jnp.int32), TOP_K)
    w_flat = top_w.reshape(A)
    order = jnp.argsort(e_flat)
    e_s, t_s, w_s = e_flat[order], t_flat[order], w_flat[order]
    counts = jnp.bincount(e_flat, length=N_EXPERTS)
    starts = jnp.cumsum(counts) - counts
    padded = ((counts + MOE_BLOCK - 1) // MOE_BLOCK) * MOE_BLOCK
    pad_starts = jnp.cumsum(padded) - padded
    pad_ends = pad_starts + padded
    dest = pad_starts[e_s] + (jnp.arange(A) - starts[e_s])
    n_blocks = (A + N_EXPERTS * (MOE_BLOCK - 1) + MOE_BLOCK - 1) // MOE_BLOCK
    P = n_blocks * MOE_BLOCK
    row_tok = jnp.zeros((P,), jnp.int32).at[dest].set(t_s)
    row_w = jnp.zeros((P,), h.dtype).at[dest].set(w_s.astype(h.dtype))
    blk_start = jnp.arange(n_blocks) * MOE_BLOCK
    blk_exp = jnp.minimum(jnp.searchsorted(pad_ends, blk_start, side='right'), N_EXPERTS - 1)

    def body(acc, blk):
        tok, wts, e = blk
        xs = h[tok]
        y = swiglu(xs, wg[e], wu[e], wd[e]) * wts[:, None]
        return acc.at[tok].add(y), None

    acc, _ = lax.scan(body, jnp.zeros_like(h),
                      (row_tok.reshape(n_blocks, MOE_BLOCK), row_w.reshape(n_blocks, MOE_BLOCK), blk_exp))
    return acc


def moe_sublayer(x, w_router, router_bias, w_exp_gate, w_exp_up, w_exp_down, w_sh_gate, w_sh_up, w_sh_down):
    B, S, D = x.shape
    h = x.reshape(B * S, D)
    N = B * S
    scores = jax.nn.sigmoid(h.astype(jnp.float32) @ w_router.astype(jnp.float32))
    biased = scores + router_bias.astype(jnp.float32)
    grp = biased.reshape(N, N_GROUPS, N_EXPERTS // N_GROUPS)
    grp_score = jnp.sum(lax.top_k(grp, 2)[0], axis=-1)
    _, grp_idx = lax.top_k(grp_score, TOPK_GROUPS)
    grp_mask = jnp.sum(jax.nn.one_hot(grp_idx, N_GROUPS, dtype=jnp.float32), axis=1) > 0
    exp_mask = jnp.repeat(grp_mask, N_EXPERTS // N_GROUPS, axis=1)
    _, top_idx = lax.top_k(jnp.where(exp_mask, biased, -jnp.inf), TOP_K)
    top_s = jnp.take_along_axis(scores, top_idx, axis=-1)
    top_w = top_s / (jnp.sum(top_s, axis=-1, keepdims=True) + 1e-20) * ROUTED_SCALE
    routed = routed_experts(h, top_idx.astype(jnp.int32), top_w, w_exp_gate, w_exp_up, w_exp_down)
    shared = swiglu(h, w_sh_gate, w_sh_up, w_sh_down)
    return (shared + routed).reshape(B, S, D)


def setup_inputs(seed: int = 0) -> dict:
    key = jax.random.key(seed)
    ks = jax.random.split(key, 24)
    L, D = DEPTH, D_MODEL
    nrm = lambda k, shape, s: jax.random.normal(k, shape, jnp.float32) * s
    col_scale = jnp.concatenate([jnp.ones((2 * ATTN_WIDTH,), jnp.float32),
                                 jnp.full((ATTN_WIDTH,), DEEPNORM_BETA, jnp.float32),
                                 jnp.ones((IN_COLS - 3 * ATTN_WIDTH,), jnp.float32)])
    return {
        "x": nrm(ks[0], (BATCH, SEQ, D), 1.0),
        "w_in": nrm(ks[1], (L, D, IN_COLS), D ** -0.5) * col_scale,
        "b_gate": nrm(ks[2], (L, 2 * D), 0.02),
        "w_attn_o": nrm(ks[3], (L, ATTN_WIDTH, D), ATTN_WIDTH ** -0.5),
        "w_dw": nrm(ks[4], (L, CONV_WIDTH, 1, CONV_CH), CONV_WIDTH ** -0.5),
        "b_dw": nrm(ks[5], (L, CONV_CH), 0.02),
        "conv_ln_g": 1.0 + nrm(ks[6], (L, CONV_CH), 0.02),
        "conv_ln_b": nrm(ks[7], (L, CONV_CH), 0.02),
        "w_conv_o": nrm(ks[8], (L, CONV_CH, D), CONV_CH ** -0.5),
        "w_out": nrm(ks[9], (L, D, D), D ** -0.5 * DEEPNORM_BETA),
        "ln1_g": 1.0 + nrm(ks[10], (L, D), 0.02),
        "ln1_b": nrm(ks[11], (L, D), 0.02),
        "w_router": nrm(ks[12], (L, D, N_EXPERTS), D ** -0.5),
        "router_bias": nrm(ks[13], (L, N_EXPERTS), 0.01),
        "w_exp_gate": nrm(ks[14], (L, N_EXPERTS, D, EXPERT_HIDDEN), D ** -0.5),
        "w_exp_up": nrm(ks[15], (L, N_EXPERTS, D, EXPERT_HIDDEN), D ** -0.5),
        "w_exp_down": nrm(ks[16], (L, N_EXPERTS, EXPERT_HIDDEN, D), EXPERT_HIDDEN ** -0.5 * DEEPNORM_BETA),
        "w_sh_gate": nrm(ks[17], (L, D, SHARED_HIDDEN), D ** -0.5),
        "w_sh_up": nrm(ks[18], (L, D, SHARED_HIDDEN), D ** -0.5),
        "w_sh_down": nrm(ks[19], (L, SHARED_HIDDEN, D), SHARED_HIDDEN ** -0.5 * DEEPNORM_BETA),
        "ln2_g": 1.0 + nrm(ks[20], (L, D), 0.02),
        "ln2_b": nrm(ks[21], (L, D), 0.02),
    }


def reference(x, w_in, b_gate, w_attn_o, w_dw, b_dw, conv_ln_g, conv_ln_b, w_conv_o, w_out,
              ln1_g, ln1_b, w_router, router_bias, w_exp_gate, w_exp_up, w_exp_down,
              w_sh_gate, w_sh_up, w_sh_down, ln2_g, ln2_b):
    pos = jnp.arange(x.shape[1])
    for l in range(DEPTH):
        y = mixer_sublayer(x, pos, w_in[l], b_gate[l], w_attn_o[l], w_dw[l], b_dw[l],
                           conv_ln_g[l], conv_ln_b[l], w_conv_o[l], w_out[l])
        x = layer_norm(DEEPNORM_ALPHA * x + y, ln1_g[l], ln1_b[l])
        y = moe_sublayer(x, w_router[l], router_bias[l], w_exp_gate[l], w_exp_up[l], w_exp_down[l],
                         w_sh_gate[l], w_sh_up[l], w_sh_down[l])
        x = layer_norm(DEEPNORM_ALPHA * x + y, ln2_g[l], ln2_b[l])
    return x
```

```python
import functools

import jax
import jax.numpy as jnp
from jax import lax
from jax.experimental import pallas as pl
from jax.experimental.pallas import tpu as pltpu

F32 = jnp.float32
BF16 = jnp.bfloat16
NEG_INF = float("-inf")

D_MODEL = 1024
N_HEADS = 8
HEAD_DIM = 128
ROPE_THETA = 500000.0
ROPE_DIM = HEAD_DIM // 4
ROPE_HALF = ROPE_DIM // 2
MOBA_BLOCK = 256
MOBA_TOPK = 3
CONV_WIDTH = 31
CONV_HALO = 32
N_EXPERTS = 256
TOP_K = 8
N_GROUPS = 8
GROUP_SIZE = N_EXPERTS // N_GROUPS
TOPK_GROUPS = 4
EXPERT_HIDDEN = 256
ROUTED_SCALE = 2.5
LN_EPS = 1e-5
DEPTH = 1
DEEPNORM_ALPHA = (2 * DEPTH) ** 0.25
HALF = D_MODEL // 2

TOKEN_TILE = 256
EXPERT_TILE = 256
GATHER_ROWS = 256
VMEM_LIMIT = 56 * 1024 * 1024


def _params(*semantics):
    return pltpu.CompilerParams(dimension_semantics=semantics, vmem_limit_bytes=VMEM_LIMIT)


def _sigmoid(x):
    return 1.0 / (1.0 + jnp.exp(-x))


def _layer_norm(x, g, b):
    mu = jnp.mean(x, axis=-1, keepdims=True)
    xc = x - mu
    var = jnp.mean(xc * xc, axis=-1, keepdims=True)
    return xc * lax.rsqrt(var + LN_EPS) * g + b


def _pack_bf16_halves(y):
    lo = lax.bitcast_convert_type(y[:, :HALF].astype(BF16).astype(F32), jnp.uint32)
    hi = lax.bitcast_convert_type(y[:, HALF:].astype(BF16).astype(F32), jnp.uint32)
    return (hi & jnp.uint32(0xFFFF0000)) | (lo >> 16)


def _unpack_bf16_halves(p):
    lo = lax.bitcast_convert_type(p << 16, F32)
    hi = lax.bitcast_convert_type(p & jnp.uint32(0xFFFF0000), F32)
    return lo, hi


def _inproj_kernel(x_ref, w_ref, bg_ref, cos_ref, sa_ref, sb_ref,
                   q_ref, k_ref, v_ref, u_ref, ga_ref, gc_ref, km_ref):
    tm = x_ref.shape[0]
    xb = x_ref[...].astype(BF16)

    def proj(c):
        return jnp.dot(xb, w_ref[:, c * D_MODEL:(c + 1) * D_MODEL], preferred_element_type=F32)

    cos = cos_ref[...]
    sa = sa_ref[...]
    sb = sb_ref[...]

    def rope_head(t):
        return (t * cos + pltpu.roll(t, HEAD_DIM - ROPE_HALF, 1) * sa
                + pltpu.roll(t, ROPE_HALF, 1) * sb)

    q = proj(0)
    for h in range(N_HEADS):
        sl = slice(h * HEAD_DIM, (h + 1) * HEAD_DIM)
        q_ref[:, sl] = rope_head(q[:, sl]).astype(BF16)
    k = proj(1)
    for h in range(N_HEADS):
        sl = slice(h * HEAD_DIM, (h + 1) * HEAD_DIM)
        kr = rope_head(k[:, sl])
        k_ref[:, sl] = kr.astype(BF16)
        for g in range(tm // MOBA_BLOCK):
            km_ref[g, :, sl] = jnp.mean(kr[g * MOBA_BLOCK:(g + 1) * MOBA_BLOCK], axis=0, keepdims=True)
    v_ref[...] = proj(2).astype(BF16)
    u_ref[...] = proj(3) * _sigmoid(proj(4))
    ga_ref[...] = _sigmoid(proj(5) + bg_ref[:, :D_MODEL]).astype(BF16)
    gc_ref[...] = _sigmoid(proj(6) + bg_ref[:, D_MODEL:]).astype(BF16)


def _in_projection(x2, w_in_b, b_gate, cos, sa, sb, seq):
    n = x2.shape[0]
    tm = TOKEN_TILE
    n_cols = w_in_b.shape[1]
    tiles_per_seq = seq // tm
    row = lambda i: (i, 0)
    const = lambda i: (0, 0)
    pos = lambda i: (i % tiles_per_seq, 0)
    tok_bf16 = jax.ShapeDtypeStruct((n, D_MODEL), BF16)
    return pl.pallas_call(
        _inproj_kernel,
        grid=(n // tm,),
        in_specs=[
            pl.BlockSpec((tm, D_MODEL), row),
            pl.BlockSpec((D_MODEL, n_cols), const),
            pl.BlockSpec((1, 2 * D_MODEL), const),
            pl.BlockSpec((tm, HEAD_DIM), pos),
            pl.BlockSpec((tm, HEAD_DIM), pos),
            pl.BlockSpec((tm, HEAD_DIM), pos),
        ],
        out_specs=[
            pl.BlockSpec((tm, D_MODEL), row),
            pl.BlockSpec((tm, D_MODEL), row),
            pl.BlockSpec((tm, D_MODEL), row),
            pl.BlockSpec((tm, D_MODEL), row),
            pl.BlockSpec((tm, D_MODEL), row),
            pl.BlockSpec((tm, D_MODEL), row),
            pl.BlockSpec((tm // MOBA_BLOCK, 1, D_MODEL), lambda i: (i, 0, 0)),
        ],
        out_shape=[tok_bf16, tok_bf16, tok_bf16,
                   jax.ShapeDtypeStruct((n, D_MODEL), F32),
                   tok_bf16, tok_bf16,
                   jax.ShapeDtypeStruct((n // MOBA_BLOCK, 1, D_MODEL), F32)],
        compiler_params=_params("parallel"),
        name="in_projection",
    )(x2, w_in_b, b_gate, cos, sa, sb)


def _conv_kernel(u_ref, wdw_ref, bdw_ref, lng_ref, lnb_ref, wo_ref, gc_ref, o_ref, buf_ref):
    ts = u_ref.shape[0]
    s = pl.program_id(1)

    @pl.when(s == 0)
    def _():
        buf_ref[0:CONV_HALO, :] = jnp.zeros((CONV_HALO, D_MODEL), F32)

    @pl.when(s > 0)
    def _():
        buf_ref[0:CONV_HALO, :] = buf_ref[ts:ts + CONV_HALO, :]

    buf_ref[CONV_HALO:CONV_HALO + ts, :] = u_ref[...]

    base = CONV_HALO - (CONV_WIDTH - 1)
    acc = jnp.zeros((ts, D_MODEL), F32)
    for j in range(CONV_WIDTH):
        acc = acc + buf_ref[base + j:base + j + ts, :] * wdw_ref[j:j + 1, :]
    acc = acc + bdw_ref[...]
    y = _layer_norm(acc, lng_ref[...], lnb_ref[...])
    y = y * _sigmoid(y)
    z = jnp.dot(y.astype(BF16), wo_ref[...], preferred_element_type=F32)
    o_ref[...] = (z * gc_ref[...].astype(F32)).astype(BF16)


def _conv_branch(u, w_dw, b_dw, ln_g, ln_b, w_o_b, gate_c, batch, seq):
    n = u.shape[0]
    ts = TOKEN_TILE
    tiles_per_seq = seq // ts
    row = lambda b, s: (b * tiles_per_seq + s, 0)
    const = lambda b, s: (0, 0)
    return pl.pallas_call(
        _conv_kernel,
        grid=(batch, tiles_per_seq),
        in_specs=[
            pl.BlockSpec((ts, D_MODEL), row),
            pl.BlockSpec((CONV_WIDTH, D_MODEL), const),
            pl.BlockSpec((1, D_MODEL), const),
            pl.BlockSpec((1, D_MODEL), const),
            pl.BlockSpec((1, D_MODEL), const),
            pl.BlockSpec((D_MODEL, D_MODEL), const),
            pl.BlockSpec((ts, D_MODEL), row),
        ],
        out_specs=pl.BlockSpec((ts, D_MODEL), row),
        out_shape=jax.ShapeDtypeStruct((n, D_MODEL), BF16),
        scratch_shapes=[pltpu.VMEM((ts + CONV_HALO, D_MODEL), F32)],
        compiler_params=_params("parallel", "arbitrary"),
        name="conv_branch",
    )(u, w_dw, b_dw, ln_g, ln_b, w_o_b, gate_c)


def _attn_kernel(q_ref, k_ref, v_ref, km_ref, o_ref, *, n_blk):
    blk = MOBA_BLOCK
    k_sel = min(MOBA_TOPK, n_blk)
    scale = HEAD_DIM ** -0.5
    nt_dims = (((1,), (1,)), ((), ()))

    n_lane = HEAD_DIM
    km = jnp.concatenate([km_ref[:, 0, :], jnp.zeros((n_lane - n_blk, HEAD_DIM), F32)], axis=0)
    km_hi = km.astype(BF16)
    km_lo = (km - km_hi.astype(F32)).astype(BF16)
    blk_id = lax.broadcasted_iota(jnp.int32, (blk, n_lane), 1).astype(F32)
    row_id = lax.broadcasted_iota(jnp.int32, (blk, blk), 0)
    col_id = lax.broadcasted_iota(jnp.int32, (blk, blk), 1)

    def q_block(i, carry):
        i = jnp.asarray(i, jnp.int32)
        r0 = pl.multiple_of(i * blk, blk)
        q = q_ref[pl.ds(r0, blk), :]
        i_f = i.astype(F32)

        gate = (lax.dot_general(q, km_hi, nt_dims, preferred_element_type=F32)
                + lax.dot_general(q, km_lo, nt_dims, preferred_element_type=F32))
        past = blk_id < i_f
        g = jnp.where(past, gate, NEG_INF)
        sel = jnp.zeros((blk, n_lane), F32)
        for _ in range(k_sel):
            mx = jnp.max(g, axis=1, keepdims=True)
            first = jnp.min(jnp.where(g == mx, blk_id, float(n_lane)), axis=1, keepdims=True)
            pick = blk_id == first
            sel = jnp.where(pick, 1.0, sel)
            g = jnp.where(pick, NEG_INF, g)
        sel_bias = jnp.where((sel > 0.0) & past, 0.0, NEG_INF)

        k_o = k_ref[pl.ds(r0, blk), :]
        v_o = v_ref[pl.ds(r0, blk), :]
        s = lax.dot_general(q, k_o, nt_dims, preferred_element_type=F32) * scale
        s = jnp.where(col_id <= row_id, s, NEG_INF)
        m = jnp.max(s, axis=1, keepdims=True)
        p = jnp.exp(s - m)
        l = jnp.sum(p, axis=1, keepdims=True)
        acc = jnp.dot(p.astype(BF16), v_o, preferred_element_type=F32)

        def past_block(j, c):
            m, l, acc = c
            j = jnp.asarray(j, jnp.int32)
            c0 = pl.multiple_of(j * blk, blk)
            k_j = k_ref[pl.ds(c0, blk), :]
            v_j = v_ref[pl.ds(c0, blk), :]
            s = lax.dot_general(q, k_j, nt_dims, preferred_element_type=F32) * scale
            bias = jnp.min(jnp.where(blk_id == j.astype(F32), sel_bias, 0.0), axis=1, keepdims=True)
            s = s + bias
            m_new = jnp.maximum(m, jnp.max(s, axis=1, keepdims=True))
            alpha = jnp.exp(m - m_new)
            p = jnp.exp(s - m_new)
            l = alpha * l + jnp.sum(p, axis=1, keepdims=True)
            acc = alpha * acc + jnp.dot(p.astype(BF16), v_j, preferred_element_type=F32)
            return m_new, l, acc

        m, l, acc = lax.fori_loop(0, i, past_block, (m, l, acc))
        o_ref[pl.ds(r0, blk), :] = (acc / l).astype(BF16)
        return carry

    lax.fori_loop(0, n_blk, q_block, 0)


def _moba_attention(q, k, v, k_mean, batch, seq):
    n = q.shape[0]
    n_blk = seq // MOBA_BLOCK
    seq_head = lambda b, h: (b, h)
    return pl.pallas_call(
        functools.partial(_attn_kernel, n_blk=n_blk),
        grid=(batch, N_HEADS),
        in_specs=[
            pl.BlockSpec((seq, HEAD_DIM), seq_head),
            pl.BlockSpec((seq, HEAD_DIM), seq_head),
            pl.BlockSpec((seq, HEAD_DIM), seq_head),
            pl.BlockSpec((n_blk, 1, HEAD_DIM), lambda b, h: (b, 0, h)),
        ],
        out_specs=pl.BlockSpec((seq, HEAD_DIM), seq_head),
        out_shape=jax.ShapeDtypeStruct((n, D_MODEL), BF16),
        compiler_params=_params("parallel", "parallel"),
        name="moba_attention",
    )(q, k, v, k_mean)


def _merge_kernel(attn_ref, ga_ref, cg_ref, x_ref, wao_ref, wout_ref, g1_ref, b1_ref,
                  wr_ref, rb_ref, h_ref, hp_ref, idx_ref, wgt_ref):
    tm = x_ref.shape[0]
    a = jnp.dot(attn_ref[...], wao_ref[...], preferred_element_type=F32)
    merged = ga_ref[...].astype(F32) * a + cg_ref[...].astype(F32)
    y = jnp.dot(merged.astype(BF16), wout_ref[...], preferred_element_type=F32)
    h = _layer_norm(DEEPNORM_ALPHA * x_ref[...] + y, g1_ref[...], b1_ref[...])
    h_ref[...] = h
    hp_ref[...] = _pack_bf16_halves(h)

    nt_dims = (((1,), (1,)), ((), ()))
    h_hi = h.astype(BF16)
    h_lo = (h - h_hi.astype(F32)).astype(BF16)
    w = wr_ref[...]
    w_hi = w.astype(BF16)
    w_lo = (w - w_hi.astype(F32)).astype(BF16)
    logits = (lax.dot_general(w_hi, h_hi, nt_dims, preferred_element_type=F32)
              + lax.dot_general(w_hi, h_lo, nt_dims, preferred_element_type=F32)
              + lax.dot_general(w_lo, h_hi, nt_dims, preferred_element_type=F32))
    scores = _sigmoid(logits)
    biased = scores + rb_ref[...]

    g3 = biased.reshape(N_GROUPS, GROUP_SIZE, tm)
    m1 = jnp.max(g3, axis=1, keepdims=True)
    is_max = g3 == m1
    n_max = jnp.sum(jnp.where(is_max, 1.0, 0.0), axis=1, keepdims=True)
    m2 = jnp.max(jnp.where(is_max, NEG_INF, g3), axis=1, keepdims=True)
    grp = (m1 + jnp.where(n_max >= 2.0, m1, m2)).reshape(N_GROUPS, tm)

    gid = lax.broadcasted_iota(jnp.int32, (N_GROUPS, tm), 0)
    rank = jnp.zeros((N_GROUPS, tm), F32)
    for o in range(N_GROUPS):
        other = grp[o:o + 1, :]
        ahead = (other > grp) | ((other == grp) & (o < gid))
        rank = rank + jnp.where(ahead, 1.0, 0.0)
    grp_keep = jnp.where(rank < float(TOPK_GROUPS), 1.0, 0.0)
    keep = jnp.broadcast_to(grp_keep.reshape(N_GROUPS, 1, tm),
                            (N_GROUPS, GROUP_SIZE, tm)).reshape(N_EXPERTS, tm)
    cand = jnp.where(keep > 0.0, biased, NEG_INF)

    eid = lax.broadcasted_iota(jnp.int32, (N_EXPERTS, tm), 0).astype(F32)
    for r in range(TOP_K):
        mx = jnp.max(cand, axis=0, keepdims=True)
        first = jnp.min(jnp.where(cand == mx, eid, float(N_EXPERTS)), axis=0, keepdims=True)
        pick = eid == first
        idx_ref[r:r + 1, :] = first.astype(jnp.int32)
        wgt_ref[r:r + 1, :] = jnp.sum(jnp.where(pick, scores, 0.0), axis=0, keepdims=True)
        cand = jnp.where(pick, NEG_INF, cand)
    top_s = wgt_ref[...]
    wgt_ref[...] = top_s / (jnp.sum(top_s, axis=0, keepdims=True) + 1e-20) * ROUTED_SCALE


def _merge_ln_router(attn, gate_a, conv_gated, x2, w_attn_o_b, w_out_b, ln_g, ln_b, w_router_t, router_bias):
    n = x2.shape[0]
    tm = TOKEN_TILE
    row = lambda i: (i, 0)
    const = lambda i: (0, 0)
    col = lambda i: (0, i)
    return pl.pallas_call(
        _merge_kernel,
        grid=(n // tm,),
        in_specs=[
            pl.BlockSpec((tm, D_MODEL), row),
            pl.BlockSpec((tm, D_MODEL), row),
            pl.BlockSpec((tm, D_MODEL), row),
            pl.BlockSpec((tm, D_MODEL), row),
            pl.BlockSpec((D_MODEL, D_MODEL), const),
            pl.BlockSpec((D_MODEL, D_MODEL), const),
            pl.BlockSpec((1, D_MODEL), const),
            pl.BlockSpec((1, D_MODEL), const),
            pl.BlockSpec((N_EXPERTS, D_MODEL), const),
            pl.BlockSpec((N_EXPERTS, 1), const),
        ],
        out_specs=[
            pl.BlockSpec((tm, D_MODEL), row),
            pl.BlockSpec((tm, HALF), row),
            pl.BlockSpec((TOP_K, tm), col),
            pl.BlockSpec((TOP_K, tm), col),
        ],
        out_shape=[
            jax.ShapeDtypeStruct((n, D_MODEL), F32),
            jax.ShapeDtypeStruct((n, HALF), jnp.uint32),
            jax.ShapeDtypeStruct((TOP_K, n), jnp.int32),
            jax.ShapeDtypeStruct((TOP_K, n), F32),
        ],
        compiler_params=_params("parallel"),
        name="merge_ln_router",
    )(attn, gate_a, conv_gated, x2, w_attn_o_b, w_out_b, ln_g, ln_b, w_router_t, router_bias)


def _gather_kernel(idx_ref, table_ref, o_ref, sem):
    rows = o_ref.shape[0]

    def row_copy(r):
        return pltpu.make_async_copy(table_ref.at[pl.ds(idx_ref[0, 0, r], 1), :],
                                     o_ref.at[pl.ds(r, 1), :], sem)

    def issue(r, c):
        row_copy(r).start()
        return c

    lax.fori_loop(0, rows, issue, 0)

    def drain(r, c):
        row_copy(r).wait()
        return c

    lax.fori_loop(0, rows, drain, 0)


def _gather_rows(table, idx):
    m = idx.shape[0]
    width = table.shape[1]
    g = GATHER_ROWS
    idx3 = idx.reshape(m // g, 1, g)
    return pl.pallas_call(
        _gather_kernel,
        grid=(m // g,),
        in_specs=[
            pl.BlockSpec((1, 1, g), lambda i: (i, 0, 0), memory_space=pltpu.SMEM),
            pl.BlockSpec(memory_space=pl.ANY),
        ],
        out_specs=pl.BlockSpec((g, width), lambda i: (i, 0)),
        out_shape=jax.ShapeDtypeStruct((m, width), table.dtype),
        scratch_shapes=[pltpu.SemaphoreType.DMA],
        compiler_params=_params("arbitrary"),
        name="gather_rows",
    )(idx3, table)


def _expert_kernel(te_ref, xs_ref, wg_ref, wu_ref, wd_ref, y_ref):
    lo, hi = _unpack_bf16_halves(xs_ref[...])
    x = jnp.concatenate([lo, hi], axis=1).astype(BF16)
    g = jnp.dot(x, wg_ref[0], preferred_element_type=F32)
    u = jnp.dot(x, wu_ref[0], preferred_element_type=F32)
    a = (g * _sigmoid(g) * u).astype(BF16)
    y_ref[...] = _pack_bf16_halves(jnp.dot(a, wd_ref[0], preferred_element_type=F32))


def _grouped_experts(tile_expert, xs, wg_b, wu_b, wd_b):
    p = xs.shape[0]
    t = EXPERT_TILE
    row = lambda i, te: (i, 0)
    expert = lambda i, te: (te[i], 0, 0)
    return pl.pallas_call(
        _expert_kernel,
        grid_spec=pltpu.PrefetchScalarGridSpec(
            num_scalar_prefetch=1,
            grid=(p // t,),
            in_specs=[
                pl.BlockSpec((t, HALF), row),
                pl.BlockSpec((1, D_MODEL, EXPERT_HIDDEN), expert),
                pl.BlockSpec((1, D_MODEL, EXPERT_HIDDEN), expert),
                pl.BlockSpec((1, EXPERT_HIDDEN, D_MODEL), expert),
            ],
            out_specs=pl.BlockSpec((t, HALF), row),
        ),
        out_shape=jax.ShapeDtypeStruct((p, HALF), jnp.uint32),
        compiler_params=_params("arbitrary"),
        name="grouped_experts",
    )(tile_expert, xs, wg_b, wu_b, wd_b)


def _combine_kernel(h_ref, yg_ref, wt_ref, wsg_ref, wsu_ref, wsd_ref, g2_ref, b2_ref, o_ref):
    h = h_ref[...]
    hb = h.astype(BF16)
    g = jnp.dot(hb, wsg_ref[...], preferred_element_type=F32)
    u = jnp.dot(hb, wsu_ref[...], preferred_element_type=F32)
    shared = jnp.dot((g * _sigmoid(g) * u).astype(BF16), wsd_ref[...], preferred_element_type=F32)
    wt = wt_ref[...]
    r_lo = jnp.zeros((h.shape[0], HALF), F32)
    r_hi = jnp.zeros((h.shape[0], HALF), F32)
    for r in range(TOP_K):
        lo, hi = _unpack_bf16_halves(yg_ref[r])
        w = wt[:, r:r + 1]
        r_lo = r_lo + lo * w
        r_hi = r_hi + hi * w
    routed = jnp.concatenate([r_lo, r_hi], axis=1)
    o_ref[...] = _layer_norm(DEEPNORM_ALPHA * h + (shared + routed), g2_ref[...], b2_ref[...])


def _combine_ln(h, yg, w_tok, wsg_b, wsu_b, wsd_b, ln_g, ln_b):
    n = h.shape[0]
    tm = TOKEN_TILE
    row = lambda i: (i, 0)
    const = lambda i: (0, 0)
    hidden = wsg_b.shape[1]
    return pl.pallas_call(
        _combine_kernel,
        grid=(n // tm,),
        in_specs=[
            pl.BlockSpec((tm, D_MODEL), row),
            pl.BlockSpec((TOP_K, tm, HALF), lambda i: (0, i, 0)),
            pl.BlockSpec((tm, TOP_K), row),
            pl.BlockSpec((D_MODEL, hidden), const),
            pl.BlockSpec((D_MODEL, hidden), const),
            pl.BlockSpec((hidden, D_MODEL), const),
            pl.BlockSpec((1, D_MODEL), const),
            pl.BlockSpec((1, D_MODEL), const),
        ],
        out_specs=pl.BlockSpec((tm, D_MODEL), row),
        out_shape=jax.ShapeDtypeStruct((n, D_MODEL), F32),
        compiler_params=_params("parallel"),
        name="combine_ln",
    )(h, yg, w_tok, wsg_b, wsu_b, wsd_b, ln_g, ln_b)


def _dispatch_plan(top_idx, n_tokens):
    t = EXPERT_TILE
    a = TOP_K * n_tokens
    n_tiles = (a + N_EXPERTS * (t - 1)) // t
    p = n_tiles * t
    e_flat = top_idx.reshape(a)
    order = jnp.argsort(e_flat).astype(jnp.int32)
    e_sorted = e_flat[order]
    counts = jnp.zeros((N_EXPERTS,), jnp.int32).at[e_flat].add(1)
    starts = jnp.cumsum(counts) - counts
    padded = ((counts + t - 1) // t) * t
    pad_ends = jnp.cumsum(padded)
    pad_starts = pad_ends - padded
    dest = pad_starts[e_sorted] + (jnp.arange(a, dtype=jnp.int32) - starts[e_sorted])
    row_tok = jnp.zeros((p,), jnp.int32).at[dest].set(order % n_tokens)
    pos = jnp.zeros((a,), jnp.int32).at[order].set(dest)
    tile_start = jnp.arange(n_tiles, dtype=jnp.int32) * t
    tile_expert = jnp.minimum(jnp.searchsorted(pad_ends, tile_start, side="right"),
                              N_EXPERTS - 1).astype(jnp.int32)
    return row_tok, pos, tile_expert


def _rope_tables(seq):
    inv_freq = ROPE_THETA ** (-jnp.arange(0, ROPE_DIM, 2, dtype=F32) / ROPE_DIM)
    ang = jnp.arange(seq).astype(F32)[:, None] * inv_freq[None, :]
    cos, sin = jnp.cos(ang), jnp.sin(ang)
    rest = HEAD_DIM - ROPE_DIM
    zeros = jnp.zeros((seq, ROPE_HALF), F32)
    cos_t = jnp.concatenate([cos, cos, jnp.ones((seq, rest), F32)], axis=1)
    sa_t = jnp.concatenate([-sin, zeros, jnp.zeros((seq, rest), F32)], axis=1)
    sb_t = jnp.concatenate([zeros, sin, jnp.zeros((seq, rest), F32)], axis=1)
    return cos_t, sa_t, sb_t


def _layer(x2, batch, seq, w_in, b_gate, w_attn_o, w_dw, b_dw, conv_ln_g, conv_ln_b, w_conv_o, w_out,
           ln1_g, ln1_b, w_router, router_bias, w_exp_gate, w_exp_up, w_exp_down,
           w_sh_gate, w_sh_up, w_sh_down, ln2_g, ln2_b):
    n = x2.shape[0]
    row = lambda v: v.reshape(1, -1)
    cos, sa, sb = _rope_tables(seq)
    q, k, v, u, gate_a, gate_c, k_mean = _in_projection(
        x2, w_in.astype(BF16), row(b_gate), cos, sa, sb, seq)
    conv_gated = _conv_branch(u, w_dw.reshape(CONV_WIDTH, D_MODEL), row(b_dw), row(conv_ln_g),
                              row(conv_ln_b), w_conv_o.astype(BF16), gate_c, batch, seq)
    attn = _moba_attention(q, k, v, k_mean, batch, seq)
    h, h_packed, top_idx, top_w = _merge_ln_router(
        attn, gate_a, conv_gated, x2, w_attn_o.astype(BF16), w_out.astype(BF16), row(ln1_g), row(ln1_b),
        w_router.T, router_bias.reshape(N_EXPERTS, 1))
    row_tok, pos, tile_expert = _dispatch_plan(top_idx, n)
    xs = _gather_rows(h_packed, row_tok)
    ys = _grouped_experts(tile_expert, xs, w_exp_gate.astype(BF16), w_exp_up.astype(BF16),
                          w_exp_down.astype(BF16))
    yg = _gather_rows(ys, pos).reshape(TOP_K, n, HALF)
    return _combine_ln(h, yg, top_w.T, w_sh_gate.astype(BF16), w_sh_up.astype(BF16),
                       w_sh_down.astype(BF16), row(ln2_g), row(ln2_b))


def kernel(x, w_in, b_gate, w_attn_o, w_dw, b_dw, conv_ln_g, conv_ln_b, w_conv_o, w_out, ln1_g, ln1_b,
           w_router, router_bias, w_exp_gate, w_exp_up, w_exp_down, w_sh_gate, w_sh_up, w_sh_down,
           ln2_g, ln2_b):
    batch, seq, d = x.shape
    assert d == D_MODEL and seq % MOBA_BLOCK == 0 and seq % TOKEN_TILE == 0
    assert w_in.shape[0] == DEPTH
    x2 = x.reshape(batch * seq, d)
    for l in range(DEPTH):
        x2 = _layer(x2, batch, seq, w_in[l], b_gate[l], w_attn_o[l], w_dw[l], b_dw[l], conv_ln_g[l],
                    conv_ln_b[l], w_conv_o[l], w_out[l], ln1_g[l], ln1_b[l], w_router[l], router_bias[l],
                    w_exp_gate[l], w_exp_up[l], w_exp_down[l], w_sh_gate[l], w_sh_up[l], w_sh_down[l],
                    ln2_g[l], ln2_b[l])
    return x2.reshape(batch, seq, d)
```

```python
import functools

import jax
import jax.numpy as jnp
from jax import lax
from jax.experimental import pallas as pl
from jax.experimental.pallas import tpu as pltpu
from jax.experimental.pallas import tpu_sc as plsc

F32 = jnp.float32
BF16 = jnp.bfloat16
NEG_INF = float("-inf")

D_MODEL = 1024
N_HEADS = 8
HEAD_DIM = 128
ROPE_THETA = 500000.0
ROPE_DIM = HEAD_DIM // 4
ROPE_HALF = ROPE_DIM // 2
MOBA_BLOCK = 256
MOBA_TOPK = 3
CONV_WIDTH = 31
CONV_HALO = 32
N_EXPERTS = 256
TOP_K = 8
N_GROUPS = 8
GROUP_SIZE = N_EXPERTS // N_GROUPS
TOPK_GROUPS = 4
EXPERT_HIDDEN = 256
ROUTED_SCALE = 2.5
LN_EPS = 1e-5
DEPTH = 1
DEEPNORM_ALPHA = (2 * DEPTH) ** 0.25
HALF = D_MODEL // 2

TOKEN_TILE = 256
EXPERT_TILE = 256
POSITION_TILE = 1024
SC_CORES = 2
SC_SUBCORES = 16
SC_CHUNK = 64
VMEM_LIMIT = 56 * 1024 * 1024


def _params(*semantics):
    return pltpu.CompilerParams(dimension_semantics=semantics, vmem_limit_bytes=VMEM_LIMIT)


def _sigmoid(x):
    return 1.0 / (1.0 + jnp.exp(-x))


def _layer_norm(x, g, b):
    mu = jnp.mean(x, axis=-1, keepdims=True)
    xc = x - mu
    var = jnp.mean(xc * xc, axis=-1, keepdims=True)
    return xc * lax.rsqrt(var + LN_EPS) * g + b


def _pack_bf16_halves(y):
    lo = lax.bitcast_convert_type(y[:, :HALF].astype(BF16).astype(F32), jnp.uint32)
    hi = lax.bitcast_convert_type(y[:, HALF:].astype(BF16).astype(F32), jnp.uint32)
    return (hi & jnp.uint32(0xFFFF0000)) | (lo >> 16)


def _unpack_bf16_halves(p):
    lo = lax.bitcast_convert_type(p << 16, F32)
    hi = lax.bitcast_convert_type(p & jnp.uint32(0xFFFF0000), F32)
    return lo, hi


def _inproj_kernel(x_ref, w_ref, bg_ref, cos_ref, sa_ref, sb_ref,
                   q_ref, k_ref, v_ref, u_ref, ga_ref, gc_ref, km_ref):
    tm = x_ref.shape[0]
    xb = x_ref[...].astype(BF16)

    def proj(c):
        return jnp.dot(xb, w_ref[:, c * D_MODEL:(c + 1) * D_MODEL], preferred_element_type=F32)

    cos = cos_ref[...]
    sa = sa_ref[...]
    sb = sb_ref[...]

    def rope_head(t):
        return (t * cos + pltpu.roll(t, HEAD_DIM - ROPE_HALF, 1) * sa
                + pltpu.roll(t, ROPE_HALF, 1) * sb)

    q = proj(0)
    for h in range(N_HEADS):
        sl = slice(h * HEAD_DIM, (h + 1) * HEAD_DIM)
        q_ref[:, sl] = rope_head(q[:, sl]).astype(BF16)
    k = proj(1)
    for h in range(N_HEADS):
        sl = slice(h * HEAD_DIM, (h + 1) * HEAD_DIM)
        kr = rope_head(k[:, sl])
        k_ref[:, sl] = kr.astype(BF16)
        for g in range(tm // MOBA_BLOCK):
            km_ref[g, :, sl] = jnp.mean(kr[g * MOBA_BLOCK:(g + 1) * MOBA_BLOCK], axis=0, keepdims=True)
    v_ref[...] = proj(2).astype(BF16)
    u_ref[...] = proj(3) * _sigmoid(proj(4))
    ga_ref[...] = _sigmoid(proj(5) + bg_ref[:, :D_MODEL]).astype(BF16)
    gc_ref[...] = _sigmoid(proj(6) + bg_ref[:, D_MODEL:]).astype(BF16)


def _in_projection(x2, w_in_b, b_gate, cos, sa, sb, seq):
    n = x2.shape[0]
    tm = TOKEN_TILE
    n_cols = w_in_b.shape[1]
    tiles_per_seq = seq // tm
    row = lambda i: (i, 0)
    const = lambda i: (0, 0)
    pos = lambda i: (i % tiles_per_seq, 0)
    tok_bf16 = jax.ShapeDtypeStruct((n, D_MODEL), BF16)
    return pl.pallas_call(
        _inproj_kernel,
        grid=(n // tm,),
        in_specs=[
            pl.BlockSpec((tm, D_MODEL), row),
            pl.BlockSpec((D_MODEL, n_cols), const),
            pl.BlockSpec((1, 2 * D_MODEL), const),
            pl.BlockSpec((tm, HEAD_DIM), pos),
            pl.BlockSpec((tm, HEAD_DIM), pos),
            pl.BlockSpec((tm, HEAD_DIM), pos),
        ],
        out_specs=[
            pl.BlockSpec((tm, D_MODEL), row),
            pl.BlockSpec((tm, D_MODEL), row),
            pl.BlockSpec((tm, D_MODEL), row),
            pl.BlockSpec((tm, D_MODEL), row),
            pl.BlockSpec((tm, D_MODEL), row),
            pl.BlockSpec((tm, D_MODEL), row),
            pl.BlockSpec((tm // MOBA_BLOCK, 1, D_MODEL), lambda i: (i, 0, 0)),
        ],
        out_shape=[tok_bf16, tok_bf16, tok_bf16,
                   jax.ShapeDtypeStruct((n, D_MODEL), F32),
                   tok_bf16, tok_bf16,
                   jax.ShapeDtypeStruct((n // MOBA_BLOCK, 1, D_MODEL), F32)],
        compiler_params=_params("parallel"),
        name="in_projection",
    )(x2, w_in_b, b_gate, cos, sa, sb)


def _conv_kernel(u_ref, wdw_ref, bdw_ref, lng_ref, lnb_ref, wo_ref, gc_ref, o_ref, buf_ref):
    ts = u_ref.shape[0]
    s = pl.program_id(1)

    @pl.when(s == 0)
    def _():
        buf_ref[0:CONV_HALO, :] = jnp.zeros((CONV_HALO, D_MODEL), F32)

    @pl.when(s > 0)
    def _():
        buf_ref[0:CONV_HALO, :] = buf_ref[ts:ts + CONV_HALO, :]

    buf_ref[CONV_HALO:CONV_HALO + ts, :] = u_ref[...]

    base = CONV_HALO - (CONV_WIDTH - 1)
    acc = jnp.zeros((ts, D_MODEL), F32)
    for j in range(CONV_WIDTH):
        acc = acc + buf_ref[base + j:base + j + ts, :] * wdw_ref[j:j + 1, :]
    acc = acc + bdw_ref[...]
    y = _layer_norm(acc, lng_ref[...], lnb_ref[...])
    y = y * _sigmoid(y)
    z = jnp.dot(y.astype(BF16), wo_ref[...], preferred_element_type=F32)
    o_ref[...] = (z * gc_ref[...].astype(F32)).astype(BF16)


def _conv_branch(u, w_dw, b_dw, ln_g, ln_b, w_o_b, gate_c, batch, seq):
    n = u.shape[0]
    ts = TOKEN_TILE
    tiles_per_seq = seq // ts
    row = lambda b, s: (b * tiles_per_seq + s, 0)
    const = lambda b, s: (0, 0)
    return pl.pallas_call(
        _conv_kernel,
        grid=(batch, tiles_per_seq),
        in_specs=[
            pl.BlockSpec((ts, D_MODEL), row),
            pl.BlockSpec((CONV_WIDTH, D_MODEL), const),
            pl.BlockSpec((1, D_MODEL), const),
            pl.BlockSpec((1, D_MODEL), const),
            pl.BlockSpec((1, D_MODEL), const),
            pl.BlockSpec((D_MODEL, D_MODEL), const),
            pl.BlockSpec((ts, D_MODEL), row),
        ],
        out_specs=pl.BlockSpec((ts, D_MODEL), row),
        out_shape=jax.ShapeDtypeStruct((n, D_MODEL), BF16),
        scratch_shapes=[pltpu.VMEM((ts + CONV_HALO, D_MODEL), F32)],
        compiler_params=_params("parallel", "arbitrary"),
        name="conv_branch",
    )(u, w_dw, b_dw, ln_g, ln_b, w_o_b, gate_c)


def _attn_kernel(q_ref, k_ref, v_ref, km_ref, o_ref, *, n_blk):
    blk = MOBA_BLOCK
    k_sel = min(MOBA_TOPK, n_blk)
    scale = HEAD_DIM ** -0.5
    nt_dims = (((1,), (1,)), ((), ()))

    n_lane = HEAD_DIM
    km = jnp.concatenate([km_ref[:, 0, :], jnp.zeros((n_lane - n_blk, HEAD_DIM), F32)], axis=0)
    km_hi = km.astype(BF16)
    km_lo = (km - km_hi.astype(F32)).astype(BF16)
    blk_id = lax.broadcasted_iota(jnp.int32, (blk, n_lane), 1).astype(F32)
    row_id = lax.broadcasted_iota(jnp.int32, (blk, blk), 0)
    col_id = lax.broadcasted_iota(jnp.int32, (blk, blk), 1)

    def q_block(i, carry):
        i = jnp.asarray(i, jnp.int32)
        r0 = pl.multiple_of(i * blk, blk)
        q = q_ref[pl.ds(r0, blk), :]
        i_f = i.astype(F32)

        gate = (lax.dot_general(q, km_hi, nt_dims, preferred_element_type=F32)
                + lax.dot_general(q, km_lo, nt_dims, preferred_element_type=F32))
        past = blk_id < i_f
        g = jnp.where(past, gate, NEG_INF)
        sel = jnp.zeros((blk, n_lane), F32)
        for _ in range(k_sel):
            mx = jnp.max(g, axis=1, keepdims=True)
            first = jnp.min(jnp.where(g == mx, blk_id, float(n_lane)), axis=1, keepdims=True)
            pick = blk_id == first
            sel = jnp.where(pick, 1.0, sel)
            g = jnp.where(pick, NEG_INF, g)
        sel_bias = jnp.where((sel > 0.0) & past, 0.0, NEG_INF)

        k_o = k_ref[pl.ds(r0, blk), :]
        v_o = v_ref[pl.ds(r0, blk), :]
        s = lax.dot_general(q, k_o, nt_dims, preferred_element_type=F32) * scale
        s = jnp.where(col_id <= row_id, s, NEG_INF)
        m = jnp.max(s, axis=1, keepdims=True)
        p = jnp.exp(s - m)
        l = jnp.sum(p, axis=1, keepdims=True)
        acc = jnp.dot(p.astype(BF16), v_o, preferred_element_type=F32)

        def past_block(j, c):
            m, l, acc = c
            j = jnp.asarray(j, jnp.int32)
            c0 = pl.multiple_of(j * blk, blk)
            k_j = k_ref[pl.ds(c0, blk), :]
            v_j = v_ref[pl.ds(c0, blk), :]
            s = lax.dot_general(q, k_j, nt_dims, preferred_element_type=F32) * scale
            bias = jnp.min(jnp.where(blk_id == j.astype(F32), sel_bias, 0.0), axis=1, keepdims=True)
            s = s + bias
            m_new = jnp.maximum(m, jnp.max(s, axis=1, keepdims=True))
            alpha = jnp.exp(m - m_new)
            p = jnp.exp(s - m_new)
            l = alpha * l + jnp.sum(p, axis=1, keepdims=True)
            acc = alpha * acc + jnp.dot(p.astype(BF16), v_j, preferred_element_type=F32)
            return m_new, l, acc

        m, l, acc = lax.fori_loop(0, i, past_block, (m, l, acc))
        o_ref[pl.ds(r0, blk), :] = (acc / l).astype(BF16)
        return carry

    lax.fori_loop(0, n_blk, q_block, 0)


def _moba_attention(q, k, v, k_mean, batch, seq):
    n = q.shape[0]
    n_blk = seq // MOBA_BLOCK
    seq_head = lambda b, h: (b, h)
    return pl.pallas_call(
        functools.partial(_attn_kernel, n_blk=n_blk),
        grid=(batch, N_HEADS),
        in_specs=[
            pl.BlockSpec((seq, HEAD_DIM), seq_head),
            pl.BlockSpec((seq, HEAD_DIM), seq_head),
            pl.BlockSpec((seq, HEAD_DIM), seq_head),
            pl.BlockSpec((n_blk, 1, HEAD_DIM), lambda b, h: (b, 0, h)),
        ],
        out_specs=pl.BlockSpec((seq, HEAD_DIM), seq_head),
        out_shape=jax.ShapeDtypeStruct((n, D_MODEL), BF16),
        compiler_params=_params("parallel", "parallel"),
        name="moba_attention",
    )(q, k, v, k_mean)


def _merge_kernel(attn_ref, ga_ref, cg_ref, x_ref, wao_ref, wout_ref, g1_ref, b1_ref,
                  wr_ref, rb_ref, h_ref, hp_ref, idx_ref, wgt_ref, rank_ref, total_ref, count_ref):
    tm = x_ref.shape[0]
    a = jnp.dot(attn_ref[...], wao_ref[...], preferred_element_type=F32)
    merged = ga_ref[...].astype(F32) * a + cg_ref[...].astype(F32)
    y = jnp.dot(merged.astype(BF16), wout_ref[...], preferred_element_type=F32)
    h = _layer_norm(DEEPNORM_ALPHA * x_ref[...] + y, g1_ref[...], b1_ref[...])
    h_ref[...] = h
    hp_ref[...] = _pack_bf16_halves(h)

    nt_dims = (((1,), (1,)), ((), ()))
    h_hi = h.astype(BF16)
    h_lo = (h - h_hi.astype(F32)).astype(BF16)
    w = wr_ref[...]
    w_hi = w.astype(BF16)
    w_lo = (w - w_hi.astype(F32)).astype(BF16)
    logits = (lax.dot_general(w_hi, h_hi, nt_dims, preferred_element_type=F32)
              + lax.dot_general(w_hi, h_lo, nt_dims, preferred_element_type=F32)
              + lax.dot_general(w_lo, h_hi, nt_dims, preferred_element_type=F32))
    scores = _sigmoid(logits)
    biased = scores + rb_ref[...]

    g3 = biased.reshape(N_GROUPS, GROUP_SIZE, tm)
    m1 = jnp.max(g3, axis=1, keepdims=True)
    is_max = g3 == m1
    n_max = jnp.sum(jnp.where(is_max, 1.0, 0.0), axis=1, keepdims=True)
    m2 = jnp.max(jnp.where(is_max, NEG_INF, g3), axis=1, keepdims=True)
    grp = (m1 + jnp.where(n_max >= 2.0, m1, m2)).reshape(N_GROUPS, tm)

    gid = lax.broadcasted_iota(jnp.int32, (N_GROUPS, tm), 0)
    rank = jnp.zeros((N_GROUPS, tm), F32)
    for o in range(N_GROUPS):
        other = grp[o:o + 1, :]
        ahead = (other > grp) | ((other == grp) & (o < gid))
        rank = rank + jnp.where(ahead, 1.0, 0.0)
    grp_keep = jnp.where(rank < float(TOPK_GROUPS), 1.0, 0.0)
    keep = jnp.broadcast_to(grp_keep.reshape(N_GROUPS, 1, tm),
                            (N_GROUPS, GROUP_SIZE, tm)).reshape(N_EXPERTS, tm)
    cand = jnp.where(keep > 0.0, biased, NEG_INF)

    eid = lax.broadcasted_iota(jnp.int32, (N_EXPERTS, tm), 0).astype(F32)
    chosen = jnp.zeros((N_EXPERTS, tm), F32)
    firsts = []
    for r in range(TOP_K):
        mx = jnp.max(cand, axis=0, keepdims=True)
        first = jnp.min(jnp.where(cand == mx, eid, float(N_EXPERTS)), axis=0, keepdims=True)
        pick = eid == first
        firsts.append(first)
        idx_ref[r:r + 1, :] = first.astype(jnp.int32)
        wgt_ref[r:r + 1, :] = jnp.sum(jnp.where(pick, scores, 0.0), axis=0, keepdims=True)
        cand = jnp.where(pick, NEG_INF, cand)
        chosen = jnp.where(pick, 1.0, chosen)
    top_s = wgt_ref[...]
    wgt_ref[...] = top_s / (jnp.sum(top_s, axis=0, keepdims=True) + 1e-20) * ROUTED_SCALE

    @pl.when(pl.program_id(0) == 0)
    def _():
        count_ref[...] = jnp.zeros((N_EXPERTS, 1), F32)

    earlier = (lax.broadcasted_iota(jnp.int32, (tm, tm), 0)
               < lax.broadcasted_iota(jnp.int32, (tm, tm), 1))
    before = jnp.dot(chosen.astype(BF16), jnp.where(earlier, 1.0, 0.0).astype(BF16),
                     preferred_element_type=F32) + count_ref[...]
    for r in range(TOP_K):
        rank_ref[r:r + 1, :] = jnp.sum(jnp.where(eid == firsts[r], before, 0.0),
                                       axis=0, keepdims=True).astype(jnp.int32)
    total = count_ref[...] + jnp.sum(chosen, axis=1, keepdims=True)
    count_ref[...] = total
    total_ref[...] = total


def _merge_ln_router(attn, gate_a, conv_gated, x2, w_attn_o_b, w_out_b, ln_g, ln_b, w_router_t, router_bias):
    n = x2.shape[0]
    tm = TOKEN_TILE
    row = lambda i: (i, 0)
    const = lambda i: (0, 0)
    col = lambda i: (0, i)
    return pl.pallas_call(
        _merge_kernel,
        grid=(n // tm,),
        in_specs=[
            pl.BlockSpec((tm, D_MODEL), row),
            pl.BlockSpec((tm, D_MODEL), row),
            pl.BlockSpec((tm, D_MODEL), row),
            pl.BlockSpec((tm, D_MODEL), row),
            pl.BlockSpec((D_MODEL, D_MODEL), const),
            pl.BlockSpec((D_MODEL, D_MODEL), const),
            pl.BlockSpec((1, D_MODEL), const),
            pl.BlockSpec((1, D_MODEL), const),
            pl.BlockSpec((N_EXPERTS, D_MODEL), const),
            pl.BlockSpec((N_EXPERTS, 1), const),
        ],
        out_specs=[
            pl.BlockSpec((tm, D_MODEL), row),
            pl.BlockSpec((tm, HALF), row),
            pl.BlockSpec((TOP_K, tm), col),
            pl.BlockSpec((TOP_K, tm), col),
            pl.BlockSpec((TOP_K, tm), col),
            pl.BlockSpec((N_EXPERTS, 1), const),
        ],
        out_shape=[
            jax.ShapeDtypeStruct((n, D_MODEL), F32),
            jax.ShapeDtypeStruct((n, HALF), jnp.uint32),
            jax.ShapeDtypeStruct((TOP_K, n), jnp.int32),
            jax.ShapeDtypeStruct((TOP_K, n), F32),
            jax.ShapeDtypeStruct((TOP_K, n), jnp.int32),
            jax.ShapeDtypeStruct((N_EXPERTS, 1), F32),
        ],
        scratch_shapes=[pltpu.VMEM((N_EXPERTS, 1), F32)],
        compiler_params=_params("arbitrary"),
        name="merge_ln_router",
    )(attn, gate_a, conv_gated, x2, w_attn_o_b, w_out_b, ln_g, ln_b, w_router_t, router_bias)


def _sc_mesh():
    return plsc.VectorSubcoreMesh(core_axis_name="c", subcore_axis_name="s",
                                  num_cores=SC_CORES, num_subcores=SC_SUBCORES)


def _sc_worker_base(rows_per_worker):
    return (lax.axis_index("s") * SC_CORES + lax.axis_index("c")) * rows_per_worker


def _sc_gather_rows(table, idx):
    m = idx.shape[0]
    width = table.shape[1]
    per_worker = m // (SC_CORES * SC_SUBCORES)
    assert per_worker * SC_CORES * SC_SUBCORES == m and per_worker % SC_CHUNK == 0

    @functools.partial(
        pl.kernel, mesh=_sc_mesh(),
        out_type=jax.ShapeDtypeStruct((m, width), table.dtype),
        scratch_types=[pltpu.VMEM((SC_CHUNK,), jnp.int32),
                       pltpu.VMEM((SC_CHUNK, width), table.dtype),
                       pltpu.SemaphoreType.DMA],
        name="sc_gather_rows")
    def gather(table_hbm, idx_hbm, out_hbm, idx_v, rows_v, sem):
        base = _sc_worker_base(per_worker)

        @pl.loop(0, per_worker // SC_CHUNK)
        def _(c):
            off = pl.multiple_of(base + c * SC_CHUNK, SC_CHUNK)
            pltpu.sync_copy(idx_hbm.at[pl.ds(off, SC_CHUNK)], idx_v)
            pltpu.async_copy(table_hbm.at[idx_v], rows_v, sem).wait()
            pltpu.sync_copy(rows_v, out_hbm.at[pl.ds(off, SC_CHUNK)])

    return gather(table, idx)


def _sc_scatter_rows(rows, pos, n_out):
    n, width = rows.shape
    per_worker = n // (SC_CORES * SC_SUBCORES)
    assert per_worker * SC_CORES * SC_SUBCORES == n and per_worker % SC_CHUNK == 0

    @functools.partial(
        pl.kernel, mesh=_sc_mesh(),
        out_type=jax.ShapeDtypeStruct((n_out, width), rows.dtype),
        scratch_types=[pltpu.VMEM((SC_CHUNK,), jnp.int32),
                       pltpu.VMEM((SC_CHUNK, width), rows.dtype),
                       pltpu.SemaphoreType.DMA],
        name="sc_scatter_rows")
    def scatter(rows_hbm, pos_hbm, out_hbm, idx_v, rows_v, sem):
        base = _sc_worker_base(per_worker)

        @pl.loop(0, per_worker // SC_CHUNK)
        def _(c):
            off = pl.multiple_of(base + c * SC_CHUNK, SC_CHUNK)
            pltpu.sync_copy(rows_hbm.at[pl.ds(off, SC_CHUNK)], rows_v)
            for r in range(TOP_K):
                pltpu.sync_copy(pos_hbm.at[pl.ds(r * n + off, SC_CHUNK)], idx_v)
                pltpu.async_copy(rows_v, out_hbm.at[idx_v], sem).wait()

    return scatter(rows, pos)


def _position_kernel(idx_ref, rank_ref, start_ref, pos_ref):
    tl = idx_ref.shape[1]
    eid = lax.broadcasted_iota(jnp.int32, (N_EXPERTS, tl), 0)
    start = start_ref[...]
    for r in range(TOP_K):
        here = jnp.sum(jnp.where(eid == idx_ref[r:r + 1, :], start, 0.0), axis=0, keepdims=True)
        pos_ref[r:r + 1, :] = here.astype(jnp.int32) + rank_ref[r:r + 1, :]


def _positions(top_idx, rank, group_start):
    n = top_idx.shape[1]
    tl = POSITION_TILE
    col = lambda i: (0, i)
    return pl.pallas_call(
        _position_kernel,
        grid=(n // tl,),
        in_specs=[pl.BlockSpec((TOP_K, tl), col), pl.BlockSpec((TOP_K, tl), col),
                  pl.BlockSpec((N_EXPERTS, 1), lambda i: (0, 0))],
        out_specs=pl.BlockSpec((TOP_K, tl), col),
        out_shape=jax.ShapeDtypeStruct((TOP_K, n), jnp.int32),
        compiler_params=_params("parallel"),
        name="positions",
    )(top_idx, rank, group_start)


def _expert_kernel(te_ref, tv_ref, xs_ref, wg_ref, wu_ref, wd_ref, y_ref):
    valid = tv_ref[pl.program_id(0)]

    @pl.when(valid > 0)
    def _():
        live = lax.broadcasted_iota(jnp.int32, xs_ref.shape, 0) < valid
        lo, hi = _unpack_bf16_halves(jnp.where(live, xs_ref[...], jnp.uint32(0)))
        x = jnp.concatenate([lo, hi], axis=1).astype(BF16)
        g = jnp.dot(x, wg_ref[0], preferred_element_type=F32)
        u = jnp.dot(x, wu_ref[0], preferred_element_type=F32)
        a = (g * _sigmoid(g) * u).astype(BF16)
        y_ref[...] = _pack_bf16_halves(jnp.dot(a, wd_ref[0], preferred_element_type=F32))

    @pl.when(valid == 0)
    def _():
        y_ref[...] = jnp.zeros(y_ref.shape, jnp.uint32)


def _grouped_experts(tile_expert, tile_valid, xs, wg_b, wu_b, wd_b):
    p = xs.shape[0]
    t = EXPERT_TILE
    row = lambda i, te, tv: (i, 0)
    expert = lambda i, te, tv: (te[i], 0, 0)
    return pl.pallas_call(
        _expert_kernel,
        grid_spec=pltpu.PrefetchScalarGridSpec(
            num_scalar_prefetch=2,
            grid=(p // t,),
            in_specs=[
                pl.BlockSpec((t, HALF), row),
                pl.BlockSpec((1, D_MODEL, EXPERT_HIDDEN), expert),
                pl.BlockSpec((1, D_MODEL, EXPERT_HIDDEN), expert),
                pl.BlockSpec((1, EXPERT_HIDDEN, D_MODEL), expert),
            ],
            out_specs=pl.BlockSpec((t, HALF), row),
        ),
        out_shape=jax.ShapeDtypeStruct((p, HALF), jnp.uint32),
        compiler_params=_params("arbitrary"),
        name="grouped_experts",
    )(tile_expert, tile_valid, xs, wg_b, wu_b, wd_b)


def _combine_kernel(h_ref, yg_ref, wt_ref, wsg_ref, wsu_ref, wsd_ref, g2_ref, b2_ref, o_ref):
    h = h_ref[...]
    hb = h.astype(BF16)
    g = jnp.dot(hb, wsg_ref[...], preferred_element_type=F32)
    u = jnp.dot(hb, wsu_ref[...], preferred_element_type=F32)
    shared = jnp.dot((g * _sigmoid(g) * u).astype(BF16), wsd_ref[...], preferred_element_type=F32)
    wt = wt_ref[...]
    r_lo = jnp.zeros((h.shape[0], HALF), F32)
    r_hi = jnp.zeros((h.shape[0], HALF), F32)
    for r in range(TOP_K):
        lo, hi = _unpack_bf16_halves(yg_ref[r])
        w = wt[:, r:r + 1]
        r_lo = r_lo + lo * w
        r_hi = r_hi + hi * w
    routed = jnp.concatenate([r_lo, r_hi], axis=1)
    o_ref[...] = _layer_norm(DEEPNORM_ALPHA * h + (shared + routed), g2_ref[...], b2_ref[...])


def _combine_ln(h, yg, w_tok, wsg_b, wsu_b, wsd_b, ln_g, ln_b):
    n = h.shape[0]
    tm = TOKEN_TILE
    row = lambda i: (i, 0)
    const = lambda i: (0, 0)
    hidden = wsg_b.shape[1]
    return pl.pallas_call(
        _combine_kernel,
        grid=(n // tm,),
        in_specs=[
            pl.BlockSpec((tm, D_MODEL), row),
            pl.BlockSpec((TOP_K, tm, HALF), lambda i: (0, i, 0)),
            pl.BlockSpec((tm, TOP_K), row),
            pl.BlockSpec((D_MODEL, hidden), const),
            pl.BlockSpec((D_MODEL, hidden), const),
            pl.BlockSpec((hidden, D_MODEL), const),
            pl.BlockSpec((1, D_MODEL), const),
            pl.BlockSpec((1, D_MODEL), const),
        ],
        out_specs=pl.BlockSpec((tm, D_MODEL), row),
        out_shape=jax.ShapeDtypeStruct((n, D_MODEL), F32),
        compiler_params=_params("parallel"),
        name="combine_ln",
    )(h, yg, w_tok, wsg_b, wsu_b, wsd_b, ln_g, ln_b)


def _group_layout(totals, n_tokens):
    t = EXPERT_TILE
    n_tiles = (TOP_K * n_tokens + N_EXPERTS * (t - 1)) // t
    counts = totals[:, 0].astype(jnp.int32)
    padded = ((counts + t - 1) // t) * t
    group_end = jnp.cumsum(padded)
    group_start = group_end - padded
    tile_start = jnp.arange(n_tiles, dtype=jnp.int32) * t
    tile_expert = jnp.minimum(jnp.sum((group_end[None, :] <= tile_start[:, None]).astype(jnp.int32), axis=1),
                              N_EXPERTS - 1)
    of_tile = tile_expert[:, None] == jnp.arange(N_EXPERTS, dtype=jnp.int32)[None, :]
    real_end = jnp.sum(jnp.where(of_tile, (group_start + counts)[None, :], 0), axis=1)
    tile_valid = jnp.clip(real_end - tile_start, 0, t).astype(jnp.int32)
    return group_start.astype(F32).reshape(N_EXPERTS, 1), tile_expert, tile_valid, n_tiles * t


def _rope_tables(seq):
    inv_freq = ROPE_THETA ** (-jnp.arange(0, ROPE_DIM, 2, dtype=F32) / ROPE_DIM)
    ang = jnp.arange(seq).astype(F32)[:, None] * inv_freq[None, :]
    cos, sin = jnp.cos(ang), jnp.sin(ang)
    rest = HEAD_DIM - ROPE_DIM
    zeros = jnp.zeros((seq, ROPE_HALF), F32)
    cos_t = jnp.concatenate([cos, cos, jnp.ones((seq, rest), F32)], axis=1)
    sa_t = jnp.concatenate([-sin, zeros, jnp.zeros((seq, rest), F32)], axis=1)
    sb_t = jnp.concatenate([zeros, sin, jnp.zeros((seq, rest), F32)], axis=1)
    return cos_t, sa_t, sb_t


def _layer(x2, batch, seq, w_in, b_gate, w_attn_o, w_dw, b_dw, conv_ln_g, conv_ln_b, w_conv_o, w_out,
           ln1_g, ln1_b, w_router, router_bias, w_exp_gate, w_exp_up, w_exp_down,
           w_sh_gate, w_sh_up, w_sh_down, ln2_g, ln2_b):
    n = x2.shape[0]
    row = lambda v: v.reshape(1, -1)
    cos, sa, sb = _rope_tables(seq)
    q, k, v, u, gate_a, gate_c, k_mean = _in_projection(
        x2, w_in.astype(BF16), row(b_gate), cos, sa, sb, seq)
    conv_gated = _conv_branch(u, w_dw.reshape(CONV_WIDTH, D_MODEL), row(b_dw), row(conv_ln_g),
                              row(conv_ln_b), w_conv_o.astype(BF16), gate_c, batch, seq)
    attn = _moba_attention(q, k, v, k_mean, batch, seq)
    h, h_packed, top_idx, top_w, rank, totals = _merge_ln_router(
        attn, gate_a, conv_gated, x2, w_attn_o.astype(BF16), w_out.astype(BF16), row(ln1_g), row(ln1_b),
        w_router.T, router_bias.reshape(N_EXPERTS, 1))
    group_start, tile_expert, tile_valid, n_rows = _group_layout(totals, n)
    pos = _positions(top_idx, rank, group_start).reshape(TOP_K * n)
    xs = _sc_scatter_rows(h_packed, pos, n_rows)
    ys = _grouped_experts(tile_expert, tile_valid, xs, w_exp_gate.astype(BF16), w_exp_up.astype(BF16),
                          w_exp_down.astype(BF16))
    yg = _sc_gather_rows(ys, pos).reshape(TOP_K, n, HALF)
    return _combine_ln(h, yg, top_w.T, w_sh_gate.astype(BF16), w_sh_up.astype(BF16),
                       w_sh_down.astype(BF16), row(ln2_g), row(ln2_b))


def kernel(x, w_in, b_gate, w_attn_o, w_dw, b_dw, conv_ln_g, conv_ln_b, w_conv_o, w_out, ln1_g, ln1_b,
           w_router, router_bias, w_exp_gate, w_exp_up, w_exp_down, w_sh_gate, w_sh_up, w_sh_down,
           ln2_g, ln2_b):
    batch, seq, d = x.shape
    assert d == D_MODEL and seq % MOBA_BLOCK == 0 and seq % TOKEN_TILE == 0
    assert w_in.shape[0] == DEPTH
    x2 = x.reshape(batch * seq, d)
    for l in range(DEPTH):
        x2 = _layer(x2, batch, seq, w_in[l], b_gate[l], w_attn_o[l], w_dw[l], b_dw[l], conv_ln_g[l],
                    conv_ln_b[l], w_conv_o[l], w_out[l], ln1_g[l], ln1_b[l], w_router[l], router_bias[l],
                    w_exp_gate[l], w_exp_up[l], w_exp_down[l], w_sh_gate[l], w_sh_up[l], w_sh_down[l],
                    ln2_g[l], ln2_b[l])
    return x2.reshape(batch, seq, d)
```

```python
import functools

import jax
import jax.numpy as jnp
from jax import lax
from jax.experimental import pallas as pl
from jax.experimental.pallas import tpu as pltpu
from jax.experimental.pallas import tpu_sc as plsc

F32 = jnp.float32
BF16 = jnp.bfloat16
NEG_INF = float("-inf")
MASKED = -1e30
LOG2_E = 1.4426950408889634

D_MODEL = 1024
N_HEADS = 8
HEAD_DIM = 128
ROPE_THETA = 500000.0
ROPE_DIM = HEAD_DIM // 4
ROPE_HALF = ROPE_DIM // 2
MOBA_BLOCK = 256
MOBA_TOPK = 3
CONV_WIDTH = 31
CONV_HALO = 32
N_EXPERTS = 256
TOP_K = 8
N_GROUPS = 8
GROUP_SIZE = N_EXPERTS // N_GROUPS
TOPK_GROUPS = 4
EXPERT_HIDDEN = 256
ROUTED_SCALE = 2.5
LN_EPS = 1e-5
DEPTH = 1
DEEPNORM_ALPHA = (2 * DEPTH) ** 0.25
HALF = D_MODEL // 2

TOKEN_TILE = 256
EXPERT_TILE = 256
POSITION_TILE = 1024
SC_CORES = 2
SC_SUBCORES = 16
SC_CHUNK = 64
VMEM_LIMIT = 56 * 1024 * 1024


def _params(*semantics):
    return pltpu.CompilerParams(dimension_semantics=semantics, vmem_limit_bytes=VMEM_LIMIT)


def _sigmoid(x):
    return 1.0 / (1.0 + jnp.exp(-x))


def _layer_norm(x, g, b):
    mu = jnp.mean(x, axis=-1, keepdims=True)
    xc = x - mu
    var = jnp.mean(xc * xc, axis=-1, keepdims=True)
    return xc * lax.rsqrt(var + LN_EPS) * g + b


def _pack_bf16_halves(y):
    lo = lax.bitcast_convert_type(y[:, :HALF].astype(BF16).astype(F32), jnp.uint32)
    hi = lax.bitcast_convert_type(y[:, HALF:].astype(BF16).astype(F32), jnp.uint32)
    return (hi & jnp.uint32(0xFFFF0000)) | (lo >> 16)


def _unpack_bf16_halves(p):
    lo = lax.bitcast_convert_type(p << 16, F32)
    hi = lax.bitcast_convert_type(p & jnp.uint32(0xFFFF0000), F32)
    return lo, hi


def _inproj_kernel(x_ref, w_ref, bg_ref, cos_ref, sa_ref, sb_ref,
                   q_ref, k_ref, v_ref, u_ref, ga_ref, gc_ref, km_ref):
    tm = x_ref.shape[0]
    xb = x_ref[...].astype(BF16)

    def proj(c):
        return jnp.dot(xb, w_ref[:, c * D_MODEL:(c + 1) * D_MODEL], preferred_element_type=F32)

    cos = cos_ref[...]
    sa = sa_ref[...]
    sb = sb_ref[...]

    def rope_head(t):
        return (t * cos + pltpu.roll(t, HEAD_DIM - ROPE_HALF, 1) * sa
                + pltpu.roll(t, ROPE_HALF, 1) * sb)

    q = proj(0)
    for h in range(N_HEADS):
        sl = slice(h * HEAD_DIM, (h + 1) * HEAD_DIM)
        q_ref[:, sl] = rope_head(q[:, sl]).astype(BF16)
    k = proj(1)
    for h in range(N_HEADS):
        sl = slice(h * HEAD_DIM, (h + 1) * HEAD_DIM)
        kr = rope_head(k[:, sl])
        k_ref[:, sl] = kr.astype(BF16)
        for g in range(tm // MOBA_BLOCK):
            km_ref[g, :, sl] = jnp.mean(kr[g * MOBA_BLOCK:(g + 1) * MOBA_BLOCK], axis=0, keepdims=True)
    v_ref[...] = proj(2).astype(BF16)
    u_ref[...] = proj(3) * _sigmoid(proj(4))
    ga_ref[...] = _sigmoid(proj(5) + bg_ref[:, :D_MODEL]).astype(BF16)
    gc_ref[...] = _sigmoid(proj(6) + bg_ref[:, D_MODEL:]).astype(BF16)


def _in_projection(x2, w_in_b, b_gate, cos, sa, sb, seq):
    n = x2.shape[0]
    tm = TOKEN_TILE
    n_cols = w_in_b.shape[1]
    tiles_per_seq = seq // tm
    row = lambda i: (i, 0)
    const = lambda i: (0, 0)
    pos = lambda i: (i % tiles_per_seq, 0)
    tok_bf16 = jax.ShapeDtypeStruct((n, D_MODEL), BF16)
    return pl.pallas_call(
        _inproj_kernel,
        grid=(n // tm,),
        in_specs=[
            pl.BlockSpec((tm, D_MODEL), row),
            pl.BlockSpec((D_MODEL, n_cols), const),
            pl.BlockSpec((1, 2 * D_MODEL), const),
            pl.BlockSpec((tm, HEAD_DIM), pos),
            pl.BlockSpec((tm, HEAD_DIM), pos),
            pl.BlockSpec((tm, HEAD_DIM), pos),
        ],
        out_specs=[
            pl.BlockSpec((tm, D_MODEL), row),
            pl.BlockSpec((tm, D_MODEL), row),
            pl.BlockSpec((tm, D_MODEL), row),
            pl.BlockSpec((tm, D_MODEL), row),
            pl.BlockSpec((tm, D_MODEL), row),
            pl.BlockSpec((tm, D_MODEL), row),
            pl.BlockSpec((tm // MOBA_BLOCK, 1, D_MODEL), lambda i: (i, 0, 0)),
        ],
        out_shape=[tok_bf16, tok_bf16, tok_bf16,
                   jax.ShapeDtypeStruct((n, D_MODEL), F32),
                   tok_bf16, tok_bf16,
                   jax.ShapeDtypeStruct((n // MOBA_BLOCK, 1, D_MODEL), F32)],
        compiler_params=_params("parallel"),
        name="in_projection",
    )(x2, w_in_b, b_gate, cos, sa, sb)


def _conv_kernel(u_ref, wdw_ref, bdw_ref, lng_ref, lnb_ref, wo_ref, gc_ref, o_ref, buf_ref):
    ts = u_ref.shape[0]
    s = pl.program_id(1)

    @pl.when(s == 0)
    def _():
        buf_ref[0:CONV_HALO, :] = jnp.zeros((CONV_HALO, D_MODEL), F32)

    @pl.when(s > 0)
    def _():
        buf_ref[0:CONV_HALO, :] = buf_ref[ts:ts + CONV_HALO, :]

    buf_ref[CONV_HALO:CONV_HALO + ts, :] = u_ref[...]

    base = CONV_HALO - (CONV_WIDTH - 1)
    acc = jnp.zeros((ts, D_MODEL), F32)
    for j in range(CONV_WIDTH):
        acc = acc + buf_ref[base + j:base + j + ts, :] * wdw_ref[j:j + 1, :]
    acc = acc + bdw_ref[...]
    y = _layer_norm(acc, lng_ref[...], lnb_ref[...])
    y = y * _sigmoid(y)
    z = jnp.dot(y.astype(BF16), wo_ref[...], preferred_element_type=F32)
    o_ref[...] = (z * gc_ref[...].astype(F32)).astype(BF16)


def _conv_branch(u, w_dw, b_dw, ln_g, ln_b, w_o_b, gate_c, batch, seq):
    n = u.shape[0]
    ts = TOKEN_TILE
    tiles_per_seq = seq // ts
    row = lambda b, s: (b * tiles_per_seq + s, 0)
    const = lambda b, s: (0, 0)
    return pl.pallas_call(
        _conv_kernel,
        grid=(batch, tiles_per_seq),
        in_specs=[
            pl.BlockSpec((ts, D_MODEL), row),
            pl.BlockSpec((CONV_WIDTH, D_MODEL), const),
            pl.BlockSpec((1, D_MODEL), const),
            pl.BlockSpec((1, D_MODEL), const),
            pl.BlockSpec((1, D_MODEL), const),
            pl.BlockSpec((D_MODEL, D_MODEL), const),
            pl.BlockSpec((ts, D_MODEL), row),
        ],
        out_specs=pl.BlockSpec((ts, D_MODEL), row),
        out_shape=jax.ShapeDtypeStruct((n, D_MODEL), BF16),
        scratch_shapes=[pltpu.VMEM((ts + CONV_HALO, D_MODEL), F32)],
        compiler_params=_params("parallel", "arbitrary"),
        name="conv_branch",
    )(u, w_dw, b_dw, ln_g, ln_b, w_o_b, gate_c)


def _attn_kernel(q_ref, k_ref, v_ref, km_ref, o_ref, *, n_blk):
    blk = MOBA_BLOCK
    seq = n_blk * blk
    k_sel = min(MOBA_TOPK, n_blk)
    exp2_scale = HEAD_DIM ** -0.5 * LOG2_E
    nt_dims = (((1,), (1,)), ((), ()))
    n_lane = HEAD_DIM

    q_all = q_ref[...]
    km = jnp.concatenate([km_ref[:, 0, :], jnp.zeros((n_lane - n_blk, HEAD_DIM), F32)], axis=0)
    km_hi = km.astype(BF16)
    km_lo = (km - km_hi.astype(F32)).astype(BF16)

    gate = (lax.dot_general(q_all, km_hi, nt_dims, preferred_element_type=F32)
            + lax.dot_general(q_all, km_lo, nt_dims, preferred_element_type=F32))
    blk_id = lax.broadcasted_iota(jnp.int32, (seq, n_lane), 1)
    assert blk & (blk - 1) == 0
    own_id = lax.broadcasted_iota(jnp.int32, (seq, n_lane), 0) >> (blk.bit_length() - 1)
    past = blk_id < own_id
    blk_f = blk_id.astype(F32)
    g = jnp.where(past, gate, NEG_INF)
    sel = jnp.zeros((seq, n_lane), F32)
    for _ in range(k_sel):
        mx = jnp.max(g, axis=1, keepdims=True)
        first = jnp.min(jnp.where(g == mx, blk_f, float(n_lane)), axis=1, keepdims=True)
        pick = blk_f == first
        sel = jnp.where(pick, 1.0, sel)
        g = jnp.where(pick, NEG_INF, g)
    visible = ((sel > 0.0) & past) | (blk_id == own_id)
    q_bias = jnp.where(visible, 0.0, MASKED).astype(BF16)
    k_blk = jnp.where(blk_id == own_id, 1.0, 0.0).astype(BF16)
    k_aug = jnp.concatenate([k_ref[...], k_blk], axis=1)

    causal = (lax.broadcasted_iota(jnp.int32, (blk, blk), 1)
              <= lax.broadcasted_iota(jnp.int32, (blk, blk), 0))
    for i in range(n_blk):
        rows = slice(i * blk, (i + 1) * blk)
        width = (i + 1) * blk
        q_aug = jnp.concatenate([q_all[rows], q_bias[rows]], axis=1)
        raw = lax.dot_general(q_aug, k_aug[:width], nt_dims, preferred_element_type=F32)
        own = jnp.where(causal, raw[:, i * blk:], MASKED)
        parts = [raw[:, :i * blk], own] if i else [own]
        m = jnp.max(own, axis=1, keepdims=True)
        if i:
            m = jnp.maximum(m, jnp.max(parts[0], axis=1, keepdims=True))
        p = [jnp.exp2((t - m) * exp2_scale) for t in parts]
        l = sum(jnp.sum(t, axis=1, keepdims=True) for t in p)
        pv = jnp.dot(jnp.concatenate(p, axis=1).astype(BF16), v_ref[:width, :], preferred_element_type=F32)
        o_ref[rows, :] = (pv / l).astype(BF16)


def _moba_attention(q, k, v, k_mean, batch, seq):
    n = q.shape[0]
    n_blk = seq // MOBA_BLOCK
    seq_head = lambda b, h: (b, h)
    return pl.pallas_call(
        functools.partial(_attn_kernel, n_blk=n_blk),
        grid=(batch, N_HEADS),
        in_specs=[
            pl.BlockSpec((seq, HEAD_DIM), seq_head),
            pl.BlockSpec((seq, HEAD_DIM), seq_head),
            pl.BlockSpec((seq, HEAD_DIM), seq_head),
            pl.BlockSpec((n_blk, 1, HEAD_DIM), lambda b, h: (b, 0, h)),
        ],
        out_specs=pl.BlockSpec((seq, HEAD_DIM), seq_head),
        out_shape=jax.ShapeDtypeStruct((n, D_MODEL), BF16),
        compiler_params=_params("parallel", "parallel"),
        name="moba_attention",
    )(q, k, v, k_mean)


def _merge_kernel(attn_ref, ga_ref, cg_ref, x_ref, wao_ref, wout_ref, g1_ref, b1_ref,
                  wr_ref, rb_ref, h_ref, hp_ref, idx_ref, wgt_ref, rank_ref, total_ref, count_ref):
    tm = x_ref.shape[0]
    a = jnp.dot(attn_ref[...], wao_ref[...], preferred_element_type=F32)
    merged = ga_ref[...].astype(F32) * a + cg_ref[...].astype(F32)
    y = jnp.dot(merged.astype(BF16), wout_ref[...], preferred_element_type=F32)
    h = _layer_norm(DEEPNORM_ALPHA * x_ref[...] + y, g1_ref[...], b1_ref[...])
    h_ref[...] = h
    hp_ref[...] = _pack_bf16_halves(h)

    nt_dims = (((1,), (1,)), ((), ()))
    h_hi = h.astype(BF16)
    h_lo = (h - h_hi.astype(F32)).astype(BF16)
    w = wr_ref[...]
    w_hi = w.astype(BF16)
    w_lo = (w - w_hi.astype(F32)).astype(BF16)
    logits = (lax.dot_general(w_hi, h_hi, nt_dims, preferred_element_type=F32)
              + lax.dot_general(w_hi, h_lo, nt_dims, preferred_element_type=F32)
              + lax.dot_general(w_lo, h_hi, nt_dims, preferred_element_type=F32))
    scores = _sigmoid(logits)
    biased = scores + rb_ref[...]

    g3 = biased.reshape(N_GROUPS, GROUP_SIZE, tm)
    m1 = jnp.max(g3, axis=1, keepdims=True)
    is_max = g3 == m1
    n_max = jnp.sum(jnp.where(is_max, 1.0, 0.0), axis=1, keepdims=True)
    m2 = jnp.max(jnp.where(is_max, NEG_INF, g3), axis=1, keepdims=True)
    grp = (m1 + jnp.where(n_max >= 2.0, m1, m2)).reshape(N_GROUPS, tm)

    gid = lax.broadcasted_iota(jnp.int32, (N_GROUPS, tm), 0)
    rank = jnp.zeros((N_GROUPS, tm), F32)
    for o in range(N_GROUPS):
        other = grp[o:o + 1, :]
        ahead = (other > grp) | ((other == grp) & (o < gid))
        rank = rank + jnp.where(ahead, 1.0, 0.0)
    grp_keep = jnp.where(rank < float(TOPK_GROUPS), 1.0, 0.0)
    keep = jnp.broadcast_to(grp_keep.reshape(N_GROUPS, 1, tm),
                            (N_GROUPS, GROUP_SIZE, tm)).reshape(N_EXPERTS, tm)
    cand = jnp.where(keep > 0.0, biased, NEG_INF)

    eid = lax.broadcasted_iota(jnp.int32, (N_EXPERTS, tm), 0).astype(F32)
    chosen = jnp.zeros((N_EXPERTS, tm), F32)
    firsts = []
    for r in range(TOP_K):
        mx = jnp.max(cand, axis=0, keepdims=True)
        first = jnp.min(jnp.where(cand == mx, eid, float(N_EXPERTS)), axis=0, keepdims=True)
        pick = eid == first
        firsts.append(first)
        idx_ref[r:r + 1, :] = first.astype(jnp.int32)
        wgt_ref[r:r + 1, :] = jnp.sum(jnp.where(pick, scores, 0.0), axis=0, keepdims=True)
        cand = jnp.where(pick, NEG_INF, cand)
        chosen = jnp.where(pick, 1.0, chosen)
    top_s = wgt_ref[...]
    wgt_ref[...] = top_s / (jnp.sum(top_s, axis=0, keepdims=True) + 1e-20) * ROUTED_SCALE

    @pl.when(pl.program_id(0) == 0)
    def _():
        count_ref[...] = jnp.zeros((N_EXPERTS, 1), F32)

    earlier = (lax.broadcasted_iota(jnp.int32, (tm, tm), 0)
               < lax.broadcasted_iota(jnp.int32, (tm, tm), 1))
    before = jnp.dot(chosen.astype(BF16), jnp.where(earlier, 1.0, 0.0).astype(BF16),
                     preferred_element_type=F32) + count_ref[...]
    for r in range(TOP_K):
        rank_ref[r:r + 1, :] = jnp.sum(jnp.where(eid == firsts[r], before, 0.0),
                                       axis=0, keepdims=True).astype(jnp.int32)
    total = count_ref[...] + jnp.sum(chosen, axis=1, keepdims=True)
    count_ref[...] = total
    total_ref[...] = total


def _merge_ln_router(attn, gate_a, conv_gated, x2, w_attn_o_b, w_out_b, ln_g, ln_b, w_router_t, router_bias):
    n = x2.shape[0]
    tm = TOKEN_TILE
    row = lambda i: (i, 0)
    const = lambda i: (0, 0)
    col = lambda i: (0, i)
    return pl.pallas_call(
        _merge_kernel,
        grid=(n // tm,),
        in_specs=[
            pl.BlockSpec((tm, D_MODEL), row),
            pl.BlockSpec((tm, D_MODEL), row),
            pl.BlockSpec((tm, D_MODEL), row),
            pl.BlockSpec((tm, D_MODEL), row),
            pl.BlockSpec((D_MODEL, D_MODEL), const),
            pl.BlockSpec((D_MODEL, D_MODEL), const),
            pl.BlockSpec((1, D_MODEL), const),
            pl.BlockSpec((1, D_MODEL), const),
            pl.BlockSpec((N_EXPERTS, D_MODEL), const),
            pl.BlockSpec((N_EXPERTS, 1), const),
        ],
        out_specs=[
            pl.BlockSpec((tm, D_MODEL), row),
            pl.BlockSpec((tm, HALF), row),
            pl.BlockSpec((TOP_K, tm), col),
            pl.BlockSpec((TOP_K, tm), col),
            pl.BlockSpec((TOP_K, tm), col),
            pl.BlockSpec((N_EXPERTS, 1), const),
        ],
        out_shape=[
            jax.ShapeDtypeStruct((n, D_MODEL), F32),
            jax.ShapeDtypeStruct((n, HALF), jnp.uint32),
            jax.ShapeDtypeStruct((TOP_K, n), jnp.int32),
            jax.ShapeDtypeStruct((TOP_K, n), F32),
            jax.ShapeDtypeStruct((TOP_K, n), jnp.int32),
            jax.ShapeDtypeStruct((N_EXPERTS, 1), F32),
        ],
        scratch_shapes=[pltpu.VMEM((N_EXPERTS, 1), F32)],
        compiler_params=_params("arbitrary"),
        name="merge_ln_router",
    )(attn, gate_a, conv_gated, x2, w_attn_o_b, w_out_b, ln_g, ln_b, w_router_t, router_bias)


def _sc_mesh():
    return plsc.VectorSubcoreMesh(core_axis_name="c", subcore_axis_name="s",
                                  num_cores=SC_CORES, num_subcores=SC_SUBCORES)


def _sc_worker_base(rows_per_worker):
    return (lax.axis_index("s") * SC_CORES + lax.axis_index("c")) * rows_per_worker


def _sc_gather_rows(table, idx):
    m = idx.shape[0]
    width = table.shape[1]
    per_worker = m // (SC_CORES * SC_SUBCORES)
    assert per_worker * SC_CORES * SC_SUBCORES == m and per_worker % SC_CHUNK == 0

    @functools.partial(
        pl.kernel, mesh=_sc_mesh(),
        out_type=jax.ShapeDtypeStruct((m, width), table.dtype),
        scratch_types=[pltpu.VMEM((SC_CHUNK,), jnp.int32),
                       pltpu.VMEM((SC_CHUNK, width), table.dtype),
                       pltpu.SemaphoreType.DMA],
        name="sc_gather_rows")
    def gather(table_hbm, idx_hbm, out_hbm, idx_v, rows_v, sem):
        base = _sc_worker_base(per_worker)

        @pl.loop(0, per_worker // SC_CHUNK)
        def _(c):
            off = pl.multiple_of(base + c * SC_CHUNK, SC_CHUNK)
            pltpu.sync_copy(idx_hbm.at[pl.ds(off, SC_CHUNK)], idx_v)
            pltpu.async_copy(table_hbm.at[idx_v], rows_v, sem).wait()
            pltpu.sync_copy(rows_v, out_hbm.at[pl.ds(off, SC_CHUNK)])

    return gather(table, idx)


def _sc_scatter_rows(rows, pos, n_out):
    n, width = rows.shape
    per_worker = n // (SC_CORES * SC_SUBCORES)
    assert per_worker * SC_CORES * SC_SUBCORES == n and per_worker % SC_CHUNK == 0

    @functools.partial(
        pl.kernel, mesh=_sc_mesh(),
        out_type=jax.ShapeDtypeStruct((n_out, width), rows.dtype),
        scratch_types=[pltpu.VMEM((SC_CHUNK,), jnp.int32),
                       pltpu.VMEM((SC_CHUNK, width), rows.dtype),
                       pltpu.SemaphoreType.DMA],
        name="sc_scatter_rows")
    def scatter(rows_hbm, pos_hbm, out_hbm, idx_v, rows_v, sem):
        base = _sc_worker_base(per_worker)

        @pl.loop(0, per_worker // SC_CHUNK)
        def _(c):
            off = pl.multiple_of(base + c * SC_CHUNK, SC_CHUNK)
            pltpu.sync_copy(rows_hbm.at[pl.ds(off, SC_CHUNK)], rows_v)
            for r in range(TOP_K):
                pltpu.sync_copy(pos_hbm.at[pl.ds(r * n + off, SC_CHUNK)], idx_v)
                pltpu.async_copy(rows_v, out_hbm.at[idx_v], sem).wait()

    return scatter(rows, pos)


def _position_kernel(idx_ref, rank_ref, start_ref, pos_ref):
    tl = idx_ref.shape[1]
    eid = lax.broadcasted_iota(jnp.int32, (N_EXPERTS, tl), 0)
    start = start_ref[...]
    for r in range(TOP_K):
        here = jnp.sum(jnp.where(eid == idx_ref[r:r + 1, :], start, 0.0), axis=0, keepdims=True)
        pos_ref[r:r + 1, :] = here.astype(jnp.int32) + rank_ref[r:r + 1, :]


def _positions(top_idx, rank, group_start):
    n = top_idx.shape[1]
    tl = POSITION_TILE
    col = lambda i: (0, i)
    return pl.pallas_call(
        _position_kernel,
        grid=(n // tl,),
        in_specs=[pl.BlockSpec((TOP_K, tl), col), pl.BlockSpec((TOP_K, tl), col),
                  pl.BlockSpec((N_EXPERTS, 1), lambda i: (0, 0))],
        out_specs=pl.BlockSpec((TOP_K, tl), col),
        out_shape=jax.ShapeDtypeStruct((TOP_K, n), jnp.int32),
        compiler_params=_params("parallel"),
        name="positions",
    )(top_idx, rank, group_start)


def _expert_kernel(te_ref, tv_ref, xs_ref, wg_ref, wu_ref, wd_ref, y_ref):
    valid = tv_ref[pl.program_id(0)]

    @pl.when(valid > 0)
    def _():
        live = lax.broadcasted_iota(jnp.int32, xs_ref.shape, 0) < valid
        lo, hi = _unpack_bf16_halves(jnp.where(live, xs_ref[...], jnp.uint32(0)))
        x = jnp.concatenate([lo, hi], axis=1).astype(BF16)
        g = jnp.dot(x, wg_ref[0], preferred_element_type=F32)
        u = jnp.dot(x, wu_ref[0], preferred_element_type=F32)
        a = (g * _sigmoid(g) * u).astype(BF16)
        y_ref[...] = _pack_bf16_halves(jnp.dot(a, wd_ref[0], preferred_element_type=F32))

    @pl.when(valid == 0)
    def _():
        y_ref[...] = jnp.zeros(y_ref.shape, jnp.uint32)


def _grouped_experts(tile_expert, tile_valid, xs, wg_b, wu_b, wd_b):
    p = xs.shape[0]
    t = EXPERT_TILE
    row = lambda i, te, tv: (i, 0)
    expert = lambda i, te, tv: (te[i], 0, 0)
    return pl.pallas_call(
        _expert_kernel,
        grid_spec=pltpu.PrefetchScalarGridSpec(
            num_scalar_prefetch=2,
            grid=(p // t,),
            in_specs=[
                pl.BlockSpec((t, HALF), row),
                pl.BlockSpec((1, D_MODEL, EXPERT_HIDDEN), expert),
                pl.BlockSpec((1, D_MODEL, EXPERT_HIDDEN), expert),
                pl.BlockSpec((1, EXPERT_HIDDEN, D_MODEL), expert),
            ],
            out_specs=pl.BlockSpec((t, HALF), row),
        ),
        out_shape=jax.ShapeDtypeStruct((p, HALF), jnp.uint32),
        compiler_params=_params("arbitrary"),
        name="grouped_experts",
    )(tile_expert, tile_valid, xs, wg_b, wu_b, wd_b)


def _combine_kernel(h_ref, yg_ref, wt_ref, wsg_ref, wsu_ref, wsd_ref, g2_ref, b2_ref, o_ref):
    h = h_ref[...]
    hb = h.astype(BF16)
    g = jnp.dot(hb, wsg_ref[...], preferred_element_type=F32)
    u = jnp.dot(hb, wsu_ref[...], preferred_element_type=F32)
    shared = jnp.dot((g * _sigmoid(g) * u).astype(BF16), wsd_ref[...], preferred_element_type=F32)
    wt = wt_ref[...]
    r_lo = jnp.zeros((h.shape[0], HALF), F32)
    r_hi = jnp.zeros((h.shape[0], HALF), F32)
    for r in range(TOP_K):
        lo, hi = _unpack_bf16_halves(yg_ref[r])
        w = wt[:, r:r + 1]
        r_lo = r_lo + lo * w
        r_hi = r_hi + hi * w
    routed = jnp.concatenate([r_lo, r_hi], axis=1)
    o_ref[...] = _layer_norm(DEEPNORM_ALPHA * h + (shared + routed), g2_ref[...], b2_ref[...])


def _combine_ln(h, yg, w_tok, wsg_b, wsu_b, wsd_b, ln_g, ln_b):
    n = h.shape[0]
    tm = TOKEN_TILE
    row = lambda i: (i, 0)
    const = lambda i: (0, 0)
    hidden = wsg_b.shape[1]
    return pl.pallas_call(
        _combine_kernel,
        grid=(n // tm,),
        in_specs=[
            pl.BlockSpec((tm, D_MODEL), row),
            pl.BlockSpec((TOP_K, tm, HALF), lambda i: (0, i, 0)),
            pl.BlockSpec((tm, TOP_K), row),
            pl.BlockSpec((D_MODEL, hidden), const),
            pl.BlockSpec((D_MODEL, hidden), const),
            pl.BlockSpec((hidden, D_MODEL), const),
            pl.BlockSpec((1, D_MODEL), const),
            pl.BlockSpec((1, D_MODEL), const),
        ],
        out_specs=pl.BlockSpec((tm, D_MODEL), row),
        out_shape=jax.ShapeDtypeStruct((n, D_MODEL), F32),
        compiler_params=_params("parallel"),
        name="combine_ln",
    )(h, yg, w_tok, wsg_b, wsu_b, wsd_b, ln_g, ln_b)


def _group_layout(totals, n_tokens):
    t = EXPERT_TILE
    n_tiles = (TOP_K * n_tokens + N_EXPERTS * (t - 1)) // t
    counts = totals[:, 0].astype(jnp.int32)
    padded = ((counts + t - 1) // t) * t
    group_end = jnp.cumsum(padded)
    group_start = group_end - padded
    tile_start = jnp.arange(n_tiles, dtype=jnp.int32) * t
    tile_expert = jnp.minimum(jnp.sum((group_end[None, :] <= tile_start[:, None]).astype(jnp.int32), axis=1),
                              N_EXPERTS - 1)
    of_tile = tile_expert[:, None] == jnp.arange(N_EXPERTS, dtype=jnp.int32)[None, :]
    real_end = jnp.sum(jnp.where(of_tile, (group_start + counts)[None, :], 0), axis=1)
    tile_valid = jnp.clip(real_end - tile_start, 0, t).astype(jnp.int32)
    return group_start.astype(F32).reshape(N_EXPERTS, 1), tile_expert, tile_valid, n_tiles * t


def _rope_tables(seq):
    inv_freq = ROPE_THETA ** (-jnp.arange(0, ROPE_DIM, 2, dtype=F32) / ROPE_DIM)
    ang = jnp.arange(seq).astype(F32)[:, None] * inv_freq[None, :]
    cos, sin = jnp.cos(ang), jnp.sin(ang)
    rest = HEAD_DIM - ROPE_DIM
    zeros = jnp.zeros((seq, ROPE_HALF), F32)
    cos_t = jnp.concatenate([cos, cos, jnp.ones((seq, rest), F32)], axis=1)
    sa_t = jnp.concatenate([-sin, zeros, jnp.zeros((seq, rest), F32)], axis=1)
    sb_t = jnp.concatenate([zeros, sin, jnp.zeros((seq, rest), F32)], axis=1)
    return cos_t, sa_t, sb_t


def _layer(x2, batch, seq, w_in, b_gate, w_attn_o, w_dw, b_dw, conv_ln_g, conv_ln_b, w_conv_o, w_out,
           ln1_g, ln1_b, w_router, router_bias, w_exp_gate, w_exp_up, w_exp_down,
           w_sh_gate, w_sh_up, w_sh_down, ln2_g, ln2_b):
    n = x2.shape[0]
    row = lambda v: v.reshape(1, -1)
    cos, sa, sb = _rope_tables(seq)
    q, k, v, u, gate_a, gate_c, k_mean = _in_projection(
        x2, w_in.astype(BF16), row(b_gate), cos, sa, sb, seq)
    conv_gated = _conv_branch(u, w_dw.reshape(CONV_WIDTH, D_MODEL), row(b_dw), row(conv_ln_g),
                              row(conv_ln_b), w_conv_o.astype(BF16), gate_c, batch, seq)
    attn = _moba_attention(q, k, v, k_mean, batch, seq)
    h, h_packed, top_idx, top_w, rank, totals = _merge_ln_router(
        attn, gate_a, conv_gated, x2, w_attn_o.astype(BF16), w_out.astype(BF16), row(ln1_g), row(ln1_b),
        w_router.T, router_bias.reshape(N_EXPERTS, 1))
    group_start, tile_expert, tile_valid, n_rows = _group_layout(totals, n)
    pos = _positions(top_idx, rank, group_start).reshape(TOP_K * n)
    xs = _sc_scatter_rows(h_packed, pos, n_rows)
    ys = _grouped_experts(tile_expert, tile_valid, xs, w_exp_gate.astype(BF16), w_exp_up.astype(BF16),
                          w_exp_down.astype(BF16))
    yg = _sc_gather_rows(ys, pos).reshape(TOP_K, n, HALF)
    return _combine_ln(h, yg, top_w.T, w_sh_gate.astype(BF16), w_sh_up.astype(BF16),
                       w_sh_down.astype(BF16), row(ln2_g), row(ln2_b))


def kernel(x, w_in, b_gate, w_attn_o, w_dw, b_dw, conv_ln_g, conv_ln_b, w_conv_o, w_out, ln1_g, ln1_b,
           w_router, router_bias, w_exp_gate, w_exp_up, w_exp_down, w_sh_gate, w_sh_up, w_sh_down,
           ln2_g, ln2_b):
    batch, seq, d = x.shape
    assert d == D_MODEL and seq % MOBA_BLOCK == 0 and seq % TOKEN_TILE == 0
    assert w_in.shape[0] == DEPTH
    x2 = x.reshape(batch * seq, d)
    for l in range(DEPTH):
        x2 = _layer(x2, batch, seq, w_in[l], b_gate[l], w_attn_o[l], w_dw[l], b_dw[l], conv_ln_g[l],
                    conv_ln_b[l], w_conv_o[l], w_out[l], ln1_g[l], ln1_b[l], w_router[l], router_bias[l],
                    w_exp_gate[l], w_exp_up[l], w_exp_down[l], w_sh_gate[l], w_sh_up[l], w_sh_down[l],
                    ln2_g[l], ln2_b[l])
    return x2.reshape(batch, seq, d)
```

```python
import functools

import jax
import jax.numpy as jnp
from jax import lax
from jax.experimental import pallas as pl
from jax.experimental.pallas import tpu as pltpu
from jax.experimental.pallas import tpu_sc as plsc

F32 = jnp.float32
BF16 = jnp.bfloat16
NEG_INF = float("-inf")
MASKED = -1e30
LOG2_E = 1.4426950408889634

D_MODEL = 1024
N_HEADS = 8
HEAD_DIM = 128
ROPE_THETA = 500000.0
ROPE_DIM = HEAD_DIM // 4
ROPE_HALF = ROPE_DIM // 2
MOBA_BLOCK = 256
MOBA_TOPK = 3
CONV_WIDTH = 31
SUBLANES = 8
CONV_HALO = 32
CONV_ROWS = 32
N_EXPERTS = 256
TOP_K = 8
N_GROUPS = 8
GROUP_SIZE = N_EXPERTS // N_GROUPS
TOPK_GROUPS = 4
EXPERT_HIDDEN = 256
ROUTED_SCALE = 2.5
LN_EPS = 1e-5
DEPTH = 1
DEEPNORM_ALPHA = (2 * DEPTH) ** 0.25
HALF = D_MODEL // 2

TOKEN_TILE = 256
EXPERT_TILE = 256
TILES_PER_STEP = 2
POSITION_TILE = 1024
SC_CORES = 2
SC_SUBCORES = 16
SC_CHUNK = 64
VMEM_LIMIT = 56 * 1024 * 1024


def _params(*semantics):
    return pltpu.CompilerParams(dimension_semantics=semantics, vmem_limit_bytes=VMEM_LIMIT)


def _sigmoid(x):
    return 1.0 / (1.0 + jnp.exp(-x))


def _layer_norm(x, g, b):
    mu = jnp.mean(x, axis=-1, keepdims=True)
    xc = x - mu
    var = jnp.mean(xc * xc, axis=-1, keepdims=True)
    return xc * lax.rsqrt(var + LN_EPS) * g + b


def _pack_bf16_halves(y):
    lo = lax.bitcast_convert_type(y[:, :HALF].astype(BF16).astype(F32), jnp.uint32)
    hi = lax.bitcast_convert_type(y[:, HALF:].astype(BF16).astype(F32), jnp.uint32)
    return (hi & jnp.uint32(0xFFFF0000)) | (lo >> 16)


def _unpack_bf16_halves(p):
    lo = lax.bitcast_convert_type(p << 16, F32)
    hi = lax.bitcast_convert_type(p & jnp.uint32(0xFFFF0000), F32)
    return lo, hi


def _inproj_kernel(x_ref, w_ref, bg_ref, cos_ref, sa_ref, sb_ref,
                   q_ref, k_ref, v_ref, u_ref, ga_ref, gc_ref, km_ref):
    tm = x_ref.shape[0]
    xb = x_ref[...].astype(BF16)

    def proj(c):
        return jnp.dot(xb, w_ref[:, c * D_MODEL:(c + 1) * D_MODEL], preferred_element_type=F32)

    cos = cos_ref[...]
    sa = sa_ref[...]
    sb = sb_ref[...]

    def rope_head(t):
        return (t * cos + pltpu.roll(t, HEAD_DIM - ROPE_HALF, 1) * sa
                + pltpu.roll(t, ROPE_HALF, 1) * sb)

    q = proj(0)
    for h in range(N_HEADS):
        sl = slice(h * HEAD_DIM, (h + 1) * HEAD_DIM)
        q_ref[:, sl] = rope_head(q[:, sl]).astype(BF16)
    k = proj(1)
    for h in range(N_HEADS):
        sl = slice(h * HEAD_DIM, (h + 1) * HEAD_DIM)
        kr = rope_head(k[:, sl])
        k_ref[:, sl] = kr.astype(BF16)
        for g in range(tm // MOBA_BLOCK):
            km_ref[g, :, sl] = jnp.mean(kr[g * MOBA_BLOCK:(g + 1) * MOBA_BLOCK], axis=0, keepdims=True)
    v_ref[...] = proj(2).astype(BF16)
    u_ref[...] = proj(3) * _sigmoid(proj(4))
    ga_ref[...] = _sigmoid(proj(5) + bg_ref[:, :D_MODEL]).astype(BF16)
    gc_ref[...] = _sigmoid(proj(6) + bg_ref[:, D_MODEL:]).astype(BF16)


def _in_projection(x2, w_in_b, b_gate, cos, sa, sb, seq):
    n = x2.shape[0]
    tm = TOKEN_TILE
    n_cols = w_in_b.shape[1]
    tiles_per_seq = seq // tm
    row = lambda i: (i, 0)
    const = lambda i: (0, 0)
    pos = lambda i: (i % tiles_per_seq, 0)
    tok_bf16 = jax.ShapeDtypeStruct((n, D_MODEL), BF16)
    return pl.pallas_call(
        _inproj_kernel,
        grid=(n // tm,),
        in_specs=[
            pl.BlockSpec((tm, D_MODEL), row),
            pl.BlockSpec((D_MODEL, n_cols), const),
            pl.BlockSpec((1, 2 * D_MODEL), const),
            pl.BlockSpec((tm, HEAD_DIM), pos),
            pl.BlockSpec((tm, HEAD_DIM), pos),
            pl.BlockSpec((tm, HEAD_DIM), pos),
        ],
        out_specs=[
            pl.BlockSpec((tm, D_MODEL), row),
            pl.BlockSpec((tm, D_MODEL), row),
            pl.BlockSpec((tm, D_MODEL), row),
            pl.BlockSpec((tm, D_MODEL), row),
            pl.BlockSpec((tm, D_MODEL), row),
            pl.BlockSpec((tm, D_MODEL), row),
            pl.BlockSpec((tm // MOBA_BLOCK, 1, D_MODEL), lambda i: (i, 0, 0)),
        ],
        out_shape=[tok_bf16, tok_bf16, tok_bf16,
                   jax.ShapeDtypeStruct((n, D_MODEL), F32),
                   tok_bf16, tok_bf16,
                   jax.ShapeDtypeStruct((n // MOBA_BLOCK, 1, D_MODEL), F32)],
        compiler_params=_params("parallel"),
        name="in_projection",
    )(x2, w_in_b, b_gate, cos, sa, sb)


def _conv_kernel(u_ref, wdw_ref, bdw_ref, lng_ref, lnb_ref, wo_ref, gc_ref, o_ref, buf_ref, sh_ref, y_ref):
    ts = u_ref.shape[0]
    s = pl.program_id(1)

    @pl.when(s == 0)
    def _():
        buf_ref[0:CONV_HALO, :] = jnp.zeros((CONV_HALO, D_MODEL), F32)

    @pl.when(s > 0)
    def _():
        buf_ref[0:CONV_HALO, :] = buf_ref[ts:ts + CONV_HALO, :]

    buf_ref[CONV_HALO:CONV_HALO + ts, :] = u_ref[...]

    span = ts + CONV_HALO - SUBLANES
    for b in range(1, SUBLANES):
        sh_ref[b - 1, 0:span, :] = buf_ref[b:b + span, :]

    base = CONV_HALO - (CONV_WIDTH - 1)
    for c in range(ts // CONV_ROWS):
        r0 = c * CONV_ROWS
        acc = jnp.zeros((CONV_ROWS // SUBLANES, SUBLANES, D_MODEL), F32)
        for j in range(CONV_WIDTH):
            shift = (base + j) % SUBLANES
            row = r0 + base + j - shift
            src = buf_ref if shift == 0 else sh_ref.at[shift - 1]
            tap = src[row:row + CONV_ROWS, :].reshape(CONV_ROWS // SUBLANES, SUBLANES, D_MODEL)
            acc = acc + tap * wdw_ref[j]
        acc = acc.reshape(CONV_ROWS, D_MODEL)
        y = _layer_norm(acc + bdw_ref[...], lng_ref[...], lnb_ref[...])
        y_ref[r0:r0 + CONV_ROWS, :] = (y * _sigmoid(y)).astype(BF16)
    z = jnp.dot(y_ref[...], wo_ref[...], preferred_element_type=F32)
    o_ref[...] = (z * gc_ref[...].astype(F32)).astype(BF16)


def _conv_branch(u, w_dw, b_dw, ln_g, ln_b, w_o_b, gate_c, batch, seq):
    n = u.shape[0]
    ts = TOKEN_TILE
    tiles_per_seq = seq // ts
    row = lambda b, s: (b * tiles_per_seq + s, 0)
    const = lambda b, s: (0, 0)
    return pl.pallas_call(
        _conv_kernel,
        grid=(batch, tiles_per_seq),
        in_specs=[
            pl.BlockSpec((ts, D_MODEL), row),
            pl.BlockSpec((CONV_WIDTH, SUBLANES, D_MODEL), lambda b, s: (0, 0, 0)),
            pl.BlockSpec((1, D_MODEL), const),
            pl.BlockSpec((1, D_MODEL), const),
            pl.BlockSpec((1, D_MODEL), const),
            pl.BlockSpec((D_MODEL, D_MODEL), const),
            pl.BlockSpec((ts, D_MODEL), row),
        ],
        out_specs=pl.BlockSpec((ts, D_MODEL), row),
        out_shape=jax.ShapeDtypeStruct((n, D_MODEL), BF16),
        scratch_shapes=[pltpu.VMEM((ts + CONV_HALO, D_MODEL), F32),
                        pltpu.VMEM((SUBLANES - 1, ts + CONV_HALO - SUBLANES, D_MODEL), F32),
                        pltpu.VMEM((ts, D_MODEL), BF16)],
        compiler_params=_params("parallel", "arbitrary"),
        name="conv_branch",
    )(u, w_dw, b_dw, ln_g, ln_b, w_o_b, gate_c)


def _attn_kernel(q_ref, k_ref, v_ref, km_ref, o_ref, *, n_blk):
    blk = MOBA_BLOCK
    seq = n_blk * blk
    k_sel = min(MOBA_TOPK, n_blk)
    exp2_scale = HEAD_DIM ** -0.5 * LOG2_E
    nt_dims = (((1,), (1,)), ((), ()))
    n_lane = HEAD_DIM

    q_all = q_ref[...]
    km = jnp.concatenate([km_ref[:, 0, :], jnp.zeros((n_lane - n_blk, HEAD_DIM), F32)], axis=0)
    km_hi = km.astype(BF16)
    km_lo = (km - km_hi.astype(F32)).astype(BF16)

    gate = (lax.dot_general(q_all, km_hi, nt_dims, preferred_element_type=F32)
            + lax.dot_general(q_all, km_lo, nt_dims, preferred_element_type=F32))
    blk_id = lax.broadcasted_iota(jnp.int32, (seq, n_lane), 1)
    assert blk & (blk - 1) == 0
    own_id = lax.broadcasted_iota(jnp.int32, (seq, n_lane), 0) >> (blk.bit_length() - 1)
    past = blk_id < own_id
    blk_f = blk_id.astype(F32)
    g = jnp.where(past, gate, NEG_INF)
    sel = jnp.zeros((seq, n_lane), F32)
    for _ in range(k_sel):
        mx = jnp.max(g, axis=1, keepdims=True)
        first = jnp.min(jnp.where(g == mx, blk_f, float(n_lane)), axis=1, keepdims=True)
        pick = blk_f == first
        sel = jnp.where(pick, 1.0, sel)
        g = jnp.where(pick, NEG_INF, g)
    visible = ((sel > 0.0) & past) | (blk_id == own_id)
    q_bias = jnp.where(visible, 0.0, MASKED).astype(BF16)
    k_blk = jnp.where(blk_id == own_id, 1.0, 0.0).astype(BF16)
    k_aug = jnp.concatenate([k_ref[...], k_blk], axis=1)

    causal = (lax.broadcasted_iota(jnp.int32, (blk, blk), 1)
              <= lax.broadcasted_iota(jnp.int32, (blk, blk), 0))
    for i in range(n_blk):
        rows = slice(i * blk, (i + 1) * blk)
        width = (i + 1) * blk
        q_aug = jnp.concatenate([q_all[rows], q_bias[rows]], axis=1)
        raw = lax.dot_general(q_aug, k_aug[:width], nt_dims, preferred_element_type=F32)
        own = jnp.where(causal, raw[:, i * blk:], MASKED)
        parts = [raw[:, :i * blk], own] if i else [own]
        m = jnp.max(own, axis=1, keepdims=True)
        if i:
            m = jnp.maximum(m, jnp.max(parts[0], axis=1, keepdims=True))
        p = [jnp.exp2((t - m) * exp2_scale) for t in parts]
        l = sum(jnp.sum(t, axis=1, keepdims=True) for t in p)
        pv = jnp.dot(jnp.concatenate(p, axis=1).astype(BF16), v_ref[:width, :], preferred_element_type=F32)
        o_ref[rows, :] = (pv / l).astype(BF16)


def _moba_attention(q, k, v, k_mean, batch, seq):
    n = q.shape[0]
    n_blk = seq // MOBA_BLOCK
    seq_head = lambda b, h: (b, h)
    return pl.pallas_call(
        functools.partial(_attn_kernel, n_blk=n_blk),
        grid=(batch, N_HEADS),
        in_specs=[
            pl.BlockSpec((seq, HEAD_DIM), seq_head),
            pl.BlockSpec((seq, HEAD_DIM), seq_head),
            pl.BlockSpec((seq, HEAD_DIM), seq_head),
            pl.BlockSpec((n_blk, 1, HEAD_DIM), lambda b, h: (b, 0, h)),
        ],
        out_specs=pl.BlockSpec((seq, HEAD_DIM), seq_head),
        out_shape=jax.ShapeDtypeStruct((n, D_MODEL), BF16),
        compiler_params=_params("parallel", "parallel"),
        name="moba_attention",
    )(q, k, v, k_mean)


def _merge_kernel(attn_ref, ga_ref, cg_ref, x_ref, wao_ref, wout_ref, g1_ref, b1_ref,
                  wr_ref, rb_ref, h_ref, hp_ref, idx_ref, wgt_ref, rank_ref, total_ref, count_ref):
    tm = x_ref.shape[0]
    a = jnp.dot(attn_ref[...], wao_ref[...], preferred_element_type=F32)
    merged = ga_ref[...].astype(F32) * a + cg_ref[...].astype(F32)
    y = jnp.dot(merged.astype(BF16), wout_ref[...], preferred_element_type=F32)
    h = _layer_norm(DEEPNORM_ALPHA * x_ref[...] + y, g1_ref[...], b1_ref[...])
    h_ref[...] = h
    hp_ref[...] = _pack_bf16_halves(h)

    nt_dims = (((1,), (1,)), ((), ()))
    h_hi = h.astype(BF16)
    h_lo = (h - h_hi.astype(F32)).astype(BF16)
    w = wr_ref[...]
    w_hi = w.astype(BF16)
    w_lo = (w - w_hi.astype(F32)).astype(BF16)
    logits = (lax.dot_general(w_hi, h_hi, nt_dims, preferred_element_type=F32)
              + lax.dot_general(w_hi, h_lo, nt_dims, preferred_element_type=F32)
              + lax.dot_general(w_lo, h_hi, nt_dims, preferred_element_type=F32))
    scores = _sigmoid(logits)
    biased = scores + rb_ref[...]

    g3 = biased.reshape(N_GROUPS, GROUP_SIZE, tm)
    m1 = jnp.max(g3, axis=1, keepdims=True)
    is_max = g3 == m1
    n_max = jnp.sum(jnp.where(is_max, 1.0, 0.0), axis=1, keepdims=True)
    m2 = jnp.max(jnp.where(is_max, NEG_INF, g3), axis=1, keepdims=True)
    grp = (m1 + jnp.where(n_max >= 2.0, m1, m2)).reshape(N_GROUPS, tm)

    gid = lax.broadcasted_iota(jnp.int32, (N_GROUPS, tm), 0)
    rank = jnp.zeros((N_GROUPS, tm), F32)
    for o in range(N_GROUPS):
        other = grp[o:o + 1, :]
        ahead = (other > grp) | ((other == grp) & (o < gid))
        rank = rank + jnp.where(ahead, 1.0, 0.0)
    grp_keep = jnp.where(rank < float(TOPK_GROUPS), 1.0, 0.0)
    keep = jnp.broadcast_to(grp_keep.reshape(N_GROUPS, 1, tm),
                            (N_GROUPS, GROUP_SIZE, tm)).reshape(N_EXPERTS, tm)
    cand = jnp.where(keep > 0.0, biased, NEG_INF)

    eid = lax.broadcasted_iota(jnp.int32, (N_EXPERTS, tm), 0).astype(F32)
    chosen = jnp.zeros((N_EXPERTS, tm), F32)
    firsts = []
    for r in range(TOP_K):
        mx = jnp.max(cand, axis=0, keepdims=True)
        first = jnp.min(jnp.where(cand == mx, eid, float(N_EXPERTS)), axis=0, keepdims=True)
        pick = eid == first
        firsts.append(first)
        idx_ref[r:r + 1, :] = first.astype(jnp.int32)
        wgt_ref[r:r + 1, :] = jnp.sum(jnp.where(pick, scores, 0.0), axis=0, keepdims=True)
        cand = jnp.where(pick, NEG_INF, cand)
        chosen = jnp.where(pick, 1.0, chosen)
    top_s = wgt_ref[...]
    wgt_ref[...] = top_s / (jnp.sum(top_s, axis=0, keepdims=True) + 1e-20) * ROUTED_SCALE

    @pl.when(pl.program_id(0) == 0)
    def _():
        count_ref[...] = jnp.zeros((N_EXPERTS, 1), F32)

    earlier = (lax.broadcasted_iota(jnp.int32, (tm, tm), 0)
               < lax.broadcasted_iota(jnp.int32, (tm, tm), 1))
    before = jnp.dot(chosen.astype(BF16), jnp.where(earlier, 1.0, 0.0).astype(BF16),
                     preferred_element_type=F32) + count_ref[...]
    for r in range(TOP_K):
        rank_ref[r:r + 1, :] = jnp.sum(jnp.where(eid == firsts[r], before, 0.0),
                                       axis=0, keepdims=True).astype(jnp.int32)
    total = count_ref[...] + jnp.sum(chosen, axis=1, keepdims=True)
    count_ref[...] = total
    total_ref[...] = total


def _merge_ln_router(attn, gate_a, conv_gated, x2, w_attn_o_b, w_out_b, ln_g, ln_b, w_router_t, router_bias):
    n = x2.shape[0]
    tm = TOKEN_TILE
    row = lambda i: (i, 0)
    const = lambda i: (0, 0)
    col = lambda i: (0, i)
    return pl.pallas_call(
        _merge_kernel,
        grid=(n // tm,),
        in_specs=[
            pl.BlockSpec((tm, D_MODEL), row),
            pl.BlockSpec((tm, D_MODEL), row),
            pl.BlockSpec((tm, D_MODEL), row),
            pl.BlockSpec((tm, D_MODEL), row),
            pl.BlockSpec((D_MODEL, D_MODEL), const),
            pl.BlockSpec((D_MODEL, D_MODEL), const),
            pl.BlockSpec((1, D_MODEL), const),
            pl.BlockSpec((1, D_MODEL), const),
            pl.BlockSpec((N_EXPERTS, D_MODEL), const),
            pl.BlockSpec((N_EXPERTS, 1), const),
        ],
        out_specs=[
            pl.BlockSpec((tm, D_MODEL), row),
            pl.BlockSpec((tm, HALF), row),
            pl.BlockSpec((TOP_K, tm), col),
            pl.BlockSpec((TOP_K, tm), col),
            pl.BlockSpec((TOP_K, tm), col),
            pl.BlockSpec((N_EXPERTS, 1), const),
        ],
        out_shape=[
            jax.ShapeDtypeStruct((n, D_MODEL), F32),
            jax.ShapeDtypeStruct((n, HALF), jnp.uint32),
            jax.ShapeDtypeStruct((TOP_K, n), jnp.int32),
            jax.ShapeDtypeStruct((TOP_K, n), F32),
            jax.ShapeDtypeStruct((TOP_K, n), jnp.int32),
            jax.ShapeDtypeStruct((N_EXPERTS, 1), F32),
        ],
        scratch_shapes=[pltpu.VMEM((N_EXPERTS, 1), F32)],
        compiler_params=_params("arbitrary"),
        name="merge_ln_router",
    )(attn, gate_a, conv_gated, x2, w_attn_o_b, w_out_b, ln_g, ln_b, w_router_t, router_bias)


def _sc_mesh():
    return plsc.VectorSubcoreMesh(core_axis_name="c", subcore_axis_name="s",
                                  num_cores=SC_CORES, num_subcores=SC_SUBCORES)


def _sc_worker_base(rows_per_worker):
    return (lax.axis_index("s") * SC_CORES + lax.axis_index("c")) * rows_per_worker


def _sc_gather_rows(table, idx):
    m = idx.shape[0]
    width = table.shape[1]
    per_worker = m // (SC_CORES * SC_SUBCORES)
    assert per_worker * SC_CORES * SC_SUBCORES == m and per_worker % SC_CHUNK == 0

    @functools.partial(
        pl.kernel, mesh=_sc_mesh(),
        out_type=jax.ShapeDtypeStruct((m, width), table.dtype),
        scratch_types=[pltpu.VMEM((SC_CHUNK,), jnp.int32),
                       pltpu.VMEM((SC_CHUNK, width), table.dtype),
                       pltpu.SemaphoreType.DMA],
        name="sc_gather_rows")
    def gather(table_hbm, idx_hbm, out_hbm, idx_v, rows_v, sem):
        base = _sc_worker_base(per_worker)

        @pl.loop(0, per_worker // SC_CHUNK)
        def _(c):
            off = pl.multiple_of(base + c * SC_CHUNK, SC_CHUNK)
            pltpu.sync_copy(idx_hbm.at[pl.ds(off, SC_CHUNK)], idx_v)
            pltpu.async_copy(table_hbm.at[idx_v], rows_v, sem).wait()
            pltpu.sync_copy(rows_v, out_hbm.at[pl.ds(off, SC_CHUNK)])

    return gather(table, idx)


def _sc_scatter_rows(rows, pos, n_out):
    n, width = rows.shape
    per_worker = n // (SC_CORES * SC_SUBCORES)
    assert per_worker * SC_CORES * SC_SUBCORES == n and per_worker % SC_CHUNK == 0

    @functools.partial(
        pl.kernel, mesh=_sc_mesh(),
        out_type=jax.ShapeDtypeStruct((n_out, width), rows.dtype),
        scratch_types=[pltpu.VMEM((SC_CHUNK,), jnp.int32),
                       pltpu.VMEM((SC_CHUNK, width), rows.dtype),
                       pltpu.SemaphoreType.DMA],
        name="sc_scatter_rows")
    def scatter(rows_hbm, pos_hbm, out_hbm, idx_v, rows_v, sem):
        base = _sc_worker_base(per_worker)

        @pl.loop(0, per_worker // SC_CHUNK)
        def _(c):
            off = pl.multiple_of(base + c * SC_CHUNK, SC_CHUNK)
            pltpu.sync_copy(rows_hbm.at[pl.ds(off, SC_CHUNK)], rows_v)
            for r in range(TOP_K):
                pltpu.sync_copy(pos_hbm.at[pl.ds(r * n + off, SC_CHUNK)], idx_v)
                pltpu.async_copy(rows_v, out_hbm.at[idx_v], sem).wait()

    return scatter(rows, pos)


def _position_kernel(idx_ref, rank_ref, start_ref, pos_ref):
    tl = idx_ref.shape[1]
    eid = lax.broadcasted_iota(jnp.int32, (N_EXPERTS, tl), 0)
    start = start_ref[...]
    for r in range(TOP_K):
        here = jnp.sum(jnp.where(eid == idx_ref[r:r + 1, :], start, 0.0), axis=0, keepdims=True)
        pos_ref[r:r + 1, :] = here.astype(jnp.int32) + rank_ref[r:r + 1, :]


def _positions(top_idx, rank, group_start):
    n = top_idx.shape[1]
    tl = POSITION_TILE
    col = lambda i: (0, i)
    return pl.pallas_call(
        _position_kernel,
        grid=(n // tl,),
        in_specs=[pl.BlockSpec((TOP_K, tl), col), pl.BlockSpec((TOP_K, tl), col),
                  pl.BlockSpec((N_EXPERTS, 1), lambda i: (0, 0))],
        out_specs=pl.BlockSpec((TOP_K, tl), col),
        out_shape=jax.ShapeDtypeStruct((TOP_K, n), jnp.int32),
        compiler_params=_params("parallel"),
        name="positions",
    )(top_idx, rank, group_start)


def _expert_kernel(te_ref, tv_ref, xs_ref, *refs):
    y_ref = refs[-1]
    t = EXPERT_TILE
    first = pl.program_id(0) * TILES_PER_STEP
    valid = [tv_ref[first + k] for k in range(TILES_PER_STEP)]

    @pl.when(valid[0] > 0)
    def _():
        for k in range(TILES_PER_STEP):
            wg_ref, wu_ref, wd_ref = refs[3 * k:3 * k + 3]
            rows = slice(k * t, (k + 1) * t)
            live = lax.broadcasted_iota(jnp.int32, (t, HALF), 0) < valid[k]
            lo, hi = _unpack_bf16_halves(jnp.where(live, xs_ref[rows, :], jnp.uint32(0)))
            x = jnp.concatenate([lo, hi], axis=1).astype(BF16)
            g = jnp.dot(x, wg_ref[0].astype(BF16), preferred_element_type=F32)
            u = jnp.dot(x, wu_ref[0].astype(BF16), preferred_element_type=F32)
            a = (g * _sigmoid(g) * u).astype(BF16)
            y_ref[rows, :] = _pack_bf16_halves(jnp.dot(a, wd_ref[0].astype(BF16), preferred_element_type=F32))

    @pl.when(valid[0] == 0)
    def _():
        y_ref[...] = jnp.zeros(y_ref.shape, jnp.uint32)


def _grouped_experts(tile_expert, tile_valid, xs, wg, wu, wd):
    p = xs.shape[0]
    t = EXPERT_TILE * TILES_PER_STEP
    row = lambda i, te, tv: (i, 0)
    weight_specs = []
    for k in range(TILES_PER_STEP):
        expert = lambda i, te, tv, k=k: (te[i * TILES_PER_STEP + k], 0, 0)
        weight_specs += [pl.BlockSpec((1, D_MODEL, EXPERT_HIDDEN), expert),
                         pl.BlockSpec((1, D_MODEL, EXPERT_HIDDEN), expert),
                         pl.BlockSpec((1, EXPERT_HIDDEN, D_MODEL), expert)]
    return pl.pallas_call(
        _expert_kernel,
        grid_spec=pltpu.PrefetchScalarGridSpec(
            num_scalar_prefetch=2,
            grid=(p // t,),
            in_specs=[pl.BlockSpec((t, HALF), row)] + weight_specs,
            out_specs=pl.BlockSpec((t, HALF), row),
        ),
        out_shape=jax.ShapeDtypeStruct((p, HALF), jnp.uint32),
        compiler_params=_params("arbitrary"),
        name="grouped_experts",
    )(tile_expert, tile_valid, xs, *([wg, wu, wd] * TILES_PER_STEP))


def _combine_kernel(h_ref, yg_ref, wt_ref, wsg_ref, wsu_ref, wsd_ref, g2_ref, b2_ref, o_ref):
    h = h_ref[...]
    hb = h.astype(BF16)
    g = jnp.dot(hb, wsg_ref[...], preferred_element_type=F32)
    u = jnp.dot(hb, wsu_ref[...], preferred_element_type=F32)
    shared = jnp.dot((g * _sigmoid(g) * u).astype(BF16), wsd_ref[...], preferred_element_type=F32)
    wt = wt_ref[...]
    r_lo = jnp.zeros((h.shape[0], HALF), F32)
    r_hi = jnp.zeros((h.shape[0], HALF), F32)
    for r in range(TOP_K):
        lo, hi = _unpack_bf16_halves(yg_ref[r])
        w = wt[:, r:r + 1]
        r_lo = r_lo + lo * w
        r_hi = r_hi + hi * w
    routed = jnp.concatenate([r_lo, r_hi], axis=1)
    o_ref[...] = _layer_norm(DEEPNORM_ALPHA * h + (shared + routed), g2_ref[...], b2_ref[...])


def _combine_ln(h, yg, w_tok, wsg_b, wsu_b, wsd_b, ln_g, ln_b):
    n = h.shape[0]
    tm = TOKEN_TILE
    row = lambda i: (i, 0)
    const = lambda i: (0, 0)
    hidden = wsg_b.shape[1]
    return pl.pallas_call(
        _combine_kernel,
        grid=(n // tm,),
        in_specs=[
            pl.BlockSpec((tm, D_MODEL), row),
            pl.BlockSpec((TOP_K, tm, HALF), lambda i: (0, i, 0)),
            pl.BlockSpec((tm, TOP_K), row),
            pl.BlockSpec((D_MODEL, hidden), const),
            pl.BlockSpec((D_MODEL, hidden), const),
            pl.BlockSpec((hidden, D_MODEL), const),
            pl.BlockSpec((1, D_MODEL), const),
            pl.BlockSpec((1, D_MODEL), const),
        ],
        out_specs=pl.BlockSpec((tm, D_MODEL), row),
        out_shape=jax.ShapeDtypeStruct((n, D_MODEL), F32),
        compiler_params=_params("parallel"),
        name="combine_ln",
    )(h, yg, w_tok, wsg_b, wsu_b, wsd_b, ln_g, ln_b)


def _group_layout(totals, n_tokens):
    t = EXPERT_TILE
    n_tiles = (TOP_K * n_tokens + N_EXPERTS * (t - 1)) // t
    n_tiles = -(-n_tiles // TILES_PER_STEP) * TILES_PER_STEP
    counts = totals[:, 0].astype(jnp.int32)
    padded = ((counts + t - 1) // t) * t
    group_end = jnp.cumsum(padded)
    group_start = group_end - padded
    tile_start = jnp.arange(n_tiles, dtype=jnp.int32) * t
    tile_expert = jnp.minimum(jnp.sum((group_end[None, :] <= tile_start[:, None]).astype(jnp.int32), axis=1),
                              N_EXPERTS - 1)
    of_tile = tile_expert[:, None] == jnp.arange(N_EXPERTS, dtype=jnp.int32)[None, :]
    real_end = jnp.sum(jnp.where(of_tile, (group_start + counts)[None, :], 0), axis=1)
    tile_valid = jnp.clip(real_end - tile_start, 0, t).astype(jnp.int32)
    return group_start.astype(F32).reshape(N_EXPERTS, 1), tile_expert, tile_valid, n_tiles * t


def _rope_tables(seq):
    inv_freq = ROPE_THETA ** (-jnp.arange(0, ROPE_DIM, 2, dtype=F32) / ROPE_DIM)
    ang = jnp.arange(seq).astype(F32)[:, None] * inv_freq[None, :]
    cos, sin = jnp.cos(ang), jnp.sin(ang)
    rest = HEAD_DIM - ROPE_DIM
    zeros = jnp.zeros((seq, ROPE_HALF), F32)
    cos_t = jnp.concatenate([cos, cos, jnp.ones((seq, rest), F32)], axis=1)
    sa_t = jnp.concatenate([-sin, zeros, jnp.zeros((seq, rest), F32)], axis=1)
    sb_t = jnp.concatenate([zeros, sin, jnp.zeros((seq, rest), F32)], axis=1)
    return cos_t, sa_t, sb_t


def _layer(x2, batch, seq, w_in, b_gate, w_attn_o, w_dw, b_dw, conv_ln_g, conv_ln_b, w_conv_o, w_out,
           ln1_g, ln1_b, w_router, router_bias, w_exp_gate, w_exp_up, w_exp_down,
           w_sh_gate, w_sh_up, w_sh_down, ln2_g, ln2_b):
    n = x2.shape[0]
    row = lambda v: v.reshape(1, -1)
    cos, sa, sb = _rope_tables(seq)
    q, k, v, u, gate_a, gate_c, k_mean = _in_projection(
        x2, w_in.astype(BF16), row(b_gate), cos, sa, sb, seq)
    w_taps = jnp.broadcast_to(w_dw.reshape(CONV_WIDTH, 1, D_MODEL), (CONV_WIDTH, SUBLANES, D_MODEL))
    conv_gated = _conv_branch(u, w_taps, row(b_dw), row(conv_ln_g),
                              row(conv_ln_b), w_conv_o.astype(BF16), gate_c, batch, seq)
    attn = _moba_attention(q, k, v, k_mean, batch, seq)
    h, h_packed, top_idx, top_w, rank, totals = _merge_ln_router(
        attn, gate_a, conv_gated, x2, w_attn_o.astype(BF16), w_out.astype(BF16), row(ln1_g), row(ln1_b),
        w_router.T, router_bias.reshape(N_EXPERTS, 1))
    group_start, tile_expert, tile_valid, n_rows = _group_layout(totals, n)
    pos = _positions(top_idx, rank, group_start).reshape(TOP_K * n)
    xs = _sc_scatter_rows(h_packed, pos, n_rows)
    ys = _grouped_experts(tile_expert, tile_valid, xs, w_exp_gate, w_exp_up, w_exp_down)
    yg = _sc_gather_rows(ys, pos).reshape(TOP_K, n, HALF)
    return _combine_ln(h, yg, top_w.T, w_sh_gate.astype(BF16), w_sh_up.astype(BF16),
                       w_sh_down.astype(BF16), row(ln2_g), row(ln2_b))


def kernel(x, w_in, b_gate, w_attn_o, w_dw, b_dw, conv_ln_g, conv_ln_b, w_conv_o, w_out, ln1_g, ln1_b,
           w_router, router_bias, w_exp_gate, w_exp_up, w_exp_down, w_sh_gate, w_sh_up, w_sh_down,
           ln2_g, ln2_b):
    batch, seq, d = x.shape
    assert d == D_MODEL and seq % MOBA_BLOCK == 0 and seq % TOKEN_TILE == 0
    assert w_in.shape[0] == DEPTH
    x2 = x.reshape(batch * seq, d)
    for l in range(DEPTH):
        x2 = _layer(x2, batch, seq, w_in[l], b_gate[l], w_attn_o[l], w_dw[l], b_dw[l], conv_ln_g[l],
                    conv_ln_b[l], w_conv_o[l], w_out[l], ln1_g[l], ln1_b[l], w_router[l], router_bias[l],
                    w_exp_gate[l], w_exp_up[l], w_exp_down[l], w_sh_gate[l], w_sh_up[l], w_sh_down[l],
                    ln2_g[l], ln2_b[l])
    return x2.reshape(batch, seq, d)
```

```python
import functools

import jax
import jax.numpy as jnp
from jax import lax
from jax.experimental import pallas as pl
from jax.experimental.pallas import tpu as pltpu
from jax.experimental.pallas import tpu_sc as plsc

F32 = jnp.float32
BF16 = jnp.bfloat16
NEG_INF = float("-inf")
MASKED = -1e30
LOG2_E = 1.4426950408889634

D_MODEL = 1024
N_HEADS = 8
HEAD_DIM = 128
ROPE_THETA = 500000.0
ROPE_DIM = HEAD_DIM // 4
ROPE_HALF = ROPE_DIM // 2
MOBA_BLOCK = 256
MOBA_TOPK = 3
CONV_WIDTH = 31
SUBLANES = 8
CONV_HALO = 32
CONV_ROWS = 32
N_EXPERTS = 256
TOP_K = 8
N_GROUPS = 8
GROUP_SIZE = N_EXPERTS // N_GROUPS
TOPK_GROUPS = 4
EXPERT_HIDDEN = 256
ROUTED_SCALE = 2.5
LN_EPS = 1e-5
DEPTH = 1
DEEPNORM_ALPHA = (2 * DEPTH) ** 0.25
HALF = D_MODEL // 2

TOKEN_TILE = 256
EXPERT_TILE = 512
EXPERT_CHAIN = 256
POSITION_TILE = 1024
SC_CORES = 2
SC_SUBCORES = 16
SC_CHUNK = 64
VMEM_LIMIT = 56 * 1024 * 1024


def _params(*semantics):
    return pltpu.CompilerParams(dimension_semantics=semantics, vmem_limit_bytes=VMEM_LIMIT)


def _sigmoid(x):
    return 1.0 / (1.0 + jnp.exp(-x))


def _layer_norm(x, g, b):
    mu = jnp.mean(x, axis=-1, keepdims=True)
    xc = x - mu
    var = jnp.mean(xc * xc, axis=-1, keepdims=True)
    return xc * lax.rsqrt(var + LN_EPS) * g + b


def _pack_bf16_halves(y):
    lo = lax.bitcast_convert_type(y[:, :HALF].astype(BF16).astype(F32), jnp.uint32)
    hi = lax.bitcast_convert_type(y[:, HALF:].astype(BF16).astype(F32), jnp.uint32)
    return (hi & jnp.uint32(0xFFFF0000)) | (lo >> 16)


def _unpack_bf16_halves(p):
    lo = lax.bitcast_convert_type(p << 16, F32)
    hi = lax.bitcast_convert_type(p & jnp.uint32(0xFFFF0000), F32)
    return lo, hi


def _inproj_kernel(x_ref, w_ref, bg_ref, cos_ref, sa_ref, sb_ref,
                   q_ref, k_ref, v_ref, u_ref, ga_ref, gc_ref, km_ref):
    tm = x_ref.shape[0]
    xb = x_ref[...].astype(BF16)

    def proj(c):
        return jnp.dot(xb, w_ref[:, c * D_MODEL:(c + 1) * D_MODEL], preferred_element_type=F32)

    cos = cos_ref[...]
    sa = sa_ref[...]
    sb = sb_ref[...]

    def rope_head(t):
        return (t * cos + pltpu.roll(t, HEAD_DIM - ROPE_HALF, 1) * sa
                + pltpu.roll(t, ROPE_HALF, 1) * sb)

    q = proj(0)
    for h in range(N_HEADS):
        sl = slice(h * HEAD_DIM, (h + 1) * HEAD_DIM)
        q_ref[:, sl] = rope_head(q[:, sl]).astype(BF16)
    k = proj(1)
    for h in range(N_HEADS):
        sl = slice(h * HEAD_DIM, (h + 1) * HEAD_DIM)
        kr = rope_head(k[:, sl])
        k_ref[:, sl] = kr.astype(BF16)
        for g in range(tm // MOBA_BLOCK):
            km_ref[g, :, sl] = jnp.mean(kr[g * MOBA_BLOCK:(g + 1) * MOBA_BLOCK], axis=0, keepdims=True)
    v_ref[...] = proj(2).astype(BF16)
    u_ref[...] = proj(3) * _sigmoid(proj(4))
    ga_ref[...] = _sigmoid(proj(5) + bg_ref[:, :D_MODEL]).astype(BF16)
    gc_ref[...] = _sigmoid(proj(6) + bg_ref[:, D_MODEL:]).astype(BF16)


def _in_projection(x2, w_in_b, b_gate, cos, sa, sb, seq):
    n = x2.shape[0]
    tm = TOKEN_TILE
    n_cols = w_in_b.shape[1]
    tiles_per_seq = seq // tm
    row = lambda i: (i, 0)
    const = lambda i: (0, 0)
    pos = lambda i: (i % tiles_per_seq, 0)
    tok_bf16 = jax.ShapeDtypeStruct((n, D_MODEL), BF16)
    return pl.pallas_call(
        _inproj_kernel,
        grid=(n // tm,),
        in_specs=[
            pl.BlockSpec((tm, D_MODEL), row),
            pl.BlockSpec((D_MODEL, n_cols), const),
            pl.BlockSpec((1, 2 * D_MODEL), const),
            pl.BlockSpec((tm, HEAD_DIM), pos),
            pl.BlockSpec((tm, HEAD_DIM), pos),
            pl.BlockSpec((tm, HEAD_DIM), pos),
        ],
        out_specs=[
            pl.BlockSpec((tm, D_MODEL), row),
            pl.BlockSpec((tm, D_MODEL), row),
            pl.BlockSpec((tm, D_MODEL), row),
            pl.BlockSpec((tm, D_MODEL), row),
            pl.BlockSpec((tm, D_MODEL), row),
            pl.BlockSpec((tm, D_MODEL), row),
            pl.BlockSpec((tm // MOBA_BLOCK, 1, D_MODEL), lambda i: (i, 0, 0)),
        ],
        out_shape=[tok_bf16, tok_bf16, tok_bf16,
                   jax.ShapeDtypeStruct((n, D_MODEL), F32),
                   tok_bf16, tok_bf16,
                   jax.ShapeDtypeStruct((n // MOBA_BLOCK, 1, D_MODEL), F32)],
        compiler_params=_params("parallel"),
        name="in_projection",
    )(x2, w_in_b, b_gate, cos, sa, sb)


def _conv_kernel(u_ref, wdw_ref, bdw_ref, lng_ref, lnb_ref, wo_ref, gc_ref, o_ref, buf_ref, sh_ref, y_ref):
    ts = u_ref.shape[0]
    s = pl.program_id(1)

    @pl.when(s == 0)
    def _():
        buf_ref[0:CONV_HALO, :] = jnp.zeros((CONV_HALO, D_MODEL), F32)

    @pl.when(s > 0)
    def _():
        buf_ref[0:CONV_HALO, :] = buf_ref[ts:ts + CONV_HALO, :]

    buf_ref[CONV_HALO:CONV_HALO + ts, :] = u_ref[...]

    span = ts + CONV_HALO - SUBLANES
    for b in range(1, SUBLANES):
        sh_ref[b - 1, 0:span, :] = buf_ref[b:b + span, :]

    base = CONV_HALO - (CONV_WIDTH - 1)
    for c in range(ts // CONV_ROWS):
        r0 = c * CONV_ROWS
        acc = jnp.zeros((CONV_ROWS // SUBLANES, SUBLANES, D_MODEL), F32)
        for j in range(CONV_WIDTH):
            shift = (base + j) % SUBLANES
            row = r0 + base + j - shift
            src = buf_ref if shift == 0 else sh_ref.at[shift - 1]
            tap = src[row:row + CONV_ROWS, :].reshape(CONV_ROWS // SUBLANES, SUBLANES, D_MODEL)
            acc = acc + tap * wdw_ref[j]
        acc = acc.reshape(CONV_ROWS, D_MODEL)
        y = _layer_norm(acc + bdw_ref[...], lng_ref[...], lnb_ref[...])
        y_ref[r0:r0 + CONV_ROWS, :] = (y * _sigmoid(y)).astype(BF16)
    z = jnp.dot(y_ref[...], wo_ref[...], preferred_element_type=F32)
    o_ref[...] = (z * gc_ref[...].astype(F32)).astype(BF16)


def _conv_branch(u, w_dw, b_dw, ln_g, ln_b, w_o_b, gate_c, batch, seq):
    n = u.shape[0]
    ts = TOKEN_TILE
    tiles_per_seq = seq // ts
    row = lambda b, s: (b * tiles_per_seq + s, 0)
    const = lambda b, s: (0, 0)
    return pl.pallas_call(
        _conv_kernel,
        grid=(batch, tiles_per_seq),
        in_specs=[
            pl.BlockSpec((ts, D_MODEL), row),
            pl.BlockSpec((CONV_WIDTH, SUBLANES, D_MODEL), lambda b, s: (0, 0, 0)),
            pl.BlockSpec((1, D_MODEL), const),
            pl.BlockSpec((1, D_MODEL), const),
            pl.BlockSpec((1, D_MODEL), const),
            pl.BlockSpec((D_MODEL, D_MODEL), const),
            pl.BlockSpec((ts, D_MODEL), row),
        ],
        out_specs=pl.BlockSpec((ts, D_MODEL), row),
        out_shape=jax.ShapeDtypeStruct((n, D_MODEL), BF16),
        scratch_shapes=[pltpu.VMEM((ts + CONV_HALO, D_MODEL), F32),
                        pltpu.VMEM((SUBLANES - 1, ts + CONV_HALO - SUBLANES, D_MODEL), F32),
                        pltpu.VMEM((ts, D_MODEL), BF16)],
        compiler_params=_params("parallel", "arbitrary"),
        name="conv_branch",
    )(u, w_dw, b_dw, ln_g, ln_b, w_o_b, gate_c)


def _attn_kernel(q_ref, k_ref, v_ref, km_ref, o_ref, *, n_blk):
    blk = MOBA_BLOCK
    seq = n_blk * blk
    k_sel = min(MOBA_TOPK, n_blk)
    exp2_scale = HEAD_DIM ** -0.5 * LOG2_E
    nt_dims = (((1,), (1,)), ((), ()))
    n_lane = HEAD_DIM

    assert blk & (blk - 1) == 0
    blk_shift = blk.bit_length() - 1
    n_sub = -(-n_blk // SUBLANES) * SUBLANES

    q_all = q_ref[...]
    km = km_ref[:, 0, :]
    if n_sub > n_blk:
        km = jnp.concatenate([km, jnp.zeros((n_sub - n_blk, HEAD_DIM), F32)], axis=0)
    km_hi = km.astype(BF16)
    km_lo = (km - km_hi.astype(F32)).astype(BF16)

    gate = (lax.dot_general(km_hi, q_all, nt_dims, preferred_element_type=F32)
            + lax.dot_general(km_lo, q_all, nt_dims, preferred_element_type=F32))
    blk_t = lax.broadcasted_iota(jnp.int32, (n_sub, seq), 0)
    own_t = lax.broadcasted_iota(jnp.int32, (n_sub, seq), 1) >> blk_shift
    past = blk_t < own_t
    blk_f = blk_t.astype(F32)
    g = jnp.where(past, gate, NEG_INF)
    sel = jnp.zeros((n_sub, seq), F32)
    for _ in range(k_sel):
        mx = jnp.max(g, axis=0, keepdims=True)
        first = jnp.min(jnp.where(g == mx, blk_f, float(n_sub)), axis=0, keepdims=True)
        pick = blk_f == first
        sel = jnp.where(pick, 1.0, sel)
        g = jnp.where(pick, NEG_INF, g)
    visible = ((sel > 0.0) & past) | (blk_t == own_t)
    bias_t = jnp.concatenate([jnp.where(visible, 0.0, MASKED), jnp.zeros((n_lane - n_sub, seq), F32)], axis=0)
    q_bias = bias_t.T.astype(BF16)
    blk_id = lax.broadcasted_iota(jnp.int32, (seq, n_lane), 1)
    own_id = lax.broadcasted_iota(jnp.int32, (seq, n_lane), 0) >> blk_shift
    k_blk = jnp.where(blk_id == own_id, 1.0, 0.0).astype(BF16)
    k_aug = jnp.concatenate([k_ref[...], k_blk], axis=1)

    causal = (lax.broadcasted_iota(jnp.int32, (blk, blk), 1)
              <= lax.broadcasted_iota(jnp.int32, (blk, blk), 0))
    for i in range(n_blk):
        rows = slice(i * blk, (i + 1) * blk)
        width = (i + 1) * blk
        q_aug = jnp.concatenate([q_all[rows], q_bias[rows]], axis=1)
        raw = lax.dot_general(q_aug, k_aug[:width], nt_dims, preferred_element_type=F32)
        own = jnp.where(causal, raw[:, i * blk:], MASKED)
        parts = [raw[:, :i * blk], own] if i else [own]
        m = jnp.max(own, axis=1, keepdims=True)
        if i:
            m = jnp.maximum(m, jnp.max(parts[0], axis=1, keepdims=True))
        p = [jnp.exp2((t - m) * exp2_scale) for t in parts]
        l = sum(jnp.sum(t, axis=1, keepdims=True) for t in p)
        pv = jnp.dot(jnp.concatenate(p, axis=1).astype(BF16), v_ref[:width, :], preferred_element_type=F32)
        o_ref[rows, :] = (pv / l).astype(BF16)


def _moba_attention(q, k, v, k_mean, batch, seq):
    n = q.shape[0]
    n_blk = seq // MOBA_BLOCK
    seq_head = lambda b, h: (b, h)
    return pl.pallas_call(
        functools.partial(_attn_kernel, n_blk=n_blk),
        grid=(batch, N_HEADS),
        in_specs=[
            pl.BlockSpec((seq, HEAD_DIM), seq_head),
            pl.BlockSpec((seq, HEAD_DIM), seq_head),
            pl.BlockSpec((seq, HEAD_DIM), seq_head),
            pl.BlockSpec((n_blk, 1, HEAD_DIM), lambda b, h: (b, 0, h)),
        ],
        out_specs=pl.BlockSpec((seq, HEAD_DIM), seq_head),
        out_shape=jax.ShapeDtypeStruct((n, D_MODEL), BF16),
        compiler_params=_params("parallel", "parallel"),
        name="moba_attention",
    )(q, k, v, k_mean)


def _merge_kernel(attn_ref, ga_ref, cg_ref, x_ref, wao_ref, wout_ref, g1_ref, b1_ref,
                  wr_ref, rb_ref, h_ref, hp_ref, idx_ref, wgt_ref, rank_ref, total_ref, count_ref):
    tm = x_ref.shape[0]
    a = jnp.dot(attn_ref[...], wao_ref[...], preferred_element_type=F32)
    merged = ga_ref[...].astype(F32) * a + cg_ref[...].astype(F32)
    y = jnp.dot(merged.astype(BF16), wout_ref[...], preferred_element_type=F32)
    h = _layer_norm(DEEPNORM_ALPHA * x_ref[...] + y, g1_ref[...], b1_ref[...])
    h_ref[...] = h
    hp_ref[...] = _pack_bf16_halves(h)

    nt_dims = (((1,), (1,)), ((), ()))
    h_hi = h.astype(BF16)
    h_lo = (h - h_hi.astype(F32)).astype(BF16)
    w = wr_ref[...]
    w_hi = w.astype(BF16)
    w_lo = (w - w_hi.astype(F32)).astype(BF16)
    logits = (lax.dot_general(w_hi, h_hi, nt_dims, preferred_element_type=F32)
              + lax.dot_general(w_hi, h_lo, nt_dims, preferred_element_type=F32)
              + lax.dot_general(w_lo, h_hi, nt_dims, preferred_element_type=F32))
    scores = _sigmoid(logits)
    biased = scores + rb_ref[...]

    g3 = biased.reshape(N_GROUPS, GROUP_SIZE, tm)
    m1 = jnp.max(g3, axis=1, keepdims=True)
    is_max = g3 == m1
    n_max = jnp.sum(jnp.where(is_max, 1.0, 0.0), axis=1, keepdims=True)
    m2 = jnp.max(jnp.where(is_max, NEG_INF, g3), axis=1, keepdims=True)
    grp = (m1 + jnp.where(n_max >= 2.0, m1, m2)).reshape(N_GROUPS, tm)

    gid = lax.broadcasted_iota(jnp.int32, (N_GROUPS, tm), 0)
    rank = jnp.zeros((N_GROUPS, tm), F32)
    for o in range(N_GROUPS):
        other = grp[o:o + 1, :]
        ahead = (other > grp) | ((other == grp) & (o < gid))
        rank = rank + jnp.where(ahead, 1.0, 0.0)
    grp_keep = jnp.where(rank < float(TOPK_GROUPS), 1.0, 0.0)
    keep = jnp.broadcast_to(grp_keep.reshape(N_GROUPS, 1, tm),
                            (N_GROUPS, GROUP_SIZE, tm)).reshape(N_EXPERTS, tm)
    cand = jnp.where(keep > 0.0, biased, NEG_INF)

    eid = lax.broadcasted_iota(jnp.int32, (N_EXPERTS, tm), 0).astype(F32)
    chosen = jnp.zeros((N_EXPERTS, tm), F32)
    firsts = []
    for r in range(TOP_K):
        mx = jnp.max(cand, axis=0, keepdims=True)
        first = jnp.min(jnp.where(cand == mx, eid, float(N_EXPERTS)), axis=0, keepdims=True)
        pick = eid == first
        firsts.append(first)
        idx_ref[r:r + 1, :] = first.astype(jnp.int32)
        wgt_ref[r:r + 1, :] = jnp.sum(jnp.where(pick, scores, 0.0), axis=0, keepdims=True)
        cand = jnp.where(pick, NEG_INF, cand)
        chosen = jnp.where(pick, 1.0, chosen)
    top_s = wgt_ref[...]
    wgt_ref[...] = top_s / (jnp.sum(top_s, axis=0, keepdims=True) + 1e-20) * ROUTED_SCALE

    @pl.when(pl.program_id(0) == 0)
    def _():
        count_ref[...] = jnp.zeros((N_EXPERTS, 1), F32)

    earlier = (lax.broadcasted_iota(jnp.int32, (tm, tm), 0)
               < lax.broadcasted_iota(jnp.int32, (tm, tm), 1))
    before = jnp.dot(chosen.astype(BF16), jnp.where(earlier, 1.0, 0.0).astype(BF16),
                     preferred_element_type=F32) + count_ref[...]
    for r in range(TOP_K):
        rank_ref[r:r + 1, :] = jnp.sum(jnp.where(eid == firsts[r], before, 0.0),
                                       axis=0, keepdims=True).astype(jnp.int32)
    total = count_ref[...] + jnp.sum(chosen, axis=1, keepdims=True)
    count_ref[...] = total
    total_ref[...] = total


def _merge_ln_router(attn, gate_a, conv_gated, x2, w_attn_o_b, w_out_b, ln_g, ln_b, w_router_t, router_bias):
    n = x2.shape[0]
    tm = TOKEN_TILE
    row = lambda i: (i, 0)
    const = lambda i: (0, 0)
    col = lambda i: (0, i)
    return pl.pallas_call(
        _merge_kernel,
        grid=(n // tm,),
        in_specs=[
            pl.BlockSpec((tm, D_MODEL), row),
            pl.BlockSpec((tm, D_MODEL), row),
            pl.BlockSpec((tm, D_MODEL), row),
            pl.BlockSpec((tm, D_MODEL), row),
            pl.BlockSpec((D_MODEL, D_MODEL), const),
            pl.BlockSpec((D_MODEL, D_MODEL), const),
            pl.BlockSpec((1, D_MODEL), const),
            pl.BlockSpec((1, D_MODEL), const),
            pl.BlockSpec((N_EXPERTS, D_MODEL), const),
            pl.BlockSpec((N_EXPERTS, 1), const),
        ],
        out_specs=[
            pl.BlockSpec((tm, D_MODEL), row),
            pl.BlockSpec((tm, HALF), row),
            pl.BlockSpec((TOP_K, tm), col),
            pl.BlockSpec((TOP_K, tm), col),
            pl.BlockSpec((TOP_K, tm), col),
            pl.BlockSpec((N_EXPERTS, 1), const),
        ],
        out_shape=[
            jax.ShapeDtypeStruct((n, D_MODEL), F32),
            jax.ShapeDtypeStruct((n, HALF), jnp.uint32),
            jax.ShapeDtypeStruct((TOP_K, n), jnp.int32),
            jax.ShapeDtypeStruct((TOP_K, n), F32),
            jax.ShapeDtypeStruct((TOP_K, n), jnp.int32),
            jax.ShapeDtypeStruct((N_EXPERTS, 1), F32),
        ],
        scratch_shapes=[pltpu.VMEM((N_EXPERTS, 1), F32)],
        compiler_params=_params("arbitrary"),
        name="merge_ln_router",
    )(attn, gate_a, conv_gated, x2, w_attn_o_b, w_out_b, ln_g, ln_b, w_router_t, router_bias)


def _sc_mesh():
    return plsc.VectorSubcoreMesh(core_axis_name="c", subcore_axis_name="s",
                                  num_cores=SC_CORES, num_subcores=SC_SUBCORES)


def _sc_worker_base(rows_per_worker):
    return (lax.axis_index("s") * SC_CORES + lax.axis_index("c")) * rows_per_worker


def _sc_gather_rows(table, idx):
    m = idx.shape[0]
    width = table.shape[1]
    per_worker = m // (SC_CORES * SC_SUBCORES)
    assert per_worker * SC_CORES * SC_SUBCORES == m and per_worker % SC_CHUNK == 0

    @functools.partial(
        pl.kernel, mesh=_sc_mesh(),
        out_type=jax.ShapeDtypeStruct((m, width), table.dtype),
        scratch_types=[pltpu.VMEM((SC_CHUNK,), jnp.int32),
                       pltpu.VMEM((SC_CHUNK, width), table.dtype),
                       pltpu.SemaphoreType.DMA],
        name="sc_gather_rows")
    def gather(table_hbm, idx_hbm, out_hbm, idx_v, rows_v, sem):
        base = _sc_worker_base(per_worker)

        @pl.loop(0, per_worker // SC_CHUNK)
        def _(c):
            off = pl.multiple_of(base + c * SC_CHUNK, SC_CHUNK)
            pltpu.sync_copy(idx_hbm.at[pl.ds(off, SC_CHUNK)], idx_v)
            pltpu.async_copy(table_hbm.at[idx_v], rows_v, sem).wait()
            pltpu.sync_copy(rows_v, out_hbm.at[pl.ds(off, SC_CHUNK)])

    return gather(table, idx)


def _sc_scatter_rows(rows, pos, n_out):
    n, width = rows.shape
    per_worker = n // (SC_CORES * SC_SUBCORES)
    assert per_worker * SC_CORES * SC_SUBCORES == n and per_worker % SC_CHUNK == 0

    @functools.partial(
        pl.kernel, mesh=_sc_mesh(),
        out_type=jax.ShapeDtypeStruct((n_out, width), rows.dtype),
        scratch_types=[pltpu.VMEM((SC_CHUNK,), jnp.int32),
                       pltpu.VMEM((SC_CHUNK, width), rows.dtype),
                       pltpu.SemaphoreType.DMA],
        name="sc_scatter_rows")
    def scatter(rows_hbm, pos_hbm, out_hbm, idx_v, rows_v, sem):
        base = _sc_worker_base(per_worker)

        @pl.loop(0, per_worker // SC_CHUNK)
        def _(c):
            off = pl.multiple_of(base + c * SC_CHUNK, SC_CHUNK)
            pltpu.sync_copy(rows_hbm.at[pl.ds(off, SC_CHUNK)], rows_v)
            for r in range(TOP_K):
                pltpu.sync_copy(pos_hbm.at[pl.ds(r * n + off, SC_CHUNK)], idx_v)
                pltpu.async_copy(rows_v, out_hbm.at[idx_v], sem).wait()

    return scatter(rows, pos)


def _position_kernel(idx_ref, rank_ref, start_ref, pos_ref):
    tl = idx_ref.shape[1]
    eid = lax.broadcasted_iota(jnp.int32, (N_EXPERTS, tl), 0)
    start = start_ref[...]
    for r in range(TOP_K):
        here = jnp.sum(jnp.where(eid == idx_ref[r:r + 1, :], start, 0.0), axis=0, keepdims=True)
        pos_ref[r:r + 1, :] = here.astype(jnp.int32) + rank_ref[r:r + 1, :]


def _positions(top_idx, rank, group_start):
    n = top_idx.shape[1]
    tl = POSITION_TILE
    col = lambda i: (0, i)
    return pl.pallas_call(
        _position_kernel,
        grid=(n // tl,),
        in_specs=[pl.BlockSpec((TOP_K, tl), col), pl.BlockSpec((TOP_K, tl), col),
                  pl.BlockSpec((N_EXPERTS, 1), lambda i: (0, 0))],
        out_specs=pl.BlockSpec((TOP_K, tl), col),
        out_shape=jax.ShapeDtypeStruct((TOP_K, n), jnp.int32),
        compiler_params=_params("parallel"),
        name="positions",
    )(top_idx, rank, group_start)


def _expert_kernel(te_ref, tv_ref, xs_ref, wg_ref, wu_ref, wd_ref, y_ref):
    valid = tv_ref[pl.program_id(0)]

    @pl.when(valid > 0)
    def _():
        wg = wg_ref[0].astype(BF16)
        wu = wu_ref[0].astype(BF16)
        wd = wd_ref[0].astype(BF16)
        for c in range(EXPERT_TILE // EXPERT_CHAIN):
            rows = slice(c * EXPERT_CHAIN, (c + 1) * EXPERT_CHAIN)
            live = lax.broadcasted_iota(jnp.int32, (EXPERT_CHAIN, HALF), 0) < valid - c * EXPERT_CHAIN
            lo, hi = _unpack_bf16_halves(jnp.where(live, xs_ref[rows, :], jnp.uint32(0)))
            x = jnp.concatenate([lo, hi], axis=1).astype(BF16)
            g = jnp.dot(x, wg, preferred_element_type=F32)
            u = jnp.dot(x, wu, preferred_element_type=F32)
            a = (g * _sigmoid(g) * u).astype(BF16)
            y_ref[rows, :] = _pack_bf16_halves(jnp.dot(a, wd, preferred_element_type=F32))

    @pl.when(valid == 0)
    def _():
        y_ref[...] = jnp.zeros(y_ref.shape, jnp.uint32)


def _grouped_experts(tile_expert, tile_valid, xs, wg, wu, wd):
    p = xs.shape[0]
    t = EXPERT_TILE
    row = lambda i, te, tv: (i, 0)
    expert = lambda i, te, tv: (te[i], 0, 0)
    return pl.pallas_call(
        _expert_kernel,
        grid_spec=pltpu.PrefetchScalarGridSpec(
            num_scalar_prefetch=2,
            grid=(p // t,),
            in_specs=[
                pl.BlockSpec((t, HALF), row),
                pl.BlockSpec((1, D_MODEL, EXPERT_HIDDEN), expert),
                pl.BlockSpec((1, D_MODEL, EXPERT_HIDDEN), expert),
                pl.BlockSpec((1, EXPERT_HIDDEN, D_MODEL), expert),
            ],
            out_specs=pl.BlockSpec((t, HALF), row),
        ),
        out_shape=jax.ShapeDtypeStruct((p, HALF), jnp.uint32),
        compiler_params=_params("arbitrary"),
        name="grouped_experts",
    )(tile_expert, tile_valid, xs, wg, wu, wd)


def _combine_kernel(h_ref, yg_ref, wt_ref, wsg_ref, wsu_ref, wsd_ref, g2_ref, b2_ref, o_ref):
    h = h_ref[...]
    hb = h.astype(BF16)
    g = jnp.dot(hb, wsg_ref[...], preferred_element_type=F32)
    u = jnp.dot(hb, wsu_ref[...], preferred_element_type=F32)
    shared = jnp.dot((g * _sigmoid(g) * u).astype(BF16), wsd_ref[...], preferred_element_type=F32)
    wt = wt_ref[...]
    r_lo = jnp.zeros((h.shape[0], HALF), F32)
    r_hi = jnp.zeros((h.shape[0], HALF), F32)
    for r in range(TOP_K):
        lo, hi = _unpack_bf16_halves(yg_ref[r])
        w = wt[:, r:r + 1]
        r_lo = r_lo + lo * w
        r_hi = r_hi + hi * w
    routed = jnp.concatenate([r_lo, r_hi], axis=1)
    o_ref[...] = _layer_norm(DEEPNORM_ALPHA * h + (shared + routed), g2_ref[...], b2_ref[...])


def _combine_ln(h, yg, w_tok, wsg_b, wsu_b, wsd_b, ln_g, ln_b):
    n = h.shape[0]
    tm = TOKEN_TILE
    row = lambda i: (i, 0)
    const = lambda i: (0, 0)
    hidden = wsg_b.shape[1]
    return pl.pallas_call(
        _combine_kernel,
        grid=(n // tm,),
        in_specs=[
            pl.BlockSpec((tm, D_MODEL), row),
            pl.BlockSpec((TOP_K, tm, HALF), lambda i: (0, i, 0)),
            pl.BlockSpec((tm, TOP_K), row),
            pl.BlockSpec((D_MODEL, hidden), const),
            pl.BlockSpec((D_MODEL, hidden), const),
            pl.BlockSpec((hidden, D_MODEL), const),
            pl.BlockSpec((1, D_MODEL), const),
            pl.BlockSpec((1, D_MODEL), const),
        ],
        out_specs=pl.BlockSpec((tm, D_MODEL), row),
        out_shape=jax.ShapeDtypeStruct((n, D_MODEL), F32),
        compiler_params=_params("parallel"),
        name="combine_ln",
    )(h, yg, w_tok, wsg_b, wsu_b, wsd_b, ln_g, ln_b)


def _group_layout(totals, n_tokens):
    t = EXPERT_TILE
    n_tiles = (TOP_K * n_tokens + N_EXPERTS * (t - 1)) // t
    counts = totals[:, 0].astype(jnp.int32)
    padded = ((counts + t - 1) // t) * t
    group_end = jnp.cumsum(padded)
    group_start = group_end - padded
    tile_start = jnp.arange(n_tiles, dtype=jnp.int32) * t
    tile_expert = jnp.minimum(jnp.sum((group_end[None, :] <= tile_start[:, None]).astype(jnp.int32), axis=1),
                              N_EXPERTS - 1)
    of_tile = tile_expert[:, None] == jnp.arange(N_EXPERTS, dtype=jnp.int32)[None, :]
    real_end = jnp.sum(jnp.where(of_tile, (group_start + counts)[None, :], 0), axis=1)
    tile_valid = jnp.clip(real_end - tile_start, 0, t).astype(jnp.int32)
    return group_start.astype(F32).reshape(N_EXPERTS, 1), tile_expert, tile_valid, n_tiles * t


def _rope_tables(seq):
    inv_freq = ROPE_THETA ** (-jnp.arange(0, ROPE_DIM, 2, dtype=F32) / ROPE_DIM)
    ang = jnp.arange(seq).astype(F32)[:, None] * inv_freq[None, :]
    cos, sin = jnp.cos(ang), jnp.sin(ang)
    rest = HEAD_DIM - ROPE_DIM
    zeros = jnp.zeros((seq, ROPE_HALF), F32)
    cos_t = jnp.concatenate([cos, cos, jnp.ones((seq, rest), F32)], axis=1)
    sa_t = jnp.concatenate([-sin, zeros, jnp.zeros((seq, rest), F32)], axis=1)
    sb_t = jnp.concatenate([zeros, sin, jnp.zeros((seq, rest), F32)], axis=1)
    return cos_t, sa_t, sb_t


def _layer(x2, batch, seq, w_in, b_gate, w_attn_o, w_dw, b_dw, conv_ln_g, conv_ln_b, w_conv_o, w_out,
           ln1_g, ln1_b, w_router, router_bias, w_exp_gate, w_exp_up, w_exp_down,
           w_sh_gate, w_sh_up, w_sh_down, ln2_g, ln2_b):
    n = x2.shape[0]
    row = lambda v: v.reshape(1, -1)
    cos, sa, sb = _rope_tables(seq)
    q, k, v, u, gate_a, gate_c, k_mean = _in_projection(
        x2, w_in.astype(BF16), row(b_gate), cos, sa, sb, seq)
    w_taps = jnp.broadcast_to(w_dw.reshape(CONV_WIDTH, 1, D_MODEL), (CONV_WIDTH, SUBLANES, D_MODEL))
    conv_gated = _conv_branch(u, w_taps, row(b_dw), row(conv_ln_g),
                              row(conv_ln_b), w_conv_o.astype(BF16), gate_c, batch, seq)
    attn = _moba_attention(q, k, v, k_mean, batch, seq)
    h, h_packed, top_idx, top_w, rank, totals = _merge_ln_router(
        attn, gate_a, conv_gated, x2, w_attn_o.astype(BF16), w_out.astype(BF16), row(ln1_g), row(ln1_b),
        w_router.T, router_bias.reshape(N_EXPERTS, 1))
    group_start, tile_expert, tile_valid, n_rows = _group_layout(totals, n)
    pos = _positions(top_idx, rank, group_start).reshape(TOP_K * n)
    xs = _sc_scatter_rows(h_packed, pos, n_rows)
    ys = _grouped_experts(tile_expert, tile_valid, xs, w_exp_gate, w_exp_up, w_exp_down)
    yg = _sc_gather_rows(ys, pos).reshape(TOP_K, n, HALF)
    return _combine_ln(h, yg, top_w.T, w_sh_gate.astype(BF16), w_sh_up.astype(BF16),
                       w_sh_down.astype(BF16), row(ln2_g), row(ln2_b))


def kernel(x, w_in, b_gate, w_attn_o, w_dw, b_dw, conv_ln_g, conv_ln_b, w_conv_o, w_out, ln1_g, ln1_b,
           w_router, router_bias, w_exp_gate, w_exp_up, w_exp_down, w_sh_gate, w_sh_up, w_sh_down,
           ln2_g, ln2_b):
    batch, seq, d = x.shape
    assert d == D_MODEL and seq % MOBA_BLOCK == 0 and seq % TOKEN_TILE == 0
    assert w_in.shape[0] == DEPTH
    x2 = x.reshape(batch * seq, d)
    for l in range(DEPTH):
        x2 = _layer(x2, batch, seq, w_in[l], b_gate[l], w_attn_o[l], w_dw[l], b_dw[l], conv_ln_g[l],
                    conv_ln_b[l], w_conv_o[l], w_out[l], ln1_g[l], ln1_b[l], w_router[l], router_bias[l],
                    w_exp_gate[l], w_exp_up[l], w_exp_down[l], w_sh_gate[l], w_sh_up[l], w_sh_down[l],
                    ln2_g[l], ln2_b[l])
    return x2.reshape(batch, seq, d)
```

```python
import functools

import jax
import jax.numpy as jnp
from jax import lax
from jax.experimental import pallas as pl
from jax.experimental.pallas import tpu as pltpu
from jax.experimental.pallas import tpu_sc as plsc

F32 = jnp.float32
BF16 = jnp.bfloat16
NEG_INF = float("-inf")
MASKED = -1e30
LOG2_E = 1.4426950408889634

D_MODEL = 1024
N_HEADS = 8
HEAD_DIM = 128
ROPE_THETA = 500000.0
ROPE_DIM = HEAD_DIM // 4
ROPE_HALF = ROPE_DIM // 2
MOBA_BLOCK = 256
MOBA_TOPK = 3
CONV_WIDTH = 31
SUBLANES = 8
CONV_HALO = 32
CONV_ROWS = 32
N_EXPERTS = 256
TOP_K = 8
N_GROUPS = 8
GROUP_SIZE = N_EXPERTS // N_GROUPS
TOPK_GROUPS = 4
EXPERT_HIDDEN = 256
ROUTED_SCALE = 2.5
LN_EPS = 1e-5
DEPTH = 1
DEEPNORM_ALPHA = (2 * DEPTH) ** 0.25
HALF = D_MODEL // 2

TOKEN_TILE = 256
EXPERT_TILE = 512
EXPERT_CHAIN = 256
POSITION_TILE = 1024
SC_CORES = 2
SC_SUBCORES = 16
SC_CHUNK = 64
VMEM_LIMIT = 56 * 1024 * 1024


def _params(*semantics):
    return pltpu.CompilerParams(dimension_semantics=semantics, vmem_limit_bytes=VMEM_LIMIT)


def _sigmoid(x):
    return 1.0 / (1.0 + jnp.exp(-x))


def _layer_norm(x, g, b):
    mu = jnp.mean(x, axis=-1, keepdims=True)
    xc = x - mu
    var = jnp.mean(xc * xc, axis=-1, keepdims=True)
    return xc * lax.rsqrt(var + LN_EPS) * g + b


def _pack_bf16_halves(y):
    lo = lax.bitcast_convert_type(y[:, :HALF].astype(BF16).astype(F32), jnp.uint32)
    hi = lax.bitcast_convert_type(y[:, HALF:].astype(BF16).astype(F32), jnp.uint32)
    return (hi & jnp.uint32(0xFFFF0000)) | (lo >> 16)


def _unpack_bf16_halves(p):
    lo = lax.bitcast_convert_type(p << 16, F32)
    hi = lax.bitcast_convert_type(p & jnp.uint32(0xFFFF0000), F32)
    return lo, hi


def _inproj_kernel(x_ref, w_ref, bg_ref, cos_ref, sa_ref, sb_ref,
                   q_ref, k_ref, v_ref, u_ref, ga_ref, gc_ref, km_ref):
    tm = x_ref.shape[0]
    xb = x_ref[...].astype(BF16)

    def proj(c):
        return jnp.dot(xb, w_ref[:, c * D_MODEL:(c + 1) * D_MODEL], preferred_element_type=F32)

    cos = cos_ref[...]
    sa = sa_ref[...]
    sb = sb_ref[...]

    def rope_head(t):
        return (t * cos + pltpu.roll(t, HEAD_DIM - ROPE_HALF, 1) * sa
                + pltpu.roll(t, ROPE_HALF, 1) * sb)

    q = proj(0)
    for h in range(N_HEADS):
        sl = slice(h * HEAD_DIM, (h + 1) * HEAD_DIM)
        q_ref[:, sl] = rope_head(q[:, sl]).astype(BF16)
    k = proj(1)
    for h in range(N_HEADS):
        sl = slice(h * HEAD_DIM, (h + 1) * HEAD_DIM)
        kr = rope_head(k[:, sl])
        k_ref[:, sl] = kr.astype(BF16)
        for g in range(tm // MOBA_BLOCK):
            km_ref[g, :, sl] = jnp.mean(kr[g * MOBA_BLOCK:(g + 1) * MOBA_BLOCK], axis=0, keepdims=True)
    v_ref[...] = proj(2).astype(BF16)
    u_ref[...] = proj(3) * _sigmoid(proj(4))
    ga_ref[...] = _sigmoid(proj(5) + bg_ref[:, :D_MODEL]).astype(BF16)
    gc_ref[...] = _sigmoid(proj(6) + bg_ref[:, D_MODEL:]).astype(BF16)


def _in_projection(x2, w_in_b, b_gate, cos, sa, sb, seq):
    n = x2.shape[0]
    tm = TOKEN_TILE
    n_cols = w_in_b.shape[1]
    tiles_per_seq = seq // tm
    row = lambda i: (i, 0)
    const = lambda i: (0, 0)
    pos = lambda i: (i % tiles_per_seq, 0)
    tok_bf16 = jax.ShapeDtypeStruct((n, D_MODEL), BF16)
    return pl.pallas_call(
        _inproj_kernel,
        grid=(n // tm,),
        in_specs=[
            pl.BlockSpec((tm, D_MODEL), row),
            pl.BlockSpec((D_MODEL, n_cols), const),
            pl.BlockSpec((1, 2 * D_MODEL), const),
            pl.BlockSpec((tm, HEAD_DIM), pos),
            pl.BlockSpec((tm, HEAD_DIM), pos),
            pl.BlockSpec((tm, HEAD_DIM), pos),
        ],
        out_specs=[
            pl.BlockSpec((tm, D_MODEL), row),
            pl.BlockSpec((tm, D_MODEL), row),
            pl.BlockSpec((tm, D_MODEL), row),
            pl.BlockSpec((tm, D_MODEL), row),
            pl.BlockSpec((tm, D_MODEL), row),
            pl.BlockSpec((tm, D_MODEL), row),
            pl.BlockSpec((tm // MOBA_BLOCK, 1, D_MODEL), lambda i: (i, 0, 0)),
        ],
        out_shape=[tok_bf16, tok_bf16, tok_bf16,
                   jax.ShapeDtypeStruct((n, D_MODEL), F32),
                   tok_bf16, tok_bf16,
                   jax.ShapeDtypeStruct((n // MOBA_BLOCK, 1, D_MODEL), F32)],
        compiler_params=_params("parallel"),
        name="in_projection",
    )(x2, w_in_b, b_gate, cos, sa, sb)


def _conv_kernel(u_ref, wdw_ref, bdw_ref, lng_ref, lnb_ref, wo_ref, gc_ref, o_ref, buf_ref, sh_ref, y_ref):
    ts = u_ref.shape[0]
    s = pl.program_id(1)

    @pl.when(s == 0)
    def _():
        buf_ref[0:CONV_HALO, :] = jnp.zeros((CONV_HALO, D_MODEL), F32)

    @pl.when(s > 0)
    def _():
        buf_ref[0:CONV_HALO, :] = buf_ref[ts:ts + CONV_HALO, :]

    buf_ref[CONV_HALO:CONV_HALO + ts, :] = u_ref[...]

    span = ts + CONV_HALO - SUBLANES
    for b in range(1, SUBLANES):
        sh_ref[b - 1, 0:span, :] = buf_ref[b:b + span, :]

    base = CONV_HALO - (CONV_WIDTH - 1)
    for c in range(ts // CONV_ROWS):
        r0 = c * CONV_ROWS
        acc = jnp.zeros((CONV_ROWS // SUBLANES, SUBLANES, D_MODEL), F32)
        for j in range(CONV_WIDTH):
            shift = (base + j) % SUBLANES
            row = r0 + base + j - shift
            src = buf_ref if shift == 0 else sh_ref.at[shift - 1]
            tap = src[row:row + CONV_ROWS, :].reshape(CONV_ROWS // SUBLANES, SUBLANES, D_MODEL)
            acc = acc + tap * wdw_ref[j]
        acc = acc.reshape(CONV_ROWS, D_MODEL)
        y = _layer_norm(acc + bdw_ref[...], lng_ref[...], lnb_ref[...])
        y_ref[r0:r0 + CONV_ROWS, :] = (y * _sigmoid(y)).astype(BF16)
    z = jnp.dot(y_ref[...], wo_ref[...], preferred_element_type=F32)
    o_ref[...] = (z * gc_ref[...].astype(F32)).astype(BF16)


def _conv_branch(u, w_dw, b_dw, ln_g, ln_b, w_o_b, gate_c, batch, seq):
    n = u.shape[0]
    ts = TOKEN_TILE
    tiles_per_seq = seq // ts
    row = lambda b, s: (b * tiles_per_seq + s, 0)
    const = lambda b, s: (0, 0)
    return pl.pallas_call(
        _conv_kernel,
        grid=(batch, tiles_per_seq),
        in_specs=[
            pl.BlockSpec((ts, D_MODEL), row),
            pl.BlockSpec((CONV_WIDTH, SUBLANES, D_MODEL), lambda b, s: (0, 0, 0)),
            pl.BlockSpec((1, D_MODEL), const),
            pl.BlockSpec((1, D_MODEL), const),
            pl.BlockSpec((1, D_MODEL), const),
            pl.BlockSpec((D_MODEL, D_MODEL), const),
            pl.BlockSpec((ts, D_MODEL), row),
        ],
        out_specs=pl.BlockSpec((ts, D_MODEL), row),
        out_shape=jax.ShapeDtypeStruct((n, D_MODEL), BF16),
        scratch_shapes=[pltpu.VMEM((ts + CONV_HALO, D_MODEL), F32),
                        pltpu.VMEM((SUBLANES - 1, ts + CONV_HALO - SUBLANES, D_MODEL), F32),
                        pltpu.VMEM((ts, D_MODEL), BF16)],
        compiler_params=_params("parallel", "arbitrary"),
        name="conv_branch",
    )(u, w_dw, b_dw, ln_g, ln_b, w_o_b, gate_c)


def _attn_kernel(q_ref, k_ref, v_ref, km_ref, o_ref, *, n_blk):
    blk = MOBA_BLOCK
    seq = n_blk * blk
    k_sel = min(MOBA_TOPK, n_blk)
    exp2_scale = HEAD_DIM ** -0.5 * LOG2_E
    nt_dims = (((1,), (1,)), ((), ()))
    n_lane = HEAD_DIM

    assert blk & (blk - 1) == 0
    blk_shift = blk.bit_length() - 1
    n_sub = -(-n_blk // SUBLANES) * SUBLANES

    q_all = q_ref[...]
    km = km_ref[:, 0, :]
    if n_sub > n_blk:
        km = jnp.concatenate([km, jnp.zeros((n_sub - n_blk, HEAD_DIM), F32)], axis=0)
    km_hi = km.astype(BF16)
    km_lo = (km - km_hi.astype(F32)).astype(BF16)

    gate = (lax.dot_general(km_hi, q_all, nt_dims, preferred_element_type=F32)
            + lax.dot_general(km_lo, q_all, nt_dims, preferred_element_type=F32))
    blk_t = lax.broadcasted_iota(jnp.int32, (n_sub, seq), 0)
    own_t = lax.broadcasted_iota(jnp.int32, (n_sub, seq), 1) >> blk_shift
    past = blk_t < own_t
    blk_f = blk_t.astype(F32)
    g = jnp.where(past, gate, NEG_INF)
    sel = jnp.zeros((n_sub, seq), F32)
    for _ in range(k_sel):
        mx = jnp.max(g, axis=0, keepdims=True)
        first = jnp.min(jnp.where(g == mx, blk_f, float(n_sub)), axis=0, keepdims=True)
        pick = blk_f == first
        sel = jnp.where(pick, 1.0, sel)
        g = jnp.where(pick, NEG_INF, g)
    visible = ((sel > 0.0) & past) | (blk_t == own_t)
    bias_t = jnp.concatenate([jnp.where(visible, 0.0, MASKED), jnp.zeros((n_lane - n_sub, seq), F32)], axis=0)
    q_bias = bias_t.T.astype(BF16)
    blk_id = lax.broadcasted_iota(jnp.int32, (seq, n_lane), 1)
    own_id = lax.broadcasted_iota(jnp.int32, (seq, n_lane), 0) >> blk_shift
    k_blk = jnp.where(blk_id == own_id, 1.0, 0.0).astype(BF16)
    k_aug = jnp.concatenate([k_ref[...], k_blk], axis=1)

    causal = (lax.broadcasted_iota(jnp.int32, (blk, blk), 1)
              <= lax.broadcasted_iota(jnp.int32, (blk, blk), 0))
    for i in range(n_blk):
        rows = slice(i * blk, (i + 1) * blk)
        width = (i + 1) * blk
        q_aug = jnp.concatenate([q_all[rows], q_bias[rows]], axis=1)
        raw = lax.dot_general(q_aug, k_aug[:width], nt_dims, preferred_element_type=F32)
        own = jnp.where(causal, raw[:, i * blk:], MASKED)
        parts = [raw[:, :i * blk], own] if i else [own]
        m = jnp.max(own, axis=1, keepdims=True)
        if i:
            m = jnp.maximum(m, jnp.max(parts[0], axis=1, keepdims=True))
        p = [jnp.exp2((t - m) * exp2_scale) for t in parts]
        l = sum(jnp.sum(t, axis=1, keepdims=True) for t in p)
        pv = jnp.dot(jnp.concatenate(p, axis=1).astype(BF16), v_ref[:width, :], preferred_element_type=F32)
        o_ref[rows, :] = (pv / l).astype(BF16)


def _moba_attention(q, k, v, k_mean, batch, seq):
    n = q.shape[0]
    n_blk = seq // MOBA_BLOCK
    seq_head = lambda b, h: (b, h)
    return pl.pallas_call(
        functools.partial(_attn_kernel, n_blk=n_blk),
        grid=(batch, N_HEADS),
        in_specs=[
            pl.BlockSpec((seq, HEAD_DIM), seq_head),
            pl.BlockSpec((seq, HEAD_DIM), seq_head),
            pl.BlockSpec((seq, HEAD_DIM), seq_head),
            pl.BlockSpec((n_blk, 1, HEAD_DIM), lambda b, h: (b, 0, h)),
        ],
        out_specs=pl.BlockSpec((seq, HEAD_DIM), seq_head),
        out_shape=jax.ShapeDtypeStruct((n, D_MODEL), BF16),
        compiler_params=_params("parallel", "parallel"),
        name="moba_attention",
    )(q, k, v, k_mean)


def _merge_kernel(attn_ref, ga_ref, cg_ref, x_ref, wao_ref, wout_ref, g1_ref, b1_ref,
                  wr_ref, rb_ref, h_ref, hp_ref, idx_ref, wgt_ref, rank_ref, total_ref, count_ref):
    tm = x_ref.shape[0]
    a = jnp.dot(attn_ref[...], wao_ref[...], preferred_element_type=F32)
    merged = ga_ref[...].astype(F32) * a + cg_ref[...].astype(F32)
    y = jnp.dot(merged.astype(BF16), wout_ref[...], preferred_element_type=F32)
    h = _layer_norm(DEEPNORM_ALPHA * x_ref[...] + y, g1_ref[...], b1_ref[...])
    h_ref[...] = h
    hp_ref[...] = _pack_bf16_halves(h)

    nt_dims = (((1,), (1,)), ((), ()))
    h_hi = h.astype(BF16)
    h_lo = (h - h_hi.astype(F32)).astype(BF16)
    w = wr_ref[...]
    w_hi = w.astype(BF16)
    w_lo = (w - w_hi.astype(F32)).astype(BF16)
    logits = (lax.dot_general(w_hi, h_hi, nt_dims, preferred_element_type=F32)
              + lax.dot_general(w_hi, h_lo, nt_dims, preferred_element_type=F32)
              + lax.dot_general(w_lo, h_hi, nt_dims, preferred_element_type=F32))
    scores = _sigmoid(logits)
    biased = scores + rb_ref[...]

    g3 = biased.reshape(N_GROUPS, GROUP_SIZE, tm)
    m1 = jnp.max(g3, axis=1, keepdims=True)
    is_max = g3 == m1
    n_max = jnp.sum(jnp.where(is_max, 1.0, 0.0), axis=1, keepdims=True)
    m2 = jnp.max(jnp.where(is_max, NEG_INF, g3), axis=1, keepdims=True)
    grp = (m1 + jnp.where(n_max >= 2.0, m1, m2)).reshape(N_GROUPS, tm)

    gid = lax.broadcasted_iota(jnp.int32, (N_GROUPS, tm), 0)
    rank = jnp.zeros((N_GROUPS, tm), F32)
    for o in range(N_GROUPS):
        other = grp[o:o + 1, :]
        ahead = (other > grp) | ((other == grp) & (o < gid))
        rank = rank + jnp.where(ahead, 1.0, 0.0)
    grp_keep = jnp.where(rank < float(TOPK_GROUPS), 1.0, 0.0)
    keep = jnp.broadcast_to(grp_keep.reshape(N_GROUPS, 1, tm),
                            (N_GROUPS, GROUP_SIZE, tm)).reshape(N_EXPERTS, tm)
    cand = jnp.where(keep > 0.0, biased, NEG_INF)

    eid = lax.broadcasted_iota(jnp.int32, (N_EXPERTS, tm), 0).astype(F32)
    chosen = jnp.zeros((N_EXPERTS, tm), F32)
    firsts = []
    for r in range(TOP_K):
        mx = jnp.max(cand, axis=0, keepdims=True)
        first = jnp.min(jnp.where(cand == mx, eid, float(N_EXPERTS)), axis=0, keepdims=True)
        pick = eid == first
        firsts.append(first)
        idx_ref[r:r + 1, :] = first.astype(jnp.int32)
        wgt_ref[r:r + 1, :] = jnp.sum(jnp.where(pick, scores, 0.0), axis=0, keepdims=True)
        cand = jnp.where(pick, NEG_INF, cand)
        chosen = jnp.where(pick, 1.0, chosen)
    top_s = wgt_ref[...]
    wgt_ref[...] = top_s / (jnp.sum(top_s, axis=0, keepdims=True) + 1e-20) * ROUTED_SCALE

    @pl.when(pl.program_id(0) == 0)
    def _():
        count_ref[...] = jnp.zeros((N_EXPERTS, 1), F32)

    earlier = (lax.broadcasted_iota(jnp.int32, (tm, tm), 0)
               < lax.broadcasted_iota(jnp.int32, (tm, tm), 1))
    before = jnp.dot(chosen.astype(BF16), jnp.where(earlier, 1.0, 0.0).astype(BF16),
                     preferred_element_type=F32) + count_ref[...]
    for r in range(TOP_K):
        rank_ref[r:r + 1, :] = jnp.sum(jnp.where(eid == firsts[r], before, 0.0),
                                       axis=0, keepdims=True).astype(jnp.int32)
    total = count_ref[...] + jnp.sum(chosen, axis=1, keepdims=True)
    count_ref[...] = total
    total_ref[...] = total


def _merge_ln_router(attn, gate_a, conv_gated, x2, w_attn_o_b, w_out_b, ln_g, ln_b, w_router_t, router_bias):
    n = x2.shape[0]
    tm = TOKEN_TILE
    row = lambda i: (i, 0)
    const = lambda i: (0, 0)
    col = lambda i: (0, i)
    return pl.pallas_call(
        _merge_kernel,
        grid=(n // tm,),
        in_specs=[
            pl.BlockSpec((tm, D_MODEL), row),
            pl.BlockSpec((tm, D_MODEL), row),
            pl.BlockSpec((tm, D_MODEL), row),
            pl.BlockSpec((tm, D_MODEL), row),
            pl.BlockSpec((D_MODEL, D_MODEL), const),
            pl.BlockSpec((D_MODEL, D_MODEL), const),
            pl.BlockSpec((1, D_MODEL), const),
            pl.BlockSpec((1, D_MODEL), const),
            pl.BlockSpec((N_EXPERTS, D_MODEL), const),
            pl.BlockSpec((N_EXPERTS, 1), const),
        ],
        out_specs=[
            pl.BlockSpec((tm, D_MODEL), row),
            pl.BlockSpec((tm, HALF), row),
            pl.BlockSpec((TOP_K, tm), col),
            pl.BlockSpec((TOP_K, tm), col),
            pl.BlockSpec((TOP_K, tm), col),
            pl.BlockSpec((N_EXPERTS, 1), const),
        ],
        out_shape=[
            jax.ShapeDtypeStruct((n, D_MODEL), F32),
            jax.ShapeDtypeStruct((n, HALF), jnp.uint32),
            jax.ShapeDtypeStruct((TOP_K, n), jnp.int32),
            jax.ShapeDtypeStruct((TOP_K, n), F32),
            jax.ShapeDtypeStruct((TOP_K, n), jnp.int32),
            jax.ShapeDtypeStruct((N_EXPERTS, 1), F32),
        ],
        scratch_shapes=[pltpu.VMEM((N_EXPERTS, 1), F32)],
        compiler_params=_params("arbitrary"),
        name="merge_ln_router",
    )(attn, gate_a, conv_gated, x2, w_attn_o_b, w_out_b, ln_g, ln_b, w_router_t, router_bias)


def _sc_mesh():
    return plsc.VectorSubcoreMesh(core_axis_name="c", subcore_axis_name="s",
                                  num_cores=SC_CORES, num_subcores=SC_SUBCORES)


def _sc_worker_base(rows_per_worker):
    return (lax.axis_index("s") * SC_CORES + lax.axis_index("c")) * rows_per_worker


def _sc_gather_rows(table, idx):
    m = idx.shape[0]
    width = table.shape[1]
    per_worker = m // (SC_CORES * SC_SUBCORES)
    assert per_worker * SC_CORES * SC_SUBCORES == m and per_worker % SC_CHUNK == 0

    @functools.partial(
        pl.kernel, mesh=_sc_mesh(),
        out_type=jax.ShapeDtypeStruct((m, width), table.dtype),
        scratch_types=[pltpu.VMEM((SC_CHUNK,), jnp.int32),
                       pltpu.VMEM((SC_CHUNK, width), table.dtype),
                       pltpu.SemaphoreType.DMA],
        name="sc_gather_rows")
    def gather(table_hbm, idx_hbm, out_hbm, idx_v, rows_v, sem):
        base = _sc_worker_base(per_worker)

        @pl.loop(0, per_worker // SC_CHUNK)
        def _(c):
            off = pl.multiple_of(base + c * SC_CHUNK, SC_CHUNK)
            pltpu.sync_copy(idx_hbm.at[pl.ds(off, SC_CHUNK)], idx_v)
            pltpu.async_copy(table_hbm.at[idx_v], rows_v, sem).wait()
            pltpu.sync_copy(rows_v, out_hbm.at[pl.ds(off, SC_CHUNK)])

    return gather(table, idx)


def _sc_scatter_rows(rows, pos, n_out):
    n, width = rows.shape
    per_worker = n // (SC_CORES * SC_SUBCORES)
    assert per_worker * SC_CORES * SC_SUBCORES == n and per_worker % SC_CHUNK == 0

    @functools.partial(
        pl.kernel, mesh=_sc_mesh(),
        out_type=jax.ShapeDtypeStruct((n_out, width), rows.dtype),
        scratch_types=[pltpu.VMEM((SC_CHUNK,), jnp.int32),
                       pltpu.VMEM((SC_CHUNK, width), rows.dtype),
                       pltpu.SemaphoreType.DMA],
        name="sc_scatter_rows")
    def scatter(rows_hbm, pos_hbm, out_hbm, idx_v, rows_v, sem):
        base = _sc_worker_base(per_worker)

        @pl.loop(0, per_worker // SC_CHUNK)
        def _(c):
            off = pl.multiple_of(base + c * SC_CHUNK, SC_CHUNK)
            pltpu.sync_copy(rows_hbm.at[pl.ds(off, SC_CHUNK)], rows_v)
            for r in range(TOP_K):
                pltpu.sync_copy(pos_hbm.at[pl.ds(r * n + off, SC_CHUNK)], idx_v)
                pltpu.async_copy(rows_v, out_hbm.at[idx_v], sem).wait()

    return scatter(rows, pos)


def _position_kernel(idx_ref, rank_ref, start_ref, pos_ref):
    tl = idx_ref.shape[1]
    eid = lax.broadcasted_iota(jnp.int32, (N_EXPERTS, tl), 0)
    start = start_ref[...]
    for r in range(TOP_K):
        here = jnp.sum(jnp.where(eid == idx_ref[r:r + 1, :], start, 0.0), axis=0, keepdims=True)
        pos_ref[r:r + 1, :] = here.astype(jnp.int32) + rank_ref[r:r + 1, :]


def _positions(top_idx, rank, group_start):
    n = top_idx.shape[1]
    tl = POSITION_TILE
    col = lambda i: (0, i)
    return pl.pallas_call(
        _position_kernel,
        grid=(n // tl,),
        in_specs=[pl.BlockSpec((TOP_K, tl), col), pl.BlockSpec((TOP_K, tl), col),
                  pl.BlockSpec((N_EXPERTS, 1), lambda i: (0, 0))],
        out_specs=pl.BlockSpec((TOP_K, tl), col),
        out_shape=jax.ShapeDtypeStruct((TOP_K, n), jnp.int32),
        compiler_params=_params("parallel"),
        name="positions",
    )(top_idx, rank, group_start)


def _expert_kernel(te_ref, tv_ref, xs_ref, wg_ref, wu_ref, wd_ref, y_ref, act_ref):
    i = pl.program_id(0)
    valid = tv_ref[i]
    valid_prev = tv_ref[jnp.maximum(i - 1, 0)]

    @pl.when(i == 0)
    def _():
        act_ref[...] = jnp.zeros(act_ref.shape, BF16)

    @pl.when((valid > 0) | (valid_prev > 0))
    def _():
        act_prev = act_ref[...]
        wd = wd_ref[0].astype(BF16)
        for c in range(EXPERT_TILE // EXPERT_CHAIN):
            rows = slice(c * EXPERT_CHAIN, (c + 1) * EXPERT_CHAIN)
            y_ref[rows, :] = _pack_bf16_halves(jnp.dot(act_prev[rows], wd, preferred_element_type=F32))
        wg = wg_ref[0].astype(BF16)
        wu = wu_ref[0].astype(BF16)
        for c in range(EXPERT_TILE // EXPERT_CHAIN):
            rows = slice(c * EXPERT_CHAIN, (c + 1) * EXPERT_CHAIN)
            live = lax.broadcasted_iota(jnp.int32, (EXPERT_CHAIN, HALF), 0) < valid - c * EXPERT_CHAIN
            lo, hi = _unpack_bf16_halves(jnp.where(live, xs_ref[rows, :], jnp.uint32(0)))
            x = jnp.concatenate([lo, hi], axis=1).astype(BF16)
            g = jnp.dot(x, wg, preferred_element_type=F32)
            u = jnp.dot(x, wu, preferred_element_type=F32)
            act_ref[rows, :] = (g * _sigmoid(g) * u).astype(BF16)

    @pl.when((valid == 0) & (valid_prev == 0))
    def _():
        y_ref[...] = jnp.zeros(y_ref.shape, jnp.uint32)


def _grouped_experts(tile_expert, tile_valid, xs, wg, wu, wd):
    p = xs.shape[0]
    t = EXPERT_TILE
    n_tiles = p // t
    tile_expert = jnp.concatenate([tile_expert, tile_expert[-1:]])
    tile_valid = jnp.concatenate([tile_valid, jnp.zeros((1,), jnp.int32)])
    row_in = lambda i, te, tv: (jnp.minimum(i, n_tiles - 1), 0)
    row_out = lambda i, te, tv: (jnp.maximum(i - 1, 0), 0)
    expert = lambda i, te, tv: (te[i], 0, 0)
    expert_prev = lambda i, te, tv: (te[jnp.maximum(i - 1, 0)], 0, 0)
    return pl.pallas_call(
        _expert_kernel,
        grid_spec=pltpu.PrefetchScalarGridSpec(
            num_scalar_prefetch=2,
            grid=(n_tiles + 1,),
            in_specs=[
                pl.BlockSpec((t, HALF), row_in),
                pl.BlockSpec((1, D_MODEL, EXPERT_HIDDEN), expert),
                pl.BlockSpec((1, D_MODEL, EXPERT_HIDDEN), expert),
                pl.BlockSpec((1, EXPERT_HIDDEN, D_MODEL), expert_prev),
            ],
            out_specs=pl.BlockSpec((t, HALF), row_out),
            scratch_shapes=[pltpu.VMEM((t, EXPERT_HIDDEN), BF16)],
        ),
        out_shape=jax.ShapeDtypeStruct((p, HALF), jnp.uint32),
        compiler_params=_params("arbitrary"),
        name="grouped_experts",
    )(tile_expert, tile_valid, xs, wg, wu, wd)


def _combine_kernel(h_ref, yg_ref, wt_ref, wsg_ref, wsu_ref, wsd_ref, g2_ref, b2_ref, o_ref):
    h = h_ref[...]
    hb = h.astype(BF16)
    g = jnp.dot(hb, wsg_ref[...], preferred_element_type=F32)
    u = jnp.dot(hb, wsu_ref[...], preferred_element_type=F32)
    shared = jnp.dot((g * _sigmoid(g) * u).astype(BF16), wsd_ref[...], preferred_element_type=F32)
    wt = wt_ref[...]
    r_lo = jnp.zeros((h.shape[0], HALF), F32)
    r_hi = jnp.zeros((h.shape[0], HALF), F32)
    for r in range(TOP_K):
        lo, hi = _unpack_bf16_halves(yg_ref[r])
        w = wt[:, r:r + 1]
        r_lo = r_lo + lo * w
        r_hi = r_hi + hi * w
    routed = jnp.concatenate([r_lo, r_hi], axis=1)
    o_ref[...] = _layer_norm(DEEPNORM_ALPHA * h + (shared + routed), g2_ref[...], b2_ref[...])


def _combine_ln(h, yg, w_tok, wsg_b, wsu_b, wsd_b, ln_g, ln_b):
    n = h.shape[0]
    tm = TOKEN_TILE
    row = lambda i: (i, 0)
    const = lambda i: (0, 0)
    hidden = wsg_b.shape[1]
    return pl.pallas_call(
        _combine_kernel,
        grid=(n // tm,),
        in_specs=[
            pl.BlockSpec((tm, D_MODEL), row),
            pl.BlockSpec((TOP_K, tm, HALF), lambda i: (0, i, 0)),
            pl.BlockSpec((tm, TOP_K), row),
            pl.BlockSpec((D_MODEL, hidden), const),
            pl.BlockSpec((D_MODEL, hidden), const),
            pl.BlockSpec((hidden, D_MODEL), const),
            pl.BlockSpec((1, D_MODEL), const),
            pl.BlockSpec((1, D_MODEL), const),
        ],
        out_specs=pl.BlockSpec((tm, D_MODEL), row),
        out_shape=jax.ShapeDtypeStruct((n, D_MODEL), F32),
        compiler_params=_params("parallel"),
        name="combine_ln",
    )(h, yg, w_tok, wsg_b, wsu_b, wsd_b, ln_g, ln_b)


def _group_layout(totals, n_tokens):
    t = EXPERT_TILE
    n_tiles = (TOP_K * n_tokens + N_EXPERTS * (t - 1)) // t
    counts = totals[:, 0].astype(jnp.int32)
    padded = ((counts + t - 1) // t) * t
    group_end = jnp.cumsum(padded)
    group_start = group_end - padded
    tile_start = jnp.arange(n_tiles, dtype=jnp.int32) * t
    tile_expert = jnp.minimum(jnp.sum((group_end[None, :] <= tile_start[:, None]).astype(jnp.int32), axis=1),
                              N_EXPERTS - 1)
    of_tile = tile_expert[:, None] == jnp.arange(N_EXPERTS, dtype=jnp.int32)[None, :]
    real_end = jnp.sum(jnp.where(of_tile, (group_start + counts)[None, :], 0), axis=1)
    tile_valid = jnp.clip(real_end - tile_start, 0, t).astype(jnp.int32)
    return group_start.astype(F32).reshape(N_EXPERTS, 1), tile_expert, tile_valid, n_tiles * t


def _rope_tables(seq):
    inv_freq = ROPE_THETA ** (-jnp.arange(0, ROPE_DIM, 2, dtype=F32) / ROPE_DIM)
    ang = jnp.arange(seq).astype(F32)[:, None] * inv_freq[None, :]
    cos, sin = jnp.cos(ang), jnp.sin(ang)
    rest = HEAD_DIM - ROPE_DIM
    zeros = jnp.zeros((seq, ROPE_HALF), F32)
    cos_t = jnp.concatenate([cos, cos, jnp.ones((seq, rest), F32)], axis=1)
    sa_t = jnp.concatenate([-sin, zeros, jnp.zeros((seq, rest), F32)], axis=1)
    sb_t = jnp.concatenate([zeros, sin, jnp.zeros((seq, rest), F32)], axis=1)
    return cos_t, sa_t, sb_t


def _layer(x2, batch, seq, w_in, b_gate, w_attn_o, w_dw, b_dw, conv_ln_g, conv_ln_b, w_conv_o, w_out,
           ln1_g, ln1_b, w_router, router_bias, w_exp_gate, w_exp_up, w_exp_down,
           w_sh_gate, w_sh_up, w_sh_down, ln2_g, ln2_b):
    n = x2.shape[0]
    row = lambda v: v.reshape(1, -1)
    cos, sa, sb = _rope_tables(seq)
    q, k, v, u, gate_a, gate_c, k_mean = _in_projection(
        x2, w_in.astype(BF16), row(b_gate), cos, sa, sb, seq)
    w_taps = jnp.broadcast_to(w_dw.reshape(CONV_WIDTH, 1, D_MODEL), (CONV_WIDTH, SUBLANES, D_MODEL))
    conv_gated = _conv_branch(u, w_taps, row(b_dw), row(conv_ln_g),
                              row(conv_ln_b), w_conv_o.astype(BF16), gate_c, batch, seq)
    attn = _moba_attention(q, k, v, k_mean, batch, seq)
    h, h_packed, top_idx, top_w, rank, totals = _merge_ln_router(
        attn, gate_a, conv_gated, x2, w_attn_o.astype(BF16), w_out.astype(BF16), row(ln1_g), row(ln1_b),
        w_router.T, router_bias.reshape(N_EXPERTS, 1))
    group_start, tile_expert, tile_valid, n_rows = _group_layout(totals, n)
    pos = _positions(top_idx, rank, group_start).reshape(TOP_K * n)
    xs = _sc_scatter_rows(h_packed, pos, n_rows)
    ys = _grouped_experts(tile_expert, tile_valid, xs, w_exp_gate, w_exp_up, w_exp_down)
    yg = _sc_gather_rows(ys, pos).reshape(TOP_K, n, HALF)
    return _combine_ln(h, yg, top_w.T, w_sh_gate.astype(BF16), w_sh_up.astype(BF16),
                       w_sh_down.astype(BF16), row(ln2_g), row(ln2_b))


def kernel(x, w_in, b_gate, w_attn_o, w_dw, b_dw, conv_ln_g, conv_ln_b, w_conv_o, w_out, ln1_g, ln1_b,
           w_router, router_bias, w_exp_gate, w_exp_up, w_exp_down, w_sh_gate, w_sh_up, w_sh_down,
           ln2_g, ln2_b):
    batch, seq, d = x.shape
    assert d == D_MODEL and seq % MOBA_BLOCK == 0 and seq % TOKEN_TILE == 0
    assert w_in.shape[0] == DEPTH
    x2 = x.reshape(batch * seq, d)
    for l in range(DEPTH):
        x2 = _layer(x2, batch, seq, w_in[l], b_gate[l], w_attn_o[l], w_dw[l], b_dw[l], conv_ln_g[l],
                    conv_ln_b[l], w_conv_o[l], w_out[l], ln1_g[l], ln1_b[l], w_router[l], router_bias[l],
                    w_exp_gate[l], w_exp_up[l], w_exp_down[l], w_sh_gate[l], w_sh_up[l], w_sh_down[l],
                    ln2_g[l], ln2_b[l])
    return x2.reshape(batch, seq, d)
```

```python
import functools

import jax
import jax.numpy as jnp
from jax import lax
from jax.experimental import pallas as pl
from jax.experimental.pallas import tpu as pltpu
from jax.experimental.pallas import tpu_sc as plsc

F32 = jnp.float32
BF16 = jnp.bfloat16
NEG_INF = float("-inf")
MASKED = -1e30
LOG2_E = 1.4426950408889634

D_MODEL = 1024
N_HEADS = 8
HEAD_DIM = 128
ROPE_THETA = 500000.0
ROPE_DIM = HEAD_DIM // 4
ROPE_HALF = ROPE_DIM // 2
MOBA_BLOCK = 256
MOBA_TOPK = 3
CONV_WIDTH = 31
SUBLANES = 8
CONV_HALO = 32
CONV_ROWS = 32
N_EXPERTS = 256
TOP_K = 8
N_GROUPS = 8
GROUP_SIZE = N_EXPERTS // N_GROUPS
TOPK_GROUPS = 4
EXPERT_HIDDEN = 256
ROUTED_SCALE = 2.5
LN_EPS = 1e-5
DEPTH = 1
DEEPNORM_ALPHA = (2 * DEPTH) ** 0.25
HALF = D_MODEL // 2

TOKEN_TILE = 256
EXPERT_TILE = 512
EXPERT_CHAIN = 256
POSITION_TILE = 1024
MOE_CHUNKS = 2
SC_CORES = 2
SC_SUBCORES = 16
SC_CHUNK = 64
VMEM_LIMIT = 56 * 1024 * 1024


def _params(*semantics):
    return pltpu.CompilerParams(dimension_semantics=semantics, vmem_limit_bytes=VMEM_LIMIT)


def _sigmoid(x):
    return 1.0 / (1.0 + jnp.exp(-x))


def _layer_norm(x, g, b):
    mu = jnp.mean(x, axis=-1, keepdims=True)
    xc = x - mu
    var = jnp.mean(xc * xc, axis=-1, keepdims=True)
    return xc * lax.rsqrt(var + LN_EPS) * g + b


def _pack_bf16_halves(y):
    lo = lax.bitcast_convert_type(y[:, :HALF].astype(BF16).astype(F32), jnp.uint32)
    hi = lax.bitcast_convert_type(y[:, HALF:].astype(BF16).astype(F32), jnp.uint32)
    return (hi & jnp.uint32(0xFFFF0000)) | (lo >> 16)


def _unpack_bf16_halves(p):
    lo = lax.bitcast_convert_type(p << 16, F32)
    hi = lax.bitcast_convert_type(p & jnp.uint32(0xFFFF0000), F32)
    return lo, hi


def _inproj_kernel(x_ref, w_ref, bg_ref, cos_ref, sa_ref, sb_ref,
                   q_ref, k_ref, v_ref, u_ref, ga_ref, gc_ref, km_ref):
    tm = x_ref.shape[0]
    xb = x_ref[...].astype(BF16)

    def proj(c):
        return jnp.dot(xb, w_ref[:, c * D_MODEL:(c + 1) * D_MODEL], preferred_element_type=F32)

    cos = cos_ref[...]
    sa = sa_ref[...]
    sb = sb_ref[...]

    def rope_head(t):
        return (t * cos + pltpu.roll(t, HEAD_DIM - ROPE_HALF, 1) * sa
                + pltpu.roll(t, ROPE_HALF, 1) * sb)

    q = proj(0)
    for h in range(N_HEADS):
        sl = slice(h * HEAD_DIM, (h + 1) * HEAD_DIM)
        q_ref[:, sl] = rope_head(q[:, sl]).astype(BF16)
    k = proj(1)
    for h in range(N_HEADS):
        sl = slice(h * HEAD_DIM, (h + 1) * HEAD_DIM)
        kr = rope_head(k[:, sl])
        k_ref[:, sl] = kr.astype(BF16)
        for g in range(tm // MOBA_BLOCK):
            km_ref[g, :, sl] = jnp.mean(kr[g * MOBA_BLOCK:(g + 1) * MOBA_BLOCK], axis=0, keepdims=True)
    v_ref[...] = proj(2).astype(BF16)
    u_ref[...] = proj(3) * _sigmoid(proj(4))
    ga_ref[...] = _sigmoid(proj(5) + bg_ref[:, :D_MODEL]).astype(BF16)
    gc_ref[...] = _sigmoid(proj(6) + bg_ref[:, D_MODEL:]).astype(BF16)


def _in_projection(x2, w_in_b, b_gate, cos, sa, sb, seq):
    n = x2.shape[0]
    tm = TOKEN_TILE
    n_cols = w_in_b.shape[1]
    tiles_per_seq = seq // tm
    row = lambda i: (i, 0)
    const = lambda i: (0, 0)
    pos = lambda i: (i % tiles_per_seq, 0)
    tok_bf16 = jax.ShapeDtypeStruct((n, D_MODEL), BF16)
    return pl.pallas_call(
        _inproj_kernel,
        grid=(n // tm,),
        in_specs=[
            pl.BlockSpec((tm, D_MODEL), row),
            pl.BlockSpec((D_MODEL, n_cols), const),
            pl.BlockSpec((1, 2 * D_MODEL), const),
            pl.BlockSpec((tm, HEAD_DIM), pos),
            pl.BlockSpec((tm, HEAD_DIM), pos),
            pl.BlockSpec((tm, HEAD_DIM), pos),
        ],
        out_specs=[
            pl.BlockSpec((tm, D_MODEL), row),
            pl.BlockSpec((tm, D_MODEL), row),
            pl.BlockSpec((tm, D_MODEL), row),
            pl.BlockSpec((tm, D_MODEL), row),
            pl.BlockSpec((tm, D_MODEL), row),
            pl.BlockSpec((tm, D_MODEL), row),
            pl.BlockSpec((tm // MOBA_BLOCK, 1, D_MODEL), lambda i: (i, 0, 0)),
        ],
        out_shape=[tok_bf16, tok_bf16, tok_bf16,
                   jax.ShapeDtypeStruct((n, D_MODEL), F32),
                   tok_bf16, tok_bf16,
                   jax.ShapeDtypeStruct((n // MOBA_BLOCK, 1, D_MODEL), F32)],
        compiler_params=_params("parallel"),
        name="in_projection",
    )(x2, w_in_b, b_gate, cos, sa, sb)


def _conv_kernel(u_ref, wdw_ref, bdw_ref, lng_ref, lnb_ref, wo_ref, gc_ref, o_ref, buf_ref, sh_ref, y_ref):
    ts = u_ref.shape[0]
    s = pl.program_id(1)

    @pl.when(s == 0)
    def _():
        buf_ref[0:CONV_HALO, :] = jnp.zeros((CONV_HALO, D_MODEL), F32)

    @pl.when(s > 0)
    def _():
        buf_ref[0:CONV_HALO, :] = buf_ref[ts:ts + CONV_HALO, :]

    buf_ref[CONV_HALO:CONV_HALO + ts, :] = u_ref[...]

    span = ts + CONV_HALO - SUBLANES
    for b in range(1, SUBLANES):
        sh_ref[b - 1, 0:span, :] = buf_ref[b:b + span, :]

    base = CONV_HALO - (CONV_WIDTH - 1)
    for c in range(ts // CONV_ROWS):
        r0 = c * CONV_ROWS
        acc = jnp.zeros((CONV_ROWS // SUBLANES, SUBLANES, D_MODEL), F32)
        for j in range(CONV_WIDTH):
            shift = (base + j) % SUBLANES
            row = r0 + base + j - shift
            src = buf_ref if shift == 0 else sh_ref.at[shift - 1]
            tap = src[row:row + CONV_ROWS, :].reshape(CONV_ROWS // SUBLANES, SUBLANES, D_MODEL)
            acc = acc + tap * wdw_ref[j]
        acc = acc.reshape(CONV_ROWS, D_MODEL)
        y = _layer_norm(acc + bdw_ref[...], lng_ref[...], lnb_ref[...])
        y_ref[r0:r0 + CONV_ROWS, :] = (y * _sigmoid(y)).astype(BF16)
    z = jnp.dot(y_ref[...], wo_ref[...], preferred_element_type=F32)
    o_ref[...] = (z * gc_ref[...].astype(F32)).astype(BF16)


def _conv_branch(u, w_dw, b_dw, ln_g, ln_b, w_o_b, gate_c, batch, seq):
    n = u.shape[0]
    ts = TOKEN_TILE
    tiles_per_seq = seq // ts
    row = lambda b, s: (b * tiles_per_seq + s, 0)
    const = lambda b, s: (0, 0)
    return pl.pallas_call(
        _conv_kernel,
        grid=(batch, tiles_per_seq),
        in_specs=[
            pl.BlockSpec((ts, D_MODEL), row),
            pl.BlockSpec((CONV_WIDTH, SUBLANES, D_MODEL), lambda b, s: (0, 0, 0)),
            pl.BlockSpec((1, D_MODEL), const),
            pl.BlockSpec((1, D_MODEL), const),
            pl.BlockSpec((1, D_MODEL), const),
            pl.BlockSpec((D_MODEL, D_MODEL), const),
            pl.BlockSpec((ts, D_MODEL), row),
        ],
        out_specs=pl.BlockSpec((ts, D_MODEL), row),
        out_shape=jax.ShapeDtypeStruct((n, D_MODEL), BF16),
        scratch_shapes=[pltpu.VMEM((ts + CONV_HALO, D_MODEL), F32),
                        pltpu.VMEM((SUBLANES - 1, ts + CONV_HALO - SUBLANES, D_MODEL), F32),
                        pltpu.VMEM((ts, D_MODEL), BF16)],
        compiler_params=_params("parallel", "arbitrary"),
        name="conv_branch",
    )(u, w_dw, b_dw, ln_g, ln_b, w_o_b, gate_c)


def _attn_kernel(q_ref, k_ref, v_ref, km_ref, o_ref, *, n_blk):
    blk = MOBA_BLOCK
    seq = n_blk * blk
    k_sel = min(MOBA_TOPK, n_blk)
    exp2_scale = HEAD_DIM ** -0.5 * LOG2_E
    nt_dims = (((1,), (1,)), ((), ()))
    n_lane = HEAD_DIM

    assert blk & (blk - 1) == 0
    blk_shift = blk.bit_length() - 1
    n_sub = -(-n_blk // SUBLANES) * SUBLANES

    q_all = q_ref[...]
    km = km_ref[:, 0, :]
    if n_sub > n_blk:
        km = jnp.concatenate([km, jnp.zeros((n_sub - n_blk, HEAD_DIM), F32)], axis=0)
    km_hi = km.astype(BF16)
    km_lo = (km - km_hi.astype(F32)).astype(BF16)

    gate = (lax.dot_general(km_hi, q_all, nt_dims, preferred_element_type=F32)
            + lax.dot_general(km_lo, q_all, nt_dims, preferred_element_type=F32))
    blk_t = lax.broadcasted_iota(jnp.int32, (n_sub, seq), 0)
    own_t = lax.broadcasted_iota(jnp.int32, (n_sub, seq), 1) >> blk_shift
    past = blk_t < own_t
    blk_f = blk_t.astype(F32)
    g = jnp.where(past, gate, NEG_INF)
    sel = jnp.zeros((n_sub, seq), F32)
    for _ in range(k_sel):
        mx = jnp.max(g, axis=0, keepdims=True)
        first = jnp.min(jnp.where(g == mx, blk_f, float(n_sub)), axis=0, keepdims=True)
        pick = blk_f == first
        sel = jnp.where(pick, 1.0, sel)
        g = jnp.where(pick, NEG_INF, g)
    visible = ((sel > 0.0) & past) | (blk_t == own_t)
    bias_t = jnp.concatenate([jnp.where(visible, 0.0, MASKED), jnp.zeros((n_lane - n_sub, seq), F32)], axis=0)
    q_bias = bias_t.T.astype(BF16)
    blk_id = lax.broadcasted_iota(jnp.int32, (seq, n_lane), 1)
    own_id = lax.broadcasted_iota(jnp.int32, (seq, n_lane), 0) >> blk_shift
    k_blk = jnp.where(blk_id == own_id, 1.0, 0.0).astype(BF16)
    k_aug = jnp.concatenate([k_ref[...], k_blk], axis=1)

    causal = (lax.broadcasted_iota(jnp.int32, (blk, blk), 1)
              <= lax.broadcasted_iota(jnp.int32, (blk, blk), 0))
    for i in range(n_blk):
        rows = slice(i * blk, (i + 1) * blk)
        width = (i + 1) * blk
        q_aug = jnp.concatenate([q_all[rows], q_bias[rows]], axis=1)
        raw = lax.dot_general(q_aug, k_aug[:width], nt_dims, preferred_element_type=F32)
        own = jnp.where(causal, raw[:, i * blk:], MASKED)
        parts = [raw[:, :i * blk], own] if i else [own]
        m = jnp.max(own, axis=1, keepdims=True)
        if i:
            m = jnp.maximum(m, jnp.max(parts[0], axis=1, keepdims=True))
        p = [jnp.exp2((t - m) * exp2_scale) for t in parts]
        l = sum(jnp.sum(t, axis=1, keepdims=True) for t in p)
        pv = jnp.dot(jnp.concatenate(p, axis=1).astype(BF16), v_ref[:width, :], preferred_element_type=F32)
        o_ref[rows, :] = (pv / l).astype(BF16)


def _moba_attention(q, k, v, k_mean, batch, seq):
    n = q.shape[0]
    n_blk = seq // MOBA_BLOCK
    seq_head = lambda b, h: (b, h)
    return pl.pallas_call(
        functools.partial(_attn_kernel, n_blk=n_blk),
        grid=(batch, N_HEADS),
        in_specs=[
            pl.BlockSpec((seq, HEAD_DIM), seq_head),
            pl.BlockSpec((seq, HEAD_DIM), seq_head),
            pl.BlockSpec((seq, HEAD_DIM), seq_head),
            pl.BlockSpec((n_blk, 1, HEAD_DIM), lambda b, h: (b, 0, h)),
        ],
        out_specs=pl.BlockSpec((seq, HEAD_DIM), seq_head),
        out_shape=jax.ShapeDtypeStruct((n, D_MODEL), BF16),
        compiler_params=_params("parallel", "parallel"),
        name="moba_attention",
    )(q, k, v, k_mean)


def _merge_kernel(attn_ref, ga_ref, cg_ref, x_ref, wao_ref, wout_ref, g1_ref, b1_ref,
                  wr_ref, rb_ref, h_ref, hp_ref, idx_ref, wgt_ref, rank_ref, total_ref, count_ref):
    tm = x_ref.shape[0]
    a = jnp.dot(attn_ref[...], wao_ref[...], preferred_element_type=F32)
    merged = ga_ref[...].astype(F32) * a + cg_ref[...].astype(F32)
    y = jnp.dot(merged.astype(BF16), wout_ref[...], preferred_element_type=F32)
    h = _layer_norm(DEEPNORM_ALPHA * x_ref[...] + y, g1_ref[...], b1_ref[...])
    h_ref[...] = h
    hp_ref[...] = _pack_bf16_halves(h)

    nt_dims = (((1,), (1,)), ((), ()))
    h_hi = h.astype(BF16)
    h_lo = (h - h_hi.astype(F32)).astype(BF16)
    w = wr_ref[...]
    w_hi = w.astype(BF16)
    w_lo = (w - w_hi.astype(F32)).astype(BF16)
    logits = (lax.dot_general(w_hi, h_hi, nt_dims, preferred_element_type=F32)
              + lax.dot_general(w_hi, h_lo, nt_dims, preferred_element_type=F32)
              + lax.dot_general(w_lo, h_hi, nt_dims, preferred_element_type=F32))
    scores = _sigmoid(logits)
    biased = scores + rb_ref[...]

    g3 = biased.reshape(N_GROUPS, GROUP_SIZE, tm)
    m1 = jnp.max(g3, axis=1, keepdims=True)
    is_max = g3 == m1
    n_max = jnp.sum(jnp.where(is_max, 1.0, 0.0), axis=1, keepdims=True)
    m2 = jnp.max(jnp.where(is_max, NEG_INF, g3), axis=1, keepdims=True)
    grp = (m1 + jnp.where(n_max >= 2.0, m1, m2)).reshape(N_GROUPS, tm)

    gid = lax.broadcasted_iota(jnp.int32, (N_GROUPS, tm), 0)
    rank = jnp.zeros((N_GROUPS, tm), F32)
    for o in range(N_GROUPS):
        other = grp[o:o + 1, :]
        ahead = (other > grp) | ((other == grp) & (o < gid))
        rank = rank + jnp.where(ahead, 1.0, 0.0)
    grp_keep = jnp.where(rank < float(TOPK_GROUPS), 1.0, 0.0)
    keep = jnp.broadcast_to(grp_keep.reshape(N_GROUPS, 1, tm),
                            (N_GROUPS, GROUP_SIZE, tm)).reshape(N_EXPERTS, tm)
    cand = jnp.where(keep > 0.0, biased, NEG_INF)

    eid = lax.broadcasted_iota(jnp.int32, (N_EXPERTS, tm), 0).astype(F32)
    chosen = jnp.zeros((N_EXPERTS, tm), F32)
    firsts = []
    for r in range(TOP_K):
        mx = jnp.max(cand, axis=0, keepdims=True)
        first = jnp.min(jnp.where(cand == mx, eid, float(N_EXPERTS)), axis=0, keepdims=True)
        pick = eid == first
        firsts.append(first)
        idx_ref[r:r + 1, :] = first.astype(jnp.int32)
        wgt_ref[r:r + 1, :] = jnp.sum(jnp.where(pick, scores, 0.0), axis=0, keepdims=True)
        cand = jnp.where(pick, NEG_INF, cand)
        chosen = jnp.where(pick, 1.0, chosen)
    top_s = wgt_ref[...]
    wgt_ref[...] = top_s / (jnp.sum(top_s, axis=0, keepdims=True) + 1e-20) * ROUTED_SCALE

    @pl.when(pl.program_id(0) == 0)
    def _():
        count_ref[...] = jnp.zeros((N_EXPERTS, 1), F32)

    earlier = (lax.broadcasted_iota(jnp.int32, (tm, tm), 0)
               < lax.broadcasted_iota(jnp.int32, (tm, tm), 1))
    before = jnp.dot(chosen.astype(BF16), jnp.where(earlier, 1.0, 0.0).astype(BF16),
                     preferred_element_type=F32) + count_ref[...]
    for r in range(TOP_K):
        rank_ref[r:r + 1, :] = jnp.sum(jnp.where(eid == firsts[r], before, 0.0),
                                       axis=0, keepdims=True).astype(jnp.int32)
    total = count_ref[...] + jnp.sum(chosen, axis=1, keepdims=True)
    count_ref[...] = total
    total_ref[...] = total


def _merge_ln_router(attn, gate_a, conv_gated, x2, w_attn_o_b, w_out_b, ln_g, ln_b, w_router_t, router_bias,
                     first_token, n):
    tm = TOKEN_TILE
    first_tile = first_token // tm
    row_in = lambda i: (i + first_tile, 0)
    row = lambda i: (i, 0)
    const = lambda i: (0, 0)
    col = lambda i: (0, i)
    return pl.pallas_call(
        _merge_kernel,
        grid=(n // tm,),
        in_specs=[
            pl.BlockSpec((tm, D_MODEL), row_in),
            pl.BlockSpec((tm, D_MODEL), row_in),
            pl.BlockSpec((tm, D_MODEL), row_in),
            pl.BlockSpec((tm, D_MODEL), row_in),
            pl.BlockSpec((D_MODEL, D_MODEL), const),
            pl.BlockSpec((D_MODEL, D_MODEL), const),
            pl.BlockSpec((1, D_MODEL), const),
            pl.BlockSpec((1, D_MODEL), const),
            pl.BlockSpec((N_EXPERTS, D_MODEL), const),
            pl.BlockSpec((N_EXPERTS, 1), const),
        ],
        out_specs=[
            pl.BlockSpec((tm, D_MODEL), row),
            pl.BlockSpec((tm, HALF), row),
            pl.BlockSpec((TOP_K, tm), col),
            pl.BlockSpec((TOP_K, tm), col),
            pl.BlockSpec((TOP_K, tm), col),
            pl.BlockSpec((N_EXPERTS, 1), const),
        ],
        out_shape=[
            jax.ShapeDtypeStruct((n, D_MODEL), F32),
            jax.ShapeDtypeStruct((n, HALF), jnp.uint32),
            jax.ShapeDtypeStruct((TOP_K, n), jnp.int32),
            jax.ShapeDtypeStruct((TOP_K, n), F32),
            jax.ShapeDtypeStruct((TOP_K, n), jnp.int32),
            jax.ShapeDtypeStruct((N_EXPERTS, 1), F32),
        ],
        scratch_shapes=[pltpu.VMEM((N_EXPERTS, 1), F32)],
        compiler_params=_params("arbitrary"),
        name="merge_ln_router",
    )(attn, gate_a, conv_gated, x2, w_attn_o_b, w_out_b, ln_g, ln_b, w_router_t, router_bias)


def _sc_mesh():
    return plsc.VectorSubcoreMesh(core_axis_name="c", subcore_axis_name="s",
                                  num_cores=SC_CORES, num_subcores=SC_SUBCORES)


def _sc_worker_base(rows_per_worker):
    return (lax.axis_index("s") * SC_CORES + lax.axis_index("c")) * rows_per_worker


def _sc_gather_rows(table, idx):
    m = idx.shape[0]
    width = table.shape[1]
    per_worker = m // (SC_CORES * SC_SUBCORES)
    assert per_worker * SC_CORES * SC_SUBCORES == m and per_worker % SC_CHUNK == 0

    @functools.partial(
        pl.kernel, mesh=_sc_mesh(),
        out_type=jax.ShapeDtypeStruct((m, width), table.dtype),
        scratch_types=[pltpu.VMEM((SC_CHUNK,), jnp.int32),
                       pltpu.VMEM((SC_CHUNK, width), table.dtype),
                       pltpu.SemaphoreType.DMA],
        name="sc_gather_rows")
    def gather(table_hbm, idx_hbm, out_hbm, idx_v, rows_v, sem):
        base = _sc_worker_base(per_worker)

        @pl.loop(0, per_worker // SC_CHUNK)
        def _(c):
            off = pl.multiple_of(base + c * SC_CHUNK, SC_CHUNK)
            pltpu.sync_copy(idx_hbm.at[pl.ds(off, SC_CHUNK)], idx_v)
            pltpu.async_copy(table_hbm.at[idx_v], rows_v, sem).wait()
            pltpu.sync_copy(rows_v, out_hbm.at[pl.ds(off, SC_CHUNK)])

    return gather(table, idx)


def _sc_scatter_rows(rows, pos, n_out):
    n, width = rows.shape
    per_worker = n // (SC_CORES * SC_SUBCORES)
    assert per_worker * SC_CORES * SC_SUBCORES == n and per_worker % SC_CHUNK == 0

    @functools.partial(
        pl.kernel, mesh=_sc_mesh(),
        out_type=jax.ShapeDtypeStruct((n_out, width), rows.dtype),
        scratch_types=[pltpu.VMEM((SC_CHUNK,), jnp.int32),
                       pltpu.VMEM((SC_CHUNK, width), rows.dtype),
                       pltpu.SemaphoreType.DMA],
        name="sc_scatter_rows")
    def scatter(rows_hbm, pos_hbm, out_hbm, idx_v, rows_v, sem):
        base = _sc_worker_base(per_worker)

        @pl.loop(0, per_worker // SC_CHUNK)
        def _(c):
            off = pl.multiple_of(base + c * SC_CHUNK, SC_CHUNK)
            pltpu.sync_copy(rows_hbm.at[pl.ds(off, SC_CHUNK)], rows_v)
            for r in range(TOP_K):
                pltpu.sync_copy(pos_hbm.at[pl.ds(r * n + off, SC_CHUNK)], idx_v)
                pltpu.async_copy(rows_v, out_hbm.at[idx_v], sem).wait()

    return scatter(rows, pos)


def _position_kernel(idx_ref, rank_ref, start_ref, pos_ref):
    tl = idx_ref.shape[1]
    eid = lax.broadcasted_iota(jnp.int32, (N_EXPERTS, tl), 0)
    start = start_ref[...]
    for r in range(TOP_K):
        here = jnp.sum(jnp.where(eid == idx_ref[r:r + 1, :], start, 0.0), axis=0, keepdims=True)
        pos_ref[r:r + 1, :] = here.astype(jnp.int32) + rank_ref[r:r + 1, :]


def _positions(top_idx, rank, group_start):
    n = top_idx.shape[1]
    tl = POSITION_TILE
    col = lambda i: (0, i)
    return pl.pallas_call(
        _position_kernel,
        grid=(n // tl,),
        in_specs=[pl.BlockSpec((TOP_K, tl), col), pl.BlockSpec((TOP_K, tl), col),
                  pl.BlockSpec((N_EXPERTS, 1), lambda i: (0, 0))],
        out_specs=pl.BlockSpec((TOP_K, tl), col),
        out_shape=jax.ShapeDtypeStruct((TOP_K, n), jnp.int32),
        compiler_params=_params("parallel"),
        name="positions",
    )(top_idx, rank, group_start)


def _expert_kernel(te_ref, tv_ref, xs_ref, wg_ref, wu_ref, wd_ref, y_ref, act_ref):
    i = pl.program_id(0)
    valid = tv_ref[i]
    valid_prev = tv_ref[jnp.maximum(i - 1, 0)]

    @pl.when(i == 0)
    def _():
        act_ref[...] = jnp.zeros(act_ref.shape, BF16)

    @pl.when((valid > 0) | (valid_prev > 0))
    def _():
        act_prev = act_ref[...]
        wd = wd_ref[0].astype(BF16)
        for c in range(EXPERT_TILE // EXPERT_CHAIN):
            rows = slice(c * EXPERT_CHAIN, (c + 1) * EXPERT_CHAIN)
            y_ref[rows, :] = _pack_bf16_halves(jnp.dot(act_prev[rows], wd, preferred_element_type=F32))
        wg = wg_ref[0].astype(BF16)
        wu = wu_ref[0].astype(BF16)
        for c in range(EXPERT_TILE // EXPERT_CHAIN):
            rows = slice(c * EXPERT_CHAIN, (c + 1) * EXPERT_CHAIN)
            live = lax.broadcasted_iota(jnp.int32, (EXPERT_CHAIN, HALF), 0) < valid - c * EXPERT_CHAIN
            lo, hi = _unpack_bf16_halves(jnp.where(live, xs_ref[rows, :], jnp.uint32(0)))
            x = jnp.concatenate([lo, hi], axis=1).astype(BF16)
            g = jnp.dot(x, wg, preferred_element_type=F32)
            u = jnp.dot(x, wu, preferred_element_type=F32)
            act_ref[rows, :] = (g * _sigmoid(g) * u).astype(BF16)

    @pl.when((valid == 0) & (valid_prev == 0))
    def _():
        y_ref[...] = jnp.zeros(y_ref.shape, jnp.uint32)


def _grouped_experts(tile_expert, tile_valid, xs, wg, wu, wd):
    p = xs.shape[0]
    t = EXPERT_TILE
    n_tiles = p // t
    tile_expert = jnp.concatenate([tile_expert, tile_expert[-1:]])
    tile_valid = jnp.concatenate([tile_valid, jnp.zeros((1,), jnp.int32)])
    row_in = lambda i, te, tv: (jnp.minimum(i, n_tiles - 1), 0)
    row_out = lambda i, te, tv: (jnp.maximum(i - 1, 0), 0)
    expert = lambda i, te, tv: (te[i], 0, 0)
    expert_prev = lambda i, te, tv: (te[jnp.maximum(i - 1, 0)], 0, 0)
    return pl.pallas_call(
        _expert_kernel,
        grid_spec=pltpu.PrefetchScalarGridSpec(
            num_scalar_prefetch=2,
            grid=(n_tiles + 1,),
            in_specs=[
                pl.BlockSpec((t, HALF), row_in),
                pl.BlockSpec((1, D_MODEL, EXPERT_HIDDEN), expert),
                pl.BlockSpec((1, D_MODEL, EXPERT_HIDDEN), expert),
                pl.BlockSpec((1, EXPERT_HIDDEN, D_MODEL), expert_prev),
            ],
            out_specs=pl.BlockSpec((t, HALF), row_out),
            scratch_shapes=[pltpu.VMEM((t, EXPERT_HIDDEN), BF16)],
        ),
        out_shape=jax.ShapeDtypeStruct((p, HALF), jnp.uint32),
        compiler_params=_params("arbitrary"),
        name="grouped_experts",
    )(tile_expert, tile_valid, xs, wg, wu, wd)


def _combine_kernel(h_ref, yg_ref, wt_ref, wsg_ref, wsu_ref, wsd_ref, g2_ref, b2_ref, *out_refs):
    o_ref = out_refs[-1]
    h = h_ref[...]
    hb = h.astype(BF16)
    g = jnp.dot(hb, wsg_ref[...], preferred_element_type=F32)
    u = jnp.dot(hb, wsu_ref[...], preferred_element_type=F32)
    shared = jnp.dot((g * _sigmoid(g) * u).astype(BF16), wsd_ref[...], preferred_element_type=F32)
    wt = wt_ref[...]
    r_lo = jnp.zeros((h.shape[0], HALF), F32)
    r_hi = jnp.zeros((h.shape[0], HALF), F32)
    for r in range(TOP_K):
        lo, hi = _unpack_bf16_halves(yg_ref[r])
        w = wt[:, r:r + 1]
        r_lo = r_lo + lo * w
        r_hi = r_hi + hi * w
    routed = jnp.concatenate([r_lo, r_hi], axis=1)
    o_ref[...] = _layer_norm(DEEPNORM_ALPHA * h + (shared + routed), g2_ref[...], b2_ref[...])


def _combine_ln(h, yg, w_tok, wsg_b, wsu_b, wsd_b, ln_g, ln_b, out_so_far, first_token, n_total):
    n = h.shape[0]
    tm = TOKEN_TILE
    first_tile = first_token // tm
    row = lambda i: (i, 0)
    const = lambda i: (0, 0)
    hidden = wsg_b.shape[1]
    in_specs = [
        pl.BlockSpec((tm, D_MODEL), row),
        pl.BlockSpec((TOP_K, tm, HALF), lambda i: (0, i, 0)),
        pl.BlockSpec((tm, TOP_K), row),
        pl.BlockSpec((D_MODEL, hidden), const),
        pl.BlockSpec((D_MODEL, hidden), const),
        pl.BlockSpec((hidden, D_MODEL), const),
        pl.BlockSpec((1, D_MODEL), const),
        pl.BlockSpec((1, D_MODEL), const),
    ]
    args = [h, yg, w_tok, wsg_b, wsu_b, wsd_b, ln_g, ln_b]
    aliases = {}
    if out_so_far is not None:
        in_specs.append(pl.BlockSpec(memory_space=pl.ANY))
        args.append(out_so_far)
        aliases = {len(args) - 1: 0}
    return pl.pallas_call(
        _combine_kernel,
        grid=(n // tm,),
        in_specs=in_specs,
        out_specs=pl.BlockSpec((tm, D_MODEL), lambda i: (i + first_tile, 0)),
        out_shape=jax.ShapeDtypeStruct((n_total, D_MODEL), F32),
        input_output_aliases=aliases,
        compiler_params=_params("parallel"),
        name="combine_ln",
    )(*args)


def _group_layout(totals, n_tokens):
    t = EXPERT_TILE
    n_tiles = (TOP_K * n_tokens + N_EXPERTS * (t - 1)) // t
    counts = totals[:, 0].astype(jnp.int32)
    padded = ((counts + t - 1) // t) * t
    group_end = jnp.cumsum(padded)
    group_start = group_end - padded
    tile_start = jnp.arange(n_tiles, dtype=jnp.int32) * t
    tile_expert = jnp.minimum(jnp.sum((group_end[None, :] <= tile_start[:, None]).astype(jnp.int32), axis=1),
                              N_EXPERTS - 1)
    of_tile = tile_expert[:, None] == jnp.arange(N_EXPERTS, dtype=jnp.int32)[None, :]
    real_end = jnp.sum(jnp.where(of_tile, (group_start + counts)[None, :], 0), axis=1)
    tile_valid = jnp.clip(real_end - tile_start, 0, t).astype(jnp.int32)
    return group_start.astype(F32).reshape(N_EXPERTS, 1), tile_expert, tile_valid, n_tiles * t


def _rope_tables(seq):
    inv_freq = ROPE_THETA ** (-jnp.arange(0, ROPE_DIM, 2, dtype=F32) / ROPE_DIM)
    ang = jnp.arange(seq).astype(F32)[:, None] * inv_freq[None, :]
    cos, sin = jnp.cos(ang), jnp.sin(ang)
    rest = HEAD_DIM - ROPE_DIM
    zeros = jnp.zeros((seq, ROPE_HALF), F32)
    cos_t = jnp.concatenate([cos, cos, jnp.ones((seq, rest), F32)], axis=1)
    sa_t = jnp.concatenate([-sin, zeros, jnp.zeros((seq, rest), F32)], axis=1)
    sb_t = jnp.concatenate([zeros, sin, jnp.zeros((seq, rest), F32)], axis=1)
    return cos_t, sa_t, sb_t


def _layer(x2, batch, seq, w_in, b_gate, w_attn_o, w_dw, b_dw, conv_ln_g, conv_ln_b, w_conv_o, w_out,
           ln1_g, ln1_b, w_router, router_bias, w_exp_gate, w_exp_up, w_exp_down,
           w_sh_gate, w_sh_up, w_sh_down, ln2_g, ln2_b):
    n = x2.shape[0]
    row = lambda v: v.reshape(1, -1)
    cos, sa, sb = _rope_tables(seq)
    q, k, v, u, gate_a, gate_c, k_mean = _in_projection(
        x2, w_in.astype(BF16), row(b_gate), cos, sa, sb, seq)
    w_taps = jnp.broadcast_to(w_dw.reshape(CONV_WIDTH, 1, D_MODEL), (CONV_WIDTH, SUBLANES, D_MODEL))
    conv_gated = _conv_branch(u, w_taps, row(b_dw), row(conv_ln_g),
                              row(conv_ln_b), w_conv_o.astype(BF16), gate_c, batch, seq)
    attn = _moba_attention(q, k, v, k_mean, batch, seq)
    w_attn_o_b, w_out_b, w_router_t = w_attn_o.astype(BF16), w_out.astype(BF16), w_router.T
    shared_w = (w_sh_gate.astype(BF16), w_sh_up.astype(BF16), w_sh_down.astype(BF16))
    n_chunk = n // MOE_CHUNKS
    assert n_chunk * MOE_CHUNKS == n and n_chunk % TOKEN_TILE == 0 and n_chunk % POSITION_TILE == 0
    out = None
    for c in range(MOE_CHUNKS):
        first = c * n_chunk
        h, h_packed, top_idx, top_w, rank, totals = _merge_ln_router(
            attn, gate_a, conv_gated, x2, w_attn_o_b, w_out_b, row(ln1_g), row(ln1_b),
            w_router_t, router_bias.reshape(N_EXPERTS, 1), first, n_chunk)
        group_start, tile_expert, tile_valid, n_rows = _group_layout(totals, n_chunk)
        pos = _positions(top_idx, rank, group_start).reshape(TOP_K * n_chunk)
        xs = _sc_scatter_rows(h_packed, pos, n_rows)
        ys = _grouped_experts(tile_expert, tile_valid, xs, w_exp_gate, w_exp_up, w_exp_down)
        yg = _sc_gather_rows(ys, pos).reshape(TOP_K, n_chunk, HALF)
        out = _combine_ln(h, yg, top_w.T, *shared_w, row(ln2_g), row(ln2_b), out, first, n)
    return out


def kernel(x, w_in, b_gate, w_attn_o, w_dw, b_dw, conv_ln_g, conv_ln_b, w_conv_o, w_out, ln1_g, ln1_b,
           w_router, router_bias, w_exp_gate, w_exp_up, w_exp_down, w_sh_gate, w_sh_up, w_sh_down,
           ln2_g, ln2_b):
    batch, seq, d = x.shape
    assert d == D_MODEL and seq % MOBA_BLOCK == 0 and seq % TOKEN_TILE == 0
    assert w_in.shape[0] == DEPTH
    x2 = x.reshape(batch * seq, d)
    for l in range(DEPTH):
        x2 = _layer(x2, batch, seq, w_in[l], b_gate[l], w_attn_o[l], w_dw[l], b_dw[l], conv_ln_g[l],
                    conv_ln_b[l], w_conv_o[l], w_out[l], ln1_g[l], ln1_b[l], w_router[l], router_bias[l],
                    w_exp_gate[l], w_exp_up[l], w_exp_down[l], w_sh_gate[l], w_sh_up[l], w_sh_down[l],
                    ln2_g[l], ln2_b[l])
    return x2.reshape(batch, seq, d)
```

```python
import functools

import jax
import jax.numpy as jnp
from jax import lax
from jax.experimental import pallas as pl
from jax.experimental.pallas import tpu as pltpu
from jax.experimental.pallas import tpu_sc as plsc

F32 = jnp.float32
BF16 = jnp.bfloat16
NEG_INF = float("-inf")
MASKED = -1e30
LOG2_E = 1.4426950408889634

D_MODEL = 1024
N_HEADS = 8
HEAD_DIM = 128
ROPE_THETA = 500000.0
ROPE_DIM = HEAD_DIM // 4
ROPE_HALF = ROPE_DIM // 2
MOBA_BLOCK = 256
MOBA_TOPK = 3
CONV_WIDTH = 31
SUBLANES = 8
CONV_HALO = 32
CONV_ROWS = 32
N_EXPERTS = 256
TOP_K = 8
N_GROUPS = 8
GROUP_SIZE = N_EXPERTS // N_GROUPS
TOPK_GROUPS = 4
EXPERT_HIDDEN = 256
ROUTED_SCALE = 2.5
LN_EPS = 1e-5
DEPTH = 1
DEEPNORM_ALPHA = (2 * DEPTH) ** 0.25
HALF = D_MODEL // 2

TOKEN_TILE = 256
EXPERT_TILE = 512
EXPERT_CHAIN = 256
POSITION_TILE = 1024
MOE_CHUNKS = 2
SC_CORES = 2
SC_SUBCORES = 16
SC_CHUNK = 64
VMEM_LIMIT = 56 * 1024 * 1024


def _params(*semantics):
    return pltpu.CompilerParams(dimension_semantics=semantics, vmem_limit_bytes=VMEM_LIMIT)


def _sigmoid(x):
    return 1.0 / (1.0 + jnp.exp(-x))


def _layer_norm(x, g, b):
    mu = jnp.mean(x, axis=-1, keepdims=True)
    xc = x - mu
    var = jnp.mean(xc * xc, axis=-1, keepdims=True)
    return xc * lax.rsqrt(var + LN_EPS) * g + b


def _pack_bf16_halves(y):
    lo = lax.bitcast_convert_type(y[:, :HALF].astype(BF16).astype(F32), jnp.uint32)
    hi = lax.bitcast_convert_type(y[:, HALF:].astype(BF16).astype(F32), jnp.uint32)
    return (hi & jnp.uint32(0xFFFF0000)) | (lo >> 16)


def _unpack_bf16_halves(p):
    lo = lax.bitcast_convert_type(p << 16, F32)
    hi = lax.bitcast_convert_type(p & jnp.uint32(0xFFFF0000), F32)
    return lo, hi


def _inproj_kernel(x_ref, w_ref, bg_ref, cos_ref, sa_ref, sb_ref,
                   q_ref, k_ref, v_ref, u_ref, ga_ref, gc_ref, km_ref):
    tm = x_ref.shape[0]
    xb = x_ref[...].astype(BF16)

    def proj(c):
        return jnp.dot(xb, w_ref[:, c * D_MODEL:(c + 1) * D_MODEL], preferred_element_type=F32)

    cos = cos_ref[...]
    sa = sa_ref[...]
    sb = sb_ref[...]

    def rope_head(t):
        return (t * cos + pltpu.roll(t, HEAD_DIM - ROPE_HALF, 1) * sa
                + pltpu.roll(t, ROPE_HALF, 1) * sb)

    q = proj(0)
    for h in range(N_HEADS):
        sl = slice(h * HEAD_DIM, (h + 1) * HEAD_DIM)
        q_ref[:, sl] = rope_head(q[:, sl]).astype(BF16)
    k = proj(1)
    for h in range(N_HEADS):
        sl = slice(h * HEAD_DIM, (h + 1) * HEAD_DIM)
        kr = rope_head(k[:, sl])
        k_ref[:, sl] = kr.astype(BF16)
        for g in range(tm // MOBA_BLOCK):
            km_ref[g, :, sl] = jnp.mean(kr[g * MOBA_BLOCK:(g + 1) * MOBA_BLOCK], axis=0, keepdims=True)
    v_ref[...] = proj(2).astype(BF16)
    u_ref[...] = proj(3) * _sigmoid(proj(4))
    ga_ref[...] = _sigmoid(proj(5) + bg_ref[:, :D_MODEL]).astype(BF16)
    gc_ref[...] = _sigmoid(proj(6) + bg_ref[:, D_MODEL:]).astype(BF16)


def _in_projection(x2, w_in_b, b_gate, cos, sa, sb, seq):
    n = x2.shape[0]
    tm = TOKEN_TILE
    n_cols = w_in_b.shape[1]
    tiles_per_seq = seq // tm
    row = lambda i: (i, 0)
    const = lambda i: (0, 0)
    pos = lambda i: (i % tiles_per_seq, 0)
    tok_bf16 = jax.ShapeDtypeStruct((n, D_MODEL), BF16)
    return pl.pallas_call(
        _inproj_kernel,
        grid=(n // tm,),
        in_specs=[
            pl.BlockSpec((tm, D_MODEL), row),
            pl.BlockSpec((D_MODEL, n_cols), const),
            pl.BlockSpec((1, 2 * D_MODEL), const),
            pl.BlockSpec((tm, HEAD_DIM), pos),
            pl.BlockSpec((tm, HEAD_DIM), pos),
            pl.BlockSpec((tm, HEAD_DIM), pos),
        ],
        out_specs=[
            pl.BlockSpec((tm, D_MODEL), row),
            pl.BlockSpec((tm, D_MODEL), row),
            pl.BlockSpec((tm, D_MODEL), row),
            pl.BlockSpec((tm, D_MODEL), row),
            pl.BlockSpec((tm, D_MODEL), row),
            pl.BlockSpec((tm, D_MODEL), row),
            pl.BlockSpec((tm // MOBA_BLOCK, 1, D_MODEL), lambda i: (i, 0, 0)),
        ],
        out_shape=[tok_bf16, tok_bf16, tok_bf16,
                   jax.ShapeDtypeStruct((n, D_MODEL), F32),
                   tok_bf16, tok_bf16,
                   jax.ShapeDtypeStruct((n // MOBA_BLOCK, 1, D_MODEL), F32)],
        compiler_params=_params("parallel"),
        name="in_projection",
    )(x2, w_in_b, b_gate, cos, sa, sb)


def _conv_kernel(u_ref, wdw_ref, bdw_ref, lng_ref, lnb_ref, wo_ref, gc_ref, o_ref, buf_ref, sh_ref, y_ref):
    ts = u_ref.shape[0]
    s = pl.program_id(1)

    @pl.when(s == 0)
    def _():
        buf_ref[0:CONV_HALO, :] = jnp.zeros((CONV_HALO, D_MODEL), F32)

    @pl.when(s > 0)
    def _():
        buf_ref[0:CONV_HALO, :] = buf_ref[ts:ts + CONV_HALO, :]

    buf_ref[CONV_HALO:CONV_HALO + ts, :] = u_ref[...]

    span = ts + CONV_HALO - SUBLANES
    for b in range(1, SUBLANES):
        sh_ref[b - 1, 0:span, :] = buf_ref[b:b + span, :]

    base = CONV_HALO - (CONV_WIDTH - 1)
    for c in range(ts // CONV_ROWS):
        r0 = c * CONV_ROWS
        acc = jnp.zeros((CONV_ROWS // SUBLANES, SUBLANES, D_MODEL), F32)
        for j in range(CONV_WIDTH):
            shift = (base + j) % SUBLANES
            row = r0 + base + j - shift
            src = buf_ref if shift == 0 else sh_ref.at[shift - 1]
            tap = src[row:row + CONV_ROWS, :].reshape(CONV_ROWS // SUBLANES, SUBLANES, D_MODEL)
            acc = acc + tap * wdw_ref[j]
        acc = acc.reshape(CONV_ROWS, D_MODEL)
        y = _layer_norm(acc + bdw_ref[...], lng_ref[...], lnb_ref[...])
        y_ref[r0:r0 + CONV_ROWS, :] = (y * _sigmoid(y)).astype(BF16)
    z = jnp.dot(y_ref[...], wo_ref[...], preferred_element_type=F32)
    o_ref[...] = (z * gc_ref[...].astype(F32)).astype(BF16)


def _conv_branch(u, w_dw, b_dw, ln_g, ln_b, w_o_b, gate_c, batch, seq):
    n = u.shape[0]
    ts = TOKEN_TILE
    tiles_per_seq = seq // ts
    row = lambda b, s: (b * tiles_per_seq + s, 0)
    const = lambda b, s: (0, 0)
    return pl.pallas_call(
        _conv_kernel,
        grid=(batch, tiles_per_seq),
        in_specs=[
            pl.BlockSpec((ts, D_MODEL), row),
            pl.BlockSpec((CONV_WIDTH, SUBLANES, D_MODEL), lambda b, s: (0, 0, 0)),
            pl.BlockSpec((1, D_MODEL), const),
            pl.BlockSpec((1, D_MODEL), const),
            pl.BlockSpec((1, D_MODEL), const),
            pl.BlockSpec((D_MODEL, D_MODEL), const),
            pl.BlockSpec((ts, D_MODEL), row),
        ],
        out_specs=pl.BlockSpec((ts, D_MODEL), row),
        out_shape=jax.ShapeDtypeStruct((n, D_MODEL), BF16),
        scratch_shapes=[pltpu.VMEM((ts + CONV_HALO, D_MODEL), F32),
                        pltpu.VMEM((SUBLANES - 1, ts + CONV_HALO - SUBLANES, D_MODEL), F32),
                        pltpu.VMEM((ts, D_MODEL), BF16)],
        compiler_params=_params("parallel", "arbitrary"),
        name="conv_branch",
    )(u, w_dw, b_dw, ln_g, ln_b, w_o_b, gate_c)


def _attn_kernel(q_ref, k_ref, v_ref, km_ref, o_ref, *, n_blk):
    blk = MOBA_BLOCK
    seq = n_blk * blk
    k_sel = min(MOBA_TOPK, n_blk)
    exp2_scale = HEAD_DIM ** -0.5 * LOG2_E
    nt_dims = (((1,), (1,)), ((), ()))
    n_lane = HEAD_DIM

    assert blk & (blk - 1) == 0
    blk_shift = blk.bit_length() - 1
    n_sub = -(-n_blk // SUBLANES) * SUBLANES

    q_all = q_ref[...]
    km = km_ref[:, 0, :]
    if n_sub > n_blk:
        km = jnp.concatenate([km, jnp.zeros((n_sub - n_blk, HEAD_DIM), F32)], axis=0)
    km_hi = km.astype(BF16)
    km_lo = (km - km_hi.astype(F32)).astype(BF16)

    gate = (lax.dot_general(km_hi, q_all, nt_dims, preferred_element_type=F32)
            + lax.dot_general(km_lo, q_all, nt_dims, preferred_element_type=F32))
    blk_t = lax.broadcasted_iota(jnp.int32, (n_sub, seq), 0)
    own_t = lax.broadcasted_iota(jnp.int32, (n_sub, seq), 1) >> blk_shift
    past = blk_t < own_t
    blk_f = blk_t.astype(F32)
    g = jnp.where(past, gate, NEG_INF)
    sel = jnp.zeros((n_sub, seq), F32)
    for _ in range(k_sel):
        mx = jnp.max(g, axis=0, keepdims=True)
        first = jnp.min(jnp.where(g == mx, blk_f, float(n_sub)), axis=0, keepdims=True)
        pick = blk_f == first
        sel = jnp.where(pick, 1.0, sel)
        g = jnp.where(pick, NEG_INF, g)
    visible = ((sel > 0.0) & past) | (blk_t == own_t)
    bias_t = jnp.concatenate([jnp.where(visible, 0.0, MASKED), jnp.zeros((n_lane - n_sub, seq), F32)], axis=0)
    q_bias = bias_t.T.astype(BF16)
    blk_id = lax.broadcasted_iota(jnp.int32, (seq, n_lane), 1)
    own_id = lax.broadcasted_iota(jnp.int32, (seq, n_lane), 0) >> blk_shift
    k_blk = jnp.where(blk_id == own_id, 1.0, 0.0).astype(BF16)
    k_aug = jnp.concatenate([k_ref[...], k_blk], axis=1)

    causal = (lax.broadcasted_iota(jnp.int32, (blk, blk), 1)
              <= lax.broadcasted_iota(jnp.int32, (blk, blk), 0))

    def scores(i):
        rows = slice(i * blk, (i + 1) * blk)
        q_aug = jnp.concatenate([q_all[rows], q_bias[rows]], axis=1)
        return lax.dot_general(q_aug, k_aug[:(i + 1) * blk], nt_dims, preferred_element_type=F32)

    raw_next = scores(0)
    for i in range(n_blk):
        rows = slice(i * blk, (i + 1) * blk)
        width = (i + 1) * blk
        raw = raw_next
        if i + 1 < n_blk:
            raw_next = scores(i + 1)
        own = jnp.where(causal, raw[:, i * blk:], MASKED)
        parts = [raw[:, :i * blk], own] if i else [own]
        m = jnp.max(own, axis=1, keepdims=True)
        if i:
            m = jnp.maximum(m, jnp.max(parts[0], axis=1, keepdims=True))
        p = [jnp.exp2((t - m) * exp2_scale) for t in parts]
        l = sum(jnp.sum(t, axis=1, keepdims=True) for t in p)
        pv = jnp.dot(jnp.concatenate(p, axis=1).astype(BF16), v_ref[:width, :], preferred_element_type=F32)
        o_ref[rows, :] = (pv / l).astype(BF16)


def _moba_attention(q, k, v, k_mean, batch, seq):
    n = q.shape[0]
    n_blk = seq // MOBA_BLOCK
    seq_head = lambda b, h: (b, h)
    return pl.pallas_call(
        functools.partial(_attn_kernel, n_blk=n_blk),
        grid=(batch, N_HEADS),
        in_specs=[
            pl.BlockSpec((seq, HEAD_DIM), seq_head),
            pl.BlockSpec((seq, HEAD_DIM), seq_head),
            pl.BlockSpec((seq, HEAD_DIM), seq_head),
            pl.BlockSpec((n_blk, 1, HEAD_DIM), lambda b, h: (b, 0, h)),
        ],
        out_specs=pl.BlockSpec((seq, HEAD_DIM), seq_head),
        out_shape=jax.ShapeDtypeStruct((n, D_MODEL), BF16),
        compiler_params=_params("parallel", "parallel"),
        name="moba_attention",
    )(q, k, v, k_mean)


def _merge_kernel(attn_ref, ga_ref, cg_ref, x_ref, wao_ref, wout_ref, g1_ref, b1_ref,
                  wr_ref, rb_ref, h_ref, hp_ref, idx_ref, wgt_ref, rank_ref, total_ref, count_ref, hprev_ref):
    tm = x_ref.shape[0]
    step = pl.program_id(0)

    @pl.when(step == 0)
    def _():
        count_ref[...] = jnp.zeros((N_EXPERTS, 1), F32)
        hprev_ref[...] = jnp.zeros(hprev_ref.shape, F32)

    h = hprev_ref[...]

    a = jnp.dot(attn_ref[...], wao_ref[...], preferred_element_type=F32)
    merged = ga_ref[...].astype(F32) * a + cg_ref[...].astype(F32)
    y = jnp.dot(merged.astype(BF16), wout_ref[...], preferred_element_type=F32)
    h_new = _layer_norm(DEEPNORM_ALPHA * x_ref[...] + y, g1_ref[...], b1_ref[...])
    h_ref[...] = h_new
    hp_ref[...] = _pack_bf16_halves(h_new)
    hprev_ref[...] = h_new

    nt_dims = (((1,), (1,)), ((), ()))
    h_hi = h.astype(BF16)
    h_lo = (h - h_hi.astype(F32)).astype(BF16)
    w = wr_ref[...]
    w_hi = w.astype(BF16)
    w_lo = (w - w_hi.astype(F32)).astype(BF16)
    logits = (lax.dot_general(w_hi, h_hi, nt_dims, preferred_element_type=F32)
              + lax.dot_general(w_hi, h_lo, nt_dims, preferred_element_type=F32)
              + lax.dot_general(w_lo, h_hi, nt_dims, preferred_element_type=F32))
    scores = _sigmoid(logits)
    biased = scores + rb_ref[...]

    g3 = biased.reshape(N_GROUPS, GROUP_SIZE, tm)
    m1 = jnp.max(g3, axis=1, keepdims=True)
    is_max = g3 == m1
    n_max = jnp.sum(jnp.where(is_max, 1.0, 0.0), axis=1, keepdims=True)
    m2 = jnp.max(jnp.where(is_max, NEG_INF, g3), axis=1, keepdims=True)
    grp = (m1 + jnp.where(n_max >= 2.0, m1, m2)).reshape(N_GROUPS, tm)

    gid = lax.broadcasted_iota(jnp.int32, (N_GROUPS, tm), 0)
    rank = jnp.zeros((N_GROUPS, tm), F32)
    for o in range(N_GROUPS):
        other = grp[o:o + 1, :]
        ahead = (other > grp) | ((other == grp) & (o < gid))
        rank = rank + jnp.where(ahead, 1.0, 0.0)
    grp_keep = jnp.where(rank < float(TOPK_GROUPS), 1.0, 0.0)
    keep = jnp.broadcast_to(grp_keep.reshape(N_GROUPS, 1, tm),
                            (N_GROUPS, GROUP_SIZE, tm)).reshape(N_EXPERTS, tm)
    cand = jnp.where(keep > 0.0, biased, NEG_INF)

    eid = lax.broadcasted_iota(jnp.int32, (N_EXPERTS, tm), 0).astype(F32)
    chosen = jnp.zeros((N_EXPERTS, tm), F32)
    firsts = []
    for r in range(TOP_K):
        mx = jnp.max(cand, axis=0, keepdims=True)
        first = jnp.min(jnp.where(cand == mx, eid, float(N_EXPERTS)), axis=0, keepdims=True)
        pick = eid == first
        firsts.append(first)
        idx_ref[r:r + 1, :] = first.astype(jnp.int32)
        wgt_ref[r:r + 1, :] = jnp.sum(jnp.where(pick, scores, 0.0), axis=0, keepdims=True)
        cand = jnp.where(pick, NEG_INF, cand)
        chosen = jnp.where(pick, 1.0, chosen)
    top_s = wgt_ref[...]
    wgt_ref[...] = top_s / (jnp.sum(top_s, axis=0, keepdims=True) + 1e-20) * ROUTED_SCALE

    earlier = (lax.broadcasted_iota(jnp.int32, (tm, tm), 0)
               < lax.broadcasted_iota(jnp.int32, (tm, tm), 1))
    before = jnp.dot(chosen.astype(BF16), jnp.where(earlier, 1.0, 0.0).astype(BF16),
                     preferred_element_type=F32) + count_ref[...]
    for r in range(TOP_K):
        rank_ref[r:r + 1, :] = jnp.sum(jnp.where(eid == firsts[r], before, 0.0),
                                       axis=0, keepdims=True).astype(jnp.int32)
    counted = jnp.where(step > 0, 1.0, 0.0)
    total = count_ref[...] + counted * jnp.sum(chosen, axis=1, keepdims=True)
    count_ref[...] = total
    total_ref[...] = total


def _merge_ln_router(attn, gate_a, conv_gated, x2, w_attn_o_b, w_out_b, ln_g, ln_b, w_router_t, router_bias,
                     first_token, n):
    tm = TOKEN_TILE
    first_tile = first_token // tm
    n_tiles = n // tm
    row_in = lambda i: (jnp.minimum(i, n_tiles - 1) + first_tile, 0)
    row = lambda i: (jnp.minimum(i, n_tiles - 1), 0)
    const = lambda i: (0, 0)
    col = lambda i: (0, jnp.maximum(i - 1, 0))
    return pl.pallas_call(
        _merge_kernel,
        grid=(n_tiles + 1,),
        in_specs=[
            pl.BlockSpec((tm, D_MODEL), row_in),
            pl.BlockSpec((tm, D_MODEL), row_in),
            pl.BlockSpec((tm, D_MODEL), row_in),
            pl.BlockSpec((tm, D_MODEL), row_in),
            pl.BlockSpec((D_MODEL, D_MODEL), const),
            pl.BlockSpec((D_MODEL, D_MODEL), const),
            pl.BlockSpec((1, D_MODEL), const),
            pl.BlockSpec((1, D_MODEL), const),
            pl.BlockSpec((N_EXPERTS, D_MODEL), const),
            pl.BlockSpec((N_EXPERTS, 1), const),
        ],
        out_specs=[
            pl.BlockSpec((tm, D_MODEL), row),
            pl.BlockSpec((tm, HALF), row),
            pl.BlockSpec((TOP_K, tm), col),
            pl.BlockSpec((TOP_K, tm), col),
            pl.BlockSpec((TOP_K, tm), col),
            pl.BlockSpec((N_EXPERTS, 1), const),
        ],
        out_shape=[
            jax.ShapeDtypeStruct((n, D_MODEL), F32),
            jax.ShapeDtypeStruct((n, HALF), jnp.uint32),
            jax.ShapeDtypeStruct((TOP_K, n), jnp.int32),
            jax.ShapeDtypeStruct((TOP_K, n), F32),
            jax.ShapeDtypeStruct((TOP_K, n), jnp.int32),
            jax.ShapeDtypeStruct((N_EXPERTS, 1), F32),
        ],
        scratch_shapes=[pltpu.VMEM((N_EXPERTS, 1), F32), pltpu.VMEM((tm, D_MODEL), F32)],
        compiler_params=_params("arbitrary"),
        name="merge_ln_router",
    )(attn, gate_a, conv_gated, x2, w_attn_o_b, w_out_b, ln_g, ln_b, w_router_t, router_bias)


def _sc_mesh():
    return plsc.VectorSubcoreMesh(core_axis_name="c", subcore_axis_name="s",
                                  num_cores=SC_CORES, num_subcores=SC_SUBCORES)


def _sc_worker_base(rows_per_worker):
    return (lax.axis_index("s") * SC_CORES + lax.axis_index("c")) * rows_per_worker


def _sc_gather_rows(table, idx):
    m = idx.shape[0]
    width = table.shape[1]
    per_worker = m // (SC_CORES * SC_SUBCORES)
    assert per_worker * SC_CORES * SC_SUBCORES == m and per_worker % SC_CHUNK == 0

    @functools.partial(
        pl.kernel, mesh=_sc_mesh(),
        out_type=jax.ShapeDtypeStruct((m, width), table.dtype),
        scratch_types=[pltpu.VMEM((SC_CHUNK,), jnp.int32),
                       pltpu.VMEM((SC_CHUNK, width), table.dtype),
                       pltpu.SemaphoreType.DMA],
        name="sc_gather_rows")
    def gather(table_hbm, idx_hbm, out_hbm, idx_v, rows_v, sem):
        base = _sc_worker_base(per_worker)

        @pl.loop(0, per_worker // SC_CHUNK)
        def _(c):
            off = pl.multiple_of(base + c * SC_CHUNK, SC_CHUNK)
            pltpu.sync_copy(idx_hbm.at[pl.ds(off, SC_CHUNK)], idx_v)
            pltpu.async_copy(table_hbm.at[idx_v], rows_v, sem).wait()
            pltpu.sync_copy(rows_v, out_hbm.at[pl.ds(off, SC_CHUNK)])

    return gather(table, idx)


def _sc_scatter_rows(rows, pos, n_out):
    n, width = rows.shape
    per_worker = n // (SC_CORES * SC_SUBCORES)
    assert per_worker * SC_CORES * SC_SUBCORES == n and per_worker % SC_CHUNK == 0

    @functools.partial(
        pl.kernel, mesh=_sc_mesh(),
        out_type=jax.ShapeDtypeStruct((n_out, width), rows.dtype),
        scratch_types=[pltpu.VMEM((SC_CHUNK,), jnp.int32),
                       pltpu.VMEM((SC_CHUNK, width), rows.dtype),
                       pltpu.SemaphoreType.DMA],
        name="sc_scatter_rows")
    def scatter(rows_hbm, pos_hbm, out_hbm, idx_v, rows_v, sem):
        base = _sc_worker_base(per_worker)

        @pl.loop(0, per_worker // SC_CHUNK)
        def _(c):
            off = pl.multiple_of(base + c * SC_CHUNK, SC_CHUNK)
            pltpu.sync_copy(rows_hbm.at[pl.ds(off, SC_CHUNK)], rows_v)
            for r in range(TOP_K):
                pltpu.sync_copy(pos_hbm.at[pl.ds(r * n + off, SC_CHUNK)], idx_v)
                pltpu.async_copy(rows_v, out_hbm.at[idx_v], sem).wait()

    return scatter(rows, pos)


def _position_kernel(idx_ref, rank_ref, start_ref, pos_ref):
    tl = idx_ref.shape[1]
    eid = lax.broadcasted_iota(jnp.int32, (N_EXPERTS, tl), 0)
    start = start_ref[...]
    for r in range(TOP_K):
        here = jnp.sum(jnp.where(eid == idx_ref[r:r + 1, :], start, 0.0), axis=0, keepdims=True)
        pos_ref[r:r + 1, :] = here.astype(jnp.int32) + rank_ref[r:r + 1, :]


def _positions(top_idx, rank, group_start):
    n = top_idx.shape[1]
    tl = POSITION_TILE
    col = lambda i: (0, i)
    return pl.pallas_call(
        _position_kernel,
        grid=(n // tl,),
        in_specs=[pl.BlockSpec((TOP_K, tl), col), pl.BlockSpec((TOP_K, tl), col),
                  pl.BlockSpec((N_EXPERTS, 1), lambda i: (0, 0))],
        out_specs=pl.BlockSpec((TOP_K, tl), col),
        out_shape=jax.ShapeDtypeStruct((TOP_K, n), jnp.int32),
        compiler_params=_params("parallel"),
        name="positions",
    )(top_idx, rank, group_start)


def _expert_kernel(te_ref, tv_ref, xs_ref, wg_ref, wu_ref, wd_ref, y_ref, act_ref):
    i = pl.program_id(0)
    valid = tv_ref[i]
    valid_prev = tv_ref[jnp.maximum(i - 1, 0)]

    @pl.when(i == 0)
    def _():
        act_ref[...] = jnp.zeros(act_ref.shape, BF16)

    @pl.when((valid > 0) | (valid_prev > 0))
    def _():
        act_prev = act_ref[...]
        wd = wd_ref[0].astype(BF16)
        for c in range(EXPERT_TILE // EXPERT_CHAIN):
            rows = slice(c * EXPERT_CHAIN, (c + 1) * EXPERT_CHAIN)
            y_ref[rows, :] = _pack_bf16_halves(jnp.dot(act_prev[rows], wd, preferred_element_type=F32))
        wg = wg_ref[0].astype(BF16)
        wu = wu_ref[0].astype(BF16)
        for c in range(EXPERT_TILE // EXPERT_CHAIN):
            rows = slice(c * EXPERT_CHAIN, (c + 1) * EXPERT_CHAIN)
            live = lax.broadcasted_iota(jnp.int32, (EXPERT_CHAIN, HALF), 0) < valid - c * EXPERT_CHAIN
            lo, hi = _unpack_bf16_halves(jnp.where(live, xs_ref[rows, :], jnp.uint32(0)))
            x = jnp.concatenate([lo, hi], axis=1).astype(BF16)
            g = jnp.dot(x, wg, preferred_element_type=F32)
            u = jnp.dot(x, wu, preferred_element_type=F32)
            act_ref[rows, :] = (g * _sigmoid(g) * u).astype(BF16)

    @pl.when((valid == 0) & (valid_prev == 0))
    def _():
        y_ref[...] = jnp.zeros(y_ref.shape, jnp.uint32)


def _grouped_experts(tile_expert, tile_valid, xs, wg, wu, wd):
    p = xs.shape[0]
    t = EXPERT_TILE
    n_tiles = p // t
    tile_expert = jnp.concatenate([tile_expert, tile_expert[-1:]])
    tile_valid = jnp.concatenate([tile_valid, jnp.zeros((1,), jnp.int32)])
    row_in = lambda i, te, tv: (jnp.minimum(i, n_tiles - 1), 0)
    row_out = lambda i, te, tv: (jnp.maximum(i - 1, 0), 0)
    expert = lambda i, te, tv: (te[i], 0, 0)
    expert_prev = lambda i, te, tv: (te[jnp.maximum(i - 1, 0)], 0, 0)
    return pl.pallas_call(
        _expert_kernel,
        grid_spec=pltpu.PrefetchScalarGridSpec(
            num_scalar_prefetch=2,
            grid=(n_tiles + 1,),
            in_specs=[
                pl.BlockSpec((t, HALF), row_in),
                pl.BlockSpec((1, D_MODEL, EXPERT_HIDDEN), expert),
                pl.BlockSpec((1, D_MODEL, EXPERT_HIDDEN), expert),
                pl.BlockSpec((1, EXPERT_HIDDEN, D_MODEL), expert_prev),
            ],
            out_specs=pl.BlockSpec((t, HALF), row_out),
            scratch_shapes=[pltpu.VMEM((t, EXPERT_HIDDEN), BF16)],
        ),
        out_shape=jax.ShapeDtypeStruct((p, HALF), jnp.uint32),
        compiler_params=_params("arbitrary"),
        name="grouped_experts",
    )(tile_expert, tile_valid, xs, wg, wu, wd)


def _combine_kernel(h_ref, yg_ref, wt_ref, wsg_ref, wsu_ref, wsd_ref, g2_ref, b2_ref, *out_refs):
    o_ref = out_refs[-1]
    h = h_ref[...]
    hb = h.astype(BF16)
    g = jnp.dot(hb, wsg_ref[...], preferred_element_type=F32)
    u = jnp.dot(hb, wsu_ref[...], preferred_element_type=F32)
    shared = jnp.dot((g * _sigmoid(g) * u).astype(BF16), wsd_ref[...], preferred_element_type=F32)
    wt = wt_ref[...]
    r_lo = jnp.zeros((h.shape[0], HALF), F32)
    r_hi = jnp.zeros((h.shape[0], HALF), F32)
    for r in range(TOP_K):
        lo, hi = _unpack_bf16_halves(yg_ref[r])
        w = wt[:, r:r + 1]
        r_lo = r_lo + lo * w
        r_hi = r_hi + hi * w
    routed = jnp.concatenate([r_lo, r_hi], axis=1)
    o_ref[...] = _layer_norm(DEEPNORM_ALPHA * h + (shared + routed), g2_ref[...], b2_ref[...])


def _combine_ln(h, yg, w_tok, wsg_b, wsu_b, wsd_b, ln_g, ln_b, out_so_far, first_token, n_total):
    n = h.shape[0]
    tm = TOKEN_TILE
    first_tile = first_token // tm
    row = lambda i: (i, 0)
    const = lambda i: (0, 0)
    hidden = wsg_b.shape[1]
    in_specs = [
        pl.BlockSpec((tm, D_MODEL), row),
        pl.BlockSpec((TOP_K, tm, HALF), lambda i: (0, i, 0)),
        pl.BlockSpec((tm, TOP_K), row),
        pl.BlockSpec((D_MODEL, hidden), const),
        pl.BlockSpec((D_MODEL, hidden), const),
        pl.BlockSpec((hidden, D_MODEL), const),
        pl.BlockSpec((1, D_MODEL), const),
        pl.BlockSpec((1, D_MODEL), const),
    ]
    args = [h, yg, w_tok, wsg_b, wsu_b, wsd_b, ln_g, ln_b]
    aliases = {}
    if out_so_far is not None:
        in_specs.append(pl.BlockSpec(memory_space=pl.ANY))
        args.append(out_so_far)
        aliases = {len(args) - 1: 0}
    return pl.pallas_call(
        _combine_kernel,
        grid=(n // tm,),
        in_specs=in_specs,
        out_specs=pl.BlockSpec((tm, D_MODEL), lambda i: (i + first_tile, 0)),
        out_shape=jax.ShapeDtypeStruct((n_total, D_MODEL), F32),
        input_output_aliases=aliases,
        compiler_params=_params("parallel"),
        name="combine_ln",
    )(*args)


def _group_layout(totals, n_tokens):
    t = EXPERT_TILE
    n_tiles = (TOP_K * n_tokens + N_EXPERTS * (t - 1)) // t
    counts = totals[:, 0].astype(jnp.int32)
    padded = ((counts + t - 1) // t) * t
    group_end = jnp.cumsum(padded)
    group_start = group_end - padded
    tile_start = jnp.arange(n_tiles, dtype=jnp.int32) * t
    tile_expert = jnp.minimum(jnp.sum((group_end[None, :] <= tile_start[:, None]).astype(jnp.int32), axis=1),
                              N_EXPERTS - 1)
    of_tile = tile_expert[:, None] == jnp.arange(N_EXPERTS, dtype=jnp.int32)[None, :]
    real_end = jnp.sum(jnp.where(of_tile, (group_start + counts)[None, :], 0), axis=1)
    tile_valid = jnp.clip(real_end - tile_start, 0, t).astype(jnp.int32)
    return group_start.astype(F32).reshape(N_EXPERTS, 1), tile_expert, tile_valid, n_tiles * t


def _rope_tables(seq):
    inv_freq = ROPE_THETA ** (-jnp.arange(0, ROPE_DIM, 2, dtype=F32) / ROPE_DIM)
    ang = jnp.arange(seq).astype(F32)[:, None] * inv_freq[None, :]
    cos, sin = jnp.cos(ang), jnp.sin(ang)
    rest = HEAD_DIM - ROPE_DIM
    zeros = jnp.zeros((seq, ROPE_HALF), F32)
    cos_t = jnp.concatenate([cos, cos, jnp.ones((seq, rest), F32)], axis=1)
    sa_t = jnp.concatenate([-sin, zeros, jnp.zeros((seq, rest), F32)], axis=1)
    sb_t = jnp.concatenate([zeros, sin, jnp.zeros((seq, rest), F32)], axis=1)
    return cos_t, sa_t, sb_t


def _layer(x2, batch, seq, w_in, b_gate, w_attn_o, w_dw, b_dw, conv_ln_g, conv_ln_b, w_conv_o, w_out,
           ln1_g, ln1_b, w_router, router_bias, w_exp_gate, w_exp_up, w_exp_down,
           w_sh_gate, w_sh_up, w_sh_down, ln2_g, ln2_b):
    n = x2.shape[0]
    row = lambda v: v.reshape(1, -1)
    cos, sa, sb = _rope_tables(seq)
    q, k, v, u, gate_a, gate_c, k_mean = _in_projection(
        x2, w_in.astype(BF16), row(b_gate), cos, sa, sb, seq)
    w_taps = jnp.broadcast_to(w_dw.reshape(CONV_WIDTH, 1, D_MODEL), (CONV_WIDTH, SUBLANES, D_MODEL))
    conv_gated = _conv_branch(u, w_taps, row(b_dw), row(conv_ln_g),
                              row(conv_ln_b), w_conv_o.astype(BF16), gate_c, batch, seq)
    attn = _moba_attention(q, k, v, k_mean, batch, seq)
    w_attn_o_b, w_out_b, w_router_t = w_attn_o.astype(BF16), w_out.astype(BF16), w_router.T
    shared_w = (w_sh_gate.astype(BF16), w_sh_up.astype(BF16), w_sh_down.astype(BF16))
    n_chunk = n // MOE_CHUNKS
    assert n_chunk * MOE_CHUNKS == n and n_chunk % TOKEN_TILE == 0 and n_chunk % POSITION_TILE == 0
    out = None
    for c in range(MOE_CHUNKS):
        first = c * n_chunk
        h, h_packed, top_idx, top_w, rank, totals = _merge_ln_router(
            attn, gate_a, conv_gated, x2, w_attn_o_b, w_out_b, row(ln1_g), row(ln1_b),
            w_router_t, router_bias.reshape(N_EXPERTS, 1), first, n_chunk)
        group_start, tile_expert, tile_valid, n_rows = _group_layout(totals, n_chunk)
        pos = _positions(top_idx, rank, group_start).reshape(TOP_K * n_chunk)
        xs = _sc_scatter_rows(h_packed, pos, n_rows)
        ys = _grouped_experts(tile_expert, tile_valid, xs, w_exp_gate, w_exp_up, w_exp_down)
        yg = _sc_gather_rows(ys, pos).reshape(TOP_K, n_chunk, HALF)
        out = _combine_ln(h, yg, top_w.T, *shared_w, row(ln2_g), row(ln2_b), out, first, n)
    return out


def kernel(x, w_in, b_gate, w_attn_o, w_dw, b_dw, conv_ln_g, conv_ln_b, w_conv_o, w_out, ln1_g, ln1_b,
           w_router, router_bias, w_exp_gate, w_exp_up, w_exp_down, w_sh_gate, w_sh_up, w_sh_down,
           ln2_g, ln2_b):
    batch, seq, d = x.shape
    assert d == D_MODEL and seq % MOBA_BLOCK == 0 and seq % TOKEN_TILE == 0
    assert w_in.shape[0] == DEPTH
    x2 = x.reshape(batch * seq, d)
    for l in range(DEPTH):
        x2 = _layer(x2, batch, seq, w_in[l], b_gate[l], w_attn_o[l], w_dw[l], b_dw[l], conv_ln_g[l],
                    conv_ln_b[l], w_conv_o[l], w_out[l], ln1_g[l], ln1_b[l], w_router[l], router_bias[l],
                    w_exp_gate[l], w_exp_up[l], w_exp_down[l], w_sh_gate[l], w_sh_up[l], w_sh_down[l],
                    ln2_g[l], ln2_b[l])
    return x2.reshape(batch, seq, d)
```

```python
import functools

import jax
import jax.numpy as jnp
from jax import lax
from jax.experimental import pallas as pl
from jax.experimental.pallas import tpu as pltpu
from jax.experimental.pallas import tpu_sc as plsc

F32 = jnp.float32
BF16 = jnp.bfloat16
NEG_INF = float("-inf")
MASKED = -1e30
LOG2_E = 1.4426950408889634

D_MODEL = 1024
N_HEADS = 8
HEAD_DIM = 128
ROPE_THETA = 500000.0
ROPE_DIM = HEAD_DIM // 4
ROPE_HALF = ROPE_DIM // 2
MOBA_BLOCK = 256
MOBA_TOPK = 3
CONV_WIDTH = 31
SUBLANES = 8
CONV_HALO = 32
CONV_ROWS = 32
N_EXPERTS = 256
TOP_K = 8
N_GROUPS = 8
GROUP_SIZE = N_EXPERTS // N_GROUPS
TOPK_GROUPS = 4
EXPERT_HIDDEN = 256
ROUTED_SCALE = 2.5
LN_EPS = 1e-5
DEPTH = 1
DEEPNORM_ALPHA = (2 * DEPTH) ** 0.25
HALF = D_MODEL // 2

TOKEN_TILE = 256
EXPERT_TILE = 512
EXPERT_CHAIN = 256
POSITION_TILE = 1024
MOE_CHUNKS = 2
SC_CORES = 2
SC_SUBCORES = 16
SC_CHUNK = 64
VMEM_LIMIT = 56 * 1024 * 1024


def _params(*semantics):
    return pltpu.CompilerParams(dimension_semantics=semantics, vmem_limit_bytes=VMEM_LIMIT)


def _sigmoid(x):
    return 1.0 / (1.0 + jnp.exp(-x))


def _layer_norm(x, g, b):
    mu = jnp.mean(x, axis=-1, keepdims=True)
    xc = x - mu
    var = jnp.mean(xc * xc, axis=-1, keepdims=True)
    return xc * lax.rsqrt(var + LN_EPS) * g + b


def _pack_bf16_halves(y):
    lo = lax.bitcast_convert_type(y[:, :HALF].astype(BF16).astype(F32), jnp.uint32)
    hi = lax.bitcast_convert_type(y[:, HALF:].astype(BF16).astype(F32), jnp.uint32)
    return (hi & jnp.uint32(0xFFFF0000)) | (lo >> 16)


def _unpack_bf16_halves(p):
    lo = lax.bitcast_convert_type(p << 16, F32)
    hi = lax.bitcast_convert_type(p & jnp.uint32(0xFFFF0000), F32)
    return lo, hi


def _inproj_kernel(x_ref, w_ref, bg_ref, cos_ref, sa_ref, sb_ref, wg_ref, wu_ref, wd_ref,
                   q_ref, k_ref, v_ref, u_ref, ga_ref, gc_ref, km_ref, wgb_ref, wub_ref, wdb_ref):
    wgb_ref[...] = wg_ref[...].astype(BF16)
    wub_ref[...] = wu_ref[...].astype(BF16)
    wdb_ref[...] = wd_ref[...].astype(BF16)

    tm = x_ref.shape[0]
    xb = x_ref[...].astype(BF16)

    def proj(c):
        return jnp.dot(xb, w_ref[:, c * D_MODEL:(c + 1) * D_MODEL], preferred_element_type=F32)

    cos = cos_ref[...]
    sa = sa_ref[...]
    sb = sb_ref[...]

    def rope_head(t):
        return (t * cos + pltpu.roll(t, HEAD_DIM - ROPE_HALF, 1) * sa
                + pltpu.roll(t, ROPE_HALF, 1) * sb)

    q = proj(0)
    for h in range(N_HEADS):
        sl = slice(h * HEAD_DIM, (h + 1) * HEAD_DIM)
        q_ref[:, sl] = rope_head(q[:, sl]).astype(BF16)
    k = proj(1)
    for h in range(N_HEADS):
        sl = slice(h * HEAD_DIM, (h + 1) * HEAD_DIM)
        kr = rope_head(k[:, sl])
        k_ref[:, sl] = kr.astype(BF16)
        for g in range(tm // MOBA_BLOCK):
            km_ref[g, :, sl] = jnp.mean(kr[g * MOBA_BLOCK:(g + 1) * MOBA_BLOCK], axis=0, keepdims=True)
    v_ref[...] = proj(2).astype(BF16)
    u_ref[...] = proj(3) * _sigmoid(proj(4))
    ga_ref[...] = _sigmoid(proj(5) + bg_ref[:, :D_MODEL]).astype(BF16)
    gc_ref[...] = _sigmoid(proj(6) + bg_ref[:, D_MODEL:]).astype(BF16)


def _in_projection(x2, w_in_b, b_gate, cos, sa, sb, w_exp_gate, w_exp_up, w_exp_down, seq):
    n = x2.shape[0]
    tm = TOKEN_TILE
    n_cols = w_in_b.shape[1]
    tiles_per_seq = seq // tm
    steps = n // tm
    per_step = -(-N_EXPERTS // steps)
    assert N_EXPERTS % per_step == 0
    row = lambda i: (i, 0)
    const = lambda i: (0, 0)
    pos = lambda i: (i % tiles_per_seq, 0)
    experts = lambda i: (jnp.minimum(i, N_EXPERTS // per_step - 1), 0, 0)
    tok_bf16 = jax.ShapeDtypeStruct((n, D_MODEL), BF16)
    up_block = (per_step, D_MODEL, EXPERT_HIDDEN)
    down_block = (per_step, EXPERT_HIDDEN, D_MODEL)
    return pl.pallas_call(
        _inproj_kernel,
        grid=(steps,),
        in_specs=[
            pl.BlockSpec((tm, D_MODEL), row),
            pl.BlockSpec((D_MODEL, n_cols), const),
            pl.BlockSpec((1, 2 * D_MODEL), const),
            pl.BlockSpec((tm, HEAD_DIM), pos),
            pl.BlockSpec((tm, HEAD_DIM), pos),
            pl.BlockSpec((tm, HEAD_DIM), pos),
            pl.BlockSpec(up_block, experts),
            pl.BlockSpec(up_block, experts),
            pl.BlockSpec(down_block, experts),
        ],
        out_specs=[
            pl.BlockSpec((tm, D_MODEL), row),
            pl.BlockSpec((tm, D_MODEL), row),
            pl.BlockSpec((tm, D_MODEL), row),
            pl.BlockSpec((tm, D_MODEL), row),
            pl.BlockSpec((tm, D_MODEL), row),
            pl.BlockSpec((tm, D_MODEL), row),
            pl.BlockSpec((tm // MOBA_BLOCK, 1, D_MODEL), lambda i: (i, 0, 0)),
            pl.BlockSpec(up_block, experts),
            pl.BlockSpec(up_block, experts),
            pl.BlockSpec(down_block, experts),
        ],
        out_shape=[tok_bf16, tok_bf16, tok_bf16,
                   jax.ShapeDtypeStruct((n, D_MODEL), F32),
                   tok_bf16, tok_bf16,
                   jax.ShapeDtypeStruct((n // MOBA_BLOCK, 1, D_MODEL), F32),
                   jax.ShapeDtypeStruct(w_exp_gate.shape, BF16),
                   jax.ShapeDtypeStruct(w_exp_up.shape, BF16),
                   jax.ShapeDtypeStruct(w_exp_down.shape, BF16)],
        compiler_params=_params("arbitrary"),
        name="in_projection",
    )(x2, w_in_b, b_gate, cos, sa, sb, w_exp_gate, w_exp_up, w_exp_down)


def _conv_kernel(u_ref, wdw_ref, bdw_ref, lng_ref, lnb_ref, wo_ref, gc_ref, o_ref, buf_ref, sh_ref, y_ref):
    ts = u_ref.shape[0]
    s = pl.program_id(1)

    @pl.when(s == 0)
    def _():
        buf_ref[0:CONV_HALO, :] = jnp.zeros((CONV_HALO, D_MODEL), F32)

    @pl.when(s > 0)
    def _():
        buf_ref[0:CONV_HALO, :] = buf_ref[ts:ts + CONV_HALO, :]

    buf_ref[CONV_HALO:CONV_HALO + ts, :] = u_ref[...]

    span = ts + CONV_HALO - SUBLANES
    for b in range(1, SUBLANES):
        sh_ref[b - 1, 0:span, :] = buf_ref[b:b + span, :]

    base = CONV_HALO - (CONV_WIDTH - 1)
    for c in range(ts // CONV_ROWS):
        r0 = c * CONV_ROWS
        acc = jnp.zeros((CONV_ROWS // SUBLANES, SUBLANES, D_MODEL), F32)
        for j in range(CONV_WIDTH):
            shift = (base + j) % SUBLANES
            row = r0 + base + j - shift
            src = buf_ref if shift == 0 else sh_ref.at[shift - 1]
            tap = src[row:row + CONV_ROWS, :].reshape(CONV_ROWS // SUBLANES, SUBLANES, D_MODEL)
            acc = acc + tap * wdw_ref[j]
        acc = acc.reshape(CONV_ROWS, D_MODEL)
        y = _layer_norm(acc + bdw_ref[...], lng_ref[...], lnb_ref[...])
        y_ref[r0:r0 + CONV_ROWS, :] = (y * _sigmoid(y)).astype(BF16)
    z = jnp.dot(y_ref[...], wo_ref[...], preferred_element_type=F32)
    o_ref[...] = (z * gc_ref[...].astype(F32)).astype(BF16)


def _conv_branch(u, w_dw, b_dw, ln_g, ln_b, w_o_b, gate_c, batch, seq):
    n = u.shape[0]
    ts = TOKEN_TILE
    tiles_per_seq = seq // ts
    row = lambda b, s: (b * tiles_per_seq + s, 0)
    const = lambda b, s: (0, 0)
    return pl.pallas_call(
        _conv_kernel,
        grid=(batch, tiles_per_seq),
        in_specs=[
            pl.BlockSpec((ts, D_MODEL), row),
            pl.BlockSpec((CONV_WIDTH, SUBLANES, D_MODEL), lambda b, s: (0, 0, 0)),
            pl.BlockSpec((1, D_MODEL), const),
            pl.BlockSpec((1, D_MODEL), const),
            pl.BlockSpec((1, D_MODEL), const),
            pl.BlockSpec((D_MODEL, D_MODEL), const),
            pl.BlockSpec((ts, D_MODEL), row),
        ],
        out_specs=pl.BlockSpec((ts, D_MODEL), row),
        out_shape=jax.ShapeDtypeStruct((n, D_MODEL), BF16),
        scratch_shapes=[pltpu.VMEM((ts + CONV_HALO, D_MODEL), F32),
                        pltpu.VMEM((SUBLANES - 1, ts + CONV_HALO - SUBLANES, D_MODEL), F32),
                        pltpu.VMEM((ts, D_MODEL), BF16)],
        compiler_params=_params("parallel", "arbitrary"),
        name="conv_branch",
    )(u, w_dw, b_dw, ln_g, ln_b, w_o_b, gate_c)


def _attn_kernel(q_ref, k_ref, v_ref, km_ref, o_ref, *, n_blk):
    blk = MOBA_BLOCK
    seq = n_blk * blk
    k_sel = min(MOBA_TOPK, n_blk)
    exp2_scale = HEAD_DIM ** -0.5 * LOG2_E
    nt_dims = (((1,), (1,)), ((), ()))
    n_lane = HEAD_DIM

    assert blk & (blk - 1) == 0
    blk_shift = blk.bit_length() - 1
    n_sub = -(-n_blk // SUBLANES) * SUBLANES

    q_all = q_ref[...]
    km = km_ref[:, 0, :]
    if n_sub > n_blk:
        km = jnp.concatenate([km, jnp.zeros((n_sub - n_blk, HEAD_DIM), F32)], axis=0)
    km_hi = km.astype(BF16)
    km_lo = (km - km_hi.astype(F32)).astype(BF16)

    gate = (lax.dot_general(km_hi, q_all, nt_dims, preferred_element_type=F32)
            + lax.dot_general(km_lo, q_all, nt_dims, preferred_element_type=F32))
    blk_t = lax.broadcasted_iota(jnp.int32, (n_sub, seq), 0)
    own_t = lax.broadcasted_iota(jnp.int32, (n_sub, seq), 1) >> blk_shift
    past = blk_t < own_t
    blk_f = blk_t.astype(F32)
    g = jnp.where(past, gate, NEG_INF)
    sel = jnp.zeros((n_sub, seq), F32)
    for _ in range(k_sel):
        mx = jnp.max(g, axis=0, keepdims=True)
        first = jnp.min(jnp.where(g == mx, blk_f, float(n_sub)), axis=0, keepdims=True)
        pick = blk_f == first
        sel = jnp.where(pick, 1.0, sel)
        g = jnp.where(pick, NEG_INF, g)
    visible = ((sel > 0.0) & past) | (blk_t == own_t)
    bias_t = jnp.concatenate([jnp.where(visible, 0.0, MASKED), jnp.zeros((n_lane - n_sub, seq), F32)], axis=0)
    q_bias = bias_t.T.astype(BF16)
    blk_id = lax.broadcasted_iota(jnp.int32, (seq, n_lane), 1)
    own_id = lax.broadcasted_iota(jnp.int32, (seq, n_lane), 0) >> blk_shift
    k_blk = jnp.where(blk_id == own_id, 1.0, 0.0).astype(BF16)
    k_aug = jnp.concatenate([k_ref[...], k_blk], axis=1)

    causal = (lax.broadcasted_iota(jnp.int32, (blk, blk), 1)
              <= lax.broadcasted_iota(jnp.int32, (blk, blk), 0))

    def scores(i):
        rows = slice(i * blk, (i + 1) * blk)
        q_aug = jnp.concatenate([q_all[rows], q_bias[rows]], axis=1)
        return lax.dot_general(q_aug, k_aug[:(i + 1) * blk], nt_dims, preferred_element_type=F32)

    raw_next = scores(0)
    for i in range(n_blk):
        rows = slice(i * blk, (i + 1) * blk)
        width = (i + 1) * blk
        raw = raw_next
        if i + 1 < n_blk:
            raw_next = scores(i + 1)
        own = jnp.where(causal, raw[:, i * blk:], MASKED)
        parts = [raw[:, :i * blk], own] if i else [own]
        m = jnp.max(own, axis=1, keepdims=True)
        if i:
            m = jnp.maximum(m, jnp.max(parts[0], axis=1, keepdims=True))
        p = [jnp.exp2((t - m) * exp2_scale) for t in parts]
        l = sum(jnp.sum(t, axis=1, keepdims=True) for t in p)
        pv = jnp.dot(jnp.concatenate(p, axis=1).astype(BF16), v_ref[:width, :], preferred_element_type=F32)
        o_ref[rows, :] = (pv / l).astype(BF16)


def _moba_attention(q, k, v, k_mean, batch, seq):
    n = q.shape[0]
    n_blk = seq // MOBA_BLOCK
    seq_head = lambda b, h: (b, h)
    return pl.pallas_call(
        functools.partial(_attn_kernel, n_blk=n_blk),
        grid=(batch, N_HEADS),
        in_specs=[
            pl.BlockSpec((seq, HEAD_DIM), seq_head),
            pl.BlockSpec((seq, HEAD_DIM), seq_head),
            pl.BlockSpec((seq, HEAD_DIM), seq_head),
            pl.BlockSpec((n_blk, 1, HEAD_DIM), lambda b, h: (b, 0, h)),
        ],
        out_specs=pl.BlockSpec((seq, HEAD_DIM), seq_head),
        out_shape=jax.ShapeDtypeStruct((n, D_MODEL), BF16),
        compiler_params=_params("parallel", "parallel"),
        name="moba_attention",
    )(q, k, v, k_mean)


def _merge_kernel(attn_ref, ga_ref, cg_ref, x_ref, wao_ref, wout_ref, g1_ref, b1_ref,
                  wr_ref, rb_ref, h_ref, hp_ref, idx_ref, wgt_ref, rank_ref, total_ref, count_ref, hprev_ref):
    tm = x_ref.shape[0]
    step = pl.program_id(0)

    @pl.when(step == 0)
    def _():
        count_ref[...] = jnp.zeros((N_EXPERTS, 1), F32)
        hprev_ref[...] = jnp.zeros(hprev_ref.shape, F32)

    h = hprev_ref[...]

    a = jnp.dot(attn_ref[...], wao_ref[...], preferred_element_type=F32)
    merged = ga_ref[...].astype(F32) * a + cg_ref[...].astype(F32)
    y = jnp.dot(merged.astype(BF16), wout_ref[...], preferred_element_type=F32)
    h_new = _layer_norm(DEEPNORM_ALPHA * x_ref[...] + y, g1_ref[...], b1_ref[...])
    h_ref[...] = h_new
    hp_ref[...] = _pack_bf16_halves(h_new)
    hprev_ref[...] = h_new

    nt_dims = (((1,), (1,)), ((), ()))
    h_hi = h.astype(BF16)
    h_lo = (h - h_hi.astype(F32)).astype(BF16)
    w = wr_ref[...]
    w_hi = w.astype(BF16)
    w_lo = (w - w_hi.astype(F32)).astype(BF16)
    logits = (lax.dot_general(w_hi, h_hi, nt_dims, preferred_element_type=F32)
              + lax.dot_general(w_hi, h_lo, nt_dims, preferred_element_type=F32)
              + lax.dot_general(w_lo, h_hi, nt_dims, preferred_element_type=F32))
    scores = _sigmoid(logits)
    biased = scores + rb_ref[...]

    g3 = biased.reshape(N_GROUPS, GROUP_SIZE, tm)
    m1 = jnp.max(g3, axis=1, keepdims=True)
    is_max = g3 == m1
    n_max = jnp.sum(jnp.where(is_max, 1.0, 0.0), axis=1, keepdims=True)
    m2 = jnp.max(jnp.where(is_max, NEG_INF, g3), axis=1, keepdims=True)
    grp = (m1 + jnp.where(n_max >= 2.0, m1, m2)).reshape(N_GROUPS, tm)

    gid = lax.broadcasted_iota(jnp.int32, (N_GROUPS, tm), 0)
    rank = jnp.zeros((N_GROUPS, tm), F32)
    for o in range(N_GROUPS):
        other = grp[o:o + 1, :]
        ahead = (other > grp) | ((other == grp) & (o < gid))
        rank = rank + jnp.where(ahead, 1.0, 0.0)
    grp_keep = jnp.where(rank < float(TOPK_GROUPS), 1.0, 0.0)
    keep = jnp.broadcast_to(grp_keep.reshape(N_GROUPS, 1, tm),
                            (N_GROUPS, GROUP_SIZE, tm)).reshape(N_EXPERTS, tm)
    cand = jnp.where(keep > 0.0, biased, NEG_INF)

    eid = lax.broadcasted_iota(jnp.int32, (N_EXPERTS, tm), 0).astype(F32)
    chosen = jnp.zeros((N_EXPERTS, tm), F32)
    firsts = []
    for r in range(TOP_K):
        mx = jnp.max(cand, axis=0, keepdims=True)
        first = jnp.min(jnp.where(cand == mx, eid, float(N_EXPERTS)), axis=0, keepdims=True)
        pick = eid == first
        firsts.append(first)
        idx_ref[r:r + 1, :] = first.astype(jnp.int32)
        wgt_ref[r:r + 1, :] = jnp.sum(jnp.where(pick, scores, 0.0), axis=0, keepdims=True)
        cand = jnp.where(pick, NEG_INF, cand)
        chosen = jnp.where(pick, 1.0, chosen)
    top_s = wgt_ref[...]
    wgt_ref[...] = top_s / (jnp.sum(top_s, axis=0, keepdims=True) + 1e-20) * ROUTED_SCALE

    earlier = (lax.broadcasted_iota(jnp.int32, (tm, tm), 0)
               < lax.broadcasted_iota(jnp.int32, (tm, tm), 1))
    before = jnp.dot(chosen.astype(BF16), jnp.where(earlier, 1.0, 0.0).astype(BF16),
                     preferred_element_type=F32) + count_ref[...]
    for r in range(TOP_K):
        rank_ref[r:r + 1, :] = jnp.sum(jnp.where(eid == firsts[r], before, 0.0),
                                       axis=0, keepdims=True).astype(jnp.int32)
    counted = jnp.where(step > 0, 1.0, 0.0)
    total = count_ref[...] + counted * jnp.sum(chosen, axis=1, keepdims=True)
    count_ref[...] = total
    total_ref[...] = total


def _merge_ln_router(attn, gate_a, conv_gated, x2, w_attn_o_b, w_out_b, ln_g, ln_b, w_router_t, router_bias,
                     first_token, n):
    tm = TOKEN_TILE
    first_tile = first_token // tm
    n_tiles = n // tm
    row_in = lambda i: (jnp.minimum(i, n_tiles - 1) + first_tile, 0)
    row = lambda i: (jnp.minimum(i, n_tiles - 1), 0)
    const = lambda i: (0, 0)
    col = lambda i: (0, jnp.maximum(i - 1, 0))
    return pl.pallas_call(
        _merge_kernel,
        grid=(n_tiles + 1,),
        in_specs=[
            pl.BlockSpec((tm, D_MODEL), row_in),
            pl.BlockSpec((tm, D_MODEL), row_in),
            pl.BlockSpec((tm, D_MODEL), row_in),
            pl.BlockSpec((tm, D_MODEL), row_in),
            pl.BlockSpec((D_MODEL, D_MODEL), const),
            pl.BlockSpec((D_MODEL, D_MODEL), const),
            pl.BlockSpec((1, D_MODEL), const),
            pl.BlockSpec((1, D_MODEL), const),
            pl.BlockSpec((N_EXPERTS, D_MODEL), const),
            pl.BlockSpec((N_EXPERTS, 1), const),
        ],
        out_specs=[
            pl.BlockSpec((tm, D_MODEL), row),
            pl.BlockSpec((tm, HALF), row),
            pl.BlockSpec((TOP_K, tm), col),
            pl.BlockSpec((TOP_K, tm), col),
            pl.BlockSpec((TOP_K, tm), col),
            pl.BlockSpec((N_EXPERTS, 1), const),
        ],
        out_shape=[
            jax.ShapeDtypeStruct((n, D_MODEL), F32),
            jax.ShapeDtypeStruct((n, HALF), jnp.uint32),
            jax.ShapeDtypeStruct((TOP_K, n), jnp.int32),
            jax.ShapeDtypeStruct((TOP_K, n), F32),
            jax.ShapeDtypeStruct((TOP_K, n), jnp.int32),
            jax.ShapeDtypeStruct((N_EXPERTS, 1), F32),
        ],
        scratch_shapes=[pltpu.VMEM((N_EXPERTS, 1), F32), pltpu.VMEM((tm, D_MODEL), F32)],
        compiler_params=_params("arbitrary"),
        name="merge_ln_router",
    )(attn, gate_a, conv_gated, x2, w_attn_o_b, w_out_b, ln_g, ln_b, w_router_t, router_bias)


def _sc_mesh():
    return plsc.VectorSubcoreMesh(core_axis_name="c", subcore_axis_name="s",
                                  num_cores=SC_CORES, num_subcores=SC_SUBCORES)


def _sc_worker_base(rows_per_worker):
    return (lax.axis_index("s") * SC_CORES + lax.axis_index("c")) * rows_per_worker


def _sc_gather_rows(table, idx):
    m = idx.shape[0]
    width = table.shape[1]
    per_worker = m // (SC_CORES * SC_SUBCORES)
    assert per_worker * SC_CORES * SC_SUBCORES == m and per_worker % SC_CHUNK == 0

    @functools.partial(
        pl.kernel, mesh=_sc_mesh(),
        out_type=jax.ShapeDtypeStruct((m, width), table.dtype),
        scratch_types=[pltpu.VMEM((SC_CHUNK,), jnp.int32),
                       pltpu.VMEM((SC_CHUNK, width), table.dtype),
                       pltpu.SemaphoreType.DMA],
        name="sc_gather_rows")
    def gather(table_hbm, idx_hbm, out_hbm, idx_v, rows_v, sem):
        base = _sc_worker_base(per_worker)

        @pl.loop(0, per_worker // SC_CHUNK)
        def _(c):
            off = pl.multiple_of(base + c * SC_CHUNK, SC_CHUNK)
            pltpu.sync_copy(idx_hbm.at[pl.ds(off, SC_CHUNK)], idx_v)
            pltpu.async_copy(table_hbm.at[idx_v], rows_v, sem).wait()
            pltpu.sync_copy(rows_v, out_hbm.at[pl.ds(off, SC_CHUNK)])

    return gather(table, idx)


def _sc_scatter_rows(rows, pos, n_out):
    n, width = rows.shape
    per_worker = n // (SC_CORES * SC_SUBCORES)
    assert per_worker * SC_CORES * SC_SUBCORES == n and per_worker % SC_CHUNK == 0

    @functools.partial(
        pl.kernel, mesh=_sc_mesh(),
        out_type=jax.ShapeDtypeStruct((n_out, width), rows.dtype),
        scratch_types=[pltpu.VMEM((SC_CHUNK,), jnp.int32),
                       pltpu.VMEM((SC_CHUNK, width), rows.dtype),
                       pltpu.SemaphoreType.DMA],
        name="sc_scatter_rows")
    def scatter(rows_hbm, pos_hbm, out_hbm, idx_v, rows_v, sem):
        base = _sc_worker_base(per_worker)

        @pl.loop(0, per_worker // SC_CHUNK)
        def _(c):
            off = pl.multiple_of(base + c * SC_CHUNK, SC_CHUNK)
            pltpu.sync_copy(rows_hbm.at[pl.ds(off, SC_CHUNK)], rows_v)
            for r in range(TOP_K):
                pltpu.sync_copy(pos_hbm.at[pl.ds(r * n + off, SC_CHUNK)], idx_v)
                pltpu.async_copy(rows_v, out_hbm.at[idx_v], sem).wait()

    return scatter(rows, pos)


def _position_kernel(idx_ref, rank_ref, start_ref, pos_ref):
    tl = idx_ref.shape[1]
    eid = lax.broadcasted_iota(jnp.int32, (N_EXPERTS, tl), 0)
    start = start_ref[...]
    for r in range(TOP_K):
        here = jnp.sum(jnp.where(eid == idx_ref[r:r + 1, :], start, 0.0), axis=0, keepdims=True)
        pos_ref[r:r + 1, :] = here.astype(jnp.int32) + rank_ref[r:r + 1, :]


def _positions(top_idx, rank, group_start):
    n = top_idx.shape[1]
    tl = POSITION_TILE
    col = lambda i: (0, i)
    return pl.pallas_call(
        _position_kernel,
        grid=(n // tl,),
        in_specs=[pl.BlockSpec((TOP_K, tl), col), pl.BlockSpec((TOP_K, tl), col),
                  pl.BlockSpec((N_EXPERTS, 1), lambda i: (0, 0))],
        out_specs=pl.BlockSpec((TOP_K, tl), col),
        out_shape=jax.ShapeDtypeStruct((TOP_K, n), jnp.int32),
        compiler_params=_params("parallel"),
        name="positions",
    )(top_idx, rank, group_start)


def _expert_kernel(te_ref, tv_ref, xs_ref, wg_ref, wu_ref, wd_ref, y_ref, act_ref):
    i = pl.program_id(0)
    valid = tv_ref[i]
    valid_prev = tv_ref[jnp.maximum(i - 1, 0)]

    @pl.when(i == 0)
    def _():
        act_ref[...] = jnp.zeros(act_ref.shape, BF16)

    @pl.when((valid > 0) | (valid_prev > 0))
    def _():
        act_prev = act_ref[...]
        wd = wd_ref[0]
        for c in range(EXPERT_TILE // EXPERT_CHAIN):
            rows = slice(c * EXPERT_CHAIN, (c + 1) * EXPERT_CHAIN)
            y_ref[rows, :] = _pack_bf16_halves(jnp.dot(act_prev[rows], wd, preferred_element_type=F32))
        wg = wg_ref[0]
        wu = wu_ref[0]
        for c in range(EXPERT_TILE // EXPERT_CHAIN):
            rows = slice(c * EXPERT_CHAIN, (c + 1) * EXPERT_CHAIN)
            live = lax.broadcasted_iota(jnp.int32, (EXPERT_CHAIN, HALF), 0) < valid - c * EXPERT_CHAIN
            lo, hi = _unpack_bf16_halves(jnp.where(live, xs_ref[rows, :], jnp.uint32(0)))
            x = jnp.concatenate([lo, hi], axis=1).astype(BF16)
            g = jnp.dot(x, wg, preferred_element_type=F32)
            u = jnp.dot(x, wu, preferred_element_type=F32)
            act_ref[rows, :] = (g * _sigmoid(g) * u).astype(BF16)

    @pl.when((valid == 0) & (valid_prev == 0))
    def _():
        y_ref[...] = jnp.zeros(y_ref.shape, jnp.uint32)


def _grouped_experts(tile_expert, tile_valid, xs, wg, wu, wd):
    p = xs.shape[0]
    t = EXPERT_TILE
    n_tiles = p // t
    tile_expert = jnp.concatenate([tile_expert, tile_expert[-1:]])
    tile_valid = jnp.concatenate([tile_valid, jnp.zeros((1,), jnp.int32)])
    row_in = lambda i, te, tv: (jnp.minimum(i, n_tiles - 1), 0)
    row_out = lambda i, te, tv: (jnp.maximum(i - 1, 0), 0)
    expert = lambda i, te, tv: (te[i], 0, 0)
    expert_prev = lambda i, te, tv: (te[jnp.maximum(i - 1, 0)], 0, 0)
    return pl.pallas_call(
        _expert_kernel,
        grid_spec=pltpu.PrefetchScalarGridSpec(
            num_scalar_prefetch=2,
            grid=(n_tiles + 1,),
            in_specs=[
                pl.BlockSpec((t, HALF), row_in),
                pl.BlockSpec((1, D_MODEL, EXPERT_HIDDEN), expert),
                pl.BlockSpec((1, D_MODEL, EXPERT_HIDDEN), expert),
                pl.BlockSpec((1, EXPERT_HIDDEN, D_MODEL), expert_prev),
            ],
            out_specs=pl.BlockSpec((t, HALF), row_out),
            scratch_shapes=[pltpu.VMEM((t, EXPERT_HIDDEN), BF16)],
        ),
        out_shape=jax.ShapeDtypeStruct((p, HALF), jnp.uint32),
        compiler_params=_params("arbitrary"),
        name="grouped_experts",
    )(tile_expert, tile_valid, xs, wg, wu, wd)


def _combine_kernel(h_ref, yg_ref, wt_ref, wsg_ref, wsu_ref, wsd_ref, g2_ref, b2_ref, *out_refs):
    o_ref = out_refs[-1]
    h = h_ref[...]
    hb = h.astype(BF16)
    g = jnp.dot(hb, wsg_ref[...], preferred_element_type=F32)
    u = jnp.dot(hb, wsu_ref[...], preferred_element_type=F32)
    shared = jnp.dot((g * _sigmoid(g) * u).astype(BF16), wsd_ref[...], preferred_element_type=F32)
    wt = wt_ref[...]
    r_lo = jnp.zeros((h.shape[0], HALF), F32)
    r_hi = jnp.zeros((h.shape[0], HALF), F32)
    for r in range(TOP_K):
        lo, hi = _unpack_bf16_halves(yg_ref[r])
        w = wt[:, r:r + 1]
        r_lo = r_lo + lo * w
        r_hi = r_hi + hi * w
    routed = jnp.concatenate([r_lo, r_hi], axis=1)
    o_ref[...] = _layer_norm(DEEPNORM_ALPHA * h + (shared + routed), g2_ref[...], b2_ref[...])


def _combine_ln(h, yg, w_tok, wsg_b, wsu_b, wsd_b, ln_g, ln_b, out_so_far, first_token, n_total):
    n = h.shape[0]
    tm = TOKEN_TILE
    first_tile = first_token // tm
    row = lambda i: (i, 0)
    const = lambda i: (0, 0)
    hidden = wsg_b.shape[1]
    in_specs = [
        pl.BlockSpec((tm, D_MODEL), row),
        pl.BlockSpec((TOP_K, tm, HALF), lambda i: (0, i, 0)),
        pl.BlockSpec((tm, TOP_K), row),
        pl.BlockSpec((D_MODEL, hidden), const),
        pl.BlockSpec((D_MODEL, hidden), const),
        pl.BlockSpec((hidden, D_MODEL), const),
        pl.BlockSpec((1, D_MODEL), const),
        pl.BlockSpec((1, D_MODEL), const),
    ]
    args = [h, yg, w_tok, wsg_b, wsu_b, wsd_b, ln_g, ln_b]
    aliases = {}
    if out_so_far is not None:
        in_specs.append(pl.BlockSpec(memory_space=pl.ANY))
        args.append(out_so_far)
        aliases = {len(args) - 1: 0}
    return pl.pallas_call(
        _combine_kernel,
        grid=(n // tm,),
        in_specs=in_specs,
        out_specs=pl.BlockSpec((tm, D_MODEL), lambda i: (i + first_tile, 0)),
        out_shape=jax.ShapeDtypeStruct((n_total, D_MODEL), F32),
        input_output_aliases=aliases,
        compiler_params=_params("parallel"),
        name="combine_ln",
    )(*args)


def _group_layout(totals, n_tokens):
    t = EXPERT_TILE
    n_tiles = (TOP_K * n_tokens + N_EXPERTS * (t - 1)) // t
    counts = totals[:, 0].astype(jnp.int32)
    padded = ((counts + t - 1) // t) * t
    group_end = jnp.cumsum(padded)
    group_start = group_end - padded
    tile_start = jnp.arange(n_tiles, dtype=jnp.int32) * t
    tile_expert = jnp.minimum(jnp.sum((group_end[None, :] <= tile_start[:, None]).astype(jnp.int32), axis=1),
                              N_EXPERTS - 1)
    of_tile = tile_expert[:, None] == jnp.arange(N_EXPERTS, dtype=jnp.int32)[None, :]
    real_end = jnp.sum(jnp.where(of_tile, (group_start + counts)[None, :], 0), axis=1)
    tile_valid = jnp.clip(real_end - tile_start, 0, t).astype(jnp.int32)
    return group_start.astype(F32).reshape(N_EXPERTS, 1), tile_expert, tile_valid, n_tiles * t


def _rope_tables(seq):
    inv_freq = ROPE_THETA ** (-jnp.arange(0, ROPE_DIM, 2, dtype=F32) / ROPE_DIM)
    ang = jnp.arange(seq).astype(F32)[:, None] * inv_freq[None, :]
    cos, sin = jnp.cos(ang), jnp.sin(ang)
    rest = HEAD_DIM - ROPE_DIM
    zeros = jnp.zeros((seq, ROPE_HALF), F32)
    cos_t = jnp.concatenate([cos, cos, jnp.ones((seq, rest), F32)], axis=1)
    sa_t = jnp.concatenate([-sin, zeros, jnp.zeros((seq, rest), F32)], axis=1)
    sb_t = jnp.concatenate([zeros, sin, jnp.zeros((seq, rest), F32)], axis=1)
    return cos_t, sa_t, sb_t


def _layer(x2, batch, seq, w_in, b_gate, w_attn_o, w_dw, b_dw, conv_ln_g, conv_ln_b, w_conv_o, w_out,
           ln1_g, ln1_b, w_router, router_bias, w_exp_gate, w_exp_up, w_exp_down,
           w_sh_gate, w_sh_up, w_sh_down, ln2_g, ln2_b):
    n = x2.shape[0]
    row = lambda v: v.reshape(1, -1)
    cos, sa, sb = _rope_tables(seq)
    q, k, v, u, gate_a, gate_c, k_mean, w_gate_b, w_up_b, w_down_b = _in_projection(
        x2, w_in.astype(BF16), row(b_gate), cos, sa, sb, w_exp_gate, w_exp_up, w_exp_down, seq)
    w_taps = jnp.broadcast_to(w_dw.reshape(CONV_WIDTH, 1, D_MODEL), (CONV_WIDTH, SUBLANES, D_MODEL))
    conv_gated = _conv_branch(u, w_taps, row(b_dw), row(conv_ln_g),
                              row(conv_ln_b), w_conv_o.astype(BF16), gate_c, batch, seq)
    attn = _moba_attention(q, k, v, k_mean, batch, seq)
    w_attn_o_b, w_out_b, w_router_t = w_attn_o.astype(BF16), w_out.astype(BF16), w_router.T
    shared_w = (w_sh_gate.astype(BF16), w_sh_up.astype(BF16), w_sh_down.astype(BF16))
    n_chunk = n // MOE_CHUNKS
    assert n_chunk * MOE_CHUNKS == n and n_chunk % TOKEN_TILE == 0 and n_chunk % POSITION_TILE == 0
    out = None
    for c in range(MOE_CHUNKS):
        first = c * n_chunk
        h, h_packed, top_idx, top_w, rank, totals = _merge_ln_router(
            attn, gate_a, conv_gated, x2, w_attn_o_b, w_out_b, row(ln1_g), row(ln1_b),
            w_router_t, router_bias.reshape(N_EXPERTS, 1), first, n_chunk)
        group_start, tile_expert, tile_valid, n_rows = _group_layout(totals, n_chunk)
        pos = _positions(top_idx, rank, group_start).reshape(TOP_K * n_chunk)
        xs = _sc_scatter_rows(h_packed, pos, n_rows)
        ys = _grouped_experts(tile_expert, tile_valid, xs, w_gate_b, w_up_b, w_down_b)
        yg = _sc_gather_rows(ys, pos).reshape(TOP_K, n_chunk, HALF)
        out = _combine_ln(h, yg, top_w.T, *shared_w, row(ln2_g), row(ln2_b), out, first, n)
    return out


def kernel(x, w_in, b_gate, w_attn_o, w_dw, b_dw, conv_ln_g, conv_ln_b, w_conv_o, w_out, ln1_g, ln1_b,
           w_router, router_bias, w_exp_gate, w_exp_up, w_exp_down, w_sh_gate, w_sh_up, w_sh_down,
           ln2_g, ln2_b):
    batch, seq, d = x.shape
    assert d == D_MODEL and seq % MOBA_BLOCK == 0 and seq % TOKEN_TILE == 0
    assert w_in.shape[0] == DEPTH
    x2 = x.reshape(batch * seq, d)
    for l in range(DEPTH):
        x2 = _layer(x2, batch, seq, w_in[l], b_gate[l], w_attn_o[l], w_dw[l], b_dw[l], conv_ln_g[l],
                    conv_ln_b[l], w_conv_o[l], w_out[l], ln1_g[l], ln1_b[l], w_router[l], router_bias[l],
                    w_exp_gate[l], w_exp_up[l], w_exp_down[l], w_sh_gate[l], w_sh_up[l], w_sh_down[l],
                    ln2_g[l], ln2_b[l])
    return x2.reshape(batch, seq, d)
```

```python
import functools

import jax
import jax.numpy as jnp
from jax import lax
from jax.experimental import pallas as pl
from jax.experimental.pallas import tpu as pltpu
from jax.experimental.pallas import tpu_sc as plsc

F32 = jnp.float32
BF16 = jnp.bfloat16
NEG_INF = float("-inf")
MASKED = -1e30
LOG2_E = 1.4426950408889634

D_MODEL = 1024
N_HEADS = 8
HEAD_DIM = 128
ROPE_THETA = 500000.0
ROPE_DIM = HEAD_DIM // 4
ROPE_HALF = ROPE_DIM // 2
MOBA_BLOCK = 256
MOBA_TOPK = 3
CONV_WIDTH = 31
SUBLANES = 8
CONV_HALO = 32
CONV_ROWS = 32
N_EXPERTS = 256
TOP_K = 8
N_GROUPS = 8
GROUP_SIZE = N_EXPERTS // N_GROUPS
TOPK_GROUPS = 4
EXPERT_HIDDEN = 256
ROUTED_SCALE = 2.5
LN_EPS = 1e-5
DEPTH = 1
DEEPNORM_ALPHA = (2 * DEPTH) ** 0.25
HALF = D_MODEL // 2

TOKEN_TILE = 256
EXPERT_TILE = 512
EXPERT_CHAIN = 256
POSITION_TILE = 1024
MOE_CHUNKS = 2
HEADS_PER_STEP = 2
SC_CORES = 2
SC_SUBCORES = 16
SC_CHUNK = 64
VMEM_LIMIT = 56 * 1024 * 1024


def _params(*semantics):
    return pltpu.CompilerParams(dimension_semantics=semantics, vmem_limit_bytes=VMEM_LIMIT)


def _sigmoid(x):
    return 1.0 / (1.0 + jnp.exp(-x))


def _layer_norm(x, g, b):
    mu = jnp.mean(x, axis=-1, keepdims=True)
    xc = x - mu
    var = jnp.mean(xc * xc, axis=-1, keepdims=True)
    return xc * lax.rsqrt(var + LN_EPS) * g + b


def _pack_bf16_halves(y):
    lo = lax.bitcast_convert_type(y[:, :HALF].astype(BF16).astype(F32), jnp.uint32)
    hi = lax.bitcast_convert_type(y[:, HALF:].astype(BF16).astype(F32), jnp.uint32)
    return (hi & jnp.uint32(0xFFFF0000)) | (lo >> 16)


def _unpack_bf16_halves(p):
    lo = lax.bitcast_convert_type(p << 16, F32)
    hi = lax.bitcast_convert_type(p & jnp.uint32(0xFFFF0000), F32)
    return lo, hi


def _inproj_kernel(x_ref, w_ref, bg_ref, cos_ref, sa_ref, sb_ref, wg_ref, wu_ref, wd_ref,
                   q_ref, k_ref, v_ref, u_ref, ga_ref, gc_ref, km_ref, wgb_ref, wub_ref, wdb_ref):
    wgb_ref[...] = wg_ref[...].astype(BF16)
    wub_ref[...] = wu_ref[...].astype(BF16)
    wdb_ref[...] = wd_ref[...].astype(BF16)

    tm = x_ref.shape[0]
    xb = x_ref[...].astype(BF16)

    def proj(c):
        return jnp.dot(xb, w_ref[:, c * D_MODEL:(c + 1) * D_MODEL], preferred_element_type=F32)

    cos = cos_ref[...]
    sa = sa_ref[...]
    sb = sb_ref[...]

    def rope_head(t):
        return (t * cos + pltpu.roll(t, HEAD_DIM - ROPE_HALF, 1) * sa
                + pltpu.roll(t, ROPE_HALF, 1) * sb)

    q = proj(0)
    for h in range(N_HEADS):
        sl = slice(h * HEAD_DIM, (h + 1) * HEAD_DIM)
        q_ref[:, sl] = rope_head(q[:, sl]).astype(BF16)
    k = proj(1)
    for h in range(N_HEADS):
        sl = slice(h * HEAD_DIM, (h + 1) * HEAD_DIM)
        kr = rope_head(k[:, sl])
        k_ref[:, sl] = kr.astype(BF16)
        for g in range(tm // MOBA_BLOCK):
            km_ref[g, :, sl] = jnp.mean(kr[g * MOBA_BLOCK:(g + 1) * MOBA_BLOCK], axis=0, keepdims=True)
    v_ref[...] = proj(2).astype(BF16)
    u_ref[...] = proj(3) * _sigmoid(proj(4))
    ga_ref[...] = _sigmoid(proj(5) + bg_ref[:, :D_MODEL]).astype(BF16)
    gc_ref[...] = _sigmoid(proj(6) + bg_ref[:, D_MODEL:]).astype(BF16)


def _in_projection(x2, w_in_b, b_gate, cos, sa, sb, w_exp_gate, w_exp_up, w_exp_down, seq):
    n = x2.shape[0]
    tm = TOKEN_TILE
    n_cols = w_in_b.shape[1]
    tiles_per_seq = seq // tm
    steps = n // tm
    per_step = -(-N_EXPERTS // steps)
    assert N_EXPERTS % per_step == 0
    row = lambda i: (i, 0)
    const = lambda i: (0, 0)
    pos = lambda i: (i % tiles_per_seq, 0)
    experts = lambda i: (jnp.minimum(i, N_EXPERTS // per_step - 1), 0, 0)
    tok_bf16 = jax.ShapeDtypeStruct((n, D_MODEL), BF16)
    up_block = (per_step, D_MODEL, EXPERT_HIDDEN)
    down_block = (per_step, EXPERT_HIDDEN, D_MODEL)
    return pl.pallas_call(
        _inproj_kernel,
        grid=(steps,),
        in_specs=[
            pl.BlockSpec((tm, D_MODEL), row),
            pl.BlockSpec((D_MODEL, n_cols), const),
            pl.BlockSpec((1, 2 * D_MODEL), const),
            pl.BlockSpec((tm, HEAD_DIM), pos),
            pl.BlockSpec((tm, HEAD_DIM), pos),
            pl.BlockSpec((tm, HEAD_DIM), pos),
            pl.BlockSpec(up_block, experts),
            pl.BlockSpec(up_block, experts),
            pl.BlockSpec(down_block, experts),
        ],
        out_specs=[
            pl.BlockSpec((tm, D_MODEL), row),
            pl.BlockSpec((tm, D_MODEL), row),
            pl.BlockSpec((tm, D_MODEL), row),
            pl.BlockSpec((tm, D_MODEL), row),
            pl.BlockSpec((tm, D_MODEL), row),
            pl.BlockSpec((tm, D_MODEL), row),
            pl.BlockSpec((tm // MOBA_BLOCK, 1, D_MODEL), lambda i: (i, 0, 0)),
            pl.BlockSpec(up_block, experts),
            pl.BlockSpec(up_block, experts),
            pl.BlockSpec(down_block, experts),
        ],
        out_shape=[tok_bf16, tok_bf16, tok_bf16,
                   jax.ShapeDtypeStruct((n, D_MODEL), F32),
                   tok_bf16, tok_bf16,
                   jax.ShapeDtypeStruct((n // MOBA_BLOCK, 1, D_MODEL), F32),
                   jax.ShapeDtypeStruct(w_exp_gate.shape, BF16),
                   jax.ShapeDtypeStruct(w_exp_up.shape, BF16),
                   jax.ShapeDtypeStruct(w_exp_down.shape, BF16)],
        compiler_params=_params("arbitrary"),
        name="in_projection",
    )(x2, w_in_b, b_gate, cos, sa, sb, w_exp_gate, w_exp_up, w_exp_down)


def _conv_kernel(u_ref, wdw_ref, bdw_ref, lng_ref, lnb_ref, wo_ref, gc_ref, o_ref, buf_ref, sh_ref, y_ref):
    ts = u_ref.shape[0]
    s = pl.program_id(1)

    @pl.when(s == 0)
    def _():
        buf_ref[0:CONV_HALO, :] = jnp.zeros((CONV_HALO, D_MODEL), F32)

    @pl.when(s > 0)
    def _():
        buf_ref[0:CONV_HALO, :] = buf_ref[ts:ts + CONV_HALO, :]

    buf_ref[CONV_HALO:CONV_HALO + ts, :] = u_ref[...]

    span = ts + CONV_HALO - SUBLANES
    for b in range(1, SUBLANES):
        sh_ref[b - 1, 0:span, :] = buf_ref[b:b + span, :]

    base = CONV_HALO - (CONV_WIDTH - 1)
    for c in range(ts // CONV_ROWS):
        r0 = c * CONV_ROWS
        acc = jnp.zeros((CONV_ROWS // SUBLANES, SUBLANES, D_MODEL), F32)
        for j in range(CONV_WIDTH):
            shift = (base + j) % SUBLANES
            row = r0 + base + j - shift
            src = buf_ref if shift == 0 else sh_ref.at[shift - 1]
            tap = src[row:row + CONV_ROWS, :].reshape(CONV_ROWS // SUBLANES, SUBLANES, D_MODEL)
            acc = acc + tap * wdw_ref[j]
        acc = acc.reshape(CONV_ROWS, D_MODEL)
        y = _layer_norm(acc + bdw_ref[...], lng_ref[...], lnb_ref[...])
        y_ref[r0:r0 + CONV_ROWS, :] = (y * _sigmoid(y)).astype(BF16)
    z = jnp.dot(y_ref[...], wo_ref[...], preferred_element_type=F32)
    o_ref[...] = (z * gc_ref[...].astype(F32)).astype(BF16)


def _conv_branch(u, w_dw, b_dw, ln_g, ln_b, w_o_b, gate_c, batch, seq):
    n = u.shape[0]
    ts = TOKEN_TILE
    tiles_per_seq = seq // ts
    row = lambda b, s: (b * tiles_per_seq + s, 0)
    const = lambda b, s: (0, 0)
    return pl.pallas_call(
        _conv_kernel,
        grid=(batch, tiles_per_seq),
        in_specs=[
            pl.BlockSpec((ts, D_MODEL), row),
            pl.BlockSpec((CONV_WIDTH, SUBLANES, D_MODEL), lambda b, s: (0, 0, 0)),
            pl.BlockSpec((1, D_MODEL), const),
            pl.BlockSpec((1, D_MODEL), const),
            pl.BlockSpec((1, D_MODEL), const),
            pl.BlockSpec((D_MODEL, D_MODEL), const),
            pl.BlockSpec((ts, D_MODEL), row),
        ],
        out_specs=pl.BlockSpec((ts, D_MODEL), row),
        out_shape=jax.ShapeDtypeStruct((n, D_MODEL), BF16),
        scratch_shapes=[pltpu.VMEM((ts + CONV_HALO, D_MODEL), F32),
                        pltpu.VMEM((SUBLANES - 1, ts + CONV_HALO - SUBLANES, D_MODEL), F32),
                        pltpu.VMEM((ts, D_MODEL), BF16)],
        compiler_params=_params("parallel", "arbitrary"),
        name="conv_branch",
    )(u, w_dw, b_dw, ln_g, ln_b, w_o_b, gate_c)


def _attn_kernel(q_ref, k_ref, v_ref, km_ref, o_ref, *, n_blk):
    blk = MOBA_BLOCK
    seq = n_blk * blk
    k_sel = min(MOBA_TOPK, n_blk)
    exp2_scale = HEAD_DIM ** -0.5 * LOG2_E
    nt_dims = (((1,), (1,)), ((), ()))
    n_lane = HEAD_DIM

    assert blk & (blk - 1) == 0
    blk_shift = blk.bit_length() - 1
    n_sub = -(-n_blk // SUBLANES) * SUBLANES

    blk_t = lax.broadcasted_iota(jnp.int32, (n_sub, seq), 0)
    own_t = lax.broadcasted_iota(jnp.int32, (n_sub, seq), 1) >> blk_shift
    past = blk_t < own_t
    blk_f = blk_t.astype(F32)
    blk_id = lax.broadcasted_iota(jnp.int32, (seq, n_lane), 1)
    own_id = lax.broadcasted_iota(jnp.int32, (seq, n_lane), 0) >> blk_shift
    k_blk = jnp.where(blk_id == own_id, 1.0, 0.0).astype(BF16)
    ones = jnp.ones((seq, n_lane), BF16)
    causal = (lax.broadcasted_iota(jnp.int32, (blk, blk), 1)
              <= lax.broadcasted_iota(jnp.int32, (blk, blk), 0))

    def prepare(lanes):
        q_all = q_ref[:, lanes]
        km = km_ref[:, 0, lanes]
        if n_sub > n_blk:
            km = jnp.concatenate([km, jnp.zeros((n_sub - n_blk, HEAD_DIM), F32)], axis=0)
        km_hi = km.astype(BF16)
        km_lo = (km - km_hi.astype(F32)).astype(BF16)
        gate = (lax.dot_general(km_hi, q_all, nt_dims, preferred_element_type=F32)
                + lax.dot_general(km_lo, q_all, nt_dims, preferred_element_type=F32))
        g = jnp.where(past, gate, NEG_INF)
        sel = jnp.zeros((n_sub, seq), F32)
        for _ in range(k_sel):
            mx = jnp.max(g, axis=0, keepdims=True)
            first = jnp.min(jnp.where(g == mx, blk_f, float(n_sub)), axis=0, keepdims=True)
            pick = blk_f == first
            sel = jnp.where(pick, 1.0, sel)
            g = jnp.where(pick, NEG_INF, g)
        visible = ((sel > 0.0) & past) | (blk_t == own_t)
        bias_t = jnp.concatenate([jnp.where(visible, 0.0, MASKED), jnp.zeros((n_lane - n_sub, seq), F32)],
                                 axis=0)
        q_bias = bias_t.T.astype(BF16)
        k_aug = jnp.concatenate([k_ref[:, lanes], k_blk], axis=1)
        v_aug = jnp.concatenate([v_ref[:, lanes], ones], axis=1)
        return q_all, q_bias, k_aug, v_aug, lanes

    def scores(head, i):
        q_all, q_bias, k_aug, _, _ = head
        rows = slice(i * blk, (i + 1) * blk)
        q_aug = jnp.concatenate([q_all[rows], q_bias[rows]], axis=1)
        return lax.dot_general(q_aug, k_aug[:(i + 1) * blk], nt_dims, preferred_element_type=F32)

    def finish(head, i, p):
        pv = jnp.dot(p, head[3][:(i + 1) * blk], preferred_element_type=F32)
        o_ref[i * blk:(i + 1) * blk, head[4]] = (pv[:, :HEAD_DIM] / pv[:, HEAD_DIM:HEAD_DIM + 1]).astype(BF16)

    def softmax_numerator(raw, i):
        own = jnp.where(causal, raw[:, i * blk:], MASKED)
        parts = [raw[:, :i * blk], own] if i else [own]
        m = jnp.max(own, axis=1, keepdims=True)
        if i:
            m = jnp.maximum(m, jnp.max(parts[0], axis=1, keepdims=True))
        return jnp.concatenate([jnp.exp2((t - m) * exp2_scale) for t in parts], axis=1).astype(BF16)

    heads = [prepare(slice(hh * HEAD_DIM, (hh + 1) * HEAD_DIM)) for hh in range(HEADS_PER_STEP)]
    raw_next = [scores(head, 0) for head in heads]
    p_prev = None
    for i in range(n_blk):
        raws = raw_next
        if i + 1 < n_blk:
            raw_next = [scores(head, i + 1) for head in heads]
        if p_prev is not None:
            for head, p in zip(heads, p_prev):
                finish(head, i - 1, p)
        p_prev = [softmax_numerator(raw, i) for raw in raws]
    for head, p in zip(heads, p_prev):
        finish(head, n_blk - 1, p)


def _moba_attention(q, k, v, k_mean, batch, seq):
    n = q.shape[0]
    n_blk = seq // MOBA_BLOCK
    width = HEADS_PER_STEP * HEAD_DIM
    seq_head = lambda b, h: (b, h)
    return pl.pallas_call(
        functools.partial(_attn_kernel, n_blk=n_blk),
        grid=(batch, N_HEADS // HEADS_PER_STEP),
        in_specs=[
            pl.BlockSpec((seq, width), seq_head),
            pl.BlockSpec((seq, width), seq_head),
            pl.BlockSpec((seq, width), seq_head),
            pl.BlockSpec((n_blk, 1, width), lambda b, h: (b, 0, h)),
        ],
        out_specs=pl.BlockSpec((seq, width), seq_head),
        out_shape=jax.ShapeDtypeStruct((n, D_MODEL), BF16),
        compiler_params=_params("parallel", "parallel"),
        name="moba_attention",
    )(q, k, v, k_mean)


def _merge_kernel(attn_ref, ga_ref, cg_ref, x_ref, wao_ref, wout_ref, g1_ref, b1_ref,
                  wr_ref, rb_ref, h_ref, hp_ref, idx_ref, wgt_ref, rank_ref, total_ref, count_ref, hprev_ref):
    tm = x_ref.shape[0]
    step = pl.program_id(0)

    @pl.when(step == 0)
    def _():
        count_ref[...] = jnp.zeros((N_EXPERTS, 1), F32)
        hprev_ref[...] = jnp.zeros(hprev_ref.shape, F32)

    h = hprev_ref[...]

    a = jnp.dot(attn_ref[...], wao_ref[...], preferred_element_type=F32)
    merged = ga_ref[...].astype(F32) * a + cg_ref[...].astype(F32)
    y = jnp.dot(merged.astype(BF16), wout_ref[...], preferred_element_type=F32)
    h_new = _layer_norm(DEEPNORM_ALPHA * x_ref[...] + y, g1_ref[...], b1_ref[...])
    h_ref[...] = h_new
    hp_ref[...] = _pack_bf16_halves(h_new)
    hprev_ref[...] = h_new

    nt_dims = (((1,), (1,)), ((), ()))
    h_hi = h.astype(BF16)
    h_lo = (h - h_hi.astype(F32)).astype(BF16)
    w = wr_ref[...]
    w_hi = w.astype(BF16)
    w_lo = (w - w_hi.astype(F32)).astype(BF16)
    logits = (lax.dot_general(w_hi, h_hi, nt_dims, preferred_element_type=F32)
              + lax.dot_general(w_hi, h_lo, nt_dims, preferred_element_type=F32)
              + lax.dot_general(w_lo, h_hi, nt_dims, preferred_element_type=F32))
    scores = _sigmoid(logits)
    biased = scores + rb_ref[...]

    g3 = biased.reshape(N_GROUPS, GROUP_SIZE, tm)
    m1 = jnp.max(g3, axis=1, keepdims=True)
    is_max = g3 == m1
    n_max = jnp.sum(jnp.where(is_max, 1.0, 0.0), axis=1, keepdims=True)
    m2 = jnp.max(jnp.where(is_max, NEG_INF, g3), axis=1, keepdims=True)
    grp = (m1 + jnp.where(n_max >= 2.0, m1, m2)).reshape(N_GROUPS, tm)

    gid = lax.broadcasted_iota(jnp.int32, (N_GROUPS, tm), 0)
    rank = jnp.zeros((N_GROUPS, tm), F32)
    for o in range(N_GROUPS):
        other = grp[o:o + 1, :]
        ahead = (other > grp) | ((other == grp) & (o < gid))
        rank = rank + jnp.where(ahead, 1.0, 0.0)
    grp_keep = jnp.where(rank < float(TOPK_GROUPS), 1.0, 0.0)
    keep = jnp.broadcast_to(grp_keep.reshape(N_GROUPS, 1, tm),
                            (N_GROUPS, GROUP_SIZE, tm)).reshape(N_EXPERTS, tm)
    cand = jnp.where(keep > 0.0, biased, NEG_INF)

    eid = lax.broadcasted_iota(jnp.int32, (N_EXPERTS, tm), 0).astype(F32)
    chosen = jnp.zeros((N_EXPERTS, tm), F32)
    firsts = []
    for r in range(TOP_K):
        mx = jnp.max(cand, axis=0, keepdims=True)
        first = jnp.min(jnp.where(cand == mx, eid, float(N_EXPERTS)), axis=0, keepdims=True)
        pick = eid == first
        firsts.append(first)
        idx_ref[r:r + 1, :] = first.astype(jnp.int32)
        wgt_ref[r:r + 1, :] = jnp.sum(jnp.where(pick, scores, 0.0), axis=0, keepdims=True)
        cand = jnp.where(pick, NEG_INF, cand)
        chosen = jnp.where(pick, 1.0, chosen)
    top_s = wgt_ref[...]
    wgt_ref[...] = top_s / (jnp.sum(top_s, axis=0, keepdims=True) + 1e-20) * ROUTED_SCALE

    earlier = (lax.broadcasted_iota(jnp.int32, (tm, tm), 0)
               < lax.broadcasted_iota(jnp.int32, (tm, tm), 1))
    before = jnp.dot(chosen.astype(BF16), jnp.where(earlier, 1.0, 0.0).astype(BF16),
                     preferred_element_type=F32) + count_ref[...]
    for r in range(TOP_K):
        rank_ref[r:r + 1, :] = jnp.sum(jnp.where(eid == firsts[r], before, 0.0),
                                       axis=0, keepdims=True).astype(jnp.int32)
    counted = jnp.where(step > 0, 1.0, 0.0)
    total = count_ref[...] + counted * jnp.sum(chosen, axis=1, keepdims=True)
    count_ref[...] = total
    total_ref[...] = total


def _merge_ln_router(attn, gate_a, conv_gated, x2, w_attn_o_b, w_out_b, ln_g, ln_b, w_router_t, router_bias,
                     first_token, n):
    tm = TOKEN_TILE
    first_tile = first_token // tm
    n_tiles = n // tm
    row_in = lambda i: (jnp.minimum(i, n_tiles - 1) + first_tile, 0)
    row = lambda i: (jnp.minimum(i, n_tiles - 1), 0)
    const = lambda i: (0, 0)
    col = lambda i: (0, jnp.maximum(i - 1, 0))
    return pl.pallas_call(
        _merge_kernel,
        grid=(n_tiles + 1,),
        in_specs=[
            pl.BlockSpec((tm, D_MODEL), row_in),
            pl.BlockSpec((tm, D_MODEL), row_in),
            pl.BlockSpec((tm, D_MODEL), row_in),
            pl.BlockSpec((tm, D_MODEL), row_in),
            pl.BlockSpec((D_MODEL, D_MODEL), const),
            pl.BlockSpec((D_MODEL, D_MODEL), const),
            pl.BlockSpec((1, D_MODEL), const),
            pl.BlockSpec((1, D_MODEL), const),
            pl.BlockSpec((N_EXPERTS, D_MODEL), const),
            pl.BlockSpec((N_EXPERTS, 1), const),
        ],
        out_specs=[
            pl.BlockSpec((tm, D_MODEL), row),
            pl.BlockSpec((tm, HALF), row),
            pl.BlockSpec((TOP_K, tm), col),
            pl.BlockSpec((TOP_K, tm), col),
            pl.BlockSpec((TOP_K, tm), col),
            pl.BlockSpec((N_EXPERTS, 1), const),
        ],
        out_shape=[
            jax.ShapeDtypeStruct((n, D_MODEL), F32),
            jax.ShapeDtypeStruct((n, HALF), jnp.uint32),
            jax.ShapeDtypeStruct((TOP_K, n), jnp.int32),
            jax.ShapeDtypeStruct((TOP_K, n), F32),
            jax.ShapeDtypeStruct((TOP_K, n), jnp.int32),
            jax.ShapeDtypeStruct((N_EXPERTS, 1), F32),
        ],
        scratch_shapes=[pltpu.VMEM((N_EXPERTS, 1), F32), pltpu.VMEM((tm, D_MODEL), F32)],
        compiler_params=_params("arbitrary"),
        name="merge_ln_router",
    )(attn, gate_a, conv_gated, x2, w_attn_o_b, w_out_b, ln_g, ln_b, w_router_t, router_bias)


def _sc_mesh():
    return plsc.VectorSubcoreMesh(core_axis_name="c", subcore_axis_name="s",
                                  num_cores=SC_CORES, num_subcores=SC_SUBCORES)


def _sc_worker_base(rows_per_worker):
    return (lax.axis_index("s") * SC_CORES + lax.axis_index("c")) * rows_per_worker


def _sc_gather_rows(table, idx):
    m = idx.shape[0]
    width = table.shape[1]
    per_worker = m // (SC_CORES * SC_SUBCORES)
    assert per_worker * SC_CORES * SC_SUBCORES == m and per_worker % SC_CHUNK == 0

    @functools.partial(
        pl.kernel, mesh=_sc_mesh(),
        out_type=jax.ShapeDtypeStruct((m, width), table.dtype),
        scratch_types=[pltpu.VMEM((SC_CHUNK,), jnp.int32),
                       pltpu.VMEM((SC_CHUNK, width), table.dtype),
                       pltpu.SemaphoreType.DMA],
        name="sc_gather_rows")
    def gather(table_hbm, idx_hbm, out_hbm, idx_v, rows_v, sem):
        base = _sc_worker_base(per_worker)

        @pl.loop(0, per_worker // SC_CHUNK)
        def _(c):
            off = pl.multiple_of(base + c * SC_CHUNK, SC_CHUNK)
            pltpu.sync_copy(idx_hbm.at[pl.ds(off, SC_CHUNK)], idx_v)
            pltpu.async_copy(table_hbm.at[idx_v], rows_v, sem).wait()
            pltpu.sync_copy(rows_v, out_hbm.at[pl.ds(off, SC_CHUNK)])

    return gather(table, idx)


def _sc_scatter_rows(rows, pos, n_out):
    n, width = rows.shape
    per_worker = n // (SC_CORES * SC_SUBCORES)
    assert per_worker * SC_CORES * SC_SUBCORES == n and per_worker % SC_CHUNK == 0

    @functools.partial(
        pl.kernel, mesh=_sc_mesh(),
        out_type=jax.ShapeDtypeStruct((n_out, width), rows.dtype),
        scratch_types=[pltpu.VMEM((SC_CHUNK,), jnp.int32),
                       pltpu.VMEM((SC_CHUNK, width), rows.dtype),
                       pltpu.SemaphoreType.DMA],
        name="sc_scatter_rows")
    def scatter(rows_hbm, pos_hbm, out_hbm, idx_v, rows_v, sem):
        base = _sc_worker_base(per_worker)

        @pl.loop(0, per_worker // SC_CHUNK)
        def _(c):
            off = pl.multiple_of(base + c * SC_CHUNK, SC_CHUNK)
            pltpu.sync_copy(rows_hbm.at[pl.ds(off, SC_CHUNK)], rows_v)
            for r in range(TOP_K):
                pltpu.sync_copy(pos_hbm.at[pl.ds(r * n + off, SC_CHUNK)], idx_v)
                pltpu.async_copy(rows_v, out_hbm.at[idx_v], sem).wait()

    return scatter(rows, pos)


def _position_kernel(idx_ref, rank_ref, start_ref, pos_ref):
    tl = idx_ref.shape[1]
    eid = lax.broadcasted_iota(jnp.int32, (N_EXPERTS, tl), 0)
    start = start_ref[...]
    for r in range(TOP_K):
        here = jnp.sum(jnp.where(eid == idx_ref[r:r + 1, :], start, 0.0), axis=0, keepdims=True)
        pos_ref[r:r + 1, :] = here.astype(jnp.int32) + rank_ref[r:r + 1, :]


def _positions(top_idx, rank, group_start):
    n = top_idx.shape[1]
    tl = POSITION_TILE
    col = lambda i: (0, i)
    return pl.pallas_call(
        _position_kernel,
        grid=(n // tl,),
        in_specs=[pl.BlockSpec((TOP_K, tl), col), pl.BlockSpec((TOP_K, tl), col),
                  pl.BlockSpec((N_EXPERTS, 1), lambda i: (0, 0))],
        out_specs=pl.BlockSpec((TOP_K, tl), col),
        out_shape=jax.ShapeDtypeStruct((TOP_K, n), jnp.int32),
        compiler_params=_params("parallel"),
        name="positions",
    )(top_idx, rank, group_start)


def _expert_kernel(te_ref, tv_ref, xs_ref, wg_ref, wu_ref, wd_ref, y_ref, act_ref):
    i = pl.program_id(0)
    valid = tv_ref[i]
    valid_prev = tv_ref[jnp.maximum(i - 1, 0)]

    @pl.when(i == 0)
    def _():
        act_ref[...] = jnp.zeros(act_ref.shape, BF16)

    @pl.when((valid > 0) | (valid_prev > 0))
    def _():
        act_prev = act_ref[...]
        wd = wd_ref[0]
        for c in range(EXPERT_TILE // EXPERT_CHAIN):
            rows = slice(c * EXPERT_CHAIN, (c + 1) * EXPERT_CHAIN)
            y_ref[rows, :] = _pack_bf16_halves(jnp.dot(act_prev[rows], wd, preferred_element_type=F32))
        wg = wg_ref[0]
        wu = wu_ref[0]
        for c in range(EXPERT_TILE // EXPERT_CHAIN):
            rows = slice(c * EXPERT_CHAIN, (c + 1) * EXPERT_CHAIN)
            live = lax.broadcasted_iota(jnp.int32, (EXPERT_CHAIN, HALF), 0) < valid - c * EXPERT_CHAIN
            lo, hi = _unpack_bf16_halves(jnp.where(live, xs_ref[rows, :], jnp.uint32(0)))
            x = jnp.concatenate([lo, hi], axis=1).astype(BF16)
            g = jnp.dot(x, wg, preferred_element_type=F32)
            u = jnp.dot(x, wu, preferred_element_type=F32)
            act_ref[rows, :] = (g * _sigmoid(g) * u).astype(BF16)

    @pl.when((valid == 0) & (valid_prev == 0))
    def _():
        y_ref[...] = jnp.zeros(y_ref.shape, jnp.uint32)


def _grouped_experts(tile_expert, tile_valid, xs, wg, wu, wd):
    p = xs.shape[0]
    t = EXPERT_TILE
    n_tiles = p // t
    tile_expert = jnp.concatenate([tile_expert, tile_expert[-1:]])
    tile_valid = jnp.concatenate([tile_valid, jnp.zeros((1,), jnp.int32)])
    row_in = lambda i, te, tv: (jnp.minimum(i, n_tiles - 1), 0)
    row_out = lambda i, te, tv: (jnp.maximum(i - 1, 0), 0)
    expert = lambda i, te, tv: (te[i], 0, 0)
    expert_prev = lambda i, te, tv: (te[jnp.maximum(i - 1, 0)], 0, 0)
    return pl.pallas_call(
        _expert_kernel,
        grid_spec=pltpu.PrefetchScalarGridSpec(
            num_scalar_prefetch=2,
            grid=(n_tiles + 1,),
            in_specs=[
                pl.BlockSpec((t, HALF), row_in),
                pl.BlockSpec((1, D_MODEL, EXPERT_HIDDEN), expert),
                pl.BlockSpec((1, D_MODEL, EXPERT_HIDDEN), expert),
                pl.BlockSpec((1, EXPERT_HIDDEN, D_MODEL), expert_prev),
            ],
            out_specs=pl.BlockSpec((t, HALF), row_out),
            scratch_shapes=[pltpu.VMEM((t, EXPERT_HIDDEN), BF16)],
        ),
        out_shape=jax.ShapeDtypeStruct((p, HALF), jnp.uint32),
        compiler_params=_params("arbitrary"),
        name="grouped_experts",
    )(tile_expert, tile_valid, xs, wg, wu, wd)


def _combine_kernel(h_ref, yg_ref, wt_ref, wsg_ref, wsu_ref, wsd_ref, g2_ref, b2_ref, *out_refs):
    o_ref = out_refs[-1]
    h = h_ref[...]
    hb = h.astype(BF16)
    g = jnp.dot(hb, wsg_ref[...], preferred_element_type=F32)
    u = jnp.dot(hb, wsu_ref[...], preferred_element_type=F32)
    shared = jnp.dot((g * _sigmoid(g) * u).astype(BF16), wsd_ref[...], preferred_element_type=F32)
    wt = wt_ref[...]
    r_lo = jnp.zeros((h.shape[0], HALF), F32)
    r_hi = jnp.zeros((h.shape[0], HALF), F32)
    for r in range(TOP_K):
        lo, hi = _unpack_bf16_halves(yg_ref[r])
        w = wt[:, r:r + 1]
        r_lo = r_lo + lo * w
        r_hi = r_hi + hi * w
    routed = jnp.concatenate([r_lo, r_hi], axis=1)
    o_ref[...] = _layer_norm(DEEPNORM_ALPHA * h + (shared + routed), g2_ref[...], b2_ref[...])


def _combine_ln(h, yg, w_tok, wsg_b, wsu_b, wsd_b, ln_g, ln_b, out_so_far, first_token, n_total):
    n = h.shape[0]
    tm = TOKEN_TILE
    first_tile = first_token // tm
    row = lambda i: (i, 0)
    const = lambda i: (0, 0)
    hidden = wsg_b.shape[1]
    in_specs = [
        pl.BlockSpec((tm, D_MODEL), row),
        pl.BlockSpec((TOP_K, tm, HALF), lambda i: (0, i, 0)),
        pl.BlockSpec((tm, TOP_K), row),
        pl.BlockSpec((D_MODEL, hidden), const),
        pl.BlockSpec((D_MODEL, hidden), const),
        pl.BlockSpec((hidden, D_MODEL), const),
        pl.BlockSpec((1, D_MODEL), const),
        pl.BlockSpec((1, D_MODEL), const),
    ]
    args = [h, yg, w_tok, wsg_b, wsu_b, wsd_b, ln_g, ln_b]
    aliases = {}
    if out_so_far is not None:
        in_specs.append(pl.BlockSpec(memory_space=pl.ANY))
        args.append(out_so_far)
        aliases = {len(args) - 1: 0}
    return pl.pallas_call(
        _combine_kernel,
        grid=(n // tm,),
        in_specs=in_specs,
        out_specs=pl.BlockSpec((tm, D_MODEL), lambda i: (i + first_tile, 0)),
        out_shape=jax.ShapeDtypeStruct((n_total, D_MODEL), F32),
        input_output_aliases=aliases,
        compiler_params=_params("parallel"),
        name="combine_ln",
    )(*args)


def _group_layout(totals, n_tokens):
    t = EXPERT_TILE
    n_tiles = (TOP_K * n_tokens + N_EXPERTS * (t - 1)) // t
    counts = totals[:, 0].astype(jnp.int32)
    padded = ((counts + t - 1) // t) * t
    group_end = jnp.cumsum(padded)
    group_start = group_end - padded
    tile_start = jnp.arange(n_tiles, dtype=jnp.int32) * t
    tile_expert = jnp.minimum(jnp.sum((group_end[None, :] <= tile_start[:, None]).astype(jnp.int32), axis=1),
                              N_EXPERTS - 1)
    of_tile = tile_expert[:, None] == jnp.arange(N_EXPERTS, dtype=jnp.int32)[None, :]
    real_end = jnp.sum(jnp.where(of_tile, (group_start + counts)[None, :], 0), axis=1)
    tile_valid = jnp.clip(real_end - tile_start, 0, t).astype(jnp.int32)
    return group_start.astype(F32).reshape(N_EXPERTS, 1), tile_expert, tile_valid, n_tiles * t


def _rope_tables(seq):
    inv_freq = ROPE_THETA ** (-jnp.arange(0, ROPE_DIM, 2, dtype=F32) / ROPE_DIM)
    ang = jnp.arange(seq).astype(F32)[:, None] * inv_freq[None, :]
    cos, sin = jnp.cos(ang), jnp.sin(ang)
    rest = HEAD_DIM - ROPE_DIM
    zeros = jnp.zeros((seq, ROPE_HALF), F32)
    cos_t = jnp.concatenate([cos, cos, jnp.ones((seq, rest), F32)], axis=1)
    sa_t = jnp.concatenate([-sin, zeros, jnp.zeros((seq, rest), F32)], axis=1)
    sb_t = jnp.concatenate([zeros, sin, jnp.zeros((seq, rest), F32)], axis=1)
    return cos_t, sa_t, sb_t


def _layer(x2, batch, seq, w_in, b_gate, w_attn_o, w_dw, b_dw, conv_ln_g, conv_ln_b, w_conv_o, w_out,
           ln1_g, ln1_b, w_router, router_bias, w_exp_gate, w_exp_up, w_exp_down,
           w_sh_gate, w_sh_up, w_sh_down, ln2_g, ln2_b):
    n = x2.shape[0]
    row = lambda v: v.reshape(1, -1)
    cos, sa, sb = _rope_tables(seq)
    q, k, v, u, gate_a, gate_c, k_mean, w_gate_b, w_up_b, w_down_b = _in_projection(
        x2, w_in.astype(BF16), row(b_gate), cos, sa, sb, w_exp_gate, w_exp_up, w_exp_down, seq)
    w_taps = jnp.broadcast_to(w_dw.reshape(CONV_WIDTH, 1, D_MODEL), (CONV_WIDTH, SUBLANES, D_MODEL))
    conv_gated = _conv_branch(u, w_taps, row(b_dw), row(conv_ln_g),
                              row(conv_ln_b), w_conv_o.astype(BF16), gate_c, batch, seq)
    attn = _moba_attention(q, k, v, k_mean, batch, seq)
    w_attn_o_b, w_out_b, w_router_t = w_attn_o.astype(BF16), w_out.astype(BF16), w_router.T
    shared_w = (w_sh_gate.astype(BF16), w_sh_up.astype(BF16), w_sh_down.astype(BF16))
    n_chunk = n // MOE_CHUNKS
    assert n_chunk * MOE_CHUNKS == n and n_chunk % TOKEN_TILE == 0 and n_chunk % POSITION_TILE == 0
    out = None
    for c in range(MOE_CHUNKS):
        first = c * n_chunk
        h, h_packed, top_idx, top_w, rank, totals = _merge_ln_router(
            attn, gate_a, conv_gated, x2, w_attn_o_b, w_out_b, row(ln1_g), row(ln1_b),
            w_router_t, router_bias.reshape(N_EXPERTS, 1), first, n_chunk)
        group_start, tile_expert, tile_valid, n_rows = _group_layout(totals, n_chunk)
        pos = _positions(top_idx, rank, group_start).reshape(TOP_K * n_chunk)
        xs = _sc_scatter_rows(h_packed, pos, n_rows)
        ys = _grouped_experts(tile_expert, tile_valid, xs, w_gate_b, w_up_b, w_down_b)
        yg = _sc_gather_rows(ys, pos).reshape(TOP_K, n_chunk, HALF)
        out = _combine_ln(h, yg, top_w.T, *shared_w, row(ln2_g), row(ln2_b), out, first, n)
    return out


def kernel(x, w_in, b_gate, w_attn_o, w_dw, b_dw, conv_ln_g, conv_ln_b, w_conv_o, w_out, ln1_g, ln1_b,
           w_router, router_bias, w_exp_gate, w_exp_up, w_exp_down, w_sh_gate, w_sh_up, w_sh_down,
           ln2_g, ln2_b):
    batch, seq, d = x.shape
    assert d == D_MODEL and seq % MOBA_BLOCK == 0 and seq % TOKEN_TILE == 0
    assert w_in.shape[0] == DEPTH
    x2 = x.reshape(batch * seq, d)
    for l in range(DEPTH):
        x2 = _layer(x2, batch, seq, w_in[l], b_gate[l], w_attn_o[l], w_dw[l], b_dw[l], conv_ln_g[l],
                    conv_ln_b[l], w_conv_o[l], w_out[l], ln1_g[l], ln1_b[l], w_router[l], router_bias[l],
                    w_exp_gate[l], w_exp_up[l], w_exp_down[l], w_sh_gate[l], w_sh_up[l], w_sh_down[l],
                    ln2_g[l], ln2_b[l])
    return x2.reshape(batch, seq, d)
```

```python
import functools

import jax
import jax.numpy as jnp
from jax import lax
from jax.experimental import pallas as pl
from jax.experimental.pallas import tpu as pltpu
from jax.experimental.pallas import tpu_sc as plsc

F32 = jnp.float32
BF16 = jnp.bfloat16
NEG_INF = float("-inf")
MASKED = -1e30
LOG2_E = 1.4426950408889634

D_MODEL = 1024
N_HEADS = 8
HEAD_DIM = 128
ROPE_THETA = 500000.0
ROPE_DIM = HEAD_DIM // 4
ROPE_HALF = ROPE_DIM // 2
MOBA_BLOCK = 256
MOBA_TOPK = 3
CONV_WIDTH = 31
SUBLANES = 8
CONV_HALO = 32
CONV_ROWS = 32
N_EXPERTS = 256
TOP_K = 8
N_GROUPS = 8
GROUP_SIZE = N_EXPERTS // N_GROUPS
TOPK_GROUPS = 4
EXPERT_HIDDEN = 256
ROUTED_SCALE = 2.5
LN_EPS = 1e-5
DEPTH = 1
DEEPNORM_ALPHA = (2 * DEPTH) ** 0.25
HALF = D_MODEL // 2

TOKEN_TILE = 256
MERGE_TILE = 512
EXPERT_TILE = 512
EXPERT_CHAIN = 256
POSITION_TILE = 1024
MOE_CHUNKS = 2
HEADS_PER_STEP = 2
SC_CORES = 2
SC_SUBCORES = 16
SC_CHUNK = 64
VMEM_LIMIT = 56 * 1024 * 1024


def _params(*semantics):
    return pltpu.CompilerParams(dimension_semantics=semantics, vmem_limit_bytes=VMEM_LIMIT)


def _sigmoid(x):
    return 1.0 / (1.0 + jnp.exp(-x))


def _layer_norm(x, g, b):
    mu = jnp.mean(x, axis=-1, keepdims=True)
    xc = x - mu
    var = jnp.mean(xc * xc, axis=-1, keepdims=True)
    return xc * lax.rsqrt(var + LN_EPS) * g + b


def _pack_bf16_halves(y):
    lo = lax.bitcast_convert_type(y[:, :HALF].astype(BF16).astype(F32), jnp.uint32)
    hi = lax.bitcast_convert_type(y[:, HALF:].astype(BF16).astype(F32), jnp.uint32)
    return (hi & jnp.uint32(0xFFFF0000)) | (lo >> 16)


def _unpack_bf16_halves(p):
    lo = lax.bitcast_convert_type(p << 16, F32)
    hi = lax.bitcast_convert_type(p & jnp.uint32(0xFFFF0000), F32)
    return lo, hi


def _inproj_kernel(x_ref, w_ref, bg_ref, cos_ref, sa_ref, sb_ref, wg_ref, wu_ref, wd_ref,
                   q_ref, k_ref, v_ref, u_ref, ga_ref, gc_ref, km_ref, wgb_ref, wub_ref, wdb_ref):
    wgb_ref[...] = wg_ref[...].astype(BF16)
    wub_ref[...] = wu_ref[...].astype(BF16)
    wdb_ref[...] = wd_ref[...].astype(BF16)

    tm = x_ref.shape[0]
    xb = x_ref[...].astype(BF16)

    def proj(c):
        return jnp.dot(xb, w_ref[:, c * D_MODEL:(c + 1) * D_MODEL], preferred_element_type=F32)

    cos = cos_ref[...]
    sa = sa_ref[...]
    sb = sb_ref[...]

    def rope_head(t):
        return (t * cos + pltpu.roll(t, HEAD_DIM - ROPE_HALF, 1) * sa
                + pltpu.roll(t, ROPE_HALF, 1) * sb)

    q = proj(0)
    for h in range(N_HEADS):
        sl = slice(h * HEAD_DIM, (h + 1) * HEAD_DIM)
        q_ref[:, sl] = rope_head(q[:, sl]).astype(BF16)
    k = proj(1)
    for h in range(N_HEADS):
        sl = slice(h * HEAD_DIM, (h + 1) * HEAD_DIM)
        kr = rope_head(k[:, sl])
        k_ref[:, sl] = kr.astype(BF16)
        for g in range(tm // MOBA_BLOCK):
            km_ref[g, :, sl] = jnp.mean(kr[g * MOBA_BLOCK:(g + 1) * MOBA_BLOCK], axis=0, keepdims=True)
    v_ref[...] = proj(2).astype(BF16)
    u_ref[...] = proj(3) * _sigmoid(proj(4))
    ga_ref[...] = _sigmoid(proj(5) + bg_ref[:, :D_MODEL]).astype(BF16)
    gc_ref[...] = _sigmoid(proj(6) + bg_ref[:, D_MODEL:]).astype(BF16)


def _in_projection(x2, w_in_b, b_gate, cos, sa, sb, w_exp_gate, w_exp_up, w_exp_down, seq):
    n = x2.shape[0]
    tm = TOKEN_TILE
    n_cols = w_in_b.shape[1]
    tiles_per_seq = seq // tm
    steps = n // tm
    per_step = -(-N_EXPERTS // steps)
    assert N_EXPERTS % per_step == 0
    row = lambda i: (i, 0)
    const = lambda i: (0, 0)
    pos = lambda i: (i % tiles_per_seq, 0)
    experts = lambda i: (jnp.minimum(i, N_EXPERTS // per_step - 1), 0, 0)
    tok_bf16 = jax.ShapeDtypeStruct((n, D_MODEL), BF16)
    up_block = (per_step, D_MODEL, EXPERT_HIDDEN)
    down_block = (per_step, EXPERT_HIDDEN, D_MODEL)
    return pl.pallas_call(
        _inproj_kernel,
        grid=(steps,),
        in_specs=[
            pl.BlockSpec((tm, D_MODEL), row),
            pl.BlockSpec((D_MODEL, n_cols), const),
            pl.BlockSpec((1, 2 * D_MODEL), const),
            pl.BlockSpec((tm, HEAD_DIM), pos),
            pl.BlockSpec((tm, HEAD_DIM), pos),
            pl.BlockSpec((tm, HEAD_DIM), pos),
            pl.BlockSpec(up_block, experts),
            pl.BlockSpec(up_block, experts),
            pl.BlockSpec(down_block, experts),
        ],
        out_specs=[
            pl.BlockSpec((tm, D_MODEL), row),
            pl.BlockSpec((tm, D_MODEL), row),
            pl.BlockSpec((tm, D_MODEL), row),
            pl.BlockSpec((tm, D_MODEL), row),
            pl.BlockSpec((tm, D_MODEL), row),
            pl.BlockSpec((tm, D_MODEL), row),
            pl.BlockSpec((tm // MOBA_BLOCK, 1, D_MODEL), lambda i: (i, 0, 0)),
            pl.BlockSpec(up_block, experts),
            pl.BlockSpec(up_block, experts),
            pl.BlockSpec(down_block, experts),
        ],
        out_shape=[tok_bf16, tok_bf16, tok_bf16,
                   jax.ShapeDtypeStruct((n, D_MODEL), F32),
                   tok_bf16, tok_bf16,
                   jax.ShapeDtypeStruct((n // MOBA_BLOCK, 1, D_MODEL), F32),
                   jax.ShapeDtypeStruct(w_exp_gate.shape, BF16),
                   jax.ShapeDtypeStruct(w_exp_up.shape, BF16),
                   jax.ShapeDtypeStruct(w_exp_down.shape, BF16)],
        compiler_params=_params("arbitrary"),
        name="in_projection",
    )(x2, w_in_b, b_gate, cos, sa, sb, w_exp_gate, w_exp_up, w_exp_down)


def _conv_kernel(u_ref, wdw_ref, bdw_ref, lng_ref, lnb_ref, wo_ref, gc_ref, o_ref, buf_ref, sh_ref, y_ref):
    ts = u_ref.shape[0]
    s = pl.program_id(1)

    @pl.when(s == 0)
    def _():
        buf_ref[0:CONV_HALO, :] = jnp.zeros((CONV_HALO, D_MODEL), F32)

    @pl.when(s > 0)
    def _():
        buf_ref[0:CONV_HALO, :] = buf_ref[ts:ts + CONV_HALO, :]

    buf_ref[CONV_HALO:CONV_HALO + ts, :] = u_ref[...]

    span = ts + CONV_HALO - SUBLANES
    for b in range(1, SUBLANES):
        sh_ref[b - 1, 0:span, :] = buf_ref[b:b + span, :]

    base = CONV_HALO - (CONV_WIDTH - 1)
    for c in range(ts // CONV_ROWS):
        r0 = c * CONV_ROWS
        acc = jnp.zeros((CONV_ROWS // SUBLANES, SUBLANES, D_MODEL), F32)
        for j in range(CONV_WIDTH):
            shift = (base + j) % SUBLANES
            row = r0 + base + j - shift
            src = buf_ref if shift == 0 else sh_ref.at[shift - 1]
            tap = src[row:row + CONV_ROWS, :].reshape(CONV_ROWS // SUBLANES, SUBLANES, D_MODEL)
            acc = acc + tap * wdw_ref[j]
        acc = acc.reshape(CONV_ROWS, D_MODEL)
        y = _layer_norm(acc + bdw_ref[...], lng_ref[...], lnb_ref[...])
        y_ref[r0:r0 + CONV_ROWS, :] = (y * _sigmoid(y)).astype(BF16)
    z = jnp.dot(y_ref[...], wo_ref[...], preferred_element_type=F32)
    o_ref[...] = (z * gc_ref[...].astype(F32)).astype(BF16)


def _conv_branch(u, w_dw, b_dw, ln_g, ln_b, w_o_b, gate_c, batch, seq):
    n = u.shape[0]
    ts = TOKEN_TILE
    tiles_per_seq = seq // ts
    row = lambda b, s: (b * tiles_per_seq + s, 0)
    const = lambda b, s: (0, 0)
    return pl.pallas_call(
        _conv_kernel,
        grid=(batch, tiles_per_seq),
        in_specs=[
            pl.BlockSpec((ts, D_MODEL), row),
            pl.BlockSpec((CONV_WIDTH, SUBLANES, D_MODEL), lambda b, s: (0, 0, 0)),
            pl.BlockSpec((1, D_MODEL), const),
            pl.BlockSpec((1, D_MODEL), const),
            pl.BlockSpec((1, D_MODEL), const),
            pl.BlockSpec((D_MODEL, D_MODEL), const),
            pl.BlockSpec((ts, D_MODEL), row),
        ],
        out_specs=pl.BlockSpec((ts, D_MODEL), row),
        out_shape=jax.ShapeDtypeStruct((n, D_MODEL), BF16),
        scratch_shapes=[pltpu.VMEM((ts + CONV_HALO, D_MODEL), F32),
                        pltpu.VMEM((SUBLANES - 1, ts + CONV_HALO - SUBLANES, D_MODEL), F32),
                        pltpu.VMEM((ts, D_MODEL), BF16)],
        compiler_params=_params("parallel", "arbitrary"),
        name="conv_branch",
    )(u, w_dw, b_dw, ln_g, ln_b, w_o_b, gate_c)


def _attn_kernel(q_ref, k_ref, v_ref, km_ref, o_ref, *, n_blk):
    blk = MOBA_BLOCK
    seq = n_blk * blk
    k_sel = min(MOBA_TOPK, n_blk)
    exp2_scale = HEAD_DIM ** -0.5 * LOG2_E
    nt_dims = (((1,), (1,)), ((), ()))
    n_lane = HEAD_DIM

    assert blk & (blk - 1) == 0
    blk_shift = blk.bit_length() - 1
    n_sub = -(-n_blk // SUBLANES) * SUBLANES

    blk_t = lax.broadcasted_iota(jnp.int32, (n_sub, seq), 0)
    own_t = lax.broadcasted_iota(jnp.int32, (n_sub, seq), 1) >> blk_shift
    past = blk_t < own_t
    blk_f = blk_t.astype(F32)
    blk_id = lax.broadcasted_iota(jnp.int32, (seq, n_lane), 1)
    own_id = lax.broadcasted_iota(jnp.int32, (seq, n_lane), 0) >> blk_shift
    k_blk = jnp.where(blk_id == own_id, 1.0, 0.0).astype(BF16)
    ones = jnp.ones((seq, n_lane), BF16)
    causal = (lax.broadcasted_iota(jnp.int32, (blk, blk), 1)
              <= lax.broadcasted_iota(jnp.int32, (blk, blk), 0))

    def prepare(lanes):
        q_all = q_ref[:, lanes]
        km = km_ref[:, 0, lanes]
        if n_sub > n_blk:
            km = jnp.concatenate([km, jnp.zeros((n_sub - n_blk, HEAD_DIM), F32)], axis=0)
        km_hi = km.astype(BF16)
        km_lo = (km - km_hi.astype(F32)).astype(BF16)
        gate = (lax.dot_general(km_hi, q_all, nt_dims, preferred_element_type=F32)
                + lax.dot_general(km_lo, q_all, nt_dims, preferred_element_type=F32))
        g = jnp.where(past, gate, NEG_INF)
        sel = jnp.zeros((n_sub, seq), F32)
        for _ in range(k_sel):
            mx = jnp.max(g, axis=0, keepdims=True)
            first = jnp.min(jnp.where(g == mx, blk_f, float(n_sub)), axis=0, keepdims=True)
            pick = blk_f == first
            sel = jnp.where(pick, 1.0, sel)
            g = jnp.where(pick, NEG_INF, g)
        visible = ((sel > 0.0) & past) | (blk_t == own_t)
        bias_t = jnp.concatenate([jnp.where(visible, 0.0, MASKED), jnp.zeros((n_lane - n_sub, seq), F32)],
                                 axis=0)
        q_bias = bias_t.T.astype(BF16)
        k_aug = jnp.concatenate([k_ref[:, lanes], k_blk], axis=1)
        v_aug = jnp.concatenate([v_ref[:, lanes], ones], axis=1)
        return q_all, q_bias, k_aug, v_aug, lanes

    def scores(head, i):
        q_all, q_bias, k_aug, _, _ = head
        rows = slice(i * blk, (i + 1) * blk)
        q_aug = jnp.concatenate([q_all[rows], q_bias[rows]], axis=1)
        return lax.dot_general(q_aug, k_aug[:(i + 1) * blk], nt_dims, preferred_element_type=F32)

    def finish(head, i, p):
        pv = jnp.dot(p, head[3][:(i + 1) * blk], preferred_element_type=F32)
        o_ref[i * blk:(i + 1) * blk, head[4]] = (pv[:, :HEAD_DIM] / pv[:, HEAD_DIM:HEAD_DIM + 1]).astype(BF16)

    def softmax_numerator(raw, i):
        own = jnp.where(causal, raw[:, i * blk:], MASKED)
        parts = [raw[:, :i * blk], own] if i else [own]
        m = jnp.max(own, axis=1, keepdims=True)
        if i:
            m = jnp.maximum(m, jnp.max(parts[0], axis=1, keepdims=True))
        return jnp.concatenate([jnp.exp2((t - m) * exp2_scale) for t in parts], axis=1).astype(BF16)

    heads = [prepare(slice(hh * HEAD_DIM, (hh + 1) * HEAD_DIM)) for hh in range(HEADS_PER_STEP)]
    raw_next = [scores(head, 0) for head in heads]
    p_prev = None
    for i in range(n_blk):
        raws = raw_next
        if i + 1 < n_blk:
            raw_next = [scores(head, i + 1) for head in heads]
        if p_prev is not None:
            for head, p in zip(heads, p_prev):
                finish(head, i - 1, p)
        p_prev = [softmax_numerator(raw, i) for raw in raws]
    for head, p in zip(heads, p_prev):
        finish(head, n_blk - 1, p)


def _moba_attention(q, k, v, k_mean, batch, seq):
    n = q.shape[0]
    n_blk = seq // MOBA_BLOCK
    width = HEADS_PER_STEP * HEAD_DIM
    seq_head = lambda b, h: (b, h)
    return pl.pallas_call(
        functools.partial(_attn_kernel, n_blk=n_blk),
        grid=(batch, N_HEADS // HEADS_PER_STEP),
        in_specs=[
            pl.BlockSpec((seq, width), seq_head),
            pl.BlockSpec((seq, width), seq_head),
            pl.BlockSpec((seq, width), seq_head),
            pl.BlockSpec((n_blk, 1, width), lambda b, h: (b, 0, h)),
        ],
        out_specs=pl.BlockSpec((seq, width), seq_head),
        out_shape=jax.ShapeDtypeStruct((n, D_MODEL), BF16),
        compiler_params=_params("parallel", "parallel"),
        name="moba_attention",
    )(q, k, v, k_mean)


def _merge_kernel(attn_ref, ga_ref, cg_ref, x_ref, wao_ref, wout_ref, g1_ref, b1_ref,
                  wr_ref, rb_ref, h_ref, hp_ref, idx_ref, wgt_ref, rank_ref, total_ref, count_ref, hprev_ref):
    tm = x_ref.shape[0]
    step = pl.program_id(0)

    @pl.when(step == 0)
    def _():
        count_ref[...] = jnp.zeros((N_EXPERTS, 1), F32)
        hprev_ref[...] = jnp.zeros(hprev_ref.shape, F32)

    h = hprev_ref[...]

    a = jnp.dot(attn_ref[...], wao_ref[...], preferred_element_type=F32)
    merged = ga_ref[...].astype(F32) * a + cg_ref[...].astype(F32)
    y = jnp.dot(merged.astype(BF16), wout_ref[...], preferred_element_type=F32)
    h_new = _layer_norm(DEEPNORM_ALPHA * x_ref[...] + y, g1_ref[...], b1_ref[...])
    h_ref[...] = h_new
    hp_ref[...] = _pack_bf16_halves(h_new)
    hprev_ref[...] = h_new

    nt_dims = (((1,), (1,)), ((), ()))
    h_hi = h.astype(BF16)
    h_lo = (h - h_hi.astype(F32)).astype(BF16)
    w = wr_ref[...]
    w_hi = w.astype(BF16)
    w_lo = (w - w_hi.astype(F32)).astype(BF16)
    logits = (lax.dot_general(w_hi, h_hi, nt_dims, preferred_element_type=F32)
              + lax.dot_general(w_hi, h_lo, nt_dims, preferred_element_type=F32)
              + lax.dot_general(w_lo, h_hi, nt_dims, preferred_element_type=F32))
    scores = _sigmoid(logits)
    biased = scores + rb_ref[...]

    g3 = biased.reshape(N_GROUPS, GROUP_SIZE, tm)
    m1 = jnp.max(g3, axis=1, keepdims=True)
    is_max = g3 == m1
    n_max = jnp.sum(jnp.where(is_max, 1.0, 0.0), axis=1, keepdims=True)
    m2 = jnp.max(jnp.where(is_max, NEG_INF, g3), axis=1, keepdims=True)
    grp = (m1 + jnp.where(n_max >= 2.0, m1, m2)).reshape(N_GROUPS, tm)

    gid = lax.broadcasted_iota(jnp.int32, (N_GROUPS, tm), 0)
    rank = jnp.zeros((N_GROUPS, tm), F32)
    for o in range(N_GROUPS):
        other = grp[o:o + 1, :]
        ahead = (other > grp) | ((other == grp) & (o < gid))
        rank = rank + jnp.where(ahead, 1.0, 0.0)
    grp_keep = jnp.where(rank < float(TOPK_GROUPS), 1.0, 0.0)
    keep = jnp.broadcast_to(grp_keep.reshape(N_GROUPS, 1, tm),
                            (N_GROUPS, GROUP_SIZE, tm)).reshape(N_EXPERTS, tm)
    cand = jnp.where(keep > 0.0, biased, NEG_INF)

    eid = lax.broadcasted_iota(jnp.int32, (N_EXPERTS, tm), 0).astype(F32)
    chosen = jnp.zeros((N_EXPERTS, tm), F32)
    firsts = []
    for r in range(TOP_K):
        mx = jnp.max(cand, axis=0, keepdims=True)
        first = jnp.min(jnp.where(cand == mx, eid, float(N_EXPERTS)), axis=0, keepdims=True)
        pick = eid == first
        firsts.append(first)
        idx_ref[r:r + 1, :] = first.astype(jnp.int32)
        wgt_ref[r:r + 1, :] = jnp.sum(jnp.where(pick, scores, 0.0), axis=0, keepdims=True)
        cand = jnp.where(pick, NEG_INF, cand)
        chosen = jnp.where(pick, 1.0, chosen)
    top_s = wgt_ref[...]
    wgt_ref[...] = top_s / (jnp.sum(top_s, axis=0, keepdims=True) + 1e-20) * ROUTED_SCALE

    earlier = (lax.broadcasted_iota(jnp.int32, (tm, tm), 0)
               < lax.broadcasted_iota(jnp.int32, (tm, tm), 1))
    before = jnp.dot(chosen.astype(BF16), jnp.where(earlier, 1.0, 0.0).astype(BF16),
                     preferred_element_type=F32) + count_ref[...]
    for r in range(TOP_K):
        rank_ref[r:r + 1, :] = jnp.sum(jnp.where(eid == firsts[r], before, 0.0),
                                       axis=0, keepdims=True).astype(jnp.int32)
    counted = jnp.where(step > 0, 1.0, 0.0)
    total = count_ref[...] + counted * jnp.sum(chosen, axis=1, keepdims=True)
    count_ref[...] = total
    total_ref[...] = total


def _merge_ln_router(attn, gate_a, conv_gated, x2, w_attn_o_b, w_out_b, ln_g, ln_b, w_router_t, router_bias,
                     first_token, n):
    tm = MERGE_TILE
    first_tile = first_token // tm
    n_tiles = n // tm
    row_in = lambda i: (jnp.minimum(i, n_tiles - 1) + first_tile, 0)
    row = lambda i: (jnp.minimum(i, n_tiles - 1), 0)
    const = lambda i: (0, 0)
    col = lambda i: (0, jnp.maximum(i - 1, 0))
    return pl.pallas_call(
        _merge_kernel,
        grid=(n_tiles + 1,),
        in_specs=[
            pl.BlockSpec((tm, D_MODEL), row_in),
            pl.BlockSpec((tm, D_MODEL), row_in),
            pl.BlockSpec((tm, D_MODEL), row_in),
            pl.BlockSpec((tm, D_MODEL), row_in),
            pl.BlockSpec((D_MODEL, D_MODEL), const),
            pl.BlockSpec((D_MODEL, D_MODEL), const),
            pl.BlockSpec((1, D_MODEL), const),
            pl.BlockSpec((1, D_MODEL), const),
            pl.BlockSpec((N_EXPERTS, D_MODEL), const),
            pl.BlockSpec((N_EXPERTS, 1), const),
        ],
        out_specs=[
            pl.BlockSpec((tm, D_MODEL), row),
            pl.BlockSpec((tm, HALF), row),
            pl.BlockSpec((TOP_K, tm), col),
            pl.BlockSpec((TOP_K, tm), col),
            pl.BlockSpec((TOP_K, tm), col),
            pl.BlockSpec((N_EXPERTS, 1), const),
        ],
        out_shape=[
            jax.ShapeDtypeStruct((n, D_MODEL), F32),
            jax.ShapeDtypeStruct((n, HALF), jnp.uint32),
            jax.ShapeDtypeStruct((TOP_K, n), jnp.int32),
            jax.ShapeDtypeStruct((TOP_K, n), F32),
            jax.ShapeDtypeStruct((TOP_K, n), jnp.int32),
            jax.ShapeDtypeStruct((N_EXPERTS, 1), F32),
        ],
        scratch_shapes=[pltpu.VMEM((N_EXPERTS, 1), F32), pltpu.VMEM((tm, D_MODEL), F32)],
        compiler_params=_params("arbitrary"),
        name="merge_ln_router",
    )(attn, gate_a, conv_gated, x2, w_attn_o_b, w_out_b, ln_g, ln_b, w_router_t, router_bias)


def _sc_mesh():
    return plsc.VectorSubcoreMesh(core_axis_name="c", subcore_axis_name="s",
                                  num_cores=SC_CORES, num_subcores=SC_SUBCORES)


def _sc_worker_base(rows_per_worker):
    return (lax.axis_index("s") * SC_CORES + lax.axis_index("c")) * rows_per_worker


def _sc_gather_rows(table, idx):
    m = idx.shape[0]
    width = table.shape[1]
    per_worker = m // (SC_CORES * SC_SUBCORES)
    assert per_worker * SC_CORES * SC_SUBCORES == m and per_worker % SC_CHUNK == 0

    @functools.partial(
        pl.kernel, mesh=_sc_mesh(),
        out_type=jax.ShapeDtypeStruct((m, width), table.dtype),
        scratch_types=[pltpu.VMEM((SC_CHUNK,), jnp.int32),
                       pltpu.VMEM((SC_CHUNK, width), table.dtype),
                       pltpu.SemaphoreType.DMA],
        name="sc_gather_rows")
    def gather(table_hbm, idx_hbm, out_hbm, idx_v, rows_v, sem):
        base = _sc_worker_base(per_worker)

        @pl.loop(0, per_worker // SC_CHUNK)
        def _(c):
            off = pl.multiple_of(base + c * SC_CHUNK, SC_CHUNK)
            pltpu.sync_copy(idx_hbm.at[pl.ds(off, SC_CHUNK)], idx_v)
            pltpu.async_copy(table_hbm.at[idx_v], rows_v, sem).wait()
            pltpu.sync_copy(rows_v, out_hbm.at[pl.ds(off, SC_CHUNK)])

    return gather(table, idx)


def _sc_scatter_rows(rows, pos, n_out):
    n, width = rows.shape
    per_worker = n // (SC_CORES * SC_SUBCORES)
    assert per_worker * SC_CORES * SC_SUBCORES == n and per_worker % SC_CHUNK == 0

    @functools.partial(
        pl.kernel, mesh=_sc_mesh(),
        out_type=jax.ShapeDtypeStruct((n_out, width), rows.dtype),
        scratch_types=[pltpu.VMEM((SC_CHUNK,), jnp.int32),
                       pltpu.VMEM((SC_CHUNK, width), rows.dtype),
                       pltpu.SemaphoreType.DMA],
        name="sc_scatter_rows")
    def scatter(rows_hbm, pos_hbm, out_hbm, idx_v, rows_v, sem):
        base = _sc_worker_base(per_worker)

        @pl.loop(0, per_worker // SC_CHUNK)
        def _(c):
            off = pl.multiple_of(base + c * SC_CHUNK, SC_CHUNK)
            pltpu.sync_copy(rows_hbm.at[pl.ds(off, SC_CHUNK)], rows_v)
            for r in range(TOP_K):
                pltpu.sync_copy(pos_hbm.at[pl.ds(r * n + off, SC_CHUNK)], idx_v)
                pltpu.async_copy(rows_v, out_hbm.at[idx_v], sem).wait()

    return scatter(rows, pos)


def _position_kernel(idx_ref, rank_ref, start_ref, pos_ref):
    tl = idx_ref.shape[1]
    eid = lax.broadcasted_iota(jnp.int32, (N_EXPERTS, tl), 0)
    start = start_ref[...]
    for r in range(TOP_K):
        here = jnp.sum(jnp.where(eid == idx_ref[r:r + 1, :], start, 0.0), axis=0, keepdims=True)
        pos_ref[r:r + 1, :] = here.astype(jnp.int32) + rank_ref[r:r + 1, :]


def _positions(top_idx, rank, group_start):
    n = top_idx.shape[1]
    tl = POSITION_TILE
    col = lambda i: (0, i)
    return pl.pallas_call(
        _position_kernel,
        grid=(n // tl,),
        in_specs=[pl.BlockSpec((TOP_K, tl), col), pl.BlockSpec((TOP_K, tl), col),
                  pl.BlockSpec((N_EXPERTS, 1), lambda i: (0, 0))],
        out_specs=pl.BlockSpec((TOP_K, tl), col),
        out_shape=jax.ShapeDtypeStruct((TOP_K, n), jnp.int32),
        compiler_params=_params("parallel"),
        name="positions",
    )(top_idx, rank, group_start)


def _expert_kernel(te_ref, tv_ref, xs_ref, wg_ref, wu_ref, wd_ref, y_ref, act_ref):
    i = pl.program_id(0)
    valid = tv_ref[i]
    valid_prev = tv_ref[jnp.maximum(i - 1, 0)]

    @pl.when(i == 0)
    def _():
        act_ref[...] = jnp.zeros(act_ref.shape, BF16)

    @pl.when((valid > 0) | (valid_prev > 0))
    def _():
        act_prev = act_ref[...]
        wd = wd_ref[0]
        for c in range(EXPERT_TILE // EXPERT_CHAIN):
            rows = slice(c * EXPERT_CHAIN, (c + 1) * EXPERT_CHAIN)
            y_ref[rows, :] = _pack_bf16_halves(jnp.dot(act_prev[rows], wd, preferred_element_type=F32))
        wg = wg_ref[0]
        wu = wu_ref[0]
        for c in range(EXPERT_TILE // EXPERT_CHAIN):
            rows = slice(c * EXPERT_CHAIN, (c + 1) * EXPERT_CHAIN)
            live = lax.broadcasted_iota(jnp.int32, (EXPERT_CHAIN, HALF), 0) < valid - c * EXPERT_CHAIN
            lo, hi = _unpack_bf16_halves(jnp.where(live, xs_ref[rows, :], jnp.uint32(0)))
            x = jnp.concatenate([lo, hi], axis=1).astype(BF16)
            g = jnp.dot(x, wg, preferred_element_type=F32)
            u = jnp.dot(x, wu, preferred_element_type=F32)
            act_ref[rows, :] = (g * _sigmoid(g) * u).astype(BF16)

    @pl.when((valid == 0) & (valid_prev == 0))
    def _():
        y_ref[...] = jnp.zeros(y_ref.shape, jnp.uint32)


def _grouped_experts(tile_expert, tile_valid, xs, wg, wu, wd):
    p = xs.shape[0]
    t = EXPERT_TILE
    n_tiles = p // t
    tile_expert = jnp.concatenate([tile_expert, tile_expert[-1:]])
    tile_valid = jnp.concatenate([tile_valid, jnp.zeros((1,), jnp.int32)])
    row_in = lambda i, te, tv: (jnp.minimum(i, n_tiles - 1), 0)
    row_out = lambda i, te, tv: (jnp.maximum(i - 1, 0), 0)
    expert = lambda i, te, tv: (te[i], 0, 0)
    expert_prev = lambda i, te, tv: (te[jnp.maximum(i - 1, 0)], 0, 0)
    return pl.pallas_call(
        _expert_kernel,
        grid_spec=pltpu.PrefetchScalarGridSpec(
            num_scalar_prefetch=2,
            grid=(n_tiles + 1,),
            in_specs=[
                pl.BlockSpec((t, HALF), row_in),
                pl.BlockSpec((1, D_MODEL, EXPERT_HIDDEN), expert),
                pl.BlockSpec((1, D_MODEL, EXPERT_HIDDEN), expert),
                pl.BlockSpec((1, EXPERT_HIDDEN, D_MODEL), expert_prev),
            ],
            out_specs=pl.BlockSpec((t, HALF), row_out),
            scratch_shapes=[pltpu.VMEM((t, EXPERT_HIDDEN), BF16)],
        ),
        out_shape=jax.ShapeDtypeStruct((p, HALF), jnp.uint32),
        compiler_params=_params("arbitrary"),
        name="grouped_experts",
    )(tile_expert, tile_valid, xs, wg, wu, wd)


def _combine_kernel(h_ref, yg_ref, wt_ref, wsg_ref, wsu_ref, wsd_ref, g2_ref, b2_ref, *out_refs):
    o_ref = out_refs[-1]
    h = h_ref[...]
    hb = h.astype(BF16)
    g = jnp.dot(hb, wsg_ref[...], preferred_element_type=F32)
    u = jnp.dot(hb, wsu_ref[...], preferred_element_type=F32)
    shared = jnp.dot((g * _sigmoid(g) * u).astype(BF16), wsd_ref[...], preferred_element_type=F32)
    wt = wt_ref[...]
    r_lo = jnp.zeros((h.shape[0], HALF), F32)
    r_hi = jnp.zeros((h.shape[0], HALF), F32)
    for r in range(TOP_K):
        lo, hi = _unpack_bf16_halves(yg_ref[r])
        w = wt[:, r:r + 1]
        r_lo = r_lo + lo * w
        r_hi = r_hi + hi * w
    routed = jnp.concatenate([r_lo, r_hi], axis=1)
    o_ref[...] = _layer_norm(DEEPNORM_ALPHA * h + (shared + routed), g2_ref[...], b2_ref[...])


def _combine_ln(h, yg, w_tok, wsg_b, wsu_b, wsd_b, ln_g, ln_b, out_so_far, first_token, n_total):
    n = h.shape[0]
    tm = TOKEN_TILE
    first_tile = first_token // tm
    row = lambda i: (i, 0)
    const = lambda i: (0, 0)
    hidden = wsg_b.shape[1]
    in_specs = [
        pl.BlockSpec((tm, D_MODEL), row),
        pl.BlockSpec((TOP_K, tm, HALF), lambda i: (0, i, 0)),
        pl.BlockSpec((tm, TOP_K), row),
        pl.BlockSpec((D_MODEL, hidden), const),
        pl.BlockSpec((D_MODEL, hidden), const),
        pl.BlockSpec((hidden, D_MODEL), const),
        pl.BlockSpec((1, D_MODEL), const),
        pl.BlockSpec((1, D_MODEL), const),
    ]
    args = [h, yg, w_tok, wsg_b, wsu_b, wsd_b, ln_g, ln_b]
    aliases = {}
    if out_so_far is not None:
        in_specs.append(pl.BlockSpec(memory_space=pl.ANY))
        args.append(out_so_far)
        aliases = {len(args) - 1: 0}
    return pl.pallas_call(
        _combine_kernel,
        grid=(n // tm,),
        in_specs=in_specs,
        out_specs=pl.BlockSpec((tm, D_MODEL), lambda i: (i + first_tile, 0)),
        out_shape=jax.ShapeDtypeStruct((n_total, D_MODEL), F32),
        input_output_aliases=aliases,
        compiler_params=_params("parallel"),
        name="combine_ln",
    )(*args)


def _group_layout(totals, n_tokens):
    t = EXPERT_TILE
    n_tiles = (TOP_K * n_tokens + N_EXPERTS * (t - 1)) // t
    counts = totals[:, 0].astype(jnp.int32)
    padded = ((counts + t - 1) // t) * t
    group_end = jnp.cumsum(padded)
    group_start = group_end - padded
    tile_start = jnp.arange(n_tiles, dtype=jnp.int32) * t
    tile_expert = jnp.minimum(jnp.sum((group_end[None, :] <= tile_start[:, None]).astype(jnp.int32), axis=1),
                              N_EXPERTS - 1)
    of_tile = tile_expert[:, None] == jnp.arange(N_EXPERTS, dtype=jnp.int32)[None, :]
    real_end = jnp.sum(jnp.where(of_tile, (group_start + counts)[None, :], 0), axis=1)
    tile_valid = jnp.clip(real_end - tile_start, 0, t).astype(jnp.int32)
    return group_start.astype(F32).reshape(N_EXPERTS, 1), tile_expert, tile_valid, n_tiles * t


def _rope_tables(seq):
    inv_freq = ROPE_THETA ** (-jnp.arange(0, ROPE_DIM, 2, dtype=F32) / ROPE_DIM)
    ang = jnp.arange(seq).astype(F32)[:, None] * inv_freq[None, :]
    cos, sin = jnp.cos(ang), jnp.sin(ang)
    rest = HEAD_DIM - ROPE_DIM
    zeros = jnp.zeros((seq, ROPE_HALF), F32)
    cos_t = jnp.concatenate([cos, cos, jnp.ones((seq, rest), F32)], axis=1)
    sa_t = jnp.concatenate([-sin, zeros, jnp.zeros((seq, rest), F32)], axis=1)
    sb_t = jnp.concatenate([zeros, sin, jnp.zeros((seq, rest), F32)], axis=1)
    return cos_t, sa_t, sb_t


def _layer(x2, batch, seq, w_in, b_gate, w_attn_o, w_dw, b_dw, conv_ln_g, conv_ln_b, w_conv_o, w_out,
           ln1_g, ln1_b, w_router, router_bias, w_exp_gate, w_exp_up, w_exp_down,
           w_sh_gate, w_sh_up, w_sh_down, ln2_g, ln2_b):
    n = x2.shape[0]
    row = lambda v: v.reshape(1, -1)
    cos, sa, sb = _rope_tables(seq)
    q, k, v, u, gate_a, gate_c, k_mean, w_gate_b, w_up_b, w_down_b = _in_projection(
        x2, w_in.astype(BF16), row(b_gate), cos, sa, sb, w_exp_gate, w_exp_up, w_exp_down, seq)
    w_taps = jnp.broadcast_to(w_dw.reshape(CONV_WIDTH, 1, D_MODEL), (CONV_WIDTH, SUBLANES, D_MODEL))
    conv_gated = _conv_branch(u, w_taps, row(b_dw), row(conv_ln_g),
                              row(conv_ln_b), w_conv_o.astype(BF16), gate_c, batch, seq)
    attn = _moba_attention(q, k, v, k_mean, batch, seq)
    w_attn_o_b, w_out_b, w_router_t = w_attn_o.astype(BF16), w_out.astype(BF16), w_router.T
    shared_w = (w_sh_gate.astype(BF16), w_sh_up.astype(BF16), w_sh_down.astype(BF16))
    n_chunk = n // MOE_CHUNKS
    assert n_chunk * MOE_CHUNKS == n and n_chunk % TOKEN_TILE == 0 and n_chunk % POSITION_TILE == 0
    out = None
    for c in range(MOE_CHUNKS):
        first = c * n_chunk
        h, h_packed, top_idx, top_w, rank, totals = _merge_ln_router(
            attn, gate_a, conv_gated, x2, w_attn_o_b, w_out_b, row(ln1_g), row(ln1_b),
            w_router_t, router_bias.reshape(N_EXPERTS, 1), first, n_chunk)
        group_start, tile_expert, tile_valid, n_rows = _group_layout(totals, n_chunk)
        pos = _positions(top_idx, rank, group_start).reshape(TOP_K * n_chunk)
        xs = _sc_scatter_rows(h_packed, pos, n_rows)
        ys = _grouped_experts(tile_expert, tile_valid, xs, w_gate_b, w_up_b, w_down_b)
        yg = _sc_gather_rows(ys, pos).reshape(TOP_K, n_chunk, HALF)
        out = _combine_ln(h, yg, top_w.T, *shared_w, row(ln2_g), row(ln2_b), out, first, n)
    return out


def kernel(x, w_in, b_gate, w_attn_o, w_dw, b_dw, conv_ln_g, conv_ln_b, w_conv_o, w_out, ln1_g, ln1_b,
           w_router, router_bias, w_exp_gate, w_exp_up, w_exp_down, w_sh_gate, w_sh_up, w_sh_down,
           ln2_g, ln2_b):
    batch, seq, d = x.shape
    assert d == D_MODEL and seq % MOBA_BLOCK == 0 and seq % TOKEN_TILE == 0
    assert w_in.shape[0] == DEPTH
    x2 = x.reshape(batch * seq, d)
    for l in range(DEPTH):
        x2 = _layer(x2, batch, seq, w_in[l], b_gate[l], w_attn_o[l], w_dw[l], b_dw[l], conv_ln_g[l],
                    conv_ln_b[l], w_conv_o[l], w_out[l], ln1_g[l], ln1_b[l], w_router[l], router_bias[l],
                    w_exp_gate[l], w_exp_up[l], w_exp_down[l], w_sh_gate[l], w_sh_up[l], w_sh_down[l],
                    ln2_g[l], ln2_b[l])
    return x2.reshape(batch, seq, d)
```

```python
import functools

import jax
import jax.numpy as jnp
from jax import lax
from jax.experimental import pallas as pl
from jax.experimental.pallas import tpu as pltpu
from jax.experimental.pallas import tpu_sc as plsc

F32 = jnp.float32
BF16 = jnp.bfloat16
NEG_INF = float("-inf")
MASKED = -1e30
LOG2_E = 1.4426950408889634

D_MODEL = 1024
N_HEADS = 8
HEAD_DIM = 128
ROPE_THETA = 500000.0
ROPE_DIM = HEAD_DIM // 4
ROPE_HALF = ROPE_DIM // 2
MOBA_BLOCK = 256
MOBA_TOPK = 3
CONV_WIDTH = 31
SUBLANES = 8
CONV_HALO = 32
CONV_ROWS = 32
N_EXPERTS = 256
TOP_K = 8
N_GROUPS = 8
GROUP_SIZE = N_EXPERTS // N_GROUPS
TOPK_GROUPS = 4
EXPERT_HIDDEN = 256
ROUTED_SCALE = 2.5
LN_EPS = 1e-5
DEPTH = 1
DEEPNORM_ALPHA = (2 * DEPTH) ** 0.25
HALF = D_MODEL // 2

TOKEN_TILE = 256
MERGE_TILE = 512
EXPERT_TILE = 256
EXPERT_CHAIN = 256
POSITION_TILE = 1024
MOE_CHUNKS = 2
HEADS_PER_STEP = 2
SC_CORES = 2
SC_SUBCORES = 16
SC_CHUNK = 64
VMEM_LIMIT = 56 * 1024 * 1024


def _params(*semantics):
    return pltpu.CompilerParams(dimension_semantics=semantics, vmem_limit_bytes=VMEM_LIMIT)


def _sigmoid(x):
    return 1.0 / (1.0 + jnp.exp(-x))


def _layer_norm(x, g, b):
    mu = jnp.mean(x, axis=-1, keepdims=True)
    xc = x - mu
    var = jnp.mean(xc * xc, axis=-1, keepdims=True)
    return xc * lax.rsqrt(var + LN_EPS) * g + b


def _pack_bf16_halves(y):
    lo = lax.bitcast_convert_type(y[:, :HALF].astype(BF16).astype(F32), jnp.uint32)
    hi = lax.bitcast_convert_type(y[:, HALF:].astype(BF16).astype(F32), jnp.uint32)
    return (hi & jnp.uint32(0xFFFF0000)) | (lo >> 16)


def _unpack_bf16_halves(p):
    lo = lax.bitcast_convert_type(p << 16, F32)
    hi = lax.bitcast_convert_type(p & jnp.uint32(0xFFFF0000), F32)
    return lo, hi


def _inproj_kernel(x_ref, w_ref, bg_ref, cos_ref, sa_ref, sb_ref, wg_ref, wu_ref, wd_ref,
                   q_ref, k_ref, v_ref, u_ref, ga_ref, gc_ref, km_ref, wgb_ref, wub_ref, wdb_ref):
    wgb_ref[...] = wg_ref[...].astype(BF16)
    wub_ref[...] = wu_ref[...].astype(BF16)
    wdb_ref[...] = wd_ref[...].astype(BF16)

    tm = x_ref.shape[0]
    xb = x_ref[...].astype(BF16)

    def proj(c):
        return jnp.dot(xb, w_ref[:, c * D_MODEL:(c + 1) * D_MODEL], preferred_element_type=F32)

    cos = cos_ref[...]
    sa = sa_ref[...]
    sb = sb_ref[...]

    def rope_head(t):
        return (t * cos + pltpu.roll(t, HEAD_DIM - ROPE_HALF, 1) * sa
                + pltpu.roll(t, ROPE_HALF, 1) * sb)

    q = proj(0)
    for h in range(N_HEADS):
        sl = slice(h * HEAD_DIM, (h + 1) * HEAD_DIM)
        q_ref[:, sl] = rope_head(q[:, sl]).astype(BF16)
    k = proj(1)
    for h in range(N_HEADS):
        sl = slice(h * HEAD_DIM, (h + 1) * HEAD_DIM)
        kr = rope_head(k[:, sl])
        k_ref[:, sl] = kr.astype(BF16)
        for g in range(tm // MOBA_BLOCK):
            km_ref[g, :, sl] = jnp.mean(kr[g * MOBA_BLOCK:(g + 1) * MOBA_BLOCK], axis=0, keepdims=True)
    v_ref[...] = proj(2).astype(BF16)
    u_ref[...] = proj(3) * _sigmoid(proj(4))
    ga_ref[...] = _sigmoid(proj(5) + bg_ref[:, :D_MODEL]).astype(BF16)
    gc_ref[...] = _sigmoid(proj(6) + bg_ref[:, D_MODEL:]).astype(BF16)


def _in_projection(x2, w_in_b, b_gate, cos, sa, sb, w_exp_gate, w_exp_up, w_exp_down, seq):
    n = x2.shape[0]
    tm = TOKEN_TILE
    n_cols = w_in_b.shape[1]
    tiles_per_seq = seq // tm
    steps = n // tm
    per_step = -(-N_EXPERTS // steps)
    assert N_EXPERTS % per_step == 0
    row = lambda i: (i, 0)
    const = lambda i: (0, 0)
    pos = lambda i: (i % tiles_per_seq, 0)
    experts = lambda i: (jnp.minimum(i, N_EXPERTS // per_step - 1), 0, 0)
    tok_bf16 = jax.ShapeDtypeStruct((n, D_MODEL), BF16)
    up_block = (per_step, D_MODEL, EXPERT_HIDDEN)
    down_block = (per_step, EXPERT_HIDDEN, D_MODEL)
    return pl.pallas_call(
        _inproj_kernel,
        grid=(steps,),
        in_specs=[
            pl.BlockSpec((tm, D_MODEL), row),
            pl.BlockSpec((D_MODEL, n_cols), const),
            pl.BlockSpec((1, 2 * D_MODEL), const),
            pl.BlockSpec((tm, HEAD_DIM), pos),
            pl.BlockSpec((tm, HEAD_DIM), pos),
            pl.BlockSpec((tm, HEAD_DIM), pos),
            pl.BlockSpec(up_block, experts),
            pl.BlockSpec(up_block, experts),
            pl.BlockSpec(down_block, experts),
        ],
        out_specs=[
            pl.BlockSpec((tm, D_MODEL), row),
            pl.BlockSpec((tm, D_MODEL), row),
            pl.BlockSpec((tm, D_MODEL), row),
            pl.BlockSpec((tm, D_MODEL), row),
            pl.BlockSpec((tm, D_MODEL), row),
            pl.BlockSpec((tm, D_MODEL), row),
            pl.BlockSpec((tm // MOBA_BLOCK, 1, D_MODEL), lambda i: (i, 0, 0)),
            pl.BlockSpec(up_block, experts),
            pl.BlockSpec(up_block, experts),
            pl.BlockSpec(down_block, experts),
        ],
        out_shape=[tok_bf16, tok_bf16, tok_bf16,
                   jax.ShapeDtypeStruct((n, D_MODEL), F32),
                   tok_bf16, tok_bf16,
                   jax.ShapeDtypeStruct((n // MOBA_BLOCK, 1, D_MODEL), F32),
                   jax.ShapeDtypeStruct(w_exp_gate.shape, BF16),
                   jax.ShapeDtypeStruct(w_exp_up.shape, BF16),
                   jax.ShapeDtypeStruct(w_exp_down.shape, BF16)],
        compiler_params=_params("arbitrary"),
        name="in_projection",
    )(x2, w_in_b, b_gate, cos, sa, sb, w_exp_gate, w_exp_up, w_exp_down)


def _conv_kernel(u_ref, wdw_ref, bdw_ref, lng_ref, lnb_ref, wo_ref, gc_ref, o_ref, buf_ref, sh_ref, y_ref):
    ts = u_ref.shape[0]
    s = pl.program_id(1)

    @pl.when(s == 0)
    def _():
        buf_ref[0:CONV_HALO, :] = jnp.zeros((CONV_HALO, D_MODEL), F32)

    @pl.when(s > 0)
    def _():
        buf_ref[0:CONV_HALO, :] = buf_ref[ts:ts + CONV_HALO, :]

    buf_ref[CONV_HALO:CONV_HALO + ts, :] = u_ref[...]

    span = ts + CONV_HALO - SUBLANES
    for b in range(1, SUBLANES):
        sh_ref[b - 1, 0:span, :] = buf_ref[b:b + span, :]

    base = CONV_HALO - (CONV_WIDTH - 1)
    for c in range(ts // CONV_ROWS):
        r0 = c * CONV_ROWS
        acc = jnp.zeros((CONV_ROWS // SUBLANES, SUBLANES, D_MODEL), F32)
        for j in range(CONV_WIDTH):
            shift = (base + j) % SUBLANES
            row = r0 + base + j - shift
            src = buf_ref if shift == 0 else sh_ref.at[shift - 1]
            tap = src[row:row + CONV_ROWS, :].reshape(CONV_ROWS // SUBLANES, SUBLANES, D_MODEL)
            acc = acc + tap * wdw_ref[j]
        acc = acc.reshape(CONV_ROWS, D_MODEL)
        y = _layer_norm(acc + bdw_ref[...], lng_ref[...], lnb_ref[...])
        y_ref[r0:r0 + CONV_ROWS, :] = (y * _sigmoid(y)).astype(BF16)
    z = jnp.dot(y_ref[...], wo_ref[...], preferred_element_type=F32)
    o_ref[...] = (z * gc_ref[...].astype(F32)).astype(BF16)


def _conv_branch(u, w_dw, b_dw, ln_g, ln_b, w_o_b, gate_c, batch, seq):
    n = u.shape[0]
    ts = TOKEN_TILE
    tiles_per_seq = seq // ts
    row = lambda b, s: (b * tiles_per_seq + s, 0)
    const = lambda b, s: (0, 0)
    return pl.pallas_call(
        _conv_kernel,
        grid=(batch, tiles_per_seq),
        in_specs=[
            pl.BlockSpec((ts, D_MODEL), row),
            pl.BlockSpec((CONV_WIDTH, SUBLANES, D_MODEL), lambda b, s: (0, 0, 0)),
            pl.BlockSpec((1, D_MODEL), const),
            pl.BlockSpec((1, D_MODEL), const),
            pl.BlockSpec((1, D_MODEL), const),
            pl.BlockSpec((D_MODEL, D_MODEL), const),
            pl.BlockSpec((ts, D_MODEL), row),
        ],
        out_specs=pl.BlockSpec((ts, D_MODEL), row),
        out_shape=jax.ShapeDtypeStruct((n, D_MODEL), BF16),
        scratch_shapes=[pltpu.VMEM((ts + CONV_HALO, D_MODEL), F32),
                        pltpu.VMEM((SUBLANES - 1, ts + CONV_HALO - SUBLANES, D_MODEL), F32),
                        pltpu.VMEM((ts, D_MODEL), BF16)],
        compiler_params=_params("parallel", "arbitrary"),
        name="conv_branch",
    )(u, w_dw, b_dw, ln_g, ln_b, w_o_b, gate_c)


def _attn_kernel(q_ref, k_ref, v_ref, km_ref, o_ref, *, n_blk):
    blk = MOBA_BLOCK
    seq = n_blk * blk
    k_sel = min(MOBA_TOPK, n_blk)
    exp2_scale = HEAD_DIM ** -0.5 * LOG2_E
    nt_dims = (((1,), (1,)), ((), ()))
    n_lane = HEAD_DIM

    assert blk & (blk - 1) == 0
    blk_shift = blk.bit_length() - 1
    n_sub = -(-n_blk // SUBLANES) * SUBLANES

    blk_t = lax.broadcasted_iota(jnp.int32, (n_sub, seq), 0)
    own_t = lax.broadcasted_iota(jnp.int32, (n_sub, seq), 1) >> blk_shift
    past = blk_t < own_t
    blk_f = blk_t.astype(F32)
    blk_id = lax.broadcasted_iota(jnp.int32, (seq, n_lane), 1)
    own_id = lax.broadcasted_iota(jnp.int32, (seq, n_lane), 0) >> blk_shift
    k_blk = jnp.where(blk_id == own_id, 1.0, 0.0).astype(BF16)
    ones = jnp.ones((seq, n_lane), BF16)
    causal = (lax.broadcasted_iota(jnp.int32, (blk, blk), 1)
              <= lax.broadcasted_iota(jnp.int32, (blk, blk), 0))

    def prepare(lanes):
        q_all = q_ref[:, lanes]
        km = km_ref[:, 0, lanes]
        if n_sub > n_blk:
            km = jnp.concatenate([km, jnp.zeros((n_sub - n_blk, HEAD_DIM), F32)], axis=0)
        km_hi = km.astype(BF16)
        km_lo = (km - km_hi.astype(F32)).astype(BF16)
        gate = (lax.dot_general(km_hi, q_all, nt_dims, preferred_element_type=F32)
                + lax.dot_general(km_lo, q_all, nt_dims, preferred_element_type=F32))
        g = jnp.where(past, gate, NEG_INF)
        sel = jnp.zeros((n_sub, seq), F32)
        for _ in range(k_sel):
            mx = jnp.max(g, axis=0, keepdims=True)
            first = jnp.min(jnp.where(g == mx, blk_f, float(n_sub)), axis=0, keepdims=True)
            pick = blk_f == first
            sel = jnp.where(pick, 1.0, sel)
            g = jnp.where(pick, NEG_INF, g)
        visible = ((sel > 0.0) & past) | (blk_t == own_t)
        bias_t = jnp.concatenate([jnp.where(visible, 0.0, MASKED), jnp.zeros((n_lane - n_sub, seq), F32)],
                                 axis=0)
        q_bias = bias_t.T.astype(BF16)
        k_aug = jnp.concatenate([k_ref[:, lanes], k_blk], axis=1)
        v_aug = jnp.concatenate([v_ref[:, lanes], ones], axis=1)
        return q_all, q_bias, k_aug, v_aug, lanes

    def scores(head, i):
        q_all, q_bias, k_aug, _, _ = head
        rows = slice(i * blk, (i + 1) * blk)
        q_aug = jnp.concatenate([q_all[rows], q_bias[rows]], axis=1)
        return lax.dot_general(q_aug, k_aug[:(i + 1) * blk], nt_dims, preferred_element_type=F32)

    def finish(head, i, p):
        pv = jnp.dot(p, head[3][:(i + 1) * blk], preferred_element_type=F32)
        o_ref[i * blk:(i + 1) * blk, head[4]] = (pv[:, :HEAD_DIM] / pv[:, HEAD_DIM:HEAD_DIM + 1]).astype(BF16)

    def softmax_numerator(raw, i):
        own = jnp.where(causal, raw[:, i * blk:], MASKED)
        parts = [raw[:, :i * blk], own] if i else [own]
        m = jnp.max(own, axis=1, keepdims=True)
        if i:
            m = jnp.maximum(m, jnp.max(parts[0], axis=1, keepdims=True))
        return jnp.concatenate([jnp.exp2((t - m) * exp2_scale) for t in parts], axis=1).astype(BF16)

    heads = [prepare(slice(hh * HEAD_DIM, (hh + 1) * HEAD_DIM)) for hh in range(HEADS_PER_STEP)]
    raw_next = [scores(head, 0) for head in heads]
    p_prev = None
    for i in range(n_blk):
        raws = raw_next
        if i + 1 < n_blk:
            raw_next = [scores(head, i + 1) for head in heads]
        if p_prev is not None:
            for head, p in zip(heads, p_prev):
                finish(head, i - 1, p)
        p_prev = [softmax_numerator(raw, i) for raw in raws]
    for head, p in zip(heads, p_prev):
        finish(head, n_blk - 1, p)


def _moba_attention(q, k, v, k_mean, batch, seq):
    n = q.shape[0]
    n_blk = seq // MOBA_BLOCK
    width = HEADS_PER_STEP * HEAD_DIM
    seq_head = lambda b, h: (b, h)
    return pl.pallas_call(
        functools.partial(_attn_kernel, n_blk=n_blk),
        grid=(batch, N_HEADS // HEADS_PER_STEP),
        in_specs=[
            pl.BlockSpec((seq, width), seq_head),
            pl.BlockSpec((seq, width), seq_head),
            pl.BlockSpec((seq, width), seq_head),
            pl.BlockSpec((n_blk, 1, width), lambda b, h: (b, 0, h)),
        ],
        out_specs=pl.BlockSpec((seq, width), seq_head),
        out_shape=jax.ShapeDtypeStruct((n, D_MODEL), BF16),
        compiler_params=_params("parallel", "parallel"),
        name="moba_attention",
    )(q, k, v, k_mean)


def _merge_kernel(attn_ref, ga_ref, cg_ref, x_ref, wao_ref, wout_ref, g1_ref, b1_ref,
                  wr_ref, rb_ref, h_ref, hp_ref, idx_ref, wgt_ref, rank_ref, total_ref, count_ref, hprev_ref):
    tm = x_ref.shape[0]
    step = pl.program_id(0)

    @pl.when(step == 0)
    def _():
        count_ref[...] = jnp.zeros((N_EXPERTS, 1), F32)
        hprev_ref[...] = jnp.zeros(hprev_ref.shape, F32)

    h = hprev_ref[...]

    a = jnp.dot(attn_ref[...], wao_ref[...], preferred_element_type=F32)
    merged = ga_ref[...].astype(F32) * a + cg_ref[...].astype(F32)
    y = jnp.dot(merged.astype(BF16), wout_ref[...], preferred_element_type=F32)
    h_new = _layer_norm(DEEPNORM_ALPHA * x_ref[...] + y, g1_ref[...], b1_ref[...])
    h_ref[...] = h_new
    hp_ref[...] = _pack_bf16_halves(h_new)
    hprev_ref[...] = h_new

    nt_dims = (((1,), (1,)), ((), ()))
    h_hi = h.astype(BF16)
    h_lo = (h - h_hi.astype(F32)).astype(BF16)
    w = wr_ref[...]
    w_hi = w.astype(BF16)
    w_lo = (w - w_hi.astype(F32)).astype(BF16)
    logits = (lax.dot_general(w_hi, h_hi, nt_dims, preferred_element_type=F32)
              + lax.dot_general(w_hi, h_lo, nt_dims, preferred_element_type=F32)
              + lax.dot_general(w_lo, h_hi, nt_dims, preferred_element_type=F32))
    scores = _sigmoid(logits)
    biased = scores + rb_ref[...]

    g3 = biased.reshape(N_GROUPS, GROUP_SIZE, tm)
    m1 = jnp.max(g3, axis=1, keepdims=True)
    is_max = g3 == m1
    n_max = jnp.sum(jnp.where(is_max, 1.0, 0.0), axis=1, keepdims=True)
    m2 = jnp.max(jnp.where(is_max, NEG_INF, g3), axis=1, keepdims=True)
    grp = (m1 + jnp.where(n_max >= 2.0, m1, m2)).reshape(N_GROUPS, tm)

    gid = lax.broadcasted_iota(jnp.int32, (N_GROUPS, tm), 0)
    rank = jnp.zeros((N_GROUPS, tm), F32)
    for o in range(N_GROUPS):
        other = grp[o:o + 1, :]
        ahead = (other > grp) | ((other == grp) & (o < gid))
        rank = rank + jnp.where(ahead, 1.0, 0.0)
    grp_keep = jnp.where(rank < float(TOPK_GROUPS), 1.0, 0.0)
    keep = jnp.broadcast_to(grp_keep.reshape(N_GROUPS, 1, tm),
                            (N_GROUPS, GROUP_SIZE, tm)).reshape(N_EXPERTS, tm)
    cand = jnp.where(keep > 0.0, biased, NEG_INF)

    eid = lax.broadcasted_iota(jnp.int32, (N_EXPERTS, tm), 0).astype(F32)
    chosen = jnp.zeros((N_EXPERTS, tm), F32)
    firsts = []
    for r in range(TOP_K):
        mx = jnp.max(cand, axis=0, keepdims=True)
        first = jnp.min(jnp.where(cand == mx, eid, float(N_EXPERTS)), axis=0, keepdims=True)
        pick = eid == first
        firsts.append(first)
        idx_ref[r:r + 1, :] = first.astype(jnp.int32)
        wgt_ref[r:r + 1, :] = jnp.sum(jnp.where(pick, scores, 0.0), axis=0, keepdims=True)
        cand = jnp.where(pick, NEG_INF, cand)
        chosen = jnp.where(pick, 1.0, chosen)
    top_s = wgt_ref[...]
    wgt_ref[...] = top_s / (jnp.sum(top_s, axis=0, keepdims=True) + 1e-20) * ROUTED_SCALE

    earlier = (lax.broadcasted_iota(jnp.int32, (tm, tm), 0)
               < lax.broadcasted_iota(jnp.int32, (tm, tm), 1))
    before = jnp.dot(chosen.astype(BF16), jnp.where(earlier, 1.0, 0.0).astype(BF16),
                     preferred_element_type=F32) + count_ref[...]
    for r in range(TOP_K):
        rank_ref[r:r + 1, :] = jnp.sum(jnp.where(eid == firsts[r], before, 0.0),
                                       axis=0, keepdims=True).astype(jnp.int32)
    counted = jnp.where(step > 0, 1.0, 0.0)
    total = count_ref[...] + counted * jnp.sum(chosen, axis=1, keepdims=True)
    count_ref[...] = total
    total_ref[...] = total


def _merge_ln_router(attn, gate_a, conv_gated, x2, w_attn_o_b, w_out_b, ln_g, ln_b, w_router_t, router_bias,
                     first_token, n):
    tm = MERGE_TILE
    first_tile = first_token // tm
    n_tiles = n // tm
    row_in = lambda i: (jnp.minimum(i, n_tiles - 1) + first_tile, 0)
    row = lambda i: (jnp.minimum(i, n_tiles - 1), 0)
    const = lambda i: (0, 0)
    col = lambda i: (0, jnp.maximum(i - 1, 0))
    return pl.pallas_call(
        _merge_kernel,
        grid=(n_tiles + 1,),
        in_specs=[
            pl.BlockSpec((tm, D_MODEL), row_in),
            pl.BlockSpec((tm, D_MODEL), row_in),
            pl.BlockSpec((tm, D_MODEL), row_in),
            pl.BlockSpec((tm, D_MODEL), row_in),
            pl.BlockSpec((D_MODEL, D_MODEL), const),
            pl.BlockSpec((D_MODEL, D_MODEL), const),
            pl.BlockSpec((1, D_MODEL), const),
            pl.BlockSpec((1, D_MODEL), const),
            pl.BlockSpec((N_EXPERTS, D_MODEL), const),
            pl.BlockSpec((N_EXPERTS, 1), const),
        ],
        out_specs=[
            pl.BlockSpec((tm, D_MODEL), row),
            pl.BlockSpec((tm, HALF), row),
            pl.BlockSpec((TOP_K, tm), col),
            pl.BlockSpec((TOP_K, tm), col),
            pl.BlockSpec((TOP_K, tm), col),
            pl.BlockSpec((N_EXPERTS, 1), const),
        ],
        out_shape=[
            jax.ShapeDtypeStruct((n, D_MODEL), F32),
            jax.ShapeDtypeStruct((n, HALF), jnp.uint32),
            jax.ShapeDtypeStruct((TOP_K, n), jnp.int32),
            jax.ShapeDtypeStruct((TOP_K, n), F32),
            jax.ShapeDtypeStruct((TOP_K, n), jnp.int32),
            jax.ShapeDtypeStruct((N_EXPERTS, 1), F32),
        ],
        scratch_shapes=[pltpu.VMEM((N_EXPERTS, 1), F32), pltpu.VMEM((tm, D_MODEL), F32)],
        compiler_params=_params("arbitrary"),
        name="merge_ln_router",
    )(attn, gate_a, conv_gated, x2, w_attn_o_b, w_out_b, ln_g, ln_b, w_router_t, router_bias)


def _sc_mesh():
    return plsc.VectorSubcoreMesh(core_axis_name="c", subcore_axis_name="s",
                                  num_cores=SC_CORES, num_subcores=SC_SUBCORES)


def _sc_worker_base(rows_per_worker):
    return (lax.axis_index("s") * SC_CORES + lax.axis_index("c")) * rows_per_worker


def _sc_gather_rows(table, idx):
    m = idx.shape[0]
    width = table.shape[1]
    per_worker = m // (SC_CORES * SC_SUBCORES)
    assert per_worker * SC_CORES * SC_SUBCORES == m and per_worker % SC_CHUNK == 0

    @functools.partial(
        pl.kernel, mesh=_sc_mesh(),
        out_type=jax.ShapeDtypeStruct((m, width), table.dtype),
        scratch_types=[pltpu.VMEM((SC_CHUNK,), jnp.int32),
                       pltpu.VMEM((SC_CHUNK, width), table.dtype),
                       pltpu.SemaphoreType.DMA],
        name="sc_gather_rows")
    def gather(table_hbm, idx_hbm, out_hbm, idx_v, rows_v, sem):
        base = _sc_worker_base(per_worker)

        @pl.loop(0, per_worker // SC_CHUNK)
        def _(c):
            off = pl.multiple_of(base + c * SC_CHUNK, SC_CHUNK)
            pltpu.sync_copy(idx_hbm.at[pl.ds(off, SC_CHUNK)], idx_v)
            pltpu.async_copy(table_hbm.at[idx_v], rows_v, sem).wait()
            pltpu.sync_copy(rows_v, out_hbm.at[pl.ds(off, SC_CHUNK)])

    return gather(table, idx)


def _sc_scatter_rows(rows, pos, n_out):
    n, width = rows.shape
    per_worker = n // (SC_CORES * SC_SUBCORES)
    assert per_worker * SC_CORES * SC_SUBCORES == n and per_worker % SC_CHUNK == 0

    @functools.partial(
        pl.kernel, mesh=_sc_mesh(),
        out_type=jax.ShapeDtypeStruct((n_out, width), rows.dtype),
        scratch_types=[pltpu.VMEM((SC_CHUNK,), jnp.int32),
                       pltpu.VMEM((SC_CHUNK, width), rows.dtype),
                       pltpu.SemaphoreType.DMA],
        name="sc_scatter_rows")
    def scatter(rows_hbm, pos_hbm, out_hbm, idx_v, rows_v, sem):
        base = _sc_worker_base(per_worker)

        @pl.loop(0, per_worker // SC_CHUNK)
        def _(c):
            off = pl.multiple_of(base + c * SC_CHUNK, SC_CHUNK)
            pltpu.sync_copy(rows_hbm.at[pl.ds(off, SC_CHUNK)], rows_v)
            for r in range(TOP_K):
                pltpu.sync_copy(pos_hbm.at[pl.ds(r * n + off, SC_CHUNK)], idx_v)
                pltpu.async_copy(rows_v, out_hbm.at[idx_v], sem).wait()

    return scatter(rows, pos)


def _position_kernel(idx_ref, rank_ref, start_ref, pos_ref):
    tl = idx_ref.shape[1]
    eid = lax.broadcasted_iota(jnp.int32, (N_EXPERTS, tl), 0)
    start = start_ref[...]
    for r in range(TOP_K):
        here = jnp.sum(jnp.where(eid == idx_ref[r:r + 1, :], start, 0.0), axis=0, keepdims=True)
        pos_ref[r:r + 1, :] = here.astype(jnp.int32) + rank_ref[r:r + 1, :]


def _positions(top_idx, rank, group_start):
    n = top_idx.shape[1]
    tl = POSITION_TILE
    col = lambda i: (0, i)
    return pl.pallas_call(
        _position_kernel,
        grid=(n // tl,),
        in_specs=[pl.BlockSpec((TOP_K, tl), col), pl.BlockSpec((TOP_K, tl), col),
                  pl.BlockSpec((N_EXPERTS, 1), lambda i: (0, 0))],
        out_specs=pl.BlockSpec((TOP_K, tl), col),
        out_shape=jax.ShapeDtypeStruct((TOP_K, n), jnp.int32),
        compiler_params=_params("parallel"),
        name="positions",
    )(top_idx, rank, group_start)


def _expert_kernel(te_ref, tv_ref, xs_ref, wg_ref, wu_ref, wd_ref, y_ref, act_ref):
    i = pl.program_id(0)
    valid = tv_ref[i]
    valid_prev = tv_ref[jnp.maximum(i - 1, 0)]

    @pl.when(i == 0)
    def _():
        act_ref[...] = jnp.zeros(act_ref.shape, BF16)

    @pl.when((valid > 0) | (valid_prev > 0))
    def _():
        act_prev = act_ref[...]
        wd = wd_ref[0]
        for c in range(EXPERT_TILE // EXPERT_CHAIN):
            rows = slice(c * EXPERT_CHAIN, (c + 1) * EXPERT_CHAIN)
            y_ref[rows, :] = _pack_bf16_halves(jnp.dot(act_prev[rows], wd, preferred_element_type=F32))
        wg = wg_ref[0]
        wu = wu_ref[0]
        for c in range(EXPERT_TILE // EXPERT_CHAIN):
            rows = slice(c * EXPERT_CHAIN, (c + 1) * EXPERT_CHAIN)
            live = lax.broadcasted_iota(jnp.int32, (EXPERT_CHAIN, HALF), 0) < valid - c * EXPERT_CHAIN
            lo, hi = _unpack_bf16_halves(jnp.where(live, xs_ref[rows, :], jnp.uint32(0)))
            x = jnp.concatenate([lo, hi], axis=1).astype(BF16)
            g = jnp.dot(x, wg, preferred_element_type=F32)
            u = jnp.dot(x, wu, preferred_element_type=F32)
            act_ref[rows, :] = (g * _sigmoid(g) * u).astype(BF16)

    @pl.when((valid == 0) & (valid_prev == 0))
    def _():
        y_ref[...] = jnp.zeros(y_ref.shape, jnp.uint32)


def _grouped_experts(tile_expert, tile_valid, xs, wg, wu, wd):
    p = xs.shape[0]
    t = EXPERT_TILE
    n_tiles = p // t
    tile_expert = jnp.concatenate([tile_expert, tile_expert[-1:]])
    tile_valid = jnp.concatenate([tile_valid, jnp.zeros((1,), jnp.int32)])
    row_in = lambda i, te, tv: (jnp.minimum(i, n_tiles - 1), 0)
    row_out = lambda i, te, tv: (jnp.maximum(i - 1, 0), 0)
    expert = lambda i, te, tv: (te[i], 0, 0)
    expert_prev = lambda i, te, tv: (te[jnp.maximum(i - 1, 0)], 0, 0)
    return pl.pallas_call(
        _expert_kernel,
        grid_spec=pltpu.PrefetchScalarGridSpec(
            num_scalar_prefetch=2,
            grid=(n_tiles + 1,),
            in_specs=[
                pl.BlockSpec((t, HALF), row_in),
                pl.BlockSpec((1, D_MODEL, EXPERT_HIDDEN), expert),
                pl.BlockSpec((1, D_MODEL, EXPERT_HIDDEN), expert),
                pl.BlockSpec((1, EXPERT_HIDDEN, D_MODEL), expert_prev),
            ],
            out_specs=pl.BlockSpec((t, HALF), row_out),
            scratch_shapes=[pltpu.VMEM((t, EXPERT_HIDDEN), BF16)],
        ),
        out_shape=jax.ShapeDtypeStruct((p, HALF), jnp.uint32),
        compiler_params=_params("arbitrary"),
        name="grouped_experts",
    )(tile_expert, tile_valid, xs, wg, wu, wd)


def _combine_kernel(h_ref, yg_ref, wt_ref, wsg_ref, wsu_ref, wsd_ref, g2_ref, b2_ref, *out_refs):
    o_ref = out_refs[-1]
    h = h_ref[...]
    hb = h.astype(BF16)
    g = jnp.dot(hb, wsg_ref[...], preferred_element_type=F32)
    u = jnp.dot(hb, wsu_ref[...], preferred_element_type=F32)
    shared = jnp.dot((g * _sigmoid(g) * u).astype(BF16), wsd_ref[...], preferred_element_type=F32)
    wt = wt_ref[...]
    r_lo = jnp.zeros((h.shape[0], HALF), F32)
    r_hi = jnp.zeros((h.shape[0], HALF), F32)
    for r in range(TOP_K):
        lo, hi = _unpack_bf16_halves(yg_ref[r])
        w = wt[:, r:r + 1]
        r_lo = r_lo + lo * w
        r_hi = r_hi + hi * w
    routed = jnp.concatenate([r_lo, r_hi], axis=1)
    o_ref[...] = _layer_norm(DEEPNORM_ALPHA * h + (shared + routed), g2_ref[...], b2_ref[...])


def _combine_ln(h, yg, w_tok, wsg_b, wsu_b, wsd_b, ln_g, ln_b, out_so_far, first_token, n_total):
    n = h.shape[0]
    tm = TOKEN_TILE
    first_tile = first_token // tm
    row = lambda i: (i, 0)
    const = lambda i: (0, 0)
    hidden = wsg_b.shape[1]
    in_specs = [
        pl.BlockSpec((tm, D_MODEL), row),
        pl.BlockSpec((TOP_K, tm, HALF), lambda i: (0, i, 0)),
        pl.BlockSpec((tm, TOP_K), row),
        pl.BlockSpec((D_MODEL, hidden), const),
        pl.BlockSpec((D_MODEL, hidden), const),
        pl.BlockSpec((hidden, D_MODEL), const),
        pl.BlockSpec((1, D_MODEL), const),
        pl.BlockSpec((1, D_MODEL), const),
    ]
    args = [h, yg, w_tok, wsg_b, wsu_b, wsd_b, ln_g, ln_b]
    aliases = {}
    if out_so_far is not None:
        in_specs.append(pl.BlockSpec(memory_space=pl.ANY))
        args.append(out_so_far)
        aliases = {len(args) - 1: 0}
    return pl.pallas_call(
        _combine_kernel,
        grid=(n // tm,),
        in_specs=in_specs,
        out_specs=pl.BlockSpec((tm, D_MODEL), lambda i: (i + first_tile, 0)),
        out_shape=jax.ShapeDtypeStruct((n_total, D_MODEL), F32),
        input_output_aliases=aliases,
        compiler_params=_params("parallel"),
        name="combine_ln",
    )(*args)


def _group_layout(totals, n_tokens):
    t = EXPERT_TILE
    n_tiles = (TOP_K * n_tokens + N_EXPERTS * (t - 1)) // t
    counts = totals[:, 0].astype(jnp.int32)
    padded = ((counts + t - 1) // t) * t
    group_end = jnp.cumsum(padded)
    group_start = group_end - padded
    tile_start = jnp.arange(n_tiles, dtype=jnp.int32) * t
    tile_expert = jnp.minimum(jnp.sum((group_end[None, :] <= tile_start[:, None]).astype(jnp.int32), axis=1),
                              N_EXPERTS - 1)
    of_tile = tile_expert[:, None] == jnp.arange(N_EXPERTS, dtype=jnp.int32)[None, :]
    real_end = jnp.sum(jnp.where(of_tile, (group_start + counts)[None, :], 0), axis=1)
    tile_valid = jnp.clip(real_end - tile_start, 0, t).astype(jnp.int32)
    return group_start.astype(F32).reshape(N_EXPERTS, 1), tile_expert, tile_valid, n_tiles * t


def _rope_tables(seq):
    inv_freq = ROPE_THETA ** (-jnp.arange(0, ROPE_DIM, 2, dtype=F32) / ROPE_DIM)
    ang = jnp.arange(seq).astype(F32)[:, None] * inv_freq[None, :]
    cos, sin = jnp.cos(ang), jnp.sin(ang)
    rest = HEAD_DIM - ROPE_DIM
    zeros = jnp.zeros((seq, ROPE_HALF), F32)
    cos_t = jnp.concatenate([cos, cos, jnp.ones((seq, rest), F32)], axis=1)
    sa_t = jnp.concatenate([-sin, zeros, jnp.zeros((seq, rest), F32)], axis=1)
    sb_t = jnp.concatenate([zeros, sin, jnp.zeros((seq, rest), F32)], axis=1)
    return cos_t, sa_t, sb_t


def _layer(x2, batch, seq, w_in, b_gate, w_attn_o, w_dw, b_dw, conv_ln_g, conv_ln_b, w_conv_o, w_out,
           ln1_g, ln1_b, w_router, router_bias, w_exp_gate, w_exp_up, w_exp_down,
           w_sh_gate, w_sh_up, w_sh_down, ln2_g, ln2_b):
    n = x2.shape[0]
    row = lambda v: v.reshape(1, -1)
    cos, sa, sb = _rope_tables(seq)
    q, k, v, u, gate_a, gate_c, k_mean, w_gate_b, w_up_b, w_down_b = _in_projection(
        x2, w_in.astype(BF16), row(b_gate), cos, sa, sb, w_exp_gate, w_exp_up, w_exp_down, seq)
    w_taps = jnp.broadcast_to(w_dw.reshape(CONV_WIDTH, 1, D_MODEL), (CONV_WIDTH, SUBLANES, D_MODEL))
    conv_gated = _conv_branch(u, w_taps, row(b_dw), row(conv_ln_g),
                              row(conv_ln_b), w_conv_o.astype(BF16), gate_c, batch, seq)
    attn = _moba_attention(q, k, v, k_mean, batch, seq)
    w_attn_o_b, w_out_b, w_router_t = w_attn_o.astype(BF16), w_out.astype(BF16), w_router.T
    shared_w = (w_sh_gate.astype(BF16), w_sh_up.astype(BF16), w_sh_down.astype(BF16))
    n_chunk = n // MOE_CHUNKS
    assert n_chunk * MOE_CHUNKS == n and n_chunk % TOKEN_TILE == 0 and n_chunk % POSITION_TILE == 0
    out = None
    for c in range(MOE_CHUNKS):
        first = c * n_chunk
        h, h_packed, top_idx, top_w, rank, totals = _merge_ln_router(
            attn, gate_a, conv_gated, x2, w_attn_o_b, w_out_b, row(ln1_g), row(ln1_b),
            w_router_t, router_bias.reshape(N_EXPERTS, 1), first, n_chunk)
        group_start, tile_expert, tile_valid, n_rows = _group_layout(totals, n_chunk)
        pos = _positions(top_idx, rank, group_start).reshape(TOP_K * n_chunk)
        xs = _sc_scatter_rows(h_packed, pos, n_rows)
        ys = _grouped_experts(tile_expert, tile_valid, xs, w_gate_b, w_up_b, w_down_b)
        yg = _sc_gather_rows(ys, pos).reshape(TOP_K, n_chunk, HALF)
        out = _combine_ln(h, yg, top_w.T, *shared_w, row(ln2_g), row(ln2_b), out, first, n)
    return out


def kernel(x, w_in, b_gate, w_attn_o, w_dw, b_dw, conv_ln_g, conv_ln_b, w_conv_o, w_out, ln1_g, ln1_b,
           w_router, router_bias, w_exp_gate, w_exp_up, w_exp_down, w_sh_gate, w_sh_up, w_sh_down,
           ln2_g, ln2_b):
    batch, seq, d = x.shape
    assert d == D_MODEL and seq % MOBA_BLOCK == 0 and seq % TOKEN_TILE == 0
    assert w_in.shape[0] == DEPTH
    x2 = x.reshape(batch * seq, d)
    for l in range(DEPTH):
        x2 = _layer(x2, batch, seq, w_in[l], b_gate[l], w_attn_o[l], w_dw[l], b_dw[l], conv_ln_g[l],
                    conv_ln_b[l], w_conv_o[l], w_out[l], ln1_g[l], ln1_b[l], w_router[l], router_bias[l],
                    w_exp_gate[l], w_exp_up[l], w_exp_down[l], w_sh_gate[l], w_sh_up[l], w_sh_down[l],
                    ln2_g[l], ln2_b[l])
    return x2.reshape(batch, seq, d)
```

```python
import functools

import jax
import jax.numpy as jnp
from jax import lax
from jax.experimental import pallas as pl
from jax.experimental.pallas import tpu as pltpu
from jax.experimental.pallas import tpu_sc as plsc

F32 = jnp.float32
BF16 = jnp.bfloat16
NEG_INF = float("-inf")
MASKED = -1e30
LOG2_E = 1.4426950408889634

D_MODEL = 1024
N_HEADS = 8
HEAD_DIM = 128
ROPE_THETA = 500000.0
ROPE_DIM = HEAD_DIM // 4
ROPE_HALF = ROPE_DIM // 2
MOBA_BLOCK = 256
MOBA_TOPK = 3
CONV_WIDTH = 31
SUBLANES = 8
CONV_HALO = 32
CONV_ROWS = 32
N_EXPERTS = 256
TOP_K = 8
N_GROUPS = 8
GROUP_SIZE = N_EXPERTS // N_GROUPS
TOPK_GROUPS = 4
EXPERT_HIDDEN = 256
ROUTED_SCALE = 2.5
LN_EPS = 1e-5
DEPTH = 1
DEEPNORM_ALPHA = (2 * DEPTH) ** 0.25
HALF = D_MODEL // 2

TOKEN_TILE = 256
MERGE_TILE = 512
EXPERT_TILE = 512
EXPERT_CHAIN = 256
POSITION_TILE = 1024
MOE_CHUNKS = 2
HEADS_PER_STEP = 2
SC_CORES = 2
SC_SUBCORES = 16
SC_CHUNK = 64
VMEM_LIMIT = 56 * 1024 * 1024


def _params(*semantics):
    return pltpu.CompilerParams(dimension_semantics=semantics, vmem_limit_bytes=VMEM_LIMIT)


def _sigmoid(x):
    return 1.0 / (1.0 + jnp.exp(-x))


def _layer_norm(x, g, b):
    mu = jnp.mean(x, axis=-1, keepdims=True)
    xc = x - mu
    var = jnp.mean(xc * xc, axis=-1, keepdims=True)
    return xc * lax.rsqrt(var + LN_EPS) * g + b


def _pack_bf16_halves(y):
    lo = lax.bitcast_convert_type(y[:, :HALF].astype(BF16).astype(F32), jnp.uint32)
    hi = lax.bitcast_convert_type(y[:, HALF:].astype(BF16).astype(F32), jnp.uint32)
    return (hi & jnp.uint32(0xFFFF0000)) | (lo >> 16)


def _unpack_bf16_halves(p):
    lo = lax.bitcast_convert_type(p << 16, F32)
    hi = lax.bitcast_convert_type(p & jnp.uint32(0xFFFF0000), F32)
    return lo, hi


def _inproj_kernel(x_ref, w_ref, bg_ref, cos_ref, sa_ref, sb_ref, wg_ref, wu_ref, wd_ref,
                   q_ref, k_ref, v_ref, u_ref, ga_ref, gc_ref, km_ref, wgb_ref, wub_ref, wdb_ref):
    wgb_ref[...] = wg_ref[...].astype(BF16)
    wub_ref[...] = wu_ref[...].astype(BF16)
    wdb_ref[...] = wd_ref[...].astype(BF16)

    tm = x_ref.shape[0]
    xb = x_ref[...].astype(BF16)

    def proj(c):
        return jnp.dot(xb, w_ref[:, c * D_MODEL:(c + 1) * D_MODEL], preferred_element_type=F32)

    cos = cos_ref[...]
    sa = sa_ref[...]
    sb = sb_ref[...]

    def rope_head(t):
        return (t * cos + pltpu.roll(t, HEAD_DIM - ROPE_HALF, 1) * sa
                + pltpu.roll(t, ROPE_HALF, 1) * sb)

    q = proj(0)
    for h in range(N_HEADS):
        sl = slice(h * HEAD_DIM, (h + 1) * HEAD_DIM)
        q_ref[:, sl] = rope_head(q[:, sl]).astype(BF16)
    k = proj(1)
    for h in range(N_HEADS):
        sl = slice(h * HEAD_DIM, (h + 1) * HEAD_DIM)
        kr = rope_head(k[:, sl])
        k_ref[:, sl] = kr.astype(BF16)
        for g in range(tm // MOBA_BLOCK):
            km_ref[g, :, sl] = jnp.mean(kr[g * MOBA_BLOCK:(g + 1) * MOBA_BLOCK], axis=0, keepdims=True)
    v_ref[...] = proj(2).astype(BF16)
    u_ref[...] = proj(3) * _sigmoid(proj(4))
    ga_ref[...] = _sigmoid(proj(5) + bg_ref[:, :D_MODEL]).astype(BF16)
    gc_ref[...] = _sigmoid(proj(6) + bg_ref[:, D_MODEL:]).astype(BF16)


def _in_projection(x2, w_in_b, b_gate, cos, sa, sb, w_exp_gate, w_exp_up, w_exp_down, seq):
    n = x2.shape[0]
    tm = TOKEN_TILE
    n_cols = w_in_b.shape[1]
    tiles_per_seq = seq // tm
    steps = n // tm
    per_step = -(-N_EXPERTS // steps)
    assert N_EXPERTS % per_step == 0
    row = lambda i: (i, 0)
    const = lambda i: (0, 0)
    pos = lambda i: (i % tiles_per_seq, 0)
    experts = lambda i: (jnp.minimum(i, N_EXPERTS // per_step - 1), 0, 0)
    tok_bf16 = jax.ShapeDtypeStruct((n, D_MODEL), BF16)
    up_block = (per_step, D_MODEL, EXPERT_HIDDEN)
    down_block = (per_step, EXPERT_HIDDEN, D_MODEL)
    return pl.pallas_call(
        _inproj_kernel,
        grid=(steps,),
        in_specs=[
            pl.BlockSpec((tm, D_MODEL), row),
            pl.BlockSpec((D_MODEL, n_cols), const),
            pl.BlockSpec((1, 2 * D_MODEL), const),
            pl.BlockSpec((tm, HEAD_DIM), pos),
            pl.BlockSpec((tm, HEAD_DIM), pos),
            pl.BlockSpec((tm, HEAD_DIM), pos),
            pl.BlockSpec(up_block, experts),
            pl.BlockSpec(up_block, experts),
            pl.BlockSpec(down_block, experts),
        ],
        out_specs=[
            pl.BlockSpec((tm, D_MODEL), row),
            pl.BlockSpec((tm, D_MODEL), row),
            pl.BlockSpec((tm, D_MODEL), row),
            pl.BlockSpec((tm, D_MODEL), row),
            pl.BlockSpec((tm, D_MODEL), row),
            pl.BlockSpec((tm, D_MODEL), row),
            pl.BlockSpec((tm // MOBA_BLOCK, 1, D_MODEL), lambda i: (i, 0, 0)),
            pl.BlockSpec(up_block, experts),
            pl.BlockSpec(up_block, experts),
            pl.BlockSpec(down_block, experts),
        ],
        out_shape=[tok_bf16, tok_bf16, tok_bf16,
                   jax.ShapeDtypeStruct((n, D_MODEL), F32),
                   tok_bf16, tok_bf16,
                   jax.ShapeDtypeStruct((n // MOBA_BLOCK, 1, D_MODEL), F32),
                   jax.ShapeDtypeStruct(w_exp_gate.shape, BF16),
                   jax.ShapeDtypeStruct(w_exp_up.shape, BF16),
                   jax.ShapeDtypeStruct(w_exp_down.shape, BF16)],
        compiler_params=_params("arbitrary"),
        name="in_projection",
    )(x2, w_in_b, b_gate, cos, sa, sb, w_exp_gate, w_exp_up, w_exp_down)


def _conv_kernel(u_ref, wdw_ref, bdw_ref, lng_ref, lnb_ref, wo_ref, gc_ref, o_ref, buf_ref, sh_ref, y_ref):
    ts = u_ref.shape[0]
    s = pl.program_id(1)

    @pl.when(s == 0)
    def _():
        buf_ref[0:CONV_HALO, :] = jnp.zeros((CONV_HALO, D_MODEL), F32)

    @pl.when(s > 0)
    def _():
        buf_ref[0:CONV_HALO, :] = buf_ref[ts:ts + CONV_HALO, :]

    buf_ref[CONV_HALO:CONV_HALO + ts, :] = u_ref[...]

    span = ts + CONV_HALO - SUBLANES
    for b in range(1, SUBLANES):
        sh_ref[b - 1, 0:span, :] = buf_ref[b:b + span, :]

    base = CONV_HALO - (CONV_WIDTH - 1)
    for c in range(ts // CONV_ROWS):
        r0 = c * CONV_ROWS
        acc = jnp.zeros((CONV_ROWS // SUBLANES, SUBLANES, D_MODEL), F32)
        for j in range(CONV_WIDTH):
            shift = (base + j) % SUBLANES
            row = r0 + base + j - shift
            src = buf_ref if shift == 0 else sh_ref.at[shift - 1]
            tap = src[row:row + CONV_ROWS, :].reshape(CONV_ROWS // SUBLANES, SUBLANES, D_MODEL)
            acc = acc + tap * wdw_ref[j]
        acc = acc.reshape(CONV_ROWS, D_MODEL)
        y = _layer_norm(acc + bdw_ref[...], lng_ref[...], lnb_ref[...])
        y_ref[r0:r0 + CONV_ROWS, :] = (y * _sigmoid(y)).astype(BF16)
    z = jnp.dot(y_ref[...], wo_ref[...], preferred_element_type=F32)
    o_ref[...] = (z * gc_ref[...].astype(F32)).astype(BF16)


def _conv_branch(u, w_dw, b_dw, ln_g, ln_b, w_o_b, gate_c, batch, seq):
    n = u.shape[0]
    ts = TOKEN_TILE
    tiles_per_seq = seq // ts
    row = lambda b, s: (b * tiles_per_seq + s, 0)
    const = lambda b, s: (0, 0)
    return pl.pallas_call(
        _conv_kernel,
        grid=(batch, tiles_per_seq),
        in_specs=[
            pl.BlockSpec((ts, D_MODEL), row),
            pl.BlockSpec((CONV_WIDTH, SUBLANES, D_MODEL), lambda b, s: (0, 0, 0)),
            pl.BlockSpec((1, D_MODEL), const),
            pl.BlockSpec((1, D_MODEL), const),
            pl.BlockSpec((1, D_MODEL), const),
            pl.BlockSpec((D_MODEL, D_MODEL), const),
            pl.BlockSpec((ts, D_MODEL), row),
        ],
        out_specs=pl.BlockSpec((ts, D_MODEL), row),
        out_shape=jax.ShapeDtypeStruct((n, D_MODEL), BF16),
        scratch_shapes=[pltpu.VMEM((ts + CONV_HALO, D_MODEL), F32),
                        pltpu.VMEM((SUBLANES - 1, ts + CONV_HALO - SUBLANES, D_MODEL), F32),
                        pltpu.VMEM((ts, D_MODEL), BF16)],
        compiler_params=_params("parallel", "arbitrary"),
        name="conv_branch",
    )(u, w_dw, b_dw, ln_g, ln_b, w_o_b, gate_c)


def _attn_kernel(q_ref, k_ref, v_ref, km_ref, o_ref, *, n_blk):
    blk = MOBA_BLOCK
    seq = n_blk * blk
    k_sel = min(MOBA_TOPK, n_blk)
    exp2_scale = HEAD_DIM ** -0.5 * LOG2_E
    nt_dims = (((1,), (1,)), ((), ()))
    n_lane = HEAD_DIM

    assert blk & (blk - 1) == 0
    blk_shift = blk.bit_length() - 1
    n_sub = -(-n_blk // SUBLANES) * SUBLANES

    blk_t = lax.broadcasted_iota(jnp.int32, (n_sub, seq), 0)
    own_t = lax.broadcasted_iota(jnp.int32, (n_sub, seq), 1) >> blk_shift
    past = blk_t < own_t
    blk_f = blk_t.astype(F32)
    blk_id = lax.broadcasted_iota(jnp.int32, (seq, n_lane), 1)
    own_id = lax.broadcasted_iota(jnp.int32, (seq, n_lane), 0) >> blk_shift
    k_blk = jnp.where(blk_id == own_id, 1.0, 0.0).astype(BF16)
    ones = jnp.ones((seq, n_lane), BF16)
    causal = (lax.broadcasted_iota(jnp.int32, (blk, blk), 1)
              <= lax.broadcasted_iota(jnp.int32, (blk, blk), 0))

    def prepare(lanes):
        q_all = q_ref[:, lanes]
        km = km_ref[:, 0, lanes]
        if n_sub > n_blk:
            km = jnp.concatenate([km, jnp.zeros((n_sub - n_blk, HEAD_DIM), F32)], axis=0)
        km_hi = km.astype(BF16)
        km_lo = (km - km_hi.astype(F32)).astype(BF16)
        gate = (lax.dot_general(km_hi, q_all, nt_dims, preferred_element_type=F32)
                + lax.dot_general(km_lo, q_all, nt_dims, preferred_element_type=F32))
        g = jnp.where(past, gate, NEG_INF)
        sel = jnp.zeros((n_sub, seq), F32)
        for _ in range(k_sel):
            mx = jnp.max(g, axis=0, keepdims=True)
            first = jnp.min(jnp.where(g == mx, blk_f, float(n_sub)), axis=0, keepdims=True)
            pick = blk_f == first
            sel = jnp.where(pick, 1.0, sel)
            g = jnp.where(pick, NEG_INF, g)
        visible = ((sel > 0.0) & past) | (blk_t == own_t)
        bias_t = jnp.concatenate([jnp.where(visible, 0.0, MASKED), jnp.zeros((n_lane - n_sub, seq), F32)],
                                 axis=0)
        q_bias = bias_t.T.astype(BF16)
        k_aug = jnp.concatenate([k_ref[:, lanes], k_blk], axis=1)
        v_aug = jnp.concatenate([v_ref[:, lanes], ones], axis=1)
        return q_all, q_bias, k_aug, v_aug, lanes

    def scores(head, i):
        q_all, q_bias, k_aug, _, _ = head
        rows = slice(i * blk, (i + 1) * blk)
        q_aug = jnp.concatenate([q_all[rows], q_bias[rows]], axis=1)
        return lax.dot_general(q_aug, k_aug[:(i + 1) * blk], nt_dims, preferred_element_type=F32)

    def finish(head, i, p):
        pv = jnp.dot(p, head[3][:(i + 1) * blk], preferred_element_type=F32)
        o_ref[i * blk:(i + 1) * blk, head[4]] = (pv[:, :HEAD_DIM] / pv[:, HEAD_DIM:HEAD_DIM + 1]).astype(BF16)

    def softmax_numerator(raw, i):
        own = jnp.where(causal, raw[:, i * blk:], MASKED)
        parts = [raw[:, :i * blk], own] if i else [own]
        m = jnp.max(own, axis=1, keepdims=True)
        if i:
            m = jnp.maximum(m, jnp.max(parts[0], axis=1, keepdims=True))
        return jnp.concatenate([jnp.exp2((t - m) * exp2_scale) for t in parts], axis=1).astype(BF16)

    heads = [prepare(slice(hh * HEAD_DIM, (hh + 1) * HEAD_DIM)) for hh in range(HEADS_PER_STEP)]
    raw_next = [scores(head, 0) for head in heads]
    p_prev = None
    for i in range(n_blk):
        raws = raw_next
        if i + 1 < n_blk:
            raw_next = [scores(head, i + 1) for head in heads]
        if p_prev is not None:
            for head, p in zip(heads, p_prev):
                finish(head, i - 1, p)
        p_prev = [softmax_numerator(raw, i) for raw in raws]
    for head, p in zip(heads, p_prev):
        finish(head, n_blk - 1, p)


def _moba_attention(q, k, v, k_mean, batch, seq):
    n = q.shape[0]
    n_blk = seq // MOBA_BLOCK
    width = HEADS_PER_STEP * HEAD_DIM
    seq_head = lambda b, h: (b, h)
    return pl.pallas_call(
        functools.partial(_attn_kernel, n_blk=n_blk),
        grid=(batch, N_HEADS // HEADS_PER_STEP),
        in_specs=[
            pl.BlockSpec((seq, width), seq_head),
            pl.BlockSpec((seq, width), seq_head),
            pl.BlockSpec((seq, width), seq_head),
            pl.BlockSpec((n_blk, 1, width), lambda b, h: (b, 0, h)),
        ],
        out_specs=pl.BlockSpec((seq, width), seq_head),
        out_shape=jax.ShapeDtypeStruct((n, D_MODEL), BF16),
        compiler_params=_params("parallel", "parallel"),
        name="moba_attention",
    )(q, k, v, k_mean)


def _merge_kernel(attn_ref, ga_ref, cg_ref, x_ref, wao_ref, wout_ref, g1_ref, b1_ref,
                  wr_ref, rb_ref, h_ref, hp_ref, idx_ref, wgt_ref, rank_ref, total_ref, count_ref, hprev_ref):
    tm = x_ref.shape[0]
    step = pl.program_id(0)

    @pl.when(step == 0)
    def _():
        count_ref[...] = jnp.zeros((N_EXPERTS, 1), F32)
        hprev_ref[...] = jnp.zeros(hprev_ref.shape, F32)

    h = hprev_ref[...]

    a = jnp.dot(attn_ref[...], wao_ref[...], preferred_element_type=F32)
    merged = ga_ref[...].astype(F32) * a + cg_ref[...].astype(F32)
    y = jnp.dot(merged.astype(BF16), wout_ref[...], preferred_element_type=F32)
    h_new = _layer_norm(DEEPNORM_ALPHA * x_ref[...] + y, g1_ref[...], b1_ref[...])
    h_ref[...] = h_new
    hp_ref[...] = _pack_bf16_halves(h_new)
    hprev_ref[...] = h_new

    nt_dims = (((1,), (1,)), ((), ()))
    h_hi = h.astype(BF16)
    h_lo = (h - h_hi.astype(F32)).astype(BF16)
    w = wr_ref[...]
    w_hi = w.astype(BF16)
    w_lo = (w - w_hi.astype(F32)).astype(BF16)
    logits = (lax.dot_general(w_hi, h_hi, nt_dims, preferred_element_type=F32)
              + lax.dot_general(w_hi, h_lo, nt_dims, preferred_element_type=F32)
              + lax.dot_general(w_lo, h_hi, nt_dims, preferred_element_type=F32))
    scores = _sigmoid(logits)
    biased = scores + rb_ref[...]

    g3 = biased.reshape(N_GROUPS, GROUP_SIZE, tm)
    m1 = jnp.max(g3, axis=1, keepdims=True)
    is_max = g3 == m1
    n_max = jnp.sum(jnp.where(is_max, 1.0, 0.0), axis=1, keepdims=True)
    m2 = jnp.max(jnp.where(is_max, NEG_INF, g3), axis=1, keepdims=True)
    grp = (m1 + jnp.where(n_max >= 2.0, m1, m2)).reshape(N_GROUPS, tm)

    gid = lax.broadcasted_iota(jnp.int32, (N_GROUPS, tm), 0)
    rank = jnp.zeros((N_GROUPS, tm), F32)
    for o in range(N_GROUPS):
        other = grp[o:o + 1, :]
        ahead = (other > grp) | ((other == grp) & (o < gid))
        rank = rank + jnp.where(ahead, 1.0, 0.0)
    grp_keep = jnp.where(rank < float(TOPK_GROUPS), 1.0, 0.0)
    keep = jnp.broadcast_to(grp_keep.reshape(N_GROUPS, 1, tm),
                            (N_GROUPS, GROUP_SIZE, tm)).reshape(N_EXPERTS, tm)
    cand = jnp.where(keep > 0.0, biased, NEG_INF)

    eid = lax.broadcasted_iota(jnp.int32, (N_EXPERTS, tm), 0).astype(F32)
    chosen = jnp.zeros((N_EXPERTS, tm), F32)
    firsts = []
    for r in range(TOP_K):
        mx = jnp.max(cand, axis=0, keepdims=True)
        first = jnp.min(jnp.where(cand == mx, eid, float(N_EXPERTS)), axis=0, keepdims=True)
        pick = eid == first
        firsts.append(first)
        idx_ref[r:r + 1, :] = first.astype(jnp.int32)
        wgt_ref[r:r + 1, :] = jnp.sum(jnp.where(pick, scores, 0.0), axis=0, keepdims=True)
        cand = jnp.where(pick, NEG_INF, cand)
        chosen = jnp.where(pick, 1.0, chosen)
    top_s = wgt_ref[...]
    wgt_ref[...] = top_s / (jnp.sum(top_s, axis=0, keepdims=True) + 1e-20) * ROUTED_SCALE

    earlier = (lax.broadcasted_iota(jnp.int32, (tm, tm), 0)
               < lax.broadcasted_iota(jnp.int32, (tm, tm), 1))
    before = jnp.dot(chosen.astype(BF16), jnp.where(earlier, 1.0, 0.0).astype(BF16),
                     preferred_element_type=F32) + count_ref[...]
    for r in range(TOP_K):
        rank_ref[r:r + 1, :] = jnp.sum(jnp.where(eid == firsts[r], before, 0.0),
                                       axis=0, keepdims=True).astype(jnp.int32)
    counted = jnp.where(step > 0, 1.0, 0.0)
    total = count_ref[...] + counted * jnp.sum(chosen, axis=1, keepdims=True)
    count_ref[...] = total
    total_ref[...] = total


def _merge_ln_router(attn, gate_a, conv_gated, x2, w_attn_o_b, w_out_b, ln_g, ln_b, w_router_t, router_bias,
                     first_token, n):
    tm = MERGE_TILE
    first_tile = first_token // tm
    n_tiles = n // tm
    row_in = lambda i: (jnp.minimum(i, n_tiles - 1) + first_tile, 0)
    row = lambda i: (jnp.minimum(i, n_tiles - 1), 0)
    const = lambda i: (0, 0)
    col = lambda i: (0, jnp.maximum(i - 1, 0))
    return pl.pallas_call(
        _merge_kernel,
        grid=(n_tiles + 1,),
        in_specs=[
            pl.BlockSpec((tm, D_MODEL), row_in),
            pl.BlockSpec((tm, D_MODEL), row_in),
            pl.BlockSpec((tm, D_MODEL), row_in),
            pl.BlockSpec((tm, D_MODEL), row_in),
            pl.BlockSpec((D_MODEL, D_MODEL), const),
            pl.BlockSpec((D_MODEL, D_MODEL), const),
            pl.BlockSpec((1, D_MODEL), const),
            pl.BlockSpec((1, D_MODEL), const),
            pl.BlockSpec((N_EXPERTS, D_MODEL), const),
            pl.BlockSpec((N_EXPERTS, 1), const),
        ],
        out_specs=[
            pl.BlockSpec((tm, D_MODEL), row),
            pl.BlockSpec((tm, HALF), row),
            pl.BlockSpec((TOP_K, tm), col),
            pl.BlockSpec((TOP_K, tm), col),
            pl.BlockSpec((TOP_K, tm), col),
            pl.BlockSpec((N_EXPERTS, 1), const),
        ],
        out_shape=[
            jax.ShapeDtypeStruct((n, D_MODEL), F32),
            jax.ShapeDtypeStruct((n, HALF), jnp.uint32),
            jax.ShapeDtypeStruct((TOP_K, n), jnp.int32),
            jax.ShapeDtypeStruct((TOP_K, n), F32),
            jax.ShapeDtypeStruct((TOP_K, n), jnp.int32),
            jax.ShapeDtypeStruct((N_EXPERTS, 1), F32),
        ],
        scratch_shapes=[pltpu.VMEM((N_EXPERTS, 1), F32), pltpu.VMEM((tm, D_MODEL), F32)],
        compiler_params=_params("arbitrary"),
        name="merge_ln_router",
    )(attn, gate_a, conv_gated, x2, w_attn_o_b, w_out_b, ln_g, ln_b, w_router_t, router_bias)


def _sc_mesh():
    return plsc.VectorSubcoreMesh(core_axis_name="c", subcore_axis_name="s",
                                  num_cores=SC_CORES, num_subcores=SC_SUBCORES)


def _sc_worker_base(rows_per_worker):
    return (lax.axis_index("s") * SC_CORES + lax.axis_index("c")) * rows_per_worker


def _sc_gather_rows(table, idx):
    m = idx.shape[0]
    width = table.shape[1]
    per_worker = m // (SC_CORES * SC_SUBCORES)
    assert per_worker * SC_CORES * SC_SUBCORES == m and per_worker % SC_CHUNK == 0

    @functools.partial(
        pl.kernel, mesh=_sc_mesh(),
        out_type=jax.ShapeDtypeStruct((m, width), table.dtype),
        scratch_types=[pltpu.VMEM((SC_CHUNK,), jnp.int32),
                       pltpu.VMEM((SC_CHUNK, width), table.dtype),
                       pltpu.SemaphoreType.DMA],
        name="sc_gather_rows")
    def gather(table_hbm, idx_hbm, out_hbm, idx_v, rows_v, sem):
        base = _sc_worker_base(per_worker)

        @pl.loop(0, per_worker // SC_CHUNK)
        def _(c):
            off = pl.multiple_of(base + c * SC_CHUNK, SC_CHUNK)
            pltpu.sync_copy(idx_hbm.at[pl.ds(off, SC_CHUNK)], idx_v)
            pltpu.async_copy(table_hbm.at[idx_v], rows_v, sem).wait()
            pltpu.sync_copy(rows_v, out_hbm.at[pl.ds(off, SC_CHUNK)])

    return gather(table, idx)


def _sc_scatter_rows(rows, pos, n_out):
    n, width = rows.shape
    per_worker = n // (SC_CORES * SC_SUBCORES)
    assert per_worker * SC_CORES * SC_SUBCORES == n and per_worker % SC_CHUNK == 0

    @functools.partial(
        pl.kernel, mesh=_sc_mesh(),
        out_type=jax.ShapeDtypeStruct((n_out, width), rows.dtype),
        scratch_types=[pltpu.VMEM((SC_CHUNK,), jnp.int32),
                       pltpu.VMEM((SC_CHUNK, width), rows.dtype),
                       pltpu.SemaphoreType.DMA],
        name="sc_scatter_rows")
    def scatter(rows_hbm, pos_hbm, out_hbm, idx_v, rows_v, sem):
        base = _sc_worker_base(per_worker)

        @pl.loop(0, per_worker // SC_CHUNK)
        def _(c):
            off = pl.multiple_of(base + c * SC_CHUNK, SC_CHUNK)
            pltpu.sync_copy(rows_hbm.at[pl.ds(off, SC_CHUNK)], rows_v)
            for r in range(TOP_K):
                pltpu.sync_copy(pos_hbm.at[pl.ds(r * n + off, SC_CHUNK)], idx_v)
                pltpu.async_copy(rows_v, out_hbm.at[idx_v], sem).wait()

    return scatter(rows, pos)


def _position_kernel(idx_ref, rank_ref, start_ref, pos_ref):
    tl = idx_ref.shape[1]
    eid = lax.broadcasted_iota(jnp.int32, (N_EXPERTS, tl), 0)
    start = start_ref[...]
    for r in range(TOP_K):
        here = jnp.sum(jnp.where(eid == idx_ref[r:r + 1, :], start, 0.0), axis=0, keepdims=True)
        pos_ref[r:r + 1, :] = here.astype(jnp.int32) + rank_ref[r:r + 1, :]


def _positions(top_idx, rank, group_start):
    n = top_idx.shape[1]
    tl = POSITION_TILE
    col = lambda i: (0, i)
    return pl.pallas_call(
        _position_kernel,
        grid=(n // tl,),
        in_specs=[pl.BlockSpec((TOP_K, tl), col), pl.BlockSpec((TOP_K, tl), col),
                  pl.BlockSpec((N_EXPERTS, 1), lambda i: (0, 0))],
        out_specs=pl.BlockSpec((TOP_K, tl), col),
        out_shape=jax.ShapeDtypeStruct((TOP_K, n), jnp.int32),
        compiler_params=_params("parallel"),
        name="positions",
    )(top_idx, rank, group_start)


def _expert_kernel(te_ref, tv_ref, xs_ref, wg_ref, wu_ref, wd_ref, y_ref, act_ref):
    i = pl.program_id(0)
    valid = tv_ref[i]
    valid_prev = tv_ref[jnp.maximum(i - 1, 0)]

    @pl.when(i == 0)
    def _():
        act_ref[...] = jnp.zeros(act_ref.shape, BF16)

    @pl.when((valid > 0) | (valid_prev > 0))
    def _():
        act_prev = act_ref[...]
        wd = wd_ref[0]
        for c in range(EXPERT_TILE // EXPERT_CHAIN):
            rows = slice(c * EXPERT_CHAIN, (c + 1) * EXPERT_CHAIN)
            y_ref[rows, :] = _pack_bf16_halves(jnp.dot(act_prev[rows], wd, preferred_element_type=F32))
        wg = wg_ref[0]
        wu = wu_ref[0]
        for c in range(EXPERT_TILE // EXPERT_CHAIN):
            rows = slice(c * EXPERT_CHAIN, (c + 1) * EXPERT_CHAIN)
            live = lax.broadcasted_iota(jnp.int32, (EXPERT_CHAIN, HALF), 0) < valid - c * EXPERT_CHAIN
            lo, hi = _unpack_bf16_halves(jnp.where(live, xs_ref[rows, :], jnp.uint32(0)))
            x = jnp.concatenate([lo, hi], axis=1).astype(BF16)
            g = jnp.dot(x, wg, preferred_element_type=F32)
            u = jnp.dot(x, wu, preferred_element_type=F32)
            act_ref[rows, :] = (g * _sigmoid(g) * u).astype(BF16)

    @pl.when((valid == 0) & (valid_prev == 0))
    def _():
        y_ref[...] = jnp.zeros(y_ref.shape, jnp.uint32)


def _grouped_experts(tile_expert, tile_valid, xs, wg, wu, wd):
    p = xs.shape[0]
    t = EXPERT_TILE
    n_tiles = p // t
    tile_expert = jnp.concatenate([tile_expert, tile_expert[-1:]])
    tile_valid = jnp.concatenate([tile_valid, jnp.zeros((1,), jnp.int32)])
    row_in = lambda i, te, tv: (jnp.minimum(i, n_tiles - 1), 0)
    row_out = lambda i, te, tv: (jnp.maximum(i - 1, 0), 0)
    expert = lambda i, te, tv: (te[i], 0, 0)
    expert_prev = lambda i, te, tv: (te[jnp.maximum(i - 1, 0)], 0, 0)
    return pl.pallas_call(
        _expert_kernel,
        grid_spec=pltpu.PrefetchScalarGridSpec(
            num_scalar_prefetch=2,
            grid=(n_tiles + 1,),
            in_specs=[
                pl.BlockSpec((t, HALF), row_in),
                pl.BlockSpec((1, D_MODEL, EXPERT_HIDDEN), expert),
                pl.BlockSpec((1, D_MODEL, EXPERT_HIDDEN), expert),
                pl.BlockSpec((1, EXPERT_HIDDEN, D_MODEL), expert_prev),
            ],
            out_specs=pl.BlockSpec((t, HALF), row_out),
            scratch_shapes=[pltpu.VMEM((t, EXPERT_HIDDEN), BF16)],
        ),
        out_shape=jax.ShapeDtypeStruct((p, HALF), jnp.uint32),
        compiler_params=_params("arbitrary"),
        name="grouped_experts",
    )(tile_expert, tile_valid, xs, wg, wu, wd)


def _combine_kernel(h_ref, yg_ref, wt_ref, wsg_ref, wsu_ref, wsd_ref, g2_ref, b2_ref, *out_refs):
    o_ref = out_refs[-1]
    h = h_ref[...]
    hb = h.astype(BF16)
    g = jnp.dot(hb, wsg_ref[...], preferred_element_type=F32)
    u = jnp.dot(hb, wsu_ref[...], preferred_element_type=F32)
    shared = jnp.dot((g * _sigmoid(g) * u).astype(BF16), wsd_ref[...], preferred_element_type=F32)
    wt = wt_ref[...]
    r_lo = jnp.zeros((h.shape[0], HALF), F32)
    r_hi = jnp.zeros((h.shape[0], HALF), F32)
    for r in range(TOP_K):
        lo, hi = _unpack_bf16_halves(yg_ref[r])
        w = wt[:, r:r + 1]
        r_lo = r_lo + lo * w
        r_hi = r_hi + hi * w
    routed = jnp.concatenate([r_lo, r_hi], axis=1)
    o_ref[...] = _layer_norm(DEEPNORM_ALPHA * h + (shared + routed), g2_ref[...], b2_ref[...])


def _combine_ln(h, yg, w_tok, wsg_b, wsu_b, wsd_b, ln_g, ln_b, out_so_far, first_token, n_total):
    n = h.shape[0]
    tm = MERGE_TILE
    first_tile = first_token // tm
    row = lambda i: (i, 0)
    const = lambda i: (0, 0)
    hidden = wsg_b.shape[1]
    in_specs = [
        pl.BlockSpec((tm, D_MODEL), row),
        pl.BlockSpec((TOP_K, tm, HALF), lambda i: (0, i, 0)),
        pl.BlockSpec((tm, TOP_K), row),
        pl.BlockSpec((D_MODEL, hidden), const),
        pl.BlockSpec((D_MODEL, hidden), const),
        pl.BlockSpec((hidden, D_MODEL), const),
        pl.BlockSpec((1, D_MODEL), const),
        pl.BlockSpec((1, D_MODEL), const),
    ]
    args = [h, yg, w_tok, wsg_b, wsu_b, wsd_b, ln_g, ln_b]
    aliases = {}
    if out_so_far is not None:
        in_specs.append(pl.BlockSpec(memory_space=pl.ANY))
        args.append(out_so_far)
        aliases = {len(args) - 1: 0}
    return pl.pallas_call(
        _combine_kernel,
        grid=(n // tm,),
        in_specs=in_specs,
        out_specs=pl.BlockSpec((tm, D_MODEL), lambda i: (i + first_tile, 0)),
        out_shape=jax.ShapeDtypeStruct((n_total, D_MODEL), F32),
        input_output_aliases=aliases,
        compiler_params=_params("parallel"),
        name="combine_ln",
    )(*args)


def _group_layout(totals, n_tokens):
    t = EXPERT_TILE
    n_tiles = (TOP_K * n_tokens + N_EXPERTS * (t - 1)) // t
    counts = totals[:, 0].astype(jnp.int32)
    padded = ((counts + t - 1) // t) * t
    group_end = jnp.cumsum(padded)
    group_start = group_end - padded
    tile_start = jnp.arange(n_tiles, dtype=jnp.int32) * t
    tile_expert = jnp.minimum(jnp.sum((group_end[None, :] <= tile_start[:, None]).astype(jnp.int32), axis=1),
                              N_EXPERTS - 1)
    of_tile = tile_expert[:, None] == jnp.arange(N_EXPERTS, dtype=jnp.int32)[None, :]
    real_end = jnp.sum(jnp.where(of_tile, (group_start + counts)[None, :], 0), axis=1)
    tile_valid = jnp.clip(real_end - tile_start, 0, t).astype(jnp.int32)
    return group_start.astype(F32).reshape(N_EXPERTS, 1), tile_expert, tile_valid, n_tiles * t


def _rope_tables(seq):
    inv_freq = ROPE_THETA ** (-jnp.arange(0, ROPE_DIM, 2, dtype=F32) / ROPE_DIM)
    ang = jnp.arange(seq).astype(F32)[:, None] * inv_freq[None, :]
    cos, sin = jnp.cos(ang), jnp.sin(ang)
    rest = HEAD_DIM - ROPE_DIM
    zeros = jnp.zeros((seq, ROPE_HALF), F32)
    cos_t = jnp.concatenate([cos, cos, jnp.ones((seq, rest), F32)], axis=1)
    sa_t = jnp.concatenate([-sin, zeros, jnp.zeros((seq, rest), F32)], axis=1)
    sb_t = jnp.concatenate([zeros, sin, jnp.zeros((seq, rest), F32)], axis=1)
    return cos_t, sa_t, sb_t


def _layer(x2, batch, seq, w_in, b_gate, w_attn_o, w_dw, b_dw, conv_ln_g, conv_ln_b, w_conv_o, w_out,
           ln1_g, ln1_b, w_router, router_bias, w_exp_gate, w_exp_up, w_exp_down,
           w_sh_gate, w_sh_up, w_sh_down, ln2_g, ln2_b):
    n = x2.shape[0]
    row = lambda v: v.reshape(1, -1)
    cos, sa, sb = _rope_tables(seq)
    q, k, v, u, gate_a, gate_c, k_mean, w_gate_b, w_up_b, w_down_b = _in_projection(
        x2, w_in.astype(BF16), row(b_gate), cos, sa, sb, w_exp_gate, w_exp_up, w_exp_down, seq)
    w_taps = jnp.broadcast_to(w_dw.reshape(CONV_WIDTH, 1, D_MODEL), (CONV_WIDTH, SUBLANES, D_MODEL))
    conv_gated = _conv_branch(u, w_taps, row(b_dw), row(conv_ln_g),
                              row(conv_ln_b), w_conv_o.astype(BF16), gate_c, batch, seq)
    attn = _moba_attention(q, k, v, k_mean, batch, seq)
    w_attn_o_b, w_out_b, w_router_t = w_attn_o.astype(BF16), w_out.astype(BF16), w_router.T
    shared_w = (w_sh_gate.astype(BF16), w_sh_up.astype(BF16), w_sh_down.astype(BF16))
    n_chunk = n // MOE_CHUNKS
    assert n_chunk * MOE_CHUNKS == n and n_chunk % TOKEN_TILE == 0 and n_chunk % POSITION_TILE == 0
    out = None
    for c in range(MOE_CHUNKS):
        first = c * n_chunk
        h, h_packed, top_idx, top_w, rank, totals = _merge_ln_router(
            attn, gate_a, conv_gated, x2, w_attn_o_b, w_out_b, row(ln1_g), row(ln1_b),
            w_router_t, router_bias.reshape(N_EXPERTS, 1), first, n_chunk)
        group_start, tile_expert, tile_valid, n_rows = _group_layout(totals, n_chunk)
        pos = _positions(top_idx, rank, group_start).reshape(TOP_K * n_chunk)
        xs = _sc_scatter_rows(h_packed, pos, n_rows)
        ys = _grouped_experts(tile_expert, tile_valid, xs, w_gate_b, w_up_b, w_down_b)
        yg = _sc_gather_rows(ys, pos).reshape(TOP_K, n_chunk, HALF)
        out = _combine_ln(h, yg, top_w.T, *shared_w, row(ln2_g), row(ln2_b), out, first, n)
    return out


def kernel(x, w_in, b_gate, w_attn_o, w_dw, b_dw, conv_ln_g, conv_ln_b, w_conv_o, w_out, ln1_g, ln1_b,
           w_router, router_bias, w_exp_gate, w_exp_up, w_exp_down, w_sh_gate, w_sh_up, w_sh_down,
           ln2_g, ln2_b):
    batch, seq, d = x.shape
    assert d == D_MODEL and seq % MOBA_BLOCK == 0 and seq % TOKEN_TILE == 0
    assert w_in.shape[0] == DEPTH
    x2 = x.reshape(batch * seq, d)
    for l in range(DEPTH):
        x2 = _layer(x2, batch, seq, w_in[l], b_gate[l], w_attn_o[l], w_dw[l], b_dw[l], conv_ln_g[l],
                    conv_ln_b[l], w_conv_o[l], w_out[l], ln1_g[l], ln1_b[l], w_router[l], router_bias[l],
                    w_exp_gate[l], w_exp_up[l], w_exp_down[l], w_sh_gate[l], w_sh_up[l], w_sh_down[l],
                    ln2_g[l], ln2_b[l])
    return x2.reshape(batch, seq, d)
```

```python
import functools

import jax
import jax.numpy as jnp
from jax import lax
from jax.experimental import pallas as pl
from jax.experimental.pallas import tpu as pltpu
from jax.experimental.pallas import tpu_sc as plsc

F32 = jnp.float32
BF16 = jnp.bfloat16
NEG_INF = float("-inf")
MASKED = -1e30
LOG2_E = 1.4426950408889634

D_MODEL = 1024
N_HEADS = 8
HEAD_DIM = 128
ROPE_THETA = 500000.0
ROPE_DIM = HEAD_DIM // 4
ROPE_HALF = ROPE_DIM // 2
MOBA_BLOCK = 256
MOBA_TOPK = 3
CONV_WIDTH = 31
SUBLANES = 8
CONV_HALO = 32
CONV_ROWS = 32
N_EXPERTS = 256
TOP_K = 8
N_GROUPS = 8
GROUP_SIZE = N_EXPERTS // N_GROUPS
TOPK_GROUPS = 4
EXPERT_HIDDEN = 256
ROUTED_SCALE = 2.5
LN_EPS = 1e-5
DEPTH = 1
DEEPNORM_ALPHA = (2 * DEPTH) ** 0.25
HALF = D_MODEL // 2

TOKEN_TILE = 256
MERGE_TILE = 512
EXPERT_TILE = 512
EXPERT_CHAIN = 256
POSITION_TILE = 1024
MOE_CHUNKS = 2
HEADS_PER_STEP = 2
SC_CORES = 2
SC_SUBCORES = 16
SC_CHUNK = 64
VMEM_LIMIT = 56 * 1024 * 1024


def _params(*semantics):
    return pltpu.CompilerParams(dimension_semantics=semantics, vmem_limit_bytes=VMEM_LIMIT)


def _sigmoid(x):
    return 1.0 / (1.0 + jnp.exp(-x))


def _layer_norm(x, g, b):
    mu = jnp.mean(x, axis=-1, keepdims=True)
    xc = x - mu
    var = jnp.mean(xc * xc, axis=-1, keepdims=True)
    return xc * lax.rsqrt(var + LN_EPS) * g + b


def _pack_bf16_halves(y):
    lo = lax.bitcast_convert_type(y[:, :HALF].astype(BF16).astype(F32), jnp.uint32)
    hi = lax.bitcast_convert_type(y[:, HALF:].astype(BF16).astype(F32), jnp.uint32)
    return (hi & jnp.uint32(0xFFFF0000)) | (lo >> 16)


def _unpack_bf16_halves(p):
    lo = lax.bitcast_convert_type(p << 16, F32)
    hi = lax.bitcast_convert_type(p & jnp.uint32(0xFFFF0000), F32)
    return lo, hi


def _inproj_kernel(x_ref, w_ref, bg_ref, cos_ref, sa_ref, sb_ref, wg_ref, wu_ref, wd_ref,
                   q_ref, k_ref, v_ref, u_ref, ga_ref, gc_ref, km_ref, wgb_ref, wub_ref, wdb_ref):
    wgb_ref[...] = wg_ref[...].astype(BF16)
    wub_ref[...] = wu_ref[...].astype(BF16)
    wdb_ref[...] = wd_ref[...].astype(BF16)

    tm = x_ref.shape[0]
    xb = x_ref[...].astype(BF16)

    def proj(c):
        return jnp.dot(xb, w_ref[:, c * D_MODEL:(c + 1) * D_MODEL], preferred_element_type=F32)

    cos = cos_ref[...]
    sa = sa_ref[...]
    sb = sb_ref[...]

    def rope_head(t):
        return (t * cos + pltpu.roll(t, HEAD_DIM - ROPE_HALF, 1) * sa
                + pltpu.roll(t, ROPE_HALF, 1) * sb)

    q = proj(0)
    for h in range(N_HEADS):
        sl = slice(h * HEAD_DIM, (h + 1) * HEAD_DIM)
        q_ref[:, sl] = rope_head(q[:, sl]).astype(BF16)
    k = proj(1)
    for h in range(N_HEADS):
        sl = slice(h * HEAD_DIM, (h + 1) * HEAD_DIM)
        kr = rope_head(k[:, sl])
        k_ref[:, sl] = kr.astype(BF16)
        for g in range(tm // MOBA_BLOCK):
            km_ref[g, :, sl] = jnp.mean(kr[g * MOBA_BLOCK:(g + 1) * MOBA_BLOCK], axis=0, keepdims=True)
    v_ref[...] = proj(2).astype(BF16)
    u_ref[...] = proj(3) * _sigmoid(proj(4))
    ga_ref[...] = _sigmoid(proj(5) + bg_ref[:, :D_MODEL]).astype(BF16)
    gc_ref[...] = _sigmoid(proj(6) + bg_ref[:, D_MODEL:]).astype(BF16)


def _in_projection(x2, w_in_b, b_gate, cos, sa, sb, w_exp_gate, w_exp_up, w_exp_down, seq):
    n = x2.shape[0]
    tm = TOKEN_TILE
    n_cols = w_in_b.shape[1]
    tiles_per_seq = seq // tm
    steps = n // tm
    per_step = -(-N_EXPERTS // steps)
    assert N_EXPERTS % per_step == 0
    row = lambda i: (i, 0)
    const = lambda i: (0, 0)
    pos = lambda i: (i % tiles_per_seq, 0)
    experts = lambda i: (jnp.minimum(i, N_EXPERTS // per_step - 1), 0, 0)
    tok_bf16 = jax.ShapeDtypeStruct((n, D_MODEL), BF16)
    up_block = (per_step, D_MODEL, EXPERT_HIDDEN)
    down_block = (per_step, EXPERT_HIDDEN, D_MODEL)
    return pl.pallas_call(
        _inproj_kernel,
        grid=(steps,),
        in_specs=[
            pl.BlockSpec((tm, D_MODEL), row),
            pl.BlockSpec((D_MODEL, n_cols), const),
            pl.BlockSpec((1, 2 * D_MODEL), const),
            pl.BlockSpec((tm, HEAD_DIM), pos),
            pl.BlockSpec((tm, HEAD_DIM), pos),
            pl.BlockSpec((tm, HEAD_DIM), pos),
            pl.BlockSpec(up_block, experts),
            pl.BlockSpec(up_block, experts),
            pl.BlockSpec(down_block, experts),
        ],
        out_specs=[
            pl.BlockSpec((tm, D_MODEL), row),
            pl.BlockSpec((tm, D_MODEL), row),
            pl.BlockSpec((tm, D_MODEL), row),
            pl.BlockSpec((tm, D_MODEL), row),
            pl.BlockSpec((tm, D_MODEL), row),
            pl.BlockSpec((tm, D_MODEL), row),
            pl.BlockSpec((tm // MOBA_BLOCK, 1, D_MODEL), lambda i: (i, 0, 0)),
            pl.BlockSpec(up_block, experts),
            pl.BlockSpec(up_block, experts),
            pl.BlockSpec(down_block, experts),
        ],
        out_shape=[tok_bf16, tok_bf16, tok_bf16,
                   jax.ShapeDtypeStruct((n, D_MODEL), F32),
                   tok_bf16, tok_bf16,
                   jax.ShapeDtypeStruct((n // MOBA_BLOCK, 1, D_MODEL), F32),
                   jax.ShapeDtypeStruct(w_exp_gate.shape, BF16),
                   jax.ShapeDtypeStruct(w_exp_up.shape, BF16),
                   jax.ShapeDtypeStruct(w_exp_down.shape, BF16)],
        compiler_params=_params("arbitrary"),
        name="in_projection",
    )(x2, w_in_b, b_gate, cos, sa, sb, w_exp_gate, w_exp_up, w_exp_down)


def _conv_kernel(u_ref, wdw_ref, bdw_ref, lng_ref, lnb_ref, wo_ref, gc_ref, o_ref, buf_ref, sh_ref, y_ref):
    ts = u_ref.shape[0]
    s = pl.program_id(1)

    @pl.when(s == 0)
    def _():
        buf_ref[0:CONV_HALO, :] = jnp.zeros((CONV_HALO, D_MODEL), F32)

    @pl.when(s > 0)
    def _():
        buf_ref[0:CONV_HALO, :] = buf_ref[ts:ts + CONV_HALO, :]

    buf_ref[CONV_HALO:CONV_HALO + ts, :] = u_ref[...]

    span = ts + CONV_HALO - SUBLANES
    for b in range(1, SUBLANES):
        sh_ref[b - 1, 0:span, :] = buf_ref[b:b + span, :]

    base = CONV_HALO - (CONV_WIDTH - 1)
    for c in range(ts // CONV_ROWS):
        r0 = c * CONV_ROWS
        acc = jnp.zeros((CONV_ROWS // SUBLANES, SUBLANES, D_MODEL), F32)
        for j in range(CONV_WIDTH):
            shift = (base + j) % SUBLANES
            row = r0 + base + j - shift
            src = buf_ref if shift == 0 else sh_ref.at[shift - 1]
            tap = src[row:row + CONV_ROWS, :].reshape(CONV_ROWS // SUBLANES, SUBLANES, D_MODEL)
            acc = acc + tap * wdw_ref[j]
        acc = acc.reshape(CONV_ROWS, D_MODEL)
        y = _layer_norm(acc + bdw_ref[...], lng_ref[...], lnb_ref[...])
        y_ref[r0:r0 + CONV_ROWS, :] = (y * _sigmoid(y)).astype(BF16)
    z = jnp.dot(y_ref[...], wo_ref[...], preferred_element_type=F32)
    o_ref[...] = (z * gc_ref[...].astype(F32)).astype(BF16)


def _conv_branch(u, w_dw, b_dw, ln_g, ln_b, w_o_b, gate_c, batch, seq):
    n = u.shape[0]
    ts = TOKEN_TILE
    tiles_per_seq = seq // ts
    row = lambda b, s: (b * tiles_per_seq + s, 0)
    const = lambda b, s: (0, 0)
    return pl.pallas_call(
        _conv_kernel,
        grid=(batch, tiles_per_seq),
        in_specs=[
            pl.BlockSpec((ts, D_MODEL), row),
            pl.BlockSpec((CONV_WIDTH, SUBLANES, D_MODEL), lambda b, s: (0, 0, 0)),
            pl.BlockSpec((1, D_MODEL), const),
            pl.BlockSpec((1, D_MODEL), const),
            pl.BlockSpec((1, D_MODEL), const),
            pl.BlockSpec((D_MODEL, D_MODEL), const),
            pl.BlockSpec((ts, D_MODEL), row),
        ],
        out_specs=pl.BlockSpec((ts, D_MODEL), row),
        out_shape=jax.ShapeDtypeStruct((n, D_MODEL), BF16),
        scratch_shapes=[pltpu.VMEM((ts + CONV_HALO, D_MODEL), F32),
                        pltpu.VMEM((SUBLANES - 1, ts + CONV_HALO - SUBLANES, D_MODEL), F32),
                        pltpu.VMEM((ts, D_MODEL), BF16)],
        compiler_params=_params("parallel", "arbitrary"),
        name="conv_branch",
    )(u, w_dw, b_dw, ln_g, ln_b, w_o_b, gate_c)


def _attn_kernel(q_ref, k_ref, v_ref, km_ref, o_ref, *, n_blk):
    blk = MOBA_BLOCK
    seq = n_blk * blk
    k_sel = min(MOBA_TOPK, n_blk)
    exp2_scale = HEAD_DIM ** -0.5 * LOG2_E
    nt_dims = (((1,), (1,)), ((), ()))
    n_lane = HEAD_DIM

    assert blk & (blk - 1) == 0
    blk_shift = blk.bit_length() - 1
    n_sub = -(-n_blk // SUBLANES) * SUBLANES

    blk_t = lax.broadcasted_iota(jnp.int32, (n_sub, seq), 0)
    own_t = lax.broadcasted_iota(jnp.int32, (n_sub, seq), 1) >> blk_shift
    past = blk_t < own_t
    blk_f = blk_t.astype(F32)
    blk_id = lax.broadcasted_iota(jnp.int32, (seq, n_lane), 1)
    own_id = lax.broadcasted_iota(jnp.int32, (seq, n_lane), 0) >> blk_shift
    k_blk = jnp.where(blk_id == own_id, 1.0, 0.0).astype(BF16)
    ones = jnp.ones((seq, n_lane), BF16)
    causal = (lax.broadcasted_iota(jnp.int32, (blk, blk), 1)
              <= lax.broadcasted_iota(jnp.int32, (blk, blk), 0))

    def prepare(lanes):
        q_all = q_ref[:, lanes]
        km = km_ref[:, 0, lanes]
        if n_sub > n_blk:
            km = jnp.concatenate([km, jnp.zeros((n_sub - n_blk, HEAD_DIM), F32)], axis=0)
        km_hi = km.astype(BF16)
        km_lo = (km - km_hi.astype(F32)).astype(BF16)
        gate = (lax.dot_general(km_hi, q_all, nt_dims, preferred_element_type=F32)
                + lax.dot_general(km_lo, q_all, nt_dims, preferred_element_type=F32))
        g = jnp.where(past, gate, NEG_INF)
        sel = jnp.zeros((n_sub, seq), F32)
        for _ in range(k_sel):
            mx = jnp.max(g, axis=0, keepdims=True)
            first = jnp.min(jnp.where(g == mx, blk_f, float(n_sub)), axis=0, keepdims=True)
            pick = blk_f == first
            sel = jnp.where(pick, 1.0, sel)
            g = jnp.where(pick, NEG_INF, g)
        visible = ((sel > 0.0) & past) | (blk_t == own_t)
        bias_t = jnp.concatenate([jnp.where(visible, 0.0, MASKED), jnp.zeros((n_lane - n_sub, seq), F32)],
                                 axis=0)
        q_bias = bias_t.T.astype(BF16)
        k_aug = jnp.concatenate([k_ref[:, lanes], k_blk], axis=1)
        v_aug = jnp.concatenate([v_ref[:, lanes], ones], axis=1)
        return q_all, q_bias, k_aug, v_aug, lanes

    def scores(head, i):
        q_all, q_bias, k_aug, _, _ = head
        rows = slice(i * blk, (i + 1) * blk)
        q_aug = jnp.concatenate([q_all[rows], q_bias[rows]], axis=1)
        return lax.dot_general(q_aug, k_aug[:(i + 1) * blk], nt_dims, preferred_element_type=F32)

    def finish(head, i, p):
        pv = jnp.dot(p, head[3][:(i + 1) * blk], preferred_element_type=F32)
        o_ref[i * blk:(i + 1) * blk, head[4]] = (pv[:, :HEAD_DIM] / pv[:, HEAD_DIM:HEAD_DIM + 1]).astype(BF16)

    def softmax_numerator(raw, i):
        own = jnp.where(causal, raw[:, i * blk:], MASKED)
        parts = [raw[:, :i * blk], own] if i else [own]
        m = jnp.max(own, axis=1, keepdims=True)
        if i:
            m = jnp.maximum(m, jnp.max(parts[0], axis=1, keepdims=True))
        return jnp.concatenate([jnp.exp2((t - m) * exp2_scale) for t in parts], axis=1).astype(BF16)

    heads = [prepare(slice(hh * HEAD_DIM, (hh + 1) * HEAD_DIM)) for hh in range(HEADS_PER_STEP)]
    raw_next = [scores(head, 0) for head in heads]
    p_prev = None
    for i in range(n_blk):
        raws = raw_next
        if i + 1 < n_blk:
            raw_next = [scores(head, i + 1) for head in heads]
        if p_prev is not None:
            for head, p in zip(heads, p_prev):
                finish(head, i - 1, p)
        p_prev = [softmax_numerator(raw, i) for raw in raws]
    for head, p in zip(heads, p_prev):
        finish(head, n_blk - 1, p)


def _moba_attention(q, k, v, k_mean, batch, seq):
    n = q.shape[0]
    n_blk = seq // MOBA_BLOCK
    width = HEADS_PER_STEP * HEAD_DIM
    seq_head = lambda b, h: (b, h)
    return pl.pallas_call(
        functools.partial(_attn_kernel, n_blk=n_blk),
        grid=(batch, N_HEADS // HEADS_PER_STEP),
        in_specs=[
            pl.BlockSpec((seq, width), seq_head),
            pl.BlockSpec((seq, width), seq_head),
            pl.BlockSpec((seq, width), seq_head),
            pl.BlockSpec((n_blk, 1, width), lambda b, h: (b, 0, h)),
        ],
        out_specs=pl.BlockSpec((seq, width), seq_head),
        out_shape=jax.ShapeDtypeStruct((n, D_MODEL), BF16),
        compiler_params=_params("parallel", "parallel"),
        name="moba_attention",
    )(q, k, v, k_mean)


def _merge_kernel(attn_ref, ga_ref, cg_ref, x_ref, wao_ref, wout_ref, g1_ref, b1_ref,
                  wr_ref, rb_ref, h_ref, hp_ref, idx_ref, wgt_ref, rank_ref, total_ref, count_ref, hprev_ref):
    tm = x_ref.shape[0]
    step = pl.program_id(0)

    @pl.when(step == 0)
    def _():
        count_ref[...] = jnp.zeros((N_EXPERTS, 1), F32)
        hprev_ref[...] = jnp.zeros(hprev_ref.shape, F32)

    h = hprev_ref[...]

    a = jnp.dot(attn_ref[...], wao_ref[...], preferred_element_type=F32)
    merged = ga_ref[...].astype(F32) * a + cg_ref[...].astype(F32)
    y = jnp.dot(merged.astype(BF16), wout_ref[...], preferred_element_type=F32)
    h_new = _layer_norm(DEEPNORM_ALPHA * x_ref[...] + y, g1_ref[...], b1_ref[...])
    h_ref[...] = h_new
    hp_ref[...] = _pack_bf16_halves(h_new)
    hprev_ref[...] = h_new

    nt_dims = (((1,), (1,)), ((), ()))
    h_hi = h.astype(BF16)
    h_lo = (h - h_hi.astype(F32)).astype(BF16)
    w = wr_ref[...]
    w_hi = w.astype(BF16)
    w_lo = (w - w_hi.astype(F32)).astype(BF16)
    logits = (lax.dot_general(w_hi, h_hi, nt_dims, preferred_element_type=F32)
              + lax.dot_general(w_hi, h_lo, nt_dims, preferred_element_type=F32)
              + lax.dot_general(w_lo, h_hi, nt_dims, preferred_element_type=F32))
    scores = _sigmoid(logits)
    biased = scores + rb_ref[...]

    g3 = biased.reshape(N_GROUPS, GROUP_SIZE, tm)
    m1 = jnp.max(g3, axis=1, keepdims=True)
    is_max = g3 == m1
    n_max = jnp.sum(jnp.where(is_max, 1.0, 0.0), axis=1, keepdims=True)
    m2 = jnp.max(jnp.where(is_max, NEG_INF, g3), axis=1, keepdims=True)
    grp = (m1 + jnp.where(n_max >= 2.0, m1, m2)).reshape(N_GROUPS, tm)

    gid = lax.broadcasted_iota(jnp.int32, (N_GROUPS, tm), 0)
    rank = jnp.zeros((N_GROUPS, tm), F32)
    for o in range(N_GROUPS):
        other = grp[o:o + 1, :]
        ahead = (other > grp) | ((other == grp) & (o < gid))
        rank = rank + jnp.where(ahead, 1.0, 0.0)
    grp_keep = jnp.where(rank < float(TOPK_GROUPS), 1.0, 0.0)
    keep = jnp.broadcast_to(grp_keep.reshape(N_GROUPS, 1, tm),
                            (N_GROUPS, GROUP_SIZE, tm)).reshape(N_EXPERTS, tm)
    cand = jnp.where(keep > 0.0, biased, NEG_INF)

    eid = lax.broadcasted_iota(jnp.int32, (N_EXPERTS, tm), 0).astype(F32)
    chosen = jnp.zeros((N_EXPERTS, tm), F32)
    firsts = []
    for r in range(TOP_K):
        mx = jnp.max(cand, axis=0, keepdims=True)
        first = jnp.min(jnp.where(cand == mx, eid, float(N_EXPERTS)), axis=0, keepdims=True)
        pick = eid == first
        firsts.append(first)
        idx_ref[r:r + 1, :] = first.astype(jnp.int32)
        wgt_ref[r:r + 1, :] = jnp.sum(jnp.where(pick, scores, 0.0), axis=0, keepdims=True)
        cand = jnp.where(pick, NEG_INF, cand)
        chosen = jnp.where(pick, 1.0, chosen)
    top_s = wgt_ref[...]
    wgt_ref[...] = top_s / (jnp.sum(top_s, axis=0, keepdims=True) + 1e-20) * ROUTED_SCALE

    earlier = (lax.broadcasted_iota(jnp.int32, (tm, tm), 0)
               < lax.broadcasted_iota(jnp.int32, (tm, tm), 1))
    before = jnp.dot(chosen.astype(BF16), jnp.where(earlier, 1.0, 0.0).astype(BF16),
                     preferred_element_type=F32) + count_ref[...]
    for r in range(TOP_K):
        rank_ref[r:r + 1, :] = jnp.sum(jnp.where(eid == firsts[r], before, 0.0),
                                       axis=0, keepdims=True).astype(jnp.int32)
    counted = jnp.where(step > 0, 1.0, 0.0)
    total = count_ref[...] + counted * jnp.sum(chosen, axis=1, keepdims=True)
    count_ref[...] = total
    total_ref[...] = total


def _merge_ln_router(attn, gate_a, conv_gated, x2, w_attn_o_b, w_out_b, ln_g, ln_b, w_router_t, router_bias,
                     first_token, n):
    tm = MERGE_TILE
    first_tile = first_token // tm
    n_tiles = n // tm
    row_in = lambda i: (jnp.minimum(i, n_tiles - 1) + first_tile, 0)
    row = lambda i: (jnp.minimum(i, n_tiles - 1), 0)
    const = lambda i: (0, 0)
    col = lambda i: (0, jnp.maximum(i - 1, 0))
    return pl.pallas_call(
        _merge_kernel,
        grid=(n_tiles + 1,),
        in_specs=[
            pl.BlockSpec((tm, D_MODEL), row_in),
            pl.BlockSpec((tm, D_MODEL), row_in),
            pl.BlockSpec((tm, D_MODEL), row_in),
            pl.BlockSpec((tm, D_MODEL), row_in),
            pl.BlockSpec((D_MODEL, D_MODEL), const),
            pl.BlockSpec((D_MODEL, D_MODEL), const),
            pl.BlockSpec((1, D_MODEL), const),
            pl.BlockSpec((1, D_MODEL), const),
            pl.BlockSpec((N_EXPERTS, D_MODEL), const),
            pl.BlockSpec((N_EXPERTS, 1), const),
        ],
        out_specs=[
            pl.BlockSpec((tm, D_MODEL), row),
            pl.BlockSpec((tm, HALF), row),
            pl.BlockSpec((TOP_K, tm), col),
            pl.BlockSpec((TOP_K, tm), col),
            pl.BlockSpec((TOP_K, tm), col),
            pl.BlockSpec((N_EXPERTS, 1), const),
        ],
        out_shape=[
            jax.ShapeDtypeStruct((n, D_MODEL), F32),
            jax.ShapeDtypeStruct((n, HALF), jnp.uint32),
            jax.ShapeDtypeStruct((TOP_K, n), jnp.int32),
            jax.ShapeDtypeStruct((TOP_K, n), F32),
            jax.ShapeDtypeStruct((TOP_K, n), jnp.int32),
            jax.ShapeDtypeStruct((N_EXPERTS, 1), F32),
        ],
        scratch_shapes=[pltpu.VMEM((N_EXPERTS, 1), F32), pltpu.VMEM((tm, D_MODEL), F32)],
        compiler_params=_params("arbitrary"),
        name="merge_ln_router",
    )(attn, gate_a, conv_gated, x2, w_attn_o_b, w_out_b, ln_g, ln_b, w_router_t, router_bias)


def _sc_mesh():
    return plsc.VectorSubcoreMesh(core_axis_name="c", subcore_axis_name="s",
                                  num_cores=SC_CORES, num_subcores=SC_SUBCORES)


def _sc_worker_base(rows_per_worker):
    return (lax.axis_index("s") * SC_CORES + lax.axis_index("c")) * rows_per_worker


def _sc_gather_rows(table, idx):
    m = idx.shape[0]
    width = table.shape[1]
    per_worker = m // (SC_CORES * SC_SUBCORES)
    assert per_worker * SC_CORES * SC_SUBCORES == m and per_worker % SC_CHUNK == 0

    @functools.partial(
        pl.kernel, mesh=_sc_mesh(),
        out_type=jax.ShapeDtypeStruct((m, width), table.dtype),
        scratch_types=[pltpu.VMEM((SC_CHUNK,), jnp.int32),
                       pltpu.VMEM((SC_CHUNK, width), table.dtype),
                       pltpu.SemaphoreType.DMA],
        name="sc_gather_rows")
    def gather(table_hbm, idx_hbm, out_hbm, idx_v, rows_v, sem):
        base = _sc_worker_base(per_worker)

        @pl.loop(0, per_worker // SC_CHUNK)
        def _(c):
            off = pl.multiple_of(base + c * SC_CHUNK, SC_CHUNK)
            pltpu.sync_copy(idx_hbm.at[pl.ds(off, SC_CHUNK)], idx_v)
            pltpu.async_copy(table_hbm.at[idx_v], rows_v, sem).wait()
            pltpu.sync_copy(rows_v, out_hbm.at[pl.ds(off, SC_CHUNK)])

    return gather(table, idx)


def _sc_scatter_rows(rows, pos, n_out):
    n, width = rows.shape
    per_worker = n // (SC_CORES * SC_SUBCORES)
    assert per_worker * SC_CORES * SC_SUBCORES == n and per_worker % SC_CHUNK == 0

    @functools.partial(
        pl.kernel, mesh=_sc_mesh(),
        out_type=jax.ShapeDtypeStruct((n_out, width), rows.dtype),
        scratch_types=[pltpu.VMEM((SC_CHUNK,), jnp.int32),
                       pltpu.VMEM((SC_CHUNK, width), rows.dtype),
                       pltpu.SemaphoreType.DMA],
        name="sc_scatter_rows")
    def scatter(rows_hbm, pos_hbm, out_hbm, idx_v, rows_v, sem):
        base = _sc_worker_base(per_worker)

        @pl.loop(0, per_worker // SC_CHUNK)
        def _(c):
            off = pl.multiple_of(base + c * SC_CHUNK, SC_CHUNK)
            pltpu.sync_copy(rows_hbm.at[pl.ds(off, SC_CHUNK)], rows_v)
            for r in range(TOP_K):
                pltpu.sync_copy(pos_hbm.at[pl.ds(r * n + off, SC_CHUNK)], idx_v)
                pltpu.async_copy(rows_v, out_hbm.at[idx_v], sem).wait()

    return scatter(rows, pos)


def _position_kernel(idx_ref, rank_ref, start_ref, pos_ref):
    tl = idx_ref.shape[1]
    eid = lax.broadcasted_iota(jnp.int32, (N_EXPERTS, tl), 0)
    start = start_ref[...]
    for r in range(TOP_K):
        here = jnp.sum(jnp.where(eid == idx_ref[r:r + 1, :], start, 0.0), axis=0, keepdims=True)
        pos_ref[r:r + 1, :] = here.astype(jnp.int32) + rank_ref[r:r + 1, :]


def _positions(top_idx, rank, group_start):
    n = top_idx.shape[1]
    tl = POSITION_TILE
    col = lambda i: (0, i)
    return pl.pallas_call(
        _position_kernel,
        grid=(n // tl,),
        in_specs=[pl.BlockSpec((TOP_K, tl), col), pl.BlockSpec((TOP_K, tl), col),
                  pl.BlockSpec((N_EXPERTS, 1), lambda i: (0, 0))],
        out_specs=pl.BlockSpec((TOP_K, tl), col),
        out_shape=jax.ShapeDtypeStruct((TOP_K, n), jnp.int32),
        compiler_params=_params("parallel"),
        name="positions",
    )(top_idx, rank, group_start)


def _expert_kernel(te_ref, tv_ref, nu_ref, xs_ref, wg_ref, wu_ref, wd_ref, y_ref, act_ref):
    i = pl.program_id(0)
    valid = tv_ref[i]
    valid_prev = tv_ref[jnp.maximum(i - 1, 0)]

    @pl.when(i == 0)
    def _():
        act_ref[...] = jnp.zeros(act_ref.shape, BF16)

    @pl.when((valid > 0) | (valid_prev > 0))
    def _():
        act_prev = act_ref[...]
        wd = wd_ref[0]
        for c in range(EXPERT_TILE // EXPERT_CHAIN):
            rows = slice(c * EXPERT_CHAIN, (c + 1) * EXPERT_CHAIN)
            y_ref[rows, :] = _pack_bf16_halves(jnp.dot(act_prev[rows], wd, preferred_element_type=F32))
        wg = wg_ref[0]
        wu = wu_ref[0]
        for c in range(EXPERT_TILE // EXPERT_CHAIN):
            rows = slice(c * EXPERT_CHAIN, (c + 1) * EXPERT_CHAIN)
            live = lax.broadcasted_iota(jnp.int32, (EXPERT_CHAIN, HALF), 0) < valid - c * EXPERT_CHAIN
            lo, hi = _unpack_bf16_halves(jnp.where(live, xs_ref[rows, :], jnp.uint32(0)))
            x = jnp.concatenate([lo, hi], axis=1).astype(BF16)
            g = jnp.dot(x, wg, preferred_element_type=F32)
            u = jnp.dot(x, wu, preferred_element_type=F32)
            act_ref[rows, :] = (g * _sigmoid(g) * u).astype(BF16)


def _grouped_experts(tile_expert, tile_valid, xs, wg, wu, wd):
    p = xs.shape[0]
    t = EXPERT_TILE
    n_tiles = p // t
    n_used = jnp.sum((tile_valid > 0).astype(jnp.int32)).reshape(1)
    tile_valid = jnp.concatenate([tile_valid, jnp.zeros((1,), jnp.int32)])
    cur = lambda i, nu: jnp.minimum(i, nu[0] - 1)
    prev = lambda i, nu: jnp.minimum(jnp.maximum(i - 1, 0), nu[0] - 1)
    row_in = lambda i, te, tv, nu: (cur(i, nu), 0)
    row_out = lambda i, te, tv, nu: (prev(i, nu), 0)
    expert = lambda i, te, tv, nu: (te[cur(i, nu)], 0, 0)
    expert_prev = lambda i, te, tv, nu: (te[prev(i, nu)], 0, 0)
    return pl.pallas_call(
        _expert_kernel,
        grid_spec=pltpu.PrefetchScalarGridSpec(
            num_scalar_prefetch=3,
            grid=(n_tiles + 1,),
            in_specs=[
                pl.BlockSpec((t, HALF), row_in),
                pl.BlockSpec((1, D_MODEL, EXPERT_HIDDEN), expert),
                pl.BlockSpec((1, D_MODEL, EXPERT_HIDDEN), expert),
                pl.BlockSpec((1, EXPERT_HIDDEN, D_MODEL), expert_prev),
            ],
            out_specs=pl.BlockSpec((t, HALF), row_out),
            scratch_shapes=[pltpu.VMEM((t, EXPERT_HIDDEN), BF16)],
        ),
        out_shape=jax.ShapeDtypeStruct((p, HALF), jnp.uint32),
        compiler_params=_params("arbitrary"),
        name="grouped_experts",
    )(tile_expert, tile_valid, n_used, xs, wg, wu, wd)


def _combine_kernel(h_ref, yg_ref, wt_ref, wsg_ref, wsu_ref, wsd_ref, g2_ref, b2_ref, *out_refs):
    o_ref = out_refs[-1]
    h = h_ref[...]
    hb = h.astype(BF16)
    g = jnp.dot(hb, wsg_ref[...], preferred_element_type=F32)
    u = jnp.dot(hb, wsu_ref[...], preferred_element_type=F32)
    shared = jnp.dot((g * _sigmoid(g) * u).astype(BF16), wsd_ref[...], preferred_element_type=F32)
    wt = wt_ref[...]
    r_lo = jnp.zeros((h.shape[0], HALF), F32)
    r_hi = jnp.zeros((h.shape[0], HALF), F32)
    for r in range(TOP_K):
        lo, hi = _unpack_bf16_halves(yg_ref[r])
        w = wt[:, r:r + 1]
        r_lo = r_lo + lo * w
        r_hi = r_hi + hi * w
    routed = jnp.concatenate([r_lo, r_hi], axis=1)
    o_ref[...] = _layer_norm(DEEPNORM_ALPHA * h + (shared + routed), g2_ref[...], b2_ref[...])


def _combine_ln(h, yg, w_tok, wsg_b, wsu_b, wsd_b, ln_g, ln_b, out_so_far, first_token, n_total):
    n = h.shape[0]
    tm = MERGE_TILE
    first_tile = first_token // tm
    row = lambda i: (i, 0)
    const = lambda i: (0, 0)
    hidden = wsg_b.shape[1]
    in_specs = [
        pl.BlockSpec((tm, D_MODEL), row),
        pl.BlockSpec((TOP_K, tm, HALF), lambda i: (0, i, 0)),
        pl.BlockSpec((tm, TOP_K), row),
        pl.BlockSpec((D_MODEL, hidden), const),
        pl.BlockSpec((D_MODEL, hidden), const),
        pl.BlockSpec((hidden, D_MODEL), const),
        pl.BlockSpec((1, D_MODEL), const),
        pl.BlockSpec((1, D_MODEL), const),
    ]
    args = [h, yg, w_tok, wsg_b, wsu_b, wsd_b, ln_g, ln_b]
    aliases = {}
    if out_so_far is not None:
        in_specs.append(pl.BlockSpec(memory_space=pl.ANY))
        args.append(out_so_far)
        aliases = {len(args) - 1: 0}
    return pl.pallas_call(
        _combine_kernel,
        grid=(n // tm,),
        in_specs=in_specs,
        out_specs=pl.BlockSpec((tm, D_MODEL), lambda i: (i + first_tile, 0)),
        out_shape=jax.ShapeDtypeStruct((n_total, D_MODEL), F32),
        input_output_aliases=aliases,
        compiler_params=_params("parallel"),
        name="combine_ln",
    )(*args)


def _group_layout(totals, n_tokens):
    t = EXPERT_TILE
    n_tiles = (TOP_K * n_tokens + N_EXPERTS * (t - 1)) // t
    counts = totals[:, 0].astype(jnp.int32)
    padded = ((counts + t - 1) // t) * t
    group_end = jnp.cumsum(padded)
    group_start = group_end - padded
    tile_start = jnp.arange(n_tiles, dtype=jnp.int32) * t
    tile_expert = jnp.minimum(jnp.sum((group_end[None, :] <= tile_start[:, None]).astype(jnp.int32), axis=1),
                              N_EXPERTS - 1)
    of_tile = tile_expert[:, None] == jnp.arange(N_EXPERTS, dtype=jnp.int32)[None, :]
    real_end = jnp.sum(jnp.where(of_tile, (group_start + counts)[None, :], 0), axis=1)
    tile_valid = jnp.clip(real_end - tile_start, 0, t).astype(jnp.int32)
    return group_start.astype(F32).reshape(N_EXPERTS, 1), tile_expert, tile_valid, n_tiles * t


def _rope_tables(seq):
    inv_freq = ROPE_THETA ** (-jnp.arange(0, ROPE_DIM, 2, dtype=F32) / ROPE_DIM)
    ang = jnp.arange(seq).astype(F32)[:, None] * inv_freq[None, :]
    cos, sin = jnp.cos(ang), jnp.sin(ang)
    rest = HEAD_DIM - ROPE_DIM
    zeros = jnp.zeros((seq, ROPE_HALF), F32)
    cos_t = jnp.concatenate([cos, cos, jnp.ones((seq, rest), F32)], axis=1)
    sa_t = jnp.concatenate([-sin, zeros, jnp.zeros((seq, rest), F32)], axis=1)
    sb_t = jnp.concatenate([zeros, sin, jnp.zeros((seq, rest), F32)], axis=1)
    return cos_t, sa_t, sb_t


def _layer(x2, batch, seq, w_in, b_gate, w_attn_o, w_dw, b_dw, conv_ln_g, conv_ln_b, w_conv_o, w_out,
           ln1_g, ln1_b, w_router, router_bias, w_exp_gate, w_exp_up, w_exp_down,
           w_sh_gate, w_sh_up, w_sh_down, ln2_g, ln2_b):
    n = x2.shape[0]
    row = lambda v: v.reshape(1, -1)
    cos, sa, sb = _rope_tables(seq)
    q, k, v, u, gate_a, gate_c, k_mean, w_gate_b, w_up_b, w_down_b = _in_projection(
        x2, w_in.astype(BF16), row(b_gate), cos, sa, sb, w_exp_gate, w_exp_up, w_exp_down, seq)
    w_taps = jnp.broadcast_to(w_dw.reshape(CONV_WIDTH, 1, D_MODEL), (CONV_WIDTH, SUBLANES, D_MODEL))
    conv_gated = _conv_branch(u, w_taps, row(b_dw), row(conv_ln_g),
                              row(conv_ln_b), w_conv_o.astype(BF16), gate_c, batch, seq)
    attn = _moba_attention(q, k, v, k_mean, batch, seq)
    w_attn_o_b, w_out_b, w_router_t = w_attn_o.astype(BF16), w_out.astype(BF16), w_router.T
    shared_w = (w_sh_gate.astype(BF16), w_sh_up.astype(BF16), w_sh_down.astype(BF16))
    n_chunk = n // MOE_CHUNKS
    assert n_chunk * MOE_CHUNKS == n and n_chunk % TOKEN_TILE == 0 and n_chunk % POSITION_TILE == 0
    out = None
    for c in range(MOE_CHUNKS):
        first = c * n_chunk
        h, h_packed, top_idx, top_w, rank, totals = _merge_ln_router(
            attn, gate_a, conv_gated, x2, w_attn_o_b, w_out_b, row(ln1_g), row(ln1_b),
            w_router_t, router_bias.reshape(N_EXPERTS, 1), first, n_chunk)
        group_start, tile_expert, tile_valid, n_rows = _group_layout(totals, n_chunk)
        pos = _positions(top_idx, rank, group_start).reshape(TOP_K * n_chunk)
        xs = _sc_scatter_rows(h_packed, pos, n_rows)
        ys = _grouped_experts(tile_expert, tile_valid, xs, w_gate_b, w_up_b, w_down_b)
        yg = _sc_gather_rows(ys, pos).reshape(TOP_K, n_chunk, HALF)
        out = _combine_ln(h, yg, top_w.T, *shared_w, row(ln2_g), row(ln2_b), out, first, n)
    return out


def kernel(x, w_in, b_gate, w_attn_o, w_dw, b_dw, conv_ln_g, conv_ln_b, w_conv_o, w_out, ln1_g, ln1_b,
           w_router, router_bias, w_exp_gate, w_exp_up, w_exp_down, w_sh_gate, w_sh_up, w_sh_down,
           ln2_g, ln2_b):
    batch, seq, d = x.shape
    assert d == D_MODEL and seq % MOBA_BLOCK == 0 and seq % TOKEN_TILE == 0
    assert w_in.shape[0] == DEPTH
    x2 = x.reshape(batch * seq, d)
    for l in range(DEPTH):
        x2 = _layer(x2, batch, seq, w_in[l], b_gate[l], w_attn_o[l], w_dw[l], b_dw[l], conv_ln_g[l],
                    conv_ln_b[l], w_conv_o[l], w_out[l], ln1_g[l], ln1_b[l], w_router[l], router_bias[l],
                    w_exp_gate[l], w_exp_up[l], w_exp_down[l], w_sh_gate[l], w_sh_up[l], w_sh_down[l],
                    ln2_g[l], ln2_b[l])
    return x2.reshape(batch, seq, d)
```

```python
import functools

import jax
import jax.numpy as jnp
from jax import lax
from jax.experimental import pallas as pl
from jax.experimental.pallas import tpu as pltpu
from jax.experimental.pallas import tpu_sc as plsc

F32 = jnp.float32
BF16 = jnp.bfloat16
NEG_INF = float("-inf")
MASKED = -1e30
LOG2_E = 1.4426950408889634

D_MODEL = 1024
N_HEADS = 8
HEAD_DIM = 128
ROPE_THETA = 500000.0
ROPE_DIM = HEAD_DIM // 4
ROPE_HALF = ROPE_DIM // 2
MOBA_BLOCK = 256
MOBA_TOPK = 3
CONV_WIDTH = 31
SUBLANES = 8
CONV_HALO = 32
CONV_ROWS = 32
N_EXPERTS = 256
TOP_K = 8
N_GROUPS = 8
GROUP_SIZE = N_EXPERTS // N_GROUPS
TOPK_GROUPS = 4
EXPERT_HIDDEN = 256
ROUTED_SCALE = 2.5
LN_EPS = 1e-5
DEPTH = 1
DEEPNORM_ALPHA = (2 * DEPTH) ** 0.25
HALF = D_MODEL // 2

TOKEN_TILE = 256
MERGE_TILE = 512
EXPERT_TILE = 512
EXPERT_CHAIN = 256
POSITION_TILE = 1024
FIRST_CHUNK_SHARE = 9 / 16
HEADS_PER_STEP = 2
SC_CORES = 2
SC_SUBCORES = 16
SC_CHUNK = 64
VMEM_LIMIT = 56 * 1024 * 1024


def _params(*semantics):
    return pltpu.CompilerParams(dimension_semantics=semantics, vmem_limit_bytes=VMEM_LIMIT)


def _sigmoid(x):
    return 1.0 / (1.0 + jnp.exp(-x))


def _layer_norm(x, g, b):
    mu = jnp.mean(x, axis=-1, keepdims=True)
    xc = x - mu
    var = jnp.mean(xc * xc, axis=-1, keepdims=True)
    return xc * lax.rsqrt(var + LN_EPS) * g + b


def _pack_bf16_halves(y):
    lo = lax.bitcast_convert_type(y[:, :HALF].astype(BF16).astype(F32), jnp.uint32)
    hi = lax.bitcast_convert_type(y[:, HALF:].astype(BF16).astype(F32), jnp.uint32)
    return (hi & jnp.uint32(0xFFFF0000)) | (lo >> 16)


def _unpack_bf16_halves(p):
    lo = lax.bitcast_convert_type(p << 16, F32)
    hi = lax.bitcast_convert_type(p & jnp.uint32(0xFFFF0000), F32)
    return lo, hi


def _inproj_kernel(x_ref, w_ref, bg_ref, cos_ref, sa_ref, sb_ref, wg_ref, wu_ref, wd_ref,
                   q_ref, k_ref, v_ref, u_ref, ga_ref, gc_ref, km_ref, wgb_ref, wub_ref, wdb_ref):
    wgb_ref[...] = wg_ref[...].astype(BF16)
    wub_ref[...] = wu_ref[...].astype(BF16)
    wdb_ref[...] = wd_ref[...].astype(BF16)

    tm = x_ref.shape[0]
    xb = x_ref[...].astype(BF16)

    def proj(c):
        return jnp.dot(xb, w_ref[:, c * D_MODEL:(c + 1) * D_MODEL], preferred_element_type=F32)

    cos = cos_ref[...]
    sa = sa_ref[...]
    sb = sb_ref[...]

    def rope_head(t):
        return (t * cos + pltpu.roll(t, HEAD_DIM - ROPE_HALF, 1) * sa
                + pltpu.roll(t, ROPE_HALF, 1) * sb)

    q = proj(0)
    for h in range(N_HEADS):
        sl = slice(h * HEAD_DIM, (h + 1) * HEAD_DIM)
        q_ref[:, sl] = rope_head(q[:, sl]).astype(BF16)
    k = proj(1)
    for h in range(N_HEADS):
        sl = slice(h * HEAD_DIM, (h + 1) * HEAD_DIM)
        kr = rope_head(k[:, sl])
        k_ref[:, sl] = kr.astype(BF16)
        for g in range(tm // MOBA_BLOCK):
            km_ref[g, :, sl] = jnp.mean(kr[g * MOBA_BLOCK:(g + 1) * MOBA_BLOCK], axis=0, keepdims=True)
    v_ref[...] = proj(2).astype(BF16)
    u_ref[...] = proj(3) * _sigmoid(proj(4))
    ga_ref[...] = _sigmoid(proj(5) + bg_ref[:, :D_MODEL]).astype(BF16)
    gc_ref[...] = _sigmoid(proj(6) + bg_ref[:, D_MODEL:]).astype(BF16)


def _in_projection(x2, w_in_b, b_gate, cos, sa, sb, w_exp_gate, w_exp_up, w_exp_down, seq):
    n = x2.shape[0]
    tm = TOKEN_TILE
    n_cols = w_in_b.shape[1]
    tiles_per_seq = seq // tm
    steps = n // tm
    per_step = -(-N_EXPERTS // steps)
    assert N_EXPERTS % per_step == 0
    row = lambda i: (i, 0)
    const = lambda i: (0, 0)
    pos = lambda i: (i % tiles_per_seq, 0)
    experts = lambda i: (jnp.minimum(i, N_EXPERTS // per_step - 1), 0, 0)
    tok_bf16 = jax.ShapeDtypeStruct((n, D_MODEL), BF16)
    up_block = (per_step, D_MODEL, EXPERT_HIDDEN)
    down_block = (per_step, EXPERT_HIDDEN, D_MODEL)
    return pl.pallas_call(
        _inproj_kernel,
        grid=(steps,),
        in_specs=[
            pl.BlockSpec((tm, D_MODEL), row),
            pl.BlockSpec((D_MODEL, n_cols), const),
            pl.BlockSpec((1, 2 * D_MODEL), const),
            pl.BlockSpec((tm, HEAD_DIM), pos),
            pl.BlockSpec((tm, HEAD_DIM), pos),
            pl.BlockSpec((tm, HEAD_DIM), pos),
            pl.BlockSpec(up_block, experts),
            pl.BlockSpec(up_block, experts),
            pl.BlockSpec(down_block, experts),
        ],
        out_specs=[
            pl.BlockSpec((tm, D_MODEL), row),
            pl.BlockSpec((tm, D_MODEL), row),
            pl.BlockSpec((tm, D_MODEL), row),
            pl.BlockSpec((tm, D_MODEL), row),
            pl.BlockSpec((tm, D_MODEL), row),
            pl.BlockSpec((tm, D_MODEL), row),
            pl.BlockSpec((tm // MOBA_BLOCK, 1, D_MODEL), lambda i: (i, 0, 0)),
            pl.BlockSpec(up_block, experts),
            pl.BlockSpec(up_block, experts),
            pl.BlockSpec(down_block, experts),
        ],
        out_shape=[tok_bf16, tok_bf16, tok_bf16,
                   jax.ShapeDtypeStruct((n, D_MODEL), F32),
                   tok_bf16, tok_bf16,
                   jax.ShapeDtypeStruct((n // MOBA_BLOCK, 1, D_MODEL), F32),
                   jax.ShapeDtypeStruct(w_exp_gate.shape, BF16),
                   jax.ShapeDtypeStruct(w_exp_up.shape, BF16),
                   jax.ShapeDtypeStruct(w_exp_down.shape, BF16)],
        compiler_params=_params("arbitrary"),
        name="in_projection",
    )(x2, w_in_b, b_gate, cos, sa, sb, w_exp_gate, w_exp_up, w_exp_down)


def _conv_kernel(u_ref, wdw_ref, bdw_ref, lng_ref, lnb_ref, wo_ref, gc_ref, o_ref, buf_ref, sh_ref, y_ref):
    ts = u_ref.shape[0]
    s = pl.program_id(1)

    @pl.when(s == 0)
    def _():
        buf_ref[0:CONV_HALO, :] = jnp.zeros((CONV_HALO, D_MODEL), F32)

    @pl.when(s > 0)
    def _():
        buf_ref[0:CONV_HALO, :] = buf_ref[ts:ts + CONV_HALO, :]

    buf_ref[CONV_HALO:CONV_HALO + ts, :] = u_ref[...]

    span = ts + CONV_HALO - SUBLANES
    for b in range(1, SUBLANES):
        sh_ref[b - 1, 0:span, :] = buf_ref[b:b + span, :]

    base = CONV_HALO - (CONV_WIDTH - 1)
    for c in range(ts // CONV_ROWS):
        r0 = c * CONV_ROWS
        acc = jnp.zeros((CONV_ROWS // SUBLANES, SUBLANES, D_MODEL), F32)
        for j in range(CONV_WIDTH):
            shift = (base + j) % SUBLANES
            row = r0 + base + j - shift
            src = buf_ref if shift == 0 else sh_ref.at[shift - 1]
            tap = src[row:row + CONV_ROWS, :].reshape(CONV_ROWS // SUBLANES, SUBLANES, D_MODEL)
            acc = acc + tap * wdw_ref[j]
        acc = acc.reshape(CONV_ROWS, D_MODEL)
        y = _layer_norm(acc + bdw_ref[...], lng_ref[...], lnb_ref[...])
        y_ref[r0:r0 + CONV_ROWS, :] = (y * _sigmoid(y)).astype(BF16)
    z = jnp.dot(y_ref[...], wo_ref[...], preferred_element_type=F32)
    o_ref[...] = (z * gc_ref[...].astype(F32)).astype(BF16)


def _conv_branch(u, w_dw, b_dw, ln_g, ln_b, w_o_b, gate_c, batch, seq):
    n = u.shape[0]
    ts = TOKEN_TILE
    tiles_per_seq = seq // ts
    row = lambda b, s: (b * tiles_per_seq + s, 0)
    const = lambda b, s: (0, 0)
    return pl.pallas_call(
        _conv_kernel,
        grid=(batch, tiles_per_seq),
        in_specs=[
            pl.BlockSpec((ts, D_MODEL), row),
            pl.BlockSpec((CONV_WIDTH, SUBLANES, D_MODEL), lambda b, s: (0, 0, 0)),
            pl.BlockSpec((1, D_MODEL), const),
            pl.BlockSpec((1, D_MODEL), const),
            pl.BlockSpec((1, D_MODEL), const),
            pl.BlockSpec((D_MODEL, D_MODEL), const),
            pl.BlockSpec((ts, D_MODEL), row),
        ],
        out_specs=pl.BlockSpec((ts, D_MODEL), row),
        out_shape=jax.ShapeDtypeStruct((n, D_MODEL), BF16),
        scratch_shapes=[pltpu.VMEM((ts + CONV_HALO, D_MODEL), F32),
                        pltpu.VMEM((SUBLANES - 1, ts + CONV_HALO - SUBLANES, D_MODEL), F32),
                        pltpu.VMEM((ts, D_MODEL), BF16)],
        compiler_params=_params("parallel", "arbitrary"),
        name="conv_branch",
    )(u, w_dw, b_dw, ln_g, ln_b, w_o_b, gate_c)


def _attn_kernel(q_ref, k_ref, v_ref, km_ref, o_ref, *, n_blk):
    blk = MOBA_BLOCK
    seq = n_blk * blk
    k_sel = min(MOBA_TOPK, n_blk)
    exp2_scale = HEAD_DIM ** -0.5 * LOG2_E
    nt_dims = (((1,), (1,)), ((), ()))
    n_lane = HEAD_DIM

    assert blk & (blk - 1) == 0
    blk_shift = blk.bit_length() - 1
    n_sub = -(-n_blk // SUBLANES) * SUBLANES

    blk_t = lax.broadcasted_iota(jnp.int32, (n_sub, seq), 0)
    own_t = lax.broadcasted_iota(jnp.int32, (n_sub, seq), 1) >> blk_shift
    past = blk_t < own_t
    blk_f = blk_t.astype(F32)
    blk_id = lax.broadcasted_iota(jnp.int32, (seq, n_lane), 1)
    own_id = lax.broadcasted_iota(jnp.int32, (seq, n_lane), 0) >> blk_shift
    k_blk = jnp.where(blk_id == own_id, 1.0, 0.0).astype(BF16)
    ones = jnp.ones((seq, n_lane), BF16)
    causal = (lax.broadcasted_iota(jnp.int32, (blk, blk), 1)
              <= lax.broadcasted_iota(jnp.int32, (blk, blk), 0))

    def prepare(lanes):
        q_all = q_ref[:, lanes]
        km = km_ref[:, 0, lanes]
        if n_sub > n_blk:
            km = jnp.concatenate([km, jnp.zeros((n_sub - n_blk, HEAD_DIM), F32)], axis=0)
        km_hi = km.astype(BF16)
        km_lo = (km - km_hi.astype(F32)).astype(BF16)
        gate = (lax.dot_general(km_hi, q_all, nt_dims, preferred_element_type=F32)
                + lax.dot_general(km_lo, q_all, nt_dims, preferred_element_type=F32))
        g = jnp.where(past, gate, NEG_INF)
        sel = jnp.zeros((n_sub, seq), F32)
        for _ in range(k_sel):
            mx = jnp.max(g, axis=0, keepdims=True)
            first = jnp.min(jnp.where(g == mx, blk_f, float(n_sub)), axis=0, keepdims=True)
            pick = blk_f == first
            sel = jnp.where(pick, 1.0, sel)
            g = jnp.where(pick, NEG_INF, g)
        visible = ((sel > 0.0) & past) | (blk_t == own_t)
        bias_t = jnp.concatenate([jnp.where(visible, 0.0, MASKED), jnp.zeros((n_lane - n_sub, seq), F32)],
                                 axis=0)
        q_bias = bias_t.T.astype(BF16)
        k_aug = jnp.concatenate([k_ref[:, lanes], k_blk], axis=1)
        v_aug = jnp.concatenate([v_ref[:, lanes], ones], axis=1)
        return q_all, q_bias, k_aug, v_aug, lanes

    def scores(head, i):
        q_all, q_bias, k_aug, _, _ = head
        rows = slice(i * blk, (i + 1) * blk)
        q_aug = jnp.concatenate([q_all[rows], q_bias[rows]], axis=1)
        return lax.dot_general(q_aug, k_aug[:(i + 1) * blk], nt_dims, preferred_element_type=F32)

    def finish(head, i, p):
        pv = jnp.dot(p, head[3][:(i + 1) * blk], preferred_element_type=F32)
        o_ref[i * blk:(i + 1) * blk, head[4]] = (pv[:, :HEAD_DIM] / pv[:, HEAD_DIM:HEAD_DIM + 1]).astype(BF16)

    def softmax_numerator(raw, i):
        own = jnp.where(causal, raw[:, i * blk:], MASKED)
        parts = [raw[:, :i * blk], own] if i else [own]
        m = jnp.max(own, axis=1, keepdims=True)
        if i:
            m = jnp.maximum(m, jnp.max(parts[0], axis=1, keepdims=True))
        return jnp.concatenate([jnp.exp2((t - m) * exp2_scale) for t in parts], axis=1).astype(BF16)

    heads = [prepare(slice(hh * HEAD_DIM, (hh + 1) * HEAD_DIM)) for hh in range(HEADS_PER_STEP)]
    raw_next = [scores(head, 0) for head in heads]
    p_prev = None
    for i in range(n_blk):
        raws = raw_next
        if i + 1 < n_blk:
            raw_next = [scores(head, i + 1) for head in heads]
        if p_prev is not None:
            for head, p in zip(heads, p_prev):
                finish(head, i - 1, p)
        p_prev = [softmax_numerator(raw, i) for raw in raws]
    for head, p in zip(heads, p_prev):
        finish(head, n_blk - 1, p)


def _moba_attention(q, k, v, k_mean, batch, seq):
    n = q.shape[0]
    n_blk = seq // MOBA_BLOCK
    width = HEADS_PER_STEP * HEAD_DIM
    seq_head = lambda b, h: (b, h)
    return pl.pallas_call(
        functools.partial(_attn_kernel, n_blk=n_blk),
        grid=(batch, N_HEADS // HEADS_PER_STEP),
        in_specs=[
            pl.BlockSpec((seq, width), seq_head),
            pl.BlockSpec((seq, width), seq_head),
            pl.BlockSpec((seq, width), seq_head),
            pl.BlockSpec((n_blk, 1, width), lambda b, h: (b, 0, h)),
        ],
        out_specs=pl.BlockSpec((seq, width), seq_head),
        out_shape=jax.ShapeDtypeStruct((n, D_MODEL), BF16),
        compiler_params=_params("parallel", "parallel"),
        name="moba_attention",
    )(q, k, v, k_mean)


def _merge_kernel(attn_ref, ga_ref, cg_ref, x_ref, wao_ref, wout_ref, g1_ref, b1_ref,
                  wr_ref, rb_ref, h_ref, hp_ref, idx_ref, wgt_ref, rank_ref, total_ref, count_ref, hprev_ref):
    tm = x_ref.shape[0]
    step = pl.program_id(0)

    @pl.when(step == 0)
    def _():
        count_ref[...] = jnp.zeros((N_EXPERTS, 1), F32)
        hprev_ref[...] = jnp.zeros(hprev_ref.shape, F32)

    h = hprev_ref[...]

    a = jnp.dot(attn_ref[...], wao_ref[...], preferred_element_type=F32)
    merged = ga_ref[...].astype(F32) * a + cg_ref[...].astype(F32)
    y = jnp.dot(merged.astype(BF16), wout_ref[...], preferred_element_type=F32)
    h_new = _layer_norm(DEEPNORM_ALPHA * x_ref[...] + y, g1_ref[...], b1_ref[...])
    h_ref[...] = h_new
    hp_ref[...] = _pack_bf16_halves(h_new)
    hprev_ref[...] = h_new

    nt_dims = (((1,), (1,)), ((), ()))
    h_hi = h.astype(BF16)
    h_lo = (h - h_hi.astype(F32)).astype(BF16)
    w = wr_ref[...]
    w_hi = w.astype(BF16)
    w_lo = (w - w_hi.astype(F32)).astype(BF16)
    logits = (lax.dot_general(w_hi, h_hi, nt_dims, preferred_element_type=F32)
              + lax.dot_general(w_hi, h_lo, nt_dims, preferred_element_type=F32)
              + lax.dot_general(w_lo, h_hi, nt_dims, preferred_element_type=F32))
    scores = _sigmoid(logits)
    biased = scores + rb_ref[...]

    g3 = biased.reshape(N_GROUPS, GROUP_SIZE, tm)
    m1 = jnp.max(g3, axis=1, keepdims=True)
    is_max = g3 == m1
    n_max = jnp.sum(jnp.where(is_max, 1.0, 0.0), axis=1, keepdims=True)
    m2 = jnp.max(jnp.where(is_max, NEG_INF, g3), axis=1, keepdims=True)
    grp = (m1 + jnp.where(n_max >= 2.0, m1, m2)).reshape(N_GROUPS, tm)

    gid = lax.broadcasted_iota(jnp.int32, (N_GROUPS, tm), 0)
    rank = jnp.zeros((N_GROUPS, tm), F32)
    for o in range(N_GROUPS):
        other = grp[o:o + 1, :]
        ahead = (other > grp) | ((other == grp) & (o < gid))
        rank = rank + jnp.where(ahead, 1.0, 0.0)
    grp_keep = jnp.where(rank < float(TOPK_GROUPS), 1.0, 0.0)
    keep = jnp.broadcast_to(grp_keep.reshape(N_GROUPS, 1, tm),
                            (N_GROUPS, GROUP_SIZE, tm)).reshape(N_EXPERTS, tm)
    cand = jnp.where(keep > 0.0, biased, NEG_INF)

    eid = lax.broadcasted_iota(jnp.int32, (N_EXPERTS, tm), 0).astype(F32)
    chosen = jnp.zeros((N_EXPERTS, tm), F32)
    firsts = []
    for r in range(TOP_K):
        mx = jnp.max(cand, axis=0, keepdims=True)
        first = jnp.min(jnp.where(cand == mx, eid, float(N_EXPERTS)), axis=0, keepdims=True)
        pick = eid == first
        firsts.append(first)
        idx_ref[r:r + 1, :] = first.astype(jnp.int32)
        wgt_ref[r:r + 1, :] = jnp.sum(jnp.where(pick, scores, 0.0), axis=0, keepdims=True)
        cand = jnp.where(pick, NEG_INF, cand)
        chosen = jnp.where(pick, 1.0, chosen)
    top_s = wgt_ref[...]
    wgt_ref[...] = top_s / (jnp.sum(top_s, axis=0, keepdims=True) + 1e-20) * ROUTED_SCALE

    earlier = (lax.broadcasted_iota(jnp.int32, (tm, tm), 0)
               < lax.broadcasted_iota(jnp.int32, (tm, tm), 1))
    before = jnp.dot(chosen.astype(BF16), jnp.where(earlier, 1.0, 0.0).astype(BF16),
                     preferred_element_type=F32) + count_ref[...]
    for r in range(TOP_K):
        rank_ref[r:r + 1, :] = jnp.sum(jnp.where(eid == firsts[r], before, 0.0),
                                       axis=0, keepdims=True).astype(jnp.int32)
    counted = jnp.where(step > 0, 1.0, 0.0)
    total = count_ref[...] + counted * jnp.sum(chosen, axis=1, keepdims=True)
    count_ref[...] = total
    total_ref[...] = total


def _merge_ln_router(attn, gate_a, conv_gated, x2, w_attn_o_b, w_out_b, ln_g, ln_b, w_router_t, router_bias,
                     first_token, n):
    tm = MERGE_TILE
    first_tile = first_token // tm
    n_tiles = n // tm
    row_in = lambda i: (jnp.minimum(i, n_tiles - 1) + first_tile, 0)
    row = lambda i: (jnp.minimum(i, n_tiles - 1), 0)
    const = lambda i: (0, 0)
    col = lambda i: (0, jnp.maximum(i - 1, 0))
    return pl.pallas_call(
        _merge_kernel,
        grid=(n_tiles + 1,),
        in_specs=[
            pl.BlockSpec((tm, D_MODEL), row_in),
            pl.BlockSpec((tm, D_MODEL), row_in),
            pl.BlockSpec((tm, D_MODEL), row_in),
            pl.BlockSpec((tm, D_MODEL), row_in),
            pl.BlockSpec((D_MODEL, D_MODEL), const),
            pl.BlockSpec((D_MODEL, D_MODEL), const),
            pl.BlockSpec((1, D_MODEL), const),
            pl.BlockSpec((1, D_MODEL), const),
            pl.BlockSpec((N_EXPERTS, D_MODEL), const),
            pl.BlockSpec((N_EXPERTS, 1), const),
        ],
        out_specs=[
            pl.BlockSpec((tm, D_MODEL), row),
            pl.BlockSpec((tm, HALF), row),
            pl.BlockSpec((TOP_K, tm), col),
            pl.BlockSpec((TOP_K, tm), col),
            pl.BlockSpec((TOP_K, tm), col),
            pl.BlockSpec((N_EXPERTS, 1), const),
        ],
        out_shape=[
            jax.ShapeDtypeStruct((n, D_MODEL), F32),
            jax.ShapeDtypeStruct((n, HALF), jnp.uint32),
            jax.ShapeDtypeStruct((TOP_K, n), jnp.int32),
            jax.ShapeDtypeStruct((TOP_K, n), F32),
            jax.ShapeDtypeStruct((TOP_K, n), jnp.int32),
            jax.ShapeDtypeStruct((N_EXPERTS, 1), F32),
        ],
        scratch_shapes=[pltpu.VMEM((N_EXPERTS, 1), F32), pltpu.VMEM((tm, D_MODEL), F32)],
        compiler_params=_params("arbitrary"),
        name="merge_ln_router",
    )(attn, gate_a, conv_gated, x2, w_attn_o_b, w_out_b, ln_g, ln_b, w_router_t, router_bias)


def _sc_mesh():
    return plsc.VectorSubcoreMesh(core_axis_name="c", subcore_axis_name="s",
                                  num_cores=SC_CORES, num_subcores=SC_SUBCORES)


def _sc_worker_base(rows_per_worker):
    return (lax.axis_index("s") * SC_CORES + lax.axis_index("c")) * rows_per_worker


def _sc_gather_rows(table, idx):
    m = idx.shape[0]
    width = table.shape[1]
    per_worker = m // (SC_CORES * SC_SUBCORES)
    assert per_worker * SC_CORES * SC_SUBCORES == m and per_worker % SC_CHUNK == 0

    @functools.partial(
        pl.kernel, mesh=_sc_mesh(),
        out_type=jax.ShapeDtypeStruct((m, width), table.dtype),
        scratch_types=[pltpu.VMEM((SC_CHUNK,), jnp.int32),
                       pltpu.VMEM((SC_CHUNK, width), table.dtype),
                       pltpu.SemaphoreType.DMA],
        name="sc_gather_rows")
    def gather(table_hbm, idx_hbm, out_hbm, idx_v, rows_v, sem):
        base = _sc_worker_base(per_worker)

        @pl.loop(0, per_worker // SC_CHUNK)
        def _(c):
            off = pl.multiple_of(base + c * SC_CHUNK, SC_CHUNK)
            pltpu.sync_copy(idx_hbm.at[pl.ds(off, SC_CHUNK)], idx_v)
            pltpu.async_copy(table_hbm.at[idx_v], rows_v, sem).wait()
            pltpu.sync_copy(rows_v, out_hbm.at[pl.ds(off, SC_CHUNK)])

    return gather(table, idx)


def _sc_scatter_rows(rows, pos, n_out):
    n, width = rows.shape
    per_worker = n // (SC_CORES * SC_SUBCORES)
    assert per_worker * SC_CORES * SC_SUBCORES == n and per_worker % SC_CHUNK == 0

    @functools.partial(
        pl.kernel, mesh=_sc_mesh(),
        out_type=jax.ShapeDtypeStruct((n_out, width), rows.dtype),
        scratch_types=[pltpu.VMEM((SC_CHUNK,), jnp.int32),
                       pltpu.VMEM((SC_CHUNK, width), rows.dtype),
                       pltpu.SemaphoreType.DMA],
        name="sc_scatter_rows")
    def scatter(rows_hbm, pos_hbm, out_hbm, idx_v, rows_v, sem):
        base = _sc_worker_base(per_worker)

        @pl.loop(0, per_worker // SC_CHUNK)
        def _(c):
            off = pl.multiple_of(base + c * SC_CHUNK, SC_CHUNK)
            pltpu.sync_copy(rows_hbm.at[pl.ds(off, SC_CHUNK)], rows_v)
            for r in range(TOP_K):
                pltpu.sync_copy(pos_hbm.at[pl.ds(r * n + off, SC_CHUNK)], idx_v)
                pltpu.async_copy(rows_v, out_hbm.at[idx_v], sem).wait()

    return scatter(rows, pos)


def _position_kernel(idx_ref, rank_ref, start_ref, pos_ref):
    tl = idx_ref.shape[1]
    eid = lax.broadcasted_iota(jnp.int32, (N_EXPERTS, tl), 0)
    start = start_ref[...]
    for r in range(TOP_K):
        here = jnp.sum(jnp.where(eid == idx_ref[r:r + 1, :], start, 0.0), axis=0, keepdims=True)
        pos_ref[r:r + 1, :] = here.astype(jnp.int32) + rank_ref[r:r + 1, :]


def _positions(top_idx, rank, group_start):
    n = top_idx.shape[1]
    tl = POSITION_TILE
    col = lambda i: (0, i)
    return pl.pallas_call(
        _position_kernel,
        grid=(n // tl,),
        in_specs=[pl.BlockSpec((TOP_K, tl), col), pl.BlockSpec((TOP_K, tl), col),
                  pl.BlockSpec((N_EXPERTS, 1), lambda i: (0, 0))],
        out_specs=pl.BlockSpec((TOP_K, tl), col),
        out_shape=jax.ShapeDtypeStruct((TOP_K, n), jnp.int32),
        compiler_params=_params("parallel"),
        name="positions",
    )(top_idx, rank, group_start)


def _expert_kernel(te_ref, tv_ref, nu_ref, xs_ref, wg_ref, wu_ref, wd_ref, y_ref, act_ref):
    i = pl.program_id(0)
    valid = tv_ref[i]
    valid_prev = tv_ref[jnp.maximum(i - 1, 0)]

    @pl.when(i == 0)
    def _():
        act_ref[...] = jnp.zeros(act_ref.shape, BF16)

    @pl.when((valid > 0) | (valid_prev > 0))
    def _():
        act_prev = act_ref[...]
        wd = wd_ref[0]
        for c in range(EXPERT_TILE // EXPERT_CHAIN):
            rows = slice(c * EXPERT_CHAIN, (c + 1) * EXPERT_CHAIN)
            y_ref[rows, :] = _pack_bf16_halves(jnp.dot(act_prev[rows], wd, preferred_element_type=F32))
        wg = wg_ref[0]
        wu = wu_ref[0]
        for c in range(EXPERT_TILE // EXPERT_CHAIN):
            rows = slice(c * EXPERT_CHAIN, (c + 1) * EXPERT_CHAIN)
            live = lax.broadcasted_iota(jnp.int32, (EXPERT_CHAIN, HALF), 0) < valid - c * EXPERT_CHAIN
            lo, hi = _unpack_bf16_halves(jnp.where(live, xs_ref[rows, :], jnp.uint32(0)))
            x = jnp.concatenate([lo, hi], axis=1).astype(BF16)
            g = jnp.dot(x, wg, preferred_element_type=F32)
            u = jnp.dot(x, wu, preferred_element_type=F32)
            act_ref[rows, :] = (g * _sigmoid(g) * u).astype(BF16)


def _grouped_experts(tile_expert, tile_valid, xs, wg, wu, wd):
    p = xs.shape[0]
    t = EXPERT_TILE
    n_tiles = p // t
    n_used = jnp.sum((tile_valid > 0).astype(jnp.int32)).reshape(1)
    tile_valid = jnp.concatenate([tile_valid, jnp.zeros((1,), jnp.int32)])
    cur = lambda i, nu: jnp.minimum(i, nu[0] - 1)
    prev = lambda i, nu: jnp.minimum(jnp.maximum(i - 1, 0), nu[0] - 1)
    row_in = lambda i, te, tv, nu: (cur(i, nu), 0)
    row_out = lambda i, te, tv, nu: (prev(i, nu), 0)
    expert = lambda i, te, tv, nu: (te[cur(i, nu)], 0, 0)
    expert_prev = lambda i, te, tv, nu: (te[prev(i, nu)], 0, 0)
    return pl.pallas_call(
        _expert_kernel,
        grid_spec=pltpu.PrefetchScalarGridSpec(
            num_scalar_prefetch=3,
            grid=(n_tiles + 1,),
            in_specs=[
                pl.BlockSpec((t, HALF), row_in),
                pl.BlockSpec((1, D_MODEL, EXPERT_HIDDEN), expert),
                pl.BlockSpec((1, D_MODEL, EXPERT_HIDDEN), expert),
                pl.BlockSpec((1, EXPERT_HIDDEN, D_MODEL), expert_prev),
            ],
            out_specs=pl.BlockSpec((t, HALF), row_out),
            scratch_shapes=[pltpu.VMEM((t, EXPERT_HIDDEN), BF16)],
        ),
        out_shape=jax.ShapeDtypeStruct((p, HALF), jnp.uint32),
        compiler_params=_params("arbitrary"),
        name="grouped_experts",
    )(tile_expert, tile_valid, n_used, xs, wg, wu, wd)


def _combine_kernel(h_ref, yg_ref, wt_ref, wsg_ref, wsu_ref, wsd_ref, g2_ref, b2_ref, *out_refs):
    o_ref = out_refs[-1]
    h = h_ref[...]
    hb = h.astype(BF16)
    g = jnp.dot(hb, wsg_ref[...], preferred_element_type=F32)
    u = jnp.dot(hb, wsu_ref[...], preferred_element_type=F32)
    shared = jnp.dot((g * _sigmoid(g) * u).astype(BF16), wsd_ref[...], preferred_element_type=F32)
    wt = wt_ref[...]
    r_lo = jnp.zeros((h.shape[0], HALF), F32)
    r_hi = jnp.zeros((h.shape[0], HALF), F32)
    for r in range(TOP_K):
        lo, hi = _unpack_bf16_halves(yg_ref[r])
        w = wt[:, r:r + 1]
        r_lo = r_lo + lo * w
        r_hi = r_hi + hi * w
    routed = jnp.concatenate([r_lo, r_hi], axis=1)
    o_ref[...] = _layer_norm(DEEPNORM_ALPHA * h + (shared + routed), g2_ref[...], b2_ref[...])


def _combine_ln(h, yg, w_tok, wsg_b, wsu_b, wsd_b, ln_g, ln_b, out_so_far, first_token, n_total):
    n = h.shape[0]
    tm = MERGE_TILE
    first_tile = first_token // tm
    row = lambda i: (i, 0)
    const = lambda i: (0, 0)
    hidden = wsg_b.shape[1]
    in_specs = [
        pl.BlockSpec((tm, D_MODEL), row),
        pl.BlockSpec((TOP_K, tm, HALF), lambda i: (0, i, 0)),
        pl.BlockSpec((tm, TOP_K), row),
        pl.BlockSpec((D_MODEL, hidden), const),
        pl.BlockSpec((D_MODEL, hidden), const),
        pl.BlockSpec((hidden, D_MODEL), const),
        pl.BlockSpec((1, D_MODEL), const),
        pl.BlockSpec((1, D_MODEL), const),
    ]
    args = [h, yg, w_tok, wsg_b, wsu_b, wsd_b, ln_g, ln_b]
    aliases = {}
    if out_so_far is not None:
        in_specs.append(pl.BlockSpec(memory_space=pl.ANY))
        args.append(out_so_far)
        aliases = {len(args) - 1: 0}
    return pl.pallas_call(
        _combine_kernel,
        grid=(n // tm,),
        in_specs=in_specs,
        out_specs=pl.BlockSpec((tm, D_MODEL), lambda i: (i + first_tile, 0)),
        out_shape=jax.ShapeDtypeStruct((n_total, D_MODEL), F32),
        input_output_aliases=aliases,
        compiler_params=_params("parallel"),
        name="combine_ln",
    )(*args)


def _group_layout(totals, n_tokens):
    t = EXPERT_TILE
    n_tiles = (TOP_K * n_tokens + N_EXPERTS * (t - 1)) // t
    counts = totals[:, 0].astype(jnp.int32)
    padded = ((counts + t - 1) // t) * t
    group_end = jnp.cumsum(padded)
    group_start = group_end - padded
    tile_start = jnp.arange(n_tiles, dtype=jnp.int32) * t
    tile_expert = jnp.minimum(jnp.sum((group_end[None, :] <= tile_start[:, None]).astype(jnp.int32), axis=1),
                              N_EXPERTS - 1)
    of_tile = tile_expert[:, None] == jnp.arange(N_EXPERTS, dtype=jnp.int32)[None, :]
    real_end = jnp.sum(jnp.where(of_tile, (group_start + counts)[None, :], 0), axis=1)
    tile_valid = jnp.clip(real_end - tile_start, 0, t).astype(jnp.int32)
    return group_start.astype(F32).reshape(N_EXPERTS, 1), tile_expert, tile_valid, n_tiles * t


def _rope_tables(seq):
    inv_freq = ROPE_THETA ** (-jnp.arange(0, ROPE_DIM, 2, dtype=F32) / ROPE_DIM)
    ang = jnp.arange(seq).astype(F32)[:, None] * inv_freq[None, :]
    cos, sin = jnp.cos(ang), jnp.sin(ang)
    rest = HEAD_DIM - ROPE_DIM
    zeros = jnp.zeros((seq, ROPE_HALF), F32)
    cos_t = jnp.concatenate([cos, cos, jnp.ones((seq, rest), F32)], axis=1)
    sa_t = jnp.concatenate([-sin, zeros, jnp.zeros((seq, rest), F32)], axis=1)
    sb_t = jnp.concatenate([zeros, sin, jnp.zeros((seq, rest), F32)], axis=1)
    return cos_t, sa_t, sb_t


def _layer(x2, batch, seq, w_in, b_gate, w_attn_o, w_dw, b_dw, conv_ln_g, conv_ln_b, w_conv_o, w_out,
           ln1_g, ln1_b, w_router, router_bias, w_exp_gate, w_exp_up, w_exp_down,
           w_sh_gate, w_sh_up, w_sh_down, ln2_g, ln2_b):
    n = x2.shape[0]
    row = lambda v: v.reshape(1, -1)
    cos, sa, sb = _rope_tables(seq)
    q, k, v, u, gate_a, gate_c, k_mean, w_gate_b, w_up_b, w_down_b = _in_projection(
        x2, w_in.astype(BF16), row(b_gate), cos, sa, sb, w_exp_gate, w_exp_up, w_exp_down, seq)
    w_taps = jnp.broadcast_to(w_dw.reshape(CONV_WIDTH, 1, D_MODEL), (CONV_WIDTH, SUBLANES, D_MODEL))
    conv_gated = _conv_branch(u, w_taps, row(b_dw), row(conv_ln_g),
                              row(conv_ln_b), w_conv_o.astype(BF16), gate_c, batch, seq)
    attn = _moba_attention(q, k, v, k_mean, batch, seq)
    w_attn_o_b, w_out_b, w_router_t = w_attn_o.astype(BF16), w_out.astype(BF16), w_router.T
    shared_w = (w_sh_gate.astype(BF16), w_sh_up.astype(BF16), w_sh_down.astype(BF16))
    unit = SC_CORES * SC_SUBCORES * SC_CHUNK
    assert n % unit == 0 and n >= 2 * unit and unit % MERGE_TILE == 0 and unit % POSITION_TILE == 0
    first_size = min(max(unit, round(n * FIRST_CHUNK_SHARE / unit) * unit), n - unit)
    out = None
    for first, n_chunk in ((0, first_size), (first_size, n - first_size)):
        h, h_packed, top_idx, top_w, rank, totals = _merge_ln_router(
            attn, gate_a, conv_gated, x2, w_attn_o_b, w_out_b, row(ln1_g), row(ln1_b),
            w_router_t, router_bias.reshape(N_EXPERTS, 1), first, n_chunk)
        group_start, tile_expert, tile_valid, n_rows = _group_layout(totals, n_chunk)
        pos = _positions(top_idx, rank, group_start).reshape(TOP_K * n_chunk)
        xs = _sc_scatter_rows(h_packed, pos, n_rows)
        ys = _grouped_experts(tile_expert, tile_valid, xs, w_gate_b, w_up_b, w_down_b)
        yg = _sc_gather_rows(ys, pos).reshape(TOP_K, n_chunk, HALF)
        out = _combine_ln(h, yg, top_w.T, *shared_w, row(ln2_g), row(ln2_b), out, first, n)
    return out


def kernel(x, w_in, b_gate, w_attn_o, w_dw, b_dw, conv_ln_g, conv_ln_b, w_conv_o, w_out, ln1_g, ln1_b,
           w_router, router_bias, w_exp_gate, w_exp_up, w_exp_down, w_sh_gate, w_sh_up, w_sh_down,
           ln2_g, ln2_b):
    batch, seq, d = x.shape
    assert d == D_MODEL and seq % MOBA_BLOCK == 0 and seq % TOKEN_TILE == 0
    assert w_in.shape[0] == DEPTH
    x2 = x.reshape(batch * seq, d)
    for l in range(DEPTH):
        x2 = _layer(x2, batch, seq, w_in[l], b_gate[l], w_attn_o[l], w_dw[l], b_dw[l], conv_ln_g[l],
                    conv_ln_b[l], w_conv_o[l], w_out[l], ln1_g[l], ln1_b[l], w_router[l], router_bias[l],
                    w_exp_gate[l], w_exp_up[l], w_exp_down[l], w_sh_gate[l], w_sh_up[l], w_sh_down[l],
                    ln2_g[l], ln2_b[l])
    return x2.reshape(batch, seq, d)
```

```python
import functools

import jax
import jax.numpy as jnp
from jax import lax
from jax.experimental import pallas as pl
from jax.experimental.pallas import tpu as pltpu
from jax.experimental.pallas import tpu_sc as plsc

F32 = jnp.float32
BF16 = jnp.bfloat16
NEG_INF = float("-inf")
MASKED = -1e30
LOG2_E = 1.4426950408889634

D_MODEL = 1024
N_HEADS = 8
HEAD_DIM = 128
ROPE_THETA = 500000.0
ROPE_DIM = HEAD_DIM // 4
ROPE_HALF = ROPE_DIM // 2
MOBA_BLOCK = 256
MOBA_TOPK = 3
CONV_WIDTH = 31
SUBLANES = 8
CONV_HALO = 32
CONV_ROWS = 32
N_EXPERTS = 256
TOP_K = 8
N_GROUPS = 8
GROUP_SIZE = N_EXPERTS // N_GROUPS
TOPK_GROUPS = 4
EXPERT_HIDDEN = 256
ROUTED_SCALE = 2.5
LN_EPS = 1e-5
DEPTH = 1
DEEPNORM_ALPHA = (2 * DEPTH) ** 0.25
HALF = D_MODEL // 2

TOKEN_TILE = 256
MERGE_TILE = 512
EXPERT_TILE = 512
EXPERT_CHAIN = 256
POSITION_TILE = 1024
FIRST_CHUNK_SHARE = 9 / 16
HEADS_PER_STEP = 4
SC_CORES = 2
SC_SUBCORES = 16
SC_CHUNK = 64
VMEM_LIMIT = 56 * 1024 * 1024


def _params(*semantics):
    return pltpu.CompilerParams(dimension_semantics=semantics, vmem_limit_bytes=VMEM_LIMIT)


def _sigmoid(x):
    return 1.0 / (1.0 + jnp.exp(-x))


def _layer_norm(x, g, b):
    mu = jnp.mean(x, axis=-1, keepdims=True)
    xc = x - mu
    var = jnp.mean(xc * xc, axis=-1, keepdims=True)
    return xc * lax.rsqrt(var + LN_EPS) * g + b


def _pack_bf16_halves(y):
    lo = lax.bitcast_convert_type(y[:, :HALF].astype(BF16).astype(F32), jnp.uint32)
    hi = lax.bitcast_convert_type(y[:, HALF:].astype(BF16).astype(F32), jnp.uint32)
    return (hi & jnp.uint32(0xFFFF0000)) | (lo >> 16)


def _unpack_bf16_halves(p):
    lo = lax.bitcast_convert_type(p << 16, F32)
    hi = lax.bitcast_convert_type(p & jnp.uint32(0xFFFF0000), F32)
    return lo, hi


def _inproj_kernel(x_ref, w_ref, bg_ref, cos_ref, sa_ref, sb_ref, wg_ref, wu_ref, wd_ref,
                   q_ref, k_ref, v_ref, u_ref, ga_ref, gc_ref, km_ref, wgb_ref, wub_ref, wdb_ref):
    wgb_ref[...] = wg_ref[...].astype(BF16)
    wub_ref[...] = wu_ref[...].astype(BF16)
    wdb_ref[...] = wd_ref[...].astype(BF16)

    tm = x_ref.shape[0]
    xb = x_ref[...].astype(BF16)

    def proj(c):
        return jnp.dot(xb, w_ref[:, c * D_MODEL:(c + 1) * D_MODEL], preferred_element_type=F32)

    cos = cos_ref[...]
    sa = sa_ref[...]
    sb = sb_ref[...]

    def rope_head(t):
        return (t * cos + pltpu.roll(t, HEAD_DIM - ROPE_HALF, 1) * sa
                + pltpu.roll(t, ROPE_HALF, 1) * sb)

    q = proj(0)
    for h in range(N_HEADS):
        sl = slice(h * HEAD_DIM, (h + 1) * HEAD_DIM)
        q_ref[:, sl] = rope_head(q[:, sl]).astype(BF16)
    k = proj(1)
    for h in range(N_HEADS):
        sl = slice(h * HEAD_DIM, (h + 1) * HEAD_DIM)
        kr = rope_head(k[:, sl])
        k_ref[:, sl] = kr.astype(BF16)
        for g in range(tm // MOBA_BLOCK):
            km_ref[g, :, sl] = jnp.mean(kr[g * MOBA_BLOCK:(g + 1) * MOBA_BLOCK], axis=0, keepdims=True)
    v_ref[...] = proj(2).astype(BF16)
    u_ref[...] = proj(3) * _sigmoid(proj(4))
    ga_ref[...] = _sigmoid(proj(5) + bg_ref[:, :D_MODEL]).astype(BF16)
    gc_ref[...] = _sigmoid(proj(6) + bg_ref[:, D_MODEL:]).astype(BF16)


def _in_projection(x2, w_in_b, b_gate, cos, sa, sb, w_exp_gate, w_exp_up, w_exp_down, seq):
    n = x2.shape[0]
    tm = TOKEN_TILE
    n_cols = w_in_b.shape[1]
    tiles_per_seq = seq // tm
    steps = n // tm
    per_step = -(-N_EXPERTS // steps)
    assert N_EXPERTS % per_step == 0
    row = lambda i: (i, 0)
    const = lambda i: (0, 0)
    pos = lambda i: (i % tiles_per_seq, 0)
    experts = lambda i: (jnp.minimum(i, N_EXPERTS // per_step - 1), 0, 0)
    tok_bf16 = jax.ShapeDtypeStruct((n, D_MODEL), BF16)
    up_block = (per_step, D_MODEL, EXPERT_HIDDEN)
    down_block = (per_step, EXPERT_HIDDEN, D_MODEL)
    return pl.pallas_call(
        _inproj_kernel,
        grid=(steps,),
        in_specs=[
            pl.BlockSpec((tm, D_MODEL), row),
            pl.BlockSpec((D_MODEL, n_cols), const),
            pl.BlockSpec((1, 2 * D_MODEL), const),
            pl.BlockSpec((tm, HEAD_DIM), pos),
            pl.BlockSpec((tm, HEAD_DIM), pos),
            pl.BlockSpec((tm, HEAD_DIM), pos),
            pl.BlockSpec(up_block, experts),
            pl.BlockSpec(up_block, experts),
            pl.BlockSpec(down_block, experts),
        ],
        out_specs=[
            pl.BlockSpec((tm, D_MODEL), row),
            pl.BlockSpec((tm, D_MODEL), row),
            pl.BlockSpec((tm, D_MODEL), row),
            pl.BlockSpec((tm, D_MODEL), row),
            pl.BlockSpec((tm, D_MODEL), row),
            pl.BlockSpec((tm, D_MODEL), row),
            pl.BlockSpec((tm // MOBA_BLOCK, 1, D_MODEL), lambda i: (i, 0, 0)),
            pl.BlockSpec(up_block, experts),
            pl.BlockSpec(up_block, experts),
            pl.BlockSpec(down_block, experts),
        ],
        out_shape=[tok_bf16, tok_bf16, tok_bf16,
                   jax.ShapeDtypeStruct((n, D_MODEL), F32),
                   tok_bf16, tok_bf16,
                   jax.ShapeDtypeStruct((n // MOBA_BLOCK, 1, D_MODEL), F32),
                   jax.ShapeDtypeStruct(w_exp_gate.shape, BF16),
                   jax.ShapeDtypeStruct(w_exp_up.shape, BF16),
                   jax.ShapeDtypeStruct(w_exp_down.shape, BF16)],
        compiler_params=_params("arbitrary"),
        name="in_projection",
    )(x2, w_in_b, b_gate, cos, sa, sb, w_exp_gate, w_exp_up, w_exp_down)


def _conv_kernel(u_ref, wdw_ref, bdw_ref, lng_ref, lnb_ref, wo_ref, gc_ref, o_ref, buf_ref, sh_ref, y_ref):
    ts = u_ref.shape[0]
    s = pl.program_id(1)

    @pl.when(s == 0)
    def _():
        buf_ref[0:CONV_HALO, :] = jnp.zeros((CONV_HALO, D_MODEL), F32)

    @pl.when(s > 0)
    def _():
        buf_ref[0:CONV_HALO, :] = buf_ref[ts:ts + CONV_HALO, :]

    buf_ref[CONV_HALO:CONV_HALO + ts, :] = u_ref[...]

    span = ts + CONV_HALO - SUBLANES
    for b in range(1, SUBLANES):
        sh_ref[b - 1, 0:span, :] = buf_ref[b:b + span, :]

    base = CONV_HALO - (CONV_WIDTH - 1)
    for c in range(ts // CONV_ROWS):
        r0 = c * CONV_ROWS
        acc = jnp.zeros((CONV_ROWS // SUBLANES, SUBLANES, D_MODEL), F32)
        for j in range(CONV_WIDTH):
            shift = (base + j) % SUBLANES
            row = r0 + base + j - shift
            src = buf_ref if shift == 0 else sh_ref.at[shift - 1]
            tap = src[row:row + CONV_ROWS, :].reshape(CONV_ROWS // SUBLANES, SUBLANES, D_MODEL)
            acc = acc + tap * wdw_ref[j]
        acc = acc.reshape(CONV_ROWS, D_MODEL)
        y = _layer_norm(acc + bdw_ref[...], lng_ref[...], lnb_ref[...])
        y_ref[r0:r0 + CONV_ROWS, :] = (y * _sigmoid(y)).astype(BF16)
    z = jnp.dot(y_ref[...], wo_ref[...], preferred_element_type=F32)
    o_ref[...] = (z * gc_ref[...].astype(F32)).astype(BF16)


def _conv_branch(u, w_dw, b_dw, ln_g, ln_b, w_o_b, gate_c, batch, seq):
    n = u.shape[0]
    ts = TOKEN_TILE
    tiles_per_seq = seq // ts
    row = lambda b, s: (b * tiles_per_seq + s, 0)
    const = lambda b, s: (0, 0)
    return pl.pallas_call(
        _conv_kernel,
        grid=(batch, tiles_per_seq),
        in_specs=[
            pl.BlockSpec((ts, D_MODEL), row),
            pl.BlockSpec((CONV_WIDTH, SUBLANES, D_MODEL), lambda b, s: (0, 0, 0)),
            pl.BlockSpec((1, D_MODEL), const),
            pl.BlockSpec((1, D_MODEL), const),
            pl.BlockSpec((1, D_MODEL), const),
            pl.BlockSpec((D_MODEL, D_MODEL), const),
            pl.BlockSpec((ts, D_MODEL), row),
        ],
        out_specs=pl.BlockSpec((ts, D_MODEL), row),
        out_shape=jax.ShapeDtypeStruct((n, D_MODEL), BF16),
        scratch_shapes=[pltpu.VMEM((ts + CONV_HALO, D_MODEL), F32),
                        pltpu.VMEM((SUBLANES - 1, ts + CONV_HALO - SUBLANES, D_MODEL), F32),
                        pltpu.VMEM((ts, D_MODEL), BF16)],
        compiler_params=_params("parallel", "arbitrary"),
        name="conv_branch",
    )(u, w_dw, b_dw, ln_g, ln_b, w_o_b, gate_c)


def _attn_kernel(q_ref, k_ref, v_ref, km_ref, o_ref, *, n_blk):
    blk = MOBA_BLOCK
    seq = n_blk * blk
    k_sel = min(MOBA_TOPK, n_blk)
    exp2_scale = HEAD_DIM ** -0.5 * LOG2_E
    nt_dims = (((1,), (1,)), ((), ()))
    n_lane = HEAD_DIM

    assert blk & (blk - 1) == 0
    blk_shift = blk.bit_length() - 1
    n_sub = -(-n_blk // SUBLANES) * SUBLANES

    blk_t = lax.broadcasted_iota(jnp.int32, (n_sub, seq), 0)
    own_t = lax.broadcasted_iota(jnp.int32, (n_sub, seq), 1) >> blk_shift
    past = blk_t < own_t
    blk_f = blk_t.astype(F32)
    blk_id = lax.broadcasted_iota(jnp.int32, (seq, n_lane), 1)
    own_id = lax.broadcasted_iota(jnp.int32, (seq, n_lane), 0) >> blk_shift
    k_blk = jnp.where(blk_id == own_id, 1.0, 0.0).astype(BF16)
    ones = jnp.ones((seq, n_lane), BF16)
    causal = (lax.broadcasted_iota(jnp.int32, (blk, blk), 1)
              <= lax.broadcasted_iota(jnp.int32, (blk, blk), 0))

    def prepare(lanes):
        q_all = q_ref[:, lanes]
        km = km_ref[:, 0, lanes]
        if n_sub > n_blk:
            km = jnp.concatenate([km, jnp.zeros((n_sub - n_blk, HEAD_DIM), F32)], axis=0)
        km_hi = km.astype(BF16)
        km_lo = (km - km_hi.astype(F32)).astype(BF16)
        gate = (lax.dot_general(km_hi, q_all, nt_dims, preferred_element_type=F32)
                + lax.dot_general(km_lo, q_all, nt_dims, preferred_element_type=F32))
        g = jnp.where(past, gate, NEG_INF)
        sel = jnp.zeros((n_sub, seq), F32)
        for _ in range(k_sel):
            mx = jnp.max(g, axis=0, keepdims=True)
            first = jnp.min(jnp.where(g == mx, blk_f, float(n_sub)), axis=0, keepdims=True)
            pick = blk_f == first
            sel = jnp.where(pick, 1.0, sel)
            g = jnp.where(pick, NEG_INF, g)
        visible = ((sel > 0.0) & past) | (blk_t == own_t)
        bias_t = jnp.concatenate([jnp.where(visible, 0.0, MASKED), jnp.zeros((n_lane - n_sub, seq), F32)],
                                 axis=0)
        q_bias = bias_t.T.astype(BF16)
        k_aug = jnp.concatenate([k_ref[:, lanes], k_blk], axis=1)
        v_aug = jnp.concatenate([v_ref[:, lanes], ones], axis=1)
        return q_all, q_bias, k_aug, v_aug, lanes

    def scores(head, i):
        q_all, q_bias, k_aug, _, _ = head
        rows = slice(i * blk, (i + 1) * blk)
        q_aug = jnp.concatenate([q_all[rows], q_bias[rows]], axis=1)
        return lax.dot_general(q_aug, k_aug[:(i + 1) * blk], nt_dims, preferred_element_type=F32)

    def finish(head, i, p):
        pv = jnp.dot(p, head[3][:(i + 1) * blk], preferred_element_type=F32)
        o_ref[i * blk:(i + 1) * blk, head[4]] = (pv[:, :HEAD_DIM] / pv[:, HEAD_DIM:HEAD_DIM + 1]).astype(BF16)

    def softmax_numerator(raw, i):
        own = jnp.where(causal, raw[:, i * blk:], MASKED)
        parts = [raw[:, :i * blk], own] if i else [own]
        m = jnp.max(own, axis=1, keepdims=True)
        if i:
            m = jnp.maximum(m, jnp.max(parts[0], axis=1, keepdims=True))
        return jnp.concatenate([jnp.exp2((t - m) * exp2_scale) for t in parts], axis=1).astype(BF16)

    heads = [prepare(slice(hh * HEAD_DIM, (hh + 1) * HEAD_DIM)) for hh in range(HEADS_PER_STEP)]
    raw_next = [scores(head, 0) for head in heads]
    p_prev = None
    for i in range(n_blk):
        raws = raw_next
        if i + 1 < n_blk:
            raw_next = [scores(head, i + 1) for head in heads]
        if p_prev is not None:
            for head, p in zip(heads, p_prev):
                finish(head, i - 1, p)
        p_prev = [softmax_numerator(raw, i) for raw in raws]
    for head, p in zip(heads, p_prev):
        finish(head, n_blk - 1, p)


def _moba_attention(q, k, v, k_mean, batch, seq):
    n = q.shape[0]
    n_blk = seq // MOBA_BLOCK
    width = HEADS_PER_STEP * HEAD_DIM
    seq_head = lambda b, h: (b, h)
    return pl.pallas_call(
        functools.partial(_attn_kernel, n_blk=n_blk),
        grid=(batch, N_HEADS // HEADS_PER_STEP),
        in_specs=[
            pl.BlockSpec((seq, width), seq_head),
            pl.BlockSpec((seq, width), seq_head),
            pl.BlockSpec((seq, width), seq_head),
            pl.BlockSpec((n_blk, 1, width), lambda b, h: (b, 0, h)),
        ],
        out_specs=pl.BlockSpec((seq, width), seq_head),
        out_shape=jax.ShapeDtypeStruct((n, D_MODEL), BF16),
        compiler_params=_params("parallel", "parallel"),
        name="moba_attention",
    )(q, k, v, k_mean)


def _merge_kernel(attn_ref, ga_ref, cg_ref, x_ref, wao_ref, wout_ref, g1_ref, b1_ref,
                  wr_ref, rb_ref, h_ref, hp_ref, idx_ref, wgt_ref, rank_ref, total_ref, count_ref, hprev_ref):
    tm = x_ref.shape[0]
    step = pl.program_id(0)

    @pl.when(step == 0)
    def _():
        count_ref[...] = jnp.zeros((N_EXPERTS, 1), F32)
        hprev_ref[...] = jnp.zeros(hprev_ref.shape, F32)

    h = hprev_ref[...]

    a = jnp.dot(attn_ref[...], wao_ref[...], preferred_element_type=F32)
    merged = ga_ref[...].astype(F32) * a + cg_ref[...].astype(F32)
    y = jnp.dot(merged.astype(BF16), wout_ref[...], preferred_element_type=F32)
    h_new = _layer_norm(DEEPNORM_ALPHA * x_ref[...] + y, g1_ref[...], b1_ref[...])
    h_ref[...] = h_new
    hp_ref[...] = _pack_bf16_halves(h_new)
    hprev_ref[...] = h_new

    nt_dims = (((1,), (1,)), ((), ()))
    h_hi = h.astype(BF16)
    h_lo = (h - h_hi.astype(F32)).astype(BF16)
    w = wr_ref[...]
    w_hi = w.astype(BF16)
    w_lo = (w - w_hi.astype(F32)).astype(BF16)
    logits = (lax.dot_general(w_hi, h_hi, nt_dims, preferred_element_type=F32)
              + lax.dot_general(w_hi, h_lo, nt_dims, preferred_element_type=F32)
              + lax.dot_general(w_lo, h_hi, nt_dims, preferred_element_type=F32))
    scores = _sigmoid(logits)
    biased = scores + rb_ref[...]

    g3 = biased.reshape(N_GROUPS, GROUP_SIZE, tm)
    m1 = jnp.max(g3, axis=1, keepdims=True)
    is_max = g3 == m1
    n_max = jnp.sum(jnp.where(is_max, 1.0, 0.0), axis=1, keepdims=True)
    m2 = jnp.max(jnp.where(is_max, NEG_INF, g3), axis=1, keepdims=True)
    grp = (m1 + jnp.where(n_max >= 2.0, m1, m2)).reshape(N_GROUPS, tm)

    gid = lax.broadcasted_iota(jnp.int32, (N_GROUPS, tm), 0)
    rank = jnp.zeros((N_GROUPS, tm), F32)
    for o in range(N_GROUPS):
        other = grp[o:o + 1, :]
        ahead = (other > grp) | ((other == grp) & (o < gid))
        rank = rank + jnp.where(ahead, 1.0, 0.0)
    grp_keep = jnp.where(rank < float(TOPK_GROUPS), 1.0, 0.0)
    keep = jnp.broadcast_to(grp_keep.reshape(N_GROUPS, 1, tm),
                            (N_GROUPS, GROUP_SIZE, tm)).reshape(N_EXPERTS, tm)
    cand = jnp.where(keep > 0.0, biased, NEG_INF)

    eid = lax.broadcasted_iota(jnp.int32, (N_EXPERTS, tm), 0).astype(F32)
    chosen = jnp.zeros((N_EXPERTS, tm), F32)
    firsts = []
    for r in range(TOP_K):
        mx = jnp.max(cand, axis=0, keepdims=True)
        first = jnp.min(jnp.where(cand == mx, eid, float(N_EXPERTS)), axis=0, keepdims=True)
        pick = eid == first
        firsts.append(first)
        idx_ref[r:r + 1, :] = first.astype(jnp.int32)
        wgt_ref[r:r + 1, :] = jnp.sum(jnp.where(pick, scores, 0.0), axis=0, keepdims=True)
        cand = jnp.where(pick, NEG_INF, cand)
        chosen = jnp.where(pick, 1.0, chosen)
    top_s = wgt_ref[...]
    wgt_ref[...] = top_s / (jnp.sum(top_s, axis=0, keepdims=True) + 1e-20) * ROUTED_SCALE

    earlier = (lax.broadcasted_iota(jnp.int32, (tm, tm), 0)
               < lax.broadcasted_iota(jnp.int32, (tm, tm), 1))
    before = jnp.dot(chosen.astype(BF16), jnp.where(earlier, 1.0, 0.0).astype(BF16),
                     preferred_element_type=F32) + count_ref[...]
    for r in range(TOP_K):
        rank_ref[r:r + 1, :] = jnp.sum(jnp.where(eid == firsts[r], before, 0.0),
                                       axis=0, keepdims=True).astype(jnp.int32)
    counted = jnp.where(step > 0, 1.0, 0.0)
    total = count_ref[...] + counted * jnp.sum(chosen, axis=1, keepdims=True)
    count_ref[...] = total
    total_ref[...] = total


def _merge_ln_router(attn, gate_a, conv_gated, x2, w_attn_o_b, w_out_b, ln_g, ln_b, w_router_t, router_bias,
                     first_token, n):
    tm = MERGE_TILE
    first_tile = first_token // tm
    n_tiles = n // tm
    row_in = lambda i: (jnp.minimum(i, n_tiles - 1) + first_tile, 0)
    row = lambda i: (jnp.minimum(i, n_tiles - 1), 0)
    const = lambda i: (0, 0)
    col = lambda i: (0, jnp.maximum(i - 1, 0))
    return pl.pallas_call(
        _merge_kernel,
        grid=(n_tiles + 1,),
        in_specs=[
            pl.BlockSpec((tm, D_MODEL), row_in),
            pl.BlockSpec((tm, D_MODEL), row_in),
            pl.BlockSpec((tm, D_MODEL), row_in),
            pl.BlockSpec((tm, D_MODEL), row_in),
            pl.BlockSpec((D_MODEL, D_MODEL), const),
            pl.BlockSpec((D_MODEL, D_MODEL), const),
            pl.BlockSpec((1, D_MODEL), const),
            pl.BlockSpec((1, D_MODEL), const),
            pl.BlockSpec((N_EXPERTS, D_MODEL), const),
            pl.BlockSpec((N_EXPERTS, 1), const),
        ],
        out_specs=[
            pl.BlockSpec((tm, D_MODEL), row),
            pl.BlockSpec((tm, HALF), row),
            pl.BlockSpec((TOP_K, tm), col),
            pl.BlockSpec((TOP_K, tm), col),
            pl.BlockSpec((TOP_K, tm), col),
            pl.BlockSpec((N_EXPERTS, 1), const),
        ],
        out_shape=[
            jax.ShapeDtypeStruct((n, D_MODEL), F32),
            jax.ShapeDtypeStruct((n, HALF), jnp.uint32),
            jax.ShapeDtypeStruct((TOP_K, n), jnp.int32),
            jax.ShapeDtypeStruct((TOP_K, n), F32),
            jax.ShapeDtypeStruct((TOP_K, n), jnp.int32),
            jax.ShapeDtypeStruct((N_EXPERTS, 1), F32),
        ],
        scratch_shapes=[pltpu.VMEM((N_EXPERTS, 1), F32), pltpu.VMEM((tm, D_MODEL), F32)],
        compiler_params=_params("arbitrary"),
        name="merge_ln_router",
    )(attn, gate_a, conv_gated, x2, w_attn_o_b, w_out_b, ln_g, ln_b, w_router_t, router_bias)


def _sc_mesh():
    return plsc.VectorSubcoreMesh(core_axis_name="c", subcore_axis_name="s",
                                  num_cores=SC_CORES, num_subcores=SC_SUBCORES)


def _sc_worker_base(rows_per_worker):
    return (lax.axis_index("s") * SC_CORES + lax.axis_index("c")) * rows_per_worker


def _sc_gather_rows(table, idx):
    m = idx.shape[0]
    width = table.shape[1]
    per_worker = m // (SC_CORES * SC_SUBCORES)
    assert per_worker * SC_CORES * SC_SUBCORES == m and per_worker % SC_CHUNK == 0

    @functools.partial(
        pl.kernel, mesh=_sc_mesh(),
        out_type=jax.ShapeDtypeStruct((m, width), table.dtype),
        scratch_types=[pltpu.VMEM((SC_CHUNK,), jnp.int32),
                       pltpu.VMEM((SC_CHUNK, width), table.dtype),
                       pltpu.SemaphoreType.DMA],
        name="sc_gather_rows")
    def gather(table_hbm, idx_hbm, out_hbm, idx_v, rows_v, sem):
        base = _sc_worker_base(per_worker)

        @pl.loop(0, per_worker // SC_CHUNK)
        def _(c):
            off = pl.multiple_of(base + c * SC_CHUNK, SC_CHUNK)
            pltpu.sync_copy(idx_hbm.at[pl.ds(off, SC_CHUNK)], idx_v)
            pltpu.async_copy(table_hbm.at[idx_v], rows_v, sem).wait()
            pltpu.sync_copy(rows_v, out_hbm.at[pl.ds(off, SC_CHUNK)])

    return gather(table, idx)


def _sc_scatter_rows(rows, pos, n_out):
    n, width = rows.shape
    per_worker = n // (SC_CORES * SC_SUBCORES)
    assert per_worker * SC_CORES * SC_SUBCORES == n and per_worker % SC_CHUNK == 0

    @functools.partial(
        pl.kernel, mesh=_sc_mesh(),
        out_type=jax.ShapeDtypeStruct((n_out, width), rows.dtype),
        scratch_types=[pltpu.VMEM((SC_CHUNK,), jnp.int32),
                       pltpu.VMEM((SC_CHUNK, width), rows.dtype),
                       pltpu.SemaphoreType.DMA],
        name="sc_scatter_rows")
    def scatter(rows_hbm, pos_hbm, out_hbm, idx_v, rows_v, sem):
        base = _sc_worker_base(per_worker)

        @pl.loop(0, per_worker // SC_CHUNK)
        def _(c):
            off = pl.multiple_of(base + c * SC_CHUNK, SC_CHUNK)
            pltpu.sync_copy(rows_hbm.at[pl.ds(off, SC_CHUNK)], rows_v)
            for r in range(TOP_K):
                pltpu.sync_copy(pos_hbm.at[pl.ds(r * n + off, SC_CHUNK)], idx_v)
                pltpu.async_copy(rows_v, out_hbm.at[idx_v], sem).wait()

    return scatter(rows, pos)


def _position_kernel(idx_ref, rank_ref, start_ref, pos_ref):
    tl = idx_ref.shape[1]
    eid = lax.broadcasted_iota(jnp.int32, (N_EXPERTS, tl), 0)
    start = start_ref[...]
    for r in range(TOP_K):
        here = jnp.sum(jnp.where(eid == idx_ref[r:r + 1, :], start, 0.0), axis=0, keepdims=True)
        pos_ref[r:r + 1, :] = here.astype(jnp.int32) + rank_ref[r:r + 1, :]


def _positions(top_idx, rank, group_start):
    n = top_idx.shape[1]
    tl = POSITION_TILE
    col = lambda i: (0, i)
    return pl.pallas_call(
        _position_kernel,
        grid=(n // tl,),
        in_specs=[pl.BlockSpec((TOP_K, tl), col), pl.BlockSpec((TOP_K, tl), col),
                  pl.BlockSpec((N_EXPERTS, 1), lambda i: (0, 0))],
        out_specs=pl.BlockSpec((TOP_K, tl), col),
        out_shape=jax.ShapeDtypeStruct((TOP_K, n), jnp.int32),
        compiler_params=_params("parallel"),
        name="positions",
    )(top_idx, rank, group_start)


def _expert_kernel(te_ref, tv_ref, nu_ref, xs_ref, wg_ref, wu_ref, wd_ref, y_ref, act_ref):
    i = pl.program_id(0)
    valid = tv_ref[i]
    valid_prev = tv_ref[jnp.maximum(i - 1, 0)]

    @pl.when(i == 0)
    def _():
        act_ref[...] = jnp.zeros(act_ref.shape, BF16)

    @pl.when((valid > 0) | (valid_prev > 0))
    def _():
        act_prev = act_ref[...]
        wd = wd_ref[0]
        for c in range(EXPERT_TILE // EXPERT_CHAIN):
            rows = slice(c * EXPERT_CHAIN, (c + 1) * EXPERT_CHAIN)
            y_ref[rows, :] = _pack_bf16_halves(jnp.dot(act_prev[rows], wd, preferred_element_type=F32))
        wg = wg_ref[0]
        wu = wu_ref[0]
        for c in range(EXPERT_TILE // EXPERT_CHAIN):
            rows = slice(c * EXPERT_CHAIN, (c + 1) * EXPERT_CHAIN)
            live = lax.broadcasted_iota(jnp.int32, (EXPERT_CHAIN, HALF), 0) < valid - c * EXPERT_CHAIN
            lo, hi = _unpack_bf16_halves(jnp.where(live, xs_ref[rows, :], jnp.uint32(0)))
            x = jnp.concatenate([lo, hi], axis=1).astype(BF16)
            g = jnp.dot(x, wg, preferred_element_type=F32)
            u = jnp.dot(x, wu, preferred_element_type=F32)
            act_ref[rows, :] = (g * _sigmoid(g) * u).astype(BF16)


def _grouped_experts(tile_expert, tile_valid, xs, wg, wu, wd):
    p = xs.shape[0]
    t = EXPERT_TILE
    n_tiles = p // t
    n_used = jnp.sum((tile_valid > 0).astype(jnp.int32)).reshape(1)
    tile_valid = jnp.concatenate([tile_valid, jnp.zeros((1,), jnp.int32)])
    cur = lambda i, nu: jnp.minimum(i, nu[0] - 1)
    prev = lambda i, nu: jnp.minimum(jnp.maximum(i - 1, 0), nu[0] - 1)
    row_in = lambda i, te, tv, nu: (cur(i, nu), 0)
    row_out = lambda i, te, tv, nu: (prev(i, nu), 0)
    expert = lambda i, te, tv, nu: (te[cur(i, nu)], 0, 0)
    expert_prev = lambda i, te, tv, nu: (te[prev(i, nu)], 0, 0)
    return pl.pallas_call(
        _expert_kernel,
        grid_spec=pltpu.PrefetchScalarGridSpec(
            num_scalar_prefetch=3,
            grid=(n_tiles + 1,),
            in_specs=[
                pl.BlockSpec((t, HALF), row_in),
                pl.BlockSpec((1, D_MODEL, EXPERT_HIDDEN), expert),
                pl.BlockSpec((1, D_MODEL, EXPERT_HIDDEN), expert),
                pl.BlockSpec((1, EXPERT_HIDDEN, D_MODEL), expert_prev),
            ],
            out_specs=pl.BlockSpec((t, HALF), row_out),
            scratch_shapes=[pltpu.VMEM((t, EXPERT_HIDDEN), BF16)],
        ),
        out_shape=jax.ShapeDtypeStruct((p, HALF), jnp.uint32),
        compiler_params=_params("arbitrary"),
        name="grouped_experts",
    )(tile_expert, tile_valid, n_used, xs, wg, wu, wd)


def _combine_kernel(h_ref, yg_ref, wt_ref, wsg_ref, wsu_ref, wsd_ref, g2_ref, b2_ref, *out_refs):
    o_ref = out_refs[-1]
    h = h_ref[...]
    hb = h.astype(BF16)
    g = jnp.dot(hb, wsg_ref[...], preferred_element_type=F32)
    u = jnp.dot(hb, wsu_ref[...], preferred_element_type=F32)
    shared = jnp.dot((g * _sigmoid(g) * u).astype(BF16), wsd_ref[...], preferred_element_type=F32)
    wt = wt_ref[...]
    r_lo = jnp.zeros((h.shape[0], HALF), F32)
    r_hi = jnp.zeros((h.shape[0], HALF), F32)
    for r in range(TOP_K):
        lo, hi = _unpack_bf16_halves(yg_ref[r])
        w = wt[:, r:r + 1]
        r_lo = r_lo + lo * w
        r_hi = r_hi + hi * w
    routed = jnp.concatenate([r_lo, r_hi], axis=1)
    o_ref[...] = _layer_norm(DEEPNORM_ALPHA * h + (shared + routed), g2_ref[...], b2_ref[...])


def _combine_ln(h, yg, w_tok, wsg_b, wsu_b, wsd_b, ln_g, ln_b, out_so_far, first_token, n_total):
    n = h.shape[0]
    tm = MERGE_TILE
    first_tile = first_token // tm
    row = lambda i: (i, 0)
    const = lambda i: (0, 0)
    hidden = wsg_b.shape[1]
    in_specs = [
        pl.BlockSpec((tm, D_MODEL), row),
        pl.BlockSpec((TOP_K, tm, HALF), lambda i: (0, i, 0)),
        pl.BlockSpec((tm, TOP_K), row),
        pl.BlockSpec((D_MODEL, hidden), const),
        pl.BlockSpec((D_MODEL, hidden), const),
        pl.BlockSpec((hidden, D_MODEL), const),
        pl.BlockSpec((1, D_MODEL), const),
        pl.BlockSpec((1, D_MODEL), const),
    ]
    args = [h, yg, w_tok, wsg_b, wsu_b, wsd_b, ln_g, ln_b]
    aliases = {}
    if out_so_far is not None:
        in_specs.append(pl.BlockSpec(memory_space=pl.ANY))
        args.append(out_so_far)
        aliases = {len(args) - 1: 0}
    return pl.pallas_call(
        _combine_kernel,
        grid=(n // tm,),
        in_specs=in_specs,
        out_specs=pl.BlockSpec((tm, D_MODEL), lambda i: (i + first_tile, 0)),
        out_shape=jax.ShapeDtypeStruct((n_total, D_MODEL), F32),
        input_output_aliases=aliases,
        compiler_params=_params("parallel"),
        name="combine_ln",
    )(*args)


def _group_layout(totals, n_tokens):
    t = EXPERT_TILE
    n_tiles = (TOP_K * n_tokens + N_EXPERTS * (t - 1)) // t
    counts = totals[:, 0].astype(jnp.int32)
    padded = ((counts + t - 1) // t) * t
    group_end = jnp.cumsum(padded)
    group_start = group_end - padded
    tile_start = jnp.arange(n_tiles, dtype=jnp.int32) * t
    tile_expert = jnp.minimum(jnp.sum((group_end[None, :] <= tile_start[:, None]).astype(jnp.int32), axis=1),
                              N_EXPERTS - 1)
    of_tile = tile_expert[:, None] == jnp.arange(N_EXPERTS, dtype=jnp.int32)[None, :]
    real_end = jnp.sum(jnp.where(of_tile, (group_start + counts)[None, :], 0), axis=1)
    tile_valid = jnp.clip(real_end - tile_start, 0, t).astype(jnp.int32)
    return group_start.astype(F32).reshape(N_EXPERTS, 1), tile_expert, tile_valid, n_tiles * t


def _rope_tables(seq):
    inv_freq = ROPE_THETA ** (-jnp.arange(0, ROPE_DIM, 2, dtype=F32) / ROPE_DIM)
    ang = jnp.arange(seq).astype(F32)[:, None] * inv_freq[None, :]
    cos, sin = jnp.cos(ang), jnp.sin(ang)
    rest = HEAD_DIM - ROPE_DIM
    zeros = jnp.zeros((seq, ROPE_HALF), F32)
    cos_t = jnp.concatenate([cos, cos, jnp.ones((seq, rest), F32)], axis=1)
    sa_t = jnp.concatenate([-sin, zeros, jnp.zeros((seq, rest), F32)], axis=1)
    sb_t = jnp.concatenate([zeros, sin, jnp.zeros((seq, rest), F32)], axis=1)
    return cos_t, sa_t, sb_t


def _layer(x2, batch, seq, w_in, b_gate, w_attn_o, w_dw, b_dw, conv_ln_g, conv_ln_b, w_conv_o, w_out,
           ln1_g, ln1_b, w_router, router_bias, w_exp_gate, w_exp_up, w_exp_down,
           w_sh_gate, w_sh_up, w_sh_down, ln2_g, ln2_b):
    n = x2.shape[0]
    row = lambda v: v.reshape(1, -1)
    cos, sa, sb = _rope_tables(seq)
    q, k, v, u, gate_a, gate_c, k_mean, w_gate_b, w_up_b, w_down_b = _in_projection(
        x2, w_in.astype(BF16), row(b_gate), cos, sa, sb, w_exp_gate, w_exp_up, w_exp_down, seq)
    w_taps = jnp.broadcast_to(w_dw.reshape(CONV_WIDTH, 1, D_MODEL), (CONV_WIDTH, SUBLANES, D_MODEL))
    conv_gated = _conv_branch(u, w_taps, row(b_dw), row(conv_ln_g),
                              row(conv_ln_b), w_conv_o.astype(BF16), gate_c, batch, seq)
    attn = _moba_attention(q, k, v, k_mean, batch, seq)
    w_attn_o_b, w_out_b, w_router_t = w_attn_o.astype(BF16), w_out.astype(BF16), w_router.T
    shared_w = (w_sh_gate.astype(BF16), w_sh_up.astype(BF16), w_sh_down.astype(BF16))
    unit = SC_CORES * SC_SUBCORES * SC_CHUNK
    assert n % unit == 0 and n >= 2 * unit and unit % MERGE_TILE == 0 and unit % POSITION_TILE == 0
    first_size = min(max(unit, round(n * FIRST_CHUNK_SHARE / unit) * unit), n - unit)
    out = None
    for first, n_chunk in ((0, first_size), (first_size, n - first_size)):
        h, h_packed, top_idx, top_w, rank, totals = _merge_ln_router(
            attn, gate_a, conv_gated, x2, w_attn_o_b, w_out_b, row(ln1_g), row(ln1_b),
            w_router_t, router_bias.reshape(N_EXPERTS, 1), first, n_chunk)
        group_start, tile_expert, tile_valid, n_rows = _group_layout(totals, n_chunk)
        pos = _positions(top_idx, rank, group_start).reshape(TOP_K * n_chunk)
        xs = _sc_scatter_rows(h_packed, pos, n_rows)
        ys = _grouped_experts(tile_expert, tile_valid, xs, w_gate_b, w_up_b, w_down_b)
        yg = _sc_gather_rows(ys, pos).reshape(TOP_K, n_chunk, HALF)
        out = _combine_ln(h, yg, top_w.T, *shared_w, row(ln2_g), row(ln2_b), out, first, n)
    return out


def kernel(x, w_in, b_gate, w_attn_o, w_dw, b_dw, conv_ln_g, conv_ln_b, w_conv_o, w_out, ln1_g, ln1_b,
           w_router, router_bias, w_exp_gate, w_exp_up, w_exp_down, w_sh_gate, w_sh_up, w_sh_down,
           ln2_g, ln2_b):
    batch, seq, d = x.shape
    assert d == D_MODEL and seq % MOBA_BLOCK == 0 and seq % TOKEN_TILE == 0
    assert w_in.shape[0] == DEPTH
    x2 = x.reshape(batch * seq, d)
    for l in range(DEPTH):
        x2 = _layer(x2, batch, seq, w_in[l], b_gate[l], w_attn_o[l], w_dw[l], b_dw[l], conv_ln_g[l],
                    conv_ln_b[l], w_conv_o[l], w_out[l], ln1_g[l], ln1_b[l], w_router[l], router_bias[l],
                    w_exp_gate[l], w_exp_up[l], w_exp_down[l], w_sh_gate[l], w_sh_up[l], w_sh_down[l],
                    ln2_g[l], ln2_b[l])
    return x2.reshape(batch, seq, d)
```

```python
import functools

import jax
import jax.numpy as jnp
from jax import lax
from jax.experimental import pallas as pl
from jax.experimental.pallas import tpu as pltpu
from jax.experimental.pallas import tpu_sc as plsc

F32 = jnp.float32
BF16 = jnp.bfloat16
NEG_INF = float("-inf")
MASKED = -1e30
LOG2_E = 1.4426950408889634

D_MODEL = 1024
N_HEADS = 8
HEAD_DIM = 128
ROPE_THETA = 500000.0
ROPE_DIM = HEAD_DIM // 4
ROPE_HALF = ROPE_DIM // 2
MOBA_BLOCK = 256
MOBA_TOPK = 3
CONV_WIDTH = 31
SUBLANES = 8
CONV_HALO = 32
CONV_ROWS = 32
N_EXPERTS = 256
TOP_K = 8
N_GROUPS = 8
GROUP_SIZE = N_EXPERTS // N_GROUPS
TOPK_GROUPS = 4
EXPERT_HIDDEN = 256
ROUTED_SCALE = 2.5
LN_EPS = 1e-5
DEPTH = 1
DEEPNORM_ALPHA = (2 * DEPTH) ** 0.25
HALF = D_MODEL // 2

TOKEN_TILE = 256
MERGE_TILE = 512
EXPERT_TILE = 512
EXPERT_CHAIN = 256
XS_SLOTS = 3
POSITION_TILE = 1024
FIRST_CHUNK_SHARE = 9 / 16
HEADS_PER_STEP = 4
SC_CORES = 2
SC_SUBCORES = 16
SC_CHUNK = 64
VMEM_LIMIT = 56 * 1024 * 1024


def _params(*semantics):
    return pltpu.CompilerParams(dimension_semantics=semantics, vmem_limit_bytes=VMEM_LIMIT)


def _sigmoid(x):
    return 1.0 / (1.0 + jnp.exp(-x))


def _layer_norm(x, g, b):
    mu = jnp.mean(x, axis=-1, keepdims=True)
    xc = x - mu
    var = jnp.mean(xc * xc, axis=-1, keepdims=True)
    return xc * lax.rsqrt(var + LN_EPS) * g + b


def _pack_bf16_halves(y):
    lo = lax.bitcast_convert_type(y[:, :HALF].astype(BF16).astype(F32), jnp.uint32)
    hi = lax.bitcast_convert_type(y[:, HALF:].astype(BF16).astype(F32), jnp.uint32)
    return (hi & jnp.uint32(0xFFFF0000)) | (lo >> 16)


def _unpack_bf16_halves(p):
    lo = lax.bitcast_convert_type(p << 16, F32)
    hi = lax.bitcast_convert_type(p & jnp.uint32(0xFFFF0000), F32)
    return lo, hi


def _inproj_kernel(x_ref, w_ref, bg_ref, cos_ref, sa_ref, sb_ref, wg_ref, wu_ref, wd_ref,
                   q_ref, k_ref, v_ref, u_ref, ga_ref, gc_ref, km_ref, wgb_ref, wub_ref, wdb_ref):
    wgb_ref[...] = wg_ref[...].astype(BF16)
    wub_ref[...] = wu_ref[...].astype(BF16)
    wdb_ref[...] = wd_ref[...].astype(BF16)

    tm = x_ref.shape[0]
    xb = x_ref[...].astype(BF16)

    def proj(c):
        return jnp.dot(xb, w_ref[:, c * D_MODEL:(c + 1) * D_MODEL], preferred_element_type=F32)

    cos = cos_ref[...]
    sa = sa_ref[...]
    sb = sb_ref[...]

    def rope_head(t):
        return (t * cos + pltpu.roll(t, HEAD_DIM - ROPE_HALF, 1) * sa
                + pltpu.roll(t, ROPE_HALF, 1) * sb)

    q = proj(0)
    for h in range(N_HEADS):
        sl = slice(h * HEAD_DIM, (h + 1) * HEAD_DIM)
        q_ref[:, sl] = rope_head(q[:, sl]).astype(BF16)
    k = proj(1)
    for h in range(N_HEADS):
        sl = slice(h * HEAD_DIM, (h + 1) * HEAD_DIM)
        kr = rope_head(k[:, sl])
        k_ref[:, sl] = kr.astype(BF16)
        for g in range(tm // MOBA_BLOCK):
            km_ref[g, :, sl] = jnp.mean(kr[g * MOBA_BLOCK:(g + 1) * MOBA_BLOCK], axis=0, keepdims=True)
    v_ref[...] = proj(2).astype(BF16)
    u_ref[...] = proj(3) * _sigmoid(proj(4))
    ga_ref[...] = _sigmoid(proj(5) + bg_ref[:, :D_MODEL]).astype(BF16)
    gc_ref[...] = _sigmoid(proj(6) + bg_ref[:, D_MODEL:]).astype(BF16)


def _in_projection(x2, w_in_b, b_gate, cos, sa, sb, w_exp_gate, w_exp_up, w_exp_down, seq):
    n = x2.shape[0]
    tm = TOKEN_TILE
    n_cols = w_in_b.shape[1]
    tiles_per_seq = seq // tm
    steps = n // tm
    per_step = -(-N_EXPERTS // steps)
    assert N_EXPERTS % per_step == 0
    row = lambda i: (i, 0)
    const = lambda i: (0, 0)
    pos = lambda i: (i % tiles_per_seq, 0)
    experts = lambda i: (jnp.minimum(i, N_EXPERTS // per_step - 1), 0, 0)
    tok_bf16 = jax.ShapeDtypeStruct((n, D_MODEL), BF16)
    up_block = (per_step, D_MODEL, EXPERT_HIDDEN)
    down_block = (per_step, EXPERT_HIDDEN, D_MODEL)
    return pl.pallas_call(
        _inproj_kernel,
        grid=(steps,),
        in_specs=[
            pl.BlockSpec((tm, D_MODEL), row),
            pl.BlockSpec((D_MODEL, n_cols), const),
            pl.BlockSpec((1, 2 * D_MODEL), const),
            pl.BlockSpec((tm, HEAD_DIM), pos),
            pl.BlockSpec((tm, HEAD_DIM), pos),
            pl.BlockSpec((tm, HEAD_DIM), pos),
            pl.BlockSpec(up_block, experts),
            pl.BlockSpec(up_block, experts),
            pl.BlockSpec(down_block, experts),
        ],
        out_specs=[
            pl.BlockSpec((tm, D_MODEL), row),
            pl.BlockSpec((tm, D_MODEL), row),
            pl.BlockSpec((tm, D_MODEL), row),
            pl.BlockSpec((tm, D_MODEL), row),
            pl.BlockSpec((tm, D_MODEL), row),
            pl.BlockSpec((tm, D_MODEL), row),
            pl.BlockSpec((tm // MOBA_BLOCK, 1, D_MODEL), lambda i: (i, 0, 0)),
            pl.BlockSpec(up_block, experts),
            pl.BlockSpec(up_block, experts),
            pl.BlockSpec(down_block, experts),
        ],
        out_shape=[tok_bf16, tok_bf16, tok_bf16,
                   jax.ShapeDtypeStruct((n, D_MODEL), F32),
                   tok_bf16, tok_bf16,
                   jax.ShapeDtypeStruct((n // MOBA_BLOCK, 1, D_MODEL), F32),
                   jax.ShapeDtypeStruct(w_exp_gate.shape, BF16),
                   jax.ShapeDtypeStruct(w_exp_up.shape, BF16),
                   jax.ShapeDtypeStruct(w_exp_down.shape, BF16)],
        compiler_params=_params("arbitrary"),
        name="in_projection",
    )(x2, w_in_b, b_gate, cos, sa, sb, w_exp_gate, w_exp_up, w_exp_down)


def _conv_kernel(u_ref, wdw_ref, bdw_ref, lng_ref, lnb_ref, wo_ref, gc_ref, o_ref, buf_ref, sh_ref, y_ref):
    ts = u_ref.shape[0]
    s = pl.program_id(1)

    @pl.when(s == 0)
    def _():
        buf_ref[0:CONV_HALO, :] = jnp.zeros((CONV_HALO, D_MODEL), F32)

    @pl.when(s > 0)
    def _():
        buf_ref[0:CONV_HALO, :] = buf_ref[ts:ts + CONV_HALO, :]

    buf_ref[CONV_HALO:CONV_HALO + ts, :] = u_ref[...]

    span = ts + CONV_HALO - SUBLANES
    for b in range(1, SUBLANES):
        sh_ref[b - 1, 0:span, :] = buf_ref[b:b + span, :]

    base = CONV_HALO - (CONV_WIDTH - 1)
    for c in range(ts // CONV_ROWS):
        r0 = c * CONV_ROWS
        acc = jnp.zeros((CONV_ROWS // SUBLANES, SUBLANES, D_MODEL), F32)
        for j in range(CONV_WIDTH):
            shift = (base + j) % SUBLANES
            row = r0 + base + j - shift
            src = buf_ref if shift == 0 else sh_ref.at[shift - 1]
            tap = src[row:row + CONV_ROWS, :].reshape(CONV_ROWS // SUBLANES, SUBLANES, D_MODEL)
            acc = acc + tap * wdw_ref[j]
        acc = acc.reshape(CONV_ROWS, D_MODEL)
        y = _layer_norm(acc + bdw_ref[...], lng_ref[...], lnb_ref[...])
        y_ref[r0:r0 + CONV_ROWS, :] = (y * _sigmoid(y)).astype(BF16)
    z = jnp.dot(y_ref[...], wo_ref[...], preferred_element_type=F32)
    o_ref[...] = (z * gc_ref[...].astype(F32)).astype(BF16)


def _conv_branch(u, w_dw, b_dw, ln_g, ln_b, w_o_b, gate_c, batch, seq):
    n = u.shape[0]
    ts = TOKEN_TILE
    tiles_per_seq = seq // ts
    row = lambda b, s: (b * tiles_per_seq + s, 0)
    const = lambda b, s: (0, 0)
    return pl.pallas_call(
        _conv_kernel,
        grid=(batch, tiles_per_seq),
        in_specs=[
            pl.BlockSpec((ts, D_MODEL), row),
            pl.BlockSpec((CONV_WIDTH, SUBLANES, D_MODEL), lambda b, s: (0, 0, 0)),
            pl.BlockSpec((1, D_MODEL), const),
            pl.BlockSpec((1, D_MODEL), const),
            pl.BlockSpec((1, D_MODEL), const),
            pl.BlockSpec((D_MODEL, D_MODEL), const),
            pl.BlockSpec((ts, D_MODEL), row),
        ],
        out_specs=pl.BlockSpec((ts, D_MODEL), row),
        out_shape=jax.ShapeDtypeStruct((n, D_MODEL), BF16),
        scratch_shapes=[pltpu.VMEM((ts + CONV_HALO, D_MODEL), F32),
                        pltpu.VMEM((SUBLANES - 1, ts + CONV_HALO - SUBLANES, D_MODEL), F32),
                        pltpu.VMEM((ts, D_MODEL), BF16)],
        compiler_params=_params("parallel", "arbitrary"),
        name="conv_branch",
    )(u, w_dw, b_dw, ln_g, ln_b, w_o_b, gate_c)


def _attn_kernel(q_ref, k_ref, v_ref, km_ref, o_ref, *, n_blk):
    blk = MOBA_BLOCK
    seq = n_blk * blk
    k_sel = min(MOBA_TOPK, n_blk)
    exp2_scale = HEAD_DIM ** -0.5 * LOG2_E
    nt_dims = (((1,), (1,)), ((), ()))
    n_lane = HEAD_DIM

    assert blk & (blk - 1) == 0
    blk_shift = blk.bit_length() - 1
    n_sub = -(-n_blk // SUBLANES) * SUBLANES

    blk_t = lax.broadcasted_iota(jnp.int32, (n_sub, seq), 0)
    own_t = lax.broadcasted_iota(jnp.int32, (n_sub, seq), 1) >> blk_shift
    past = blk_t < own_t
    blk_f = blk_t.astype(F32)
    blk_id = lax.broadcasted_iota(jnp.int32, (seq, n_lane), 1)
    own_id = lax.broadcasted_iota(jnp.int32, (seq, n_lane), 0) >> blk_shift
    k_blk = jnp.where(blk_id == own_id, 1.0, 0.0).astype(BF16)
    ones = jnp.ones((seq, n_lane), BF16)
    causal = (lax.broadcasted_iota(jnp.int32, (blk, blk), 1)
              <= lax.broadcasted_iota(jnp.int32, (blk, blk), 0))

    def prepare(lanes):
        q_all = q_ref[:, lanes]
        km = km_ref[:, 0, lanes]
        if n_sub > n_blk:
            km = jnp.concatenate([km, jnp.zeros((n_sub - n_blk, HEAD_DIM), F32)], axis=0)
        km_hi = km.astype(BF16)
        km_lo = (km - km_hi.astype(F32)).astype(BF16)
        gate = (lax.dot_general(km_hi, q_all, nt_dims, preferred_element_type=F32)
                + lax.dot_general(km_lo, q_all, nt_dims, preferred_element_type=F32))
        g = jnp.where(past, gate, NEG_INF)
        sel = jnp.zeros((n_sub, seq), F32)
        for _ in range(k_sel):
            mx = jnp.max(g, axis=0, keepdims=True)
            first = jnp.min(jnp.where(g == mx, blk_f, float(n_sub)), axis=0, keepdims=True)
            pick = blk_f == first
            sel = jnp.where(pick, 1.0, sel)
            g = jnp.where(pick, NEG_INF, g)
        visible = ((sel > 0.0) & past) | (blk_t == own_t)
        bias_t = jnp.concatenate([jnp.where(visible, 0.0, MASKED), jnp.zeros((n_lane - n_sub, seq), F32)],
                                 axis=0)
        q_bias = bias_t.T.astype(BF16)
        k_aug = jnp.concatenate([k_ref[:, lanes], k_blk], axis=1)
        v_aug = jnp.concatenate([v_ref[:, lanes], ones], axis=1)
        return q_all, q_bias, k_aug, v_aug, lanes

    def scores(head, i):
        q_all, q_bias, k_aug, _, _ = head
        rows = slice(i * blk, (i + 1) * blk)
        q_aug = jnp.concatenate([q_all[rows], q_bias[rows]], axis=1)
        return lax.dot_general(q_aug, k_aug[:(i + 1) * blk], nt_dims, preferred_element_type=F32)

    def finish(head, i, p):
        pv = jnp.dot(p, head[3][:(i + 1) * blk], preferred_element_type=F32)
        o_ref[i * blk:(i + 1) * blk, head[4]] = (pv[:, :HEAD_DIM] / pv[:, HEAD_DIM:HEAD_DIM + 1]).astype(BF16)

    def softmax_numerator(raw, i):
        own = jnp.where(causal, raw[:, i * blk:], MASKED)
        parts = [raw[:, :i * blk], own] if i else [own]
        m = jnp.max(own, axis=1, keepdims=True)
        if i:
            m = jnp.maximum(m, jnp.max(parts[0], axis=1, keepdims=True))
        return jnp.concatenate([jnp.exp2((t - m) * exp2_scale) for t in parts], axis=1).astype(BF16)

    heads = [prepare(slice(hh * HEAD_DIM, (hh + 1) * HEAD_DIM)) for hh in range(HEADS_PER_STEP)]
    raw_next = [scores(head, 0) for head in heads]
    p_prev = None
    for i in range(n_blk):
        raws = raw_next
        if i + 1 < n_blk:
            raw_next = [scores(head, i + 1) for head in heads]
        if p_prev is not None:
            for head, p in zip(heads, p_prev):
                finish(head, i - 1, p)
        p_prev = [softmax_numerator(raw, i) for raw in raws]
    for head, p in zip(heads, p_prev):
        finish(head, n_blk - 1, p)


def _moba_attention(q, k, v, k_mean, batch, seq):
    n = q.shape[0]
    n_blk = seq // MOBA_BLOCK
    width = HEADS_PER_STEP * HEAD_DIM
    seq_head = lambda b, h: (b, h)
    return pl.pallas_call(
        functools.partial(_attn_kernel, n_blk=n_blk),
        grid=(batch, N_HEADS // HEADS_PER_STEP),
        in_specs=[
            pl.BlockSpec((seq, width), seq_head),
            pl.BlockSpec((seq, width), seq_head),
            pl.BlockSpec((seq, width), seq_head),
            pl.BlockSpec((n_blk, 1, width), lambda b, h: (b, 0, h)),
        ],
        out_specs=pl.BlockSpec((seq, width), seq_head),
        out_shape=jax.ShapeDtypeStruct((n, D_MODEL), BF16),
        compiler_params=_params("parallel", "parallel"),
        name="moba_attention",
    )(q, k, v, k_mean)


def _merge_kernel(attn_ref, ga_ref, cg_ref, x_ref, wao_ref, wout_ref, g1_ref, b1_ref,
                  wr_ref, rb_ref, h_ref, hp_ref, idx_ref, wgt_ref, rank_ref, total_ref, count_ref, hprev_ref):
    tm = x_ref.shape[0]
    step = pl.program_id(0)

    @pl.when(step == 0)
    def _():
        count_ref[...] = jnp.zeros((N_EXPERTS, 1), F32)
        hprev_ref[...] = jnp.zeros(hprev_ref.shape, F32)

    h = hprev_ref[...]

    a = jnp.dot(attn_ref[...], wao_ref[...], preferred_element_type=F32)
    merged = ga_ref[...].astype(F32) * a + cg_ref[...].astype(F32)
    y = jnp.dot(merged.astype(BF16), wout_ref[...], preferred_element_type=F32)
    h_new = _layer_norm(DEEPNORM_ALPHA * x_ref[...] + y, g1_ref[...], b1_ref[...])
    h_ref[...] = h_new
    hp_ref[...] = _pack_bf16_halves(h_new)
    hprev_ref[...] = h_new

    nt_dims = (((1,), (1,)), ((), ()))
    h_hi = h.astype(BF16)
    h_lo = (h - h_hi.astype(F32)).astype(BF16)
    w = wr_ref[...]
    w_hi = w.astype(BF16)
    w_lo = (w - w_hi.astype(F32)).astype(BF16)
    logits = (lax.dot_general(w_hi, h_hi, nt_dims, preferred_element_type=F32)
              + lax.dot_general(w_hi, h_lo, nt_dims, preferred_element_type=F32)
              + lax.dot_general(w_lo, h_hi, nt_dims, preferred_element_type=F32))
    scores = _sigmoid(logits)
    biased = scores + rb_ref[...]

    g3 = biased.reshape(N_GROUPS, GROUP_SIZE, tm)
    m1 = jnp.max(g3, axis=1, keepdims=True)
    is_max = g3 == m1
    n_max = jnp.sum(jnp.where(is_max, 1.0, 0.0), axis=1, keepdims=True)
    m2 = jnp.max(jnp.where(is_max, NEG_INF, g3), axis=1, keepdims=True)
    grp = (m1 + jnp.where(n_max >= 2.0, m1, m2)).reshape(N_GROUPS, tm)

    gid = lax.broadcasted_iota(jnp.int32, (N_GROUPS, tm), 0)
    rank = jnp.zeros((N_GROUPS, tm), F32)
    for o in range(N_GROUPS):
        other = grp[o:o + 1, :]
        ahead = (other > grp) | ((other == grp) & (o < gid))
        rank = rank + jnp.where(ahead, 1.0, 0.0)
    grp_keep = jnp.where(rank < float(TOPK_GROUPS), 1.0, 0.0)
    keep = jnp.broadcast_to(grp_keep.reshape(N_GROUPS, 1, tm),
                            (N_GROUPS, GROUP_SIZE, tm)).reshape(N_EXPERTS, tm)
    cand = jnp.where(keep > 0.0, biased, NEG_INF)

    eid = lax.broadcasted_iota(jnp.int32, (N_EXPERTS, tm), 0).astype(F32)
    chosen = jnp.zeros((N_EXPERTS, tm), F32)
    firsts = []
    for r in range(TOP_K):
        mx = jnp.max(cand, axis=0, keepdims=True)
        first = jnp.min(jnp.where(cand == mx, eid, float(N_EXPERTS)), axis=0, keepdims=True)
        pick = eid == first
        firsts.append(first)
        idx_ref[r:r + 1, :] = first.astype(jnp.int32)
        wgt_ref[r:r + 1, :] = jnp.sum(jnp.where(pick, scores, 0.0), axis=0, keepdims=True)
        cand = jnp.where(pick, NEG_INF, cand)
        chosen = jnp.where(pick, 1.0, chosen)
    top_s = wgt_ref[...]
    wgt_ref[...] = top_s / (jnp.sum(top_s, axis=0, keepdims=True) + 1e-20) * ROUTED_SCALE

    earlier = (lax.broadcasted_iota(jnp.int32, (tm, tm), 0)
               < lax.broadcasted_iota(jnp.int32, (tm, tm), 1))
    before = jnp.dot(chosen.astype(BF16), jnp.where(earlier, 1.0, 0.0).astype(BF16),
                     preferred_element_type=F32) + count_ref[...]
    for r in range(TOP_K):
        rank_ref[r:r + 1, :] = jnp.sum(jnp.where(eid == firsts[r], before, 0.0),
                                       axis=0, keepdims=True).astype(jnp.int32)
    counted = jnp.where(step > 0, 1.0, 0.0)
    total = count_ref[...] + counted * jnp.sum(chosen, axis=1, keepdims=True)
    count_ref[...] = total
    total_ref[...] = total


def _merge_ln_router(attn, gate_a, conv_gated, x2, w_attn_o_b, w_out_b, ln_g, ln_b, w_router_t, router_bias,
                     first_token, n):
    tm = MERGE_TILE
    first_tile = first_token // tm
    n_tiles = n // tm
    row_in = lambda i: (jnp.minimum(i, n_tiles - 1) + first_tile, 0)
    row = lambda i: (jnp.minimum(i, n_tiles - 1), 0)
    const = lambda i: (0, 0)
    col = lambda i: (0, jnp.maximum(i - 1, 0))
    return pl.pallas_call(
        _merge_kernel,
        grid=(n_tiles + 1,),
        in_specs=[
            pl.BlockSpec((tm, D_MODEL), row_in),
            pl.BlockSpec((tm, D_MODEL), row_in),
            pl.BlockSpec((tm, D_MODEL), row_in),
            pl.BlockSpec((tm, D_MODEL), row_in),
            pl.BlockSpec((D_MODEL, D_MODEL), const),
            pl.BlockSpec((D_MODEL, D_MODEL), const),
            pl.BlockSpec((1, D_MODEL), const),
            pl.BlockSpec((1, D_MODEL), const),
            pl.BlockSpec((N_EXPERTS, D_MODEL), const),
            pl.BlockSpec((N_EXPERTS, 1), const),
        ],
        out_specs=[
            pl.BlockSpec((tm, D_MODEL), row),
            pl.BlockSpec((tm, HALF), row),
            pl.BlockSpec((TOP_K, tm), col),
            pl.BlockSpec((TOP_K, tm), col),
            pl.BlockSpec((TOP_K, tm), col),
            pl.BlockSpec((N_EXPERTS, 1), const),
        ],
        out_shape=[
            jax.ShapeDtypeStruct((n, D_MODEL), F32),
            jax.ShapeDtypeStruct((n, HALF), jnp.uint32),
            jax.ShapeDtypeStruct((TOP_K, n), jnp.int32),
            jax.ShapeDtypeStruct((TOP_K, n), F32),
            jax.ShapeDtypeStruct((TOP_K, n), jnp.int32),
            jax.ShapeDtypeStruct((N_EXPERTS, 1), F32),
        ],
        scratch_shapes=[pltpu.VMEM((N_EXPERTS, 1), F32), pltpu.VMEM((tm, D_MODEL), F32)],
        compiler_params=_params("arbitrary"),
        name="merge_ln_router",
    )(attn, gate_a, conv_gated, x2, w_attn_o_b, w_out_b, ln_g, ln_b, w_router_t, router_bias)


def _sc_mesh():
    return plsc.VectorSubcoreMesh(core_axis_name="c", subcore_axis_name="s",
                                  num_cores=SC_CORES, num_subcores=SC_SUBCORES)


def _sc_worker_base(rows_per_worker):
    return (lax.axis_index("s") * SC_CORES + lax.axis_index("c")) * rows_per_worker


def _sc_gather_rows(table, idx):
    m = idx.shape[0]
    width = table.shape[1]
    per_worker = m // (SC_CORES * SC_SUBCORES)
    assert per_worker * SC_CORES * SC_SUBCORES == m and per_worker % SC_CHUNK == 0

    @functools.partial(
        pl.kernel, mesh=_sc_mesh(),
        out_type=jax.ShapeDtypeStruct((m, width), table.dtype),
        scratch_types=[pltpu.VMEM((SC_CHUNK,), jnp.int32),
                       pltpu.VMEM((SC_CHUNK, width), table.dtype),
                       pltpu.SemaphoreType.DMA],
        name="sc_gather_rows")
    def gather(table_hbm, idx_hbm, out_hbm, idx_v, rows_v, sem):
        base = _sc_worker_base(per_worker)

        @pl.loop(0, per_worker // SC_CHUNK)
        def _(c):
            off = pl.multiple_of(base + c * SC_CHUNK, SC_CHUNK)
            pltpu.sync_copy(idx_hbm.at[pl.ds(off, SC_CHUNK)], idx_v)
            pltpu.async_copy(table_hbm.at[idx_v], rows_v, sem).wait()
            pltpu.sync_copy(rows_v, out_hbm.at[pl.ds(off, SC_CHUNK)])

    return gather(table, idx)


def _sc_scatter_rows(rows, pos, n_out):
    n, width = rows.shape
    per_worker = n // (SC_CORES * SC_SUBCORES)
    assert per_worker * SC_CORES * SC_SUBCORES == n and per_worker % SC_CHUNK == 0

    @functools.partial(
        pl.kernel, mesh=_sc_mesh(),
        out_type=jax.ShapeDtypeStruct((n_out, width), rows.dtype),
        scratch_types=[pltpu.VMEM((SC_CHUNK,), jnp.int32),
                       pltpu.VMEM((SC_CHUNK, width), rows.dtype),
                       pltpu.SemaphoreType.DMA],
        name="sc_scatter_rows")
    def scatter(rows_hbm, pos_hbm, out_hbm, idx_v, rows_v, sem):
        base = _sc_worker_base(per_worker)

        @pl.loop(0, per_worker // SC_CHUNK)
        def _(c):
            off = pl.multiple_of(base + c * SC_CHUNK, SC_CHUNK)
            pltpu.sync_copy(rows_hbm.at[pl.ds(off, SC_CHUNK)], rows_v)
            for r in range(TOP_K):
                pltpu.sync_copy(pos_hbm.at[pl.ds(r * n + off, SC_CHUNK)], idx_v)
                pltpu.async_copy(rows_v, out_hbm.at[idx_v], sem).wait()

    return scatter(rows, pos)


def _position_kernel(idx_ref, rank_ref, start_ref, pos_ref):
    tl = idx_ref.shape[1]
    eid = lax.broadcasted_iota(jnp.int32, (N_EXPERTS, tl), 0)
    start = start_ref[...]
    for r in range(TOP_K):
        here = jnp.sum(jnp.where(eid == idx_ref[r:r + 1, :], start, 0.0), axis=0, keepdims=True)
        pos_ref[r:r + 1, :] = here.astype(jnp.int32) + rank_ref[r:r + 1, :]


def _positions(top_idx, rank, group_start):
    n = top_idx.shape[1]
    tl = POSITION_TILE
    col = lambda i: (0, i)
    return pl.pallas_call(
        _position_kernel,
        grid=(n // tl,),
        in_specs=[pl.BlockSpec((TOP_K, tl), col), pl.BlockSpec((TOP_K, tl), col),
                  pl.BlockSpec((N_EXPERTS, 1), lambda i: (0, 0))],
        out_specs=pl.BlockSpec((TOP_K, tl), col),
        out_shape=jax.ShapeDtypeStruct((TOP_K, n), jnp.int32),
        compiler_params=_params("parallel"),
        name="positions",
    )(top_idx, rank, group_start)


def _expert_kernel(te_ref, tv_ref, nu_ref, xs_hbm, wg_ref, wu_ref, wd_ref, y_ref, act_ref, xs_ring, ring_sems):
    i = pl.program_id(0)
    n_used = nu_ref[0]
    valid = tv_ref[i]
    valid_prev = tv_ref[jnp.maximum(i - 1, 0)]

    def tile_copy(t):
        slot = t % XS_SLOTS
        first_row = t * EXPERT_TILE if isinstance(t, int) else pl.multiple_of(t * EXPERT_TILE, EXPERT_TILE)
        return pltpu.make_async_copy(xs_hbm.at[pl.ds(first_row, EXPERT_TILE), :],
                                     xs_ring.at[slot], ring_sems.at[slot])

    @pl.when(i == 0)
    def _():
        act_ref[...] = jnp.zeros(act_ref.shape, BF16)

    for t in range(XS_SLOTS - 1):
        @pl.when((i == 0) & (t < n_used))
        def _():
            tile_copy(t).start()

    @pl.when(i + (XS_SLOTS - 1) < n_used)
    def _():
        tile_copy(i + (XS_SLOTS - 1)).start()

    @pl.when(i < n_used)
    def _():
        tile_copy(i).wait()

    xs_ref = xs_ring.at[i % XS_SLOTS]

    @pl.when((valid > 0) | (valid_prev > 0))
    def _():
        act_prev = act_ref[...]
        wd = wd_ref[0]
        for c in range(EXPERT_TILE // EXPERT_CHAIN):
            rows = slice(c * EXPERT_CHAIN, (c + 1) * EXPERT_CHAIN)
            y_ref[rows, :] = _pack_bf16_halves(jnp.dot(act_prev[rows], wd, preferred_element_type=F32))
        wg = wg_ref[0]
        wu = wu_ref[0]
        for c in range(EXPERT_TILE // EXPERT_CHAIN):
            rows = slice(c * EXPERT_CHAIN, (c + 1) * EXPERT_CHAIN)
            live = lax.broadcasted_iota(jnp.int32, (EXPERT_CHAIN, HALF), 0) < valid - c * EXPERT_CHAIN
            lo, hi = _unpack_bf16_halves(jnp.where(live, xs_ref[rows, :], jnp.uint32(0)))
            x = jnp.concatenate([lo, hi], axis=1).astype(BF16)
            g = jnp.dot(x, wg, preferred_element_type=F32)
            u = jnp.dot(x, wu, preferred_element_type=F32)
            act_ref[rows, :] = (g * _sigmoid(g) * u).astype(BF16)


def _grouped_experts(tile_expert, tile_valid, xs, wg, wu, wd):
    p = xs.shape[0]
    t = EXPERT_TILE
    n_tiles = p // t
    n_used = jnp.sum((tile_valid > 0).astype(jnp.int32)).reshape(1)
    tile_valid = jnp.concatenate([tile_valid, jnp.zeros((1,), jnp.int32)])
    cur = lambda i, nu: jnp.minimum(i, nu[0] - 1)
    prev = lambda i, nu: jnp.minimum(jnp.maximum(i - 1, 0), nu[0] - 1)
    row_in = lambda i, te, tv, nu: (cur(i, nu), 0)
    row_out = lambda i, te, tv, nu: (prev(i, nu), 0)
    expert = lambda i, te, tv, nu: (te[cur(i, nu)], 0, 0)
    expert_prev = lambda i, te, tv, nu: (te[prev(i, nu)], 0, 0)
    return pl.pallas_call(
        _expert_kernel,
        grid_spec=pltpu.PrefetchScalarGridSpec(
            num_scalar_prefetch=3,
            grid=(n_tiles + 1,),
            in_specs=[
                pl.BlockSpec(memory_space=pl.ANY),
                pl.BlockSpec((1, D_MODEL, EXPERT_HIDDEN), expert),
                pl.BlockSpec((1, D_MODEL, EXPERT_HIDDEN), expert),
                pl.BlockSpec((1, EXPERT_HIDDEN, D_MODEL), expert_prev),
            ],
            out_specs=pl.BlockSpec((t, HALF), row_out),
            scratch_shapes=[pltpu.VMEM((t, EXPERT_HIDDEN), BF16),
                            pltpu.VMEM((XS_SLOTS, t, HALF), jnp.uint32),
                            pltpu.SemaphoreType.DMA((XS_SLOTS,))],
        ),
        out_shape=jax.ShapeDtypeStruct((p, HALF), jnp.uint32),
        compiler_params=_params("arbitrary"),
        name="grouped_experts",
    )(tile_expert, tile_valid, n_used, xs, wg, wu, wd)


def _combine_kernel(h_ref, yg_ref, wt_ref, wsg_ref, wsu_ref, wsd_ref, g2_ref, b2_ref, *out_refs):
    o_ref = out_refs[-1]
    h = h_ref[...]
    hb = h.astype(BF16)
    g = jnp.dot(hb, wsg_ref[...], preferred_element_type=F32)
    u = jnp.dot(hb, wsu_ref[...], preferred_element_type=F32)
    shared = jnp.dot((g * _sigmoid(g) * u).astype(BF16), wsd_ref[...], preferred_element_type=F32)
    wt = wt_ref[...]
    r_lo = jnp.zeros((h.shape[0], HALF), F32)
    r_hi = jnp.zeros((h.shape[0], HALF), F32)
    for r in range(TOP_K):
        lo, hi = _unpack_bf16_halves(yg_ref[r])
        w = wt[:, r:r + 1]
        r_lo = r_lo + lo * w
        r_hi = r_hi + hi * w
    routed = jnp.concatenate([r_lo, r_hi], axis=1)
    o_ref[...] = _layer_norm(DEEPNORM_ALPHA * h + (shared + routed), g2_ref[...], b2_ref[...])


def _combine_ln(h, yg, w_tok, wsg_b, wsu_b, wsd_b, ln_g, ln_b, out_so_far, first_token, n_total):
    n = h.shape[0]
    tm = MERGE_TILE
    first_tile = first_token // tm
    row = lambda i: (i, 0)
    const = lambda i: (0, 0)
    hidden = wsg_b.shape[1]
    in_specs = [
        pl.BlockSpec((tm, D_MODEL), row),
        pl.BlockSpec((TOP_K, tm, HALF), lambda i: (0, i, 0)),
        pl.BlockSpec((tm, TOP_K), row),
        pl.BlockSpec((D_MODEL, hidden), const),
        pl.BlockSpec((D_MODEL, hidden), const),
        pl.BlockSpec((hidden, D_MODEL), const),
        pl.BlockSpec((1, D_MODEL), const),
        pl.BlockSpec((1, D_MODEL), const),
    ]
    args = [h, yg, w_tok, wsg_b, wsu_b, wsd_b, ln_g, ln_b]
    aliases = {}
    if out_so_far is not None:
        in_specs.append(pl.BlockSpec(memory_space=pl.ANY))
        args.append(out_so_far)
        aliases = {len(args) - 1: 0}
    return pl.pallas_call(
        _combine_kernel,
        grid=(n // tm,),
        in_specs=in_specs,
        out_specs=pl.BlockSpec((tm, D_MODEL), lambda i: (i + first_tile, 0)),
        out_shape=jax.ShapeDtypeStruct((n_total, D_MODEL), F32),
        input_output_aliases=aliases,
        compiler_params=_params("parallel"),
        name="combine_ln",
    )(*args)


def _group_layout(totals, n_tokens):
    t = EXPERT_TILE
    n_tiles = (TOP_K * n_tokens + N_EXPERTS * (t - 1)) // t
    counts = totals[:, 0].astype(jnp.int32)
    padded = ((counts + t - 1) // t) * t
    group_end = jnp.cumsum(padded)
    group_start = group_end - padded
    tile_start = jnp.arange(n_tiles, dtype=jnp.int32) * t
    tile_expert = jnp.minimum(jnp.sum((group_end[None, :] <= tile_start[:, None]).astype(jnp.int32), axis=1),
                              N_EXPERTS - 1)
    of_tile = tile_expert[:, None] == jnp.arange(N_EXPERTS, dtype=jnp.int32)[None, :]
    real_end = jnp.sum(jnp.where(of_tile, (group_start + counts)[None, :], 0), axis=1)
    tile_valid = jnp.clip(real_end - tile_start, 0, t).astype(jnp.int32)
    return group_start.astype(F32).reshape(N_EXPERTS, 1), tile_expert, tile_valid, n_tiles * t


def _rope_tables(seq):
    inv_freq = ROPE_THETA ** (-jnp.arange(0, ROPE_DIM, 2, dtype=F32) / ROPE_DIM)
    ang = jnp.arange(seq).astype(F32)[:, None] * inv_freq[None, :]
    cos, sin = jnp.cos(ang), jnp.sin(ang)
    rest = HEAD_DIM - ROPE_DIM
    zeros = jnp.zeros((seq, ROPE_HALF), F32)
    cos_t = jnp.concatenate([cos, cos, jnp.ones((seq, rest), F32)], axis=1)
    sa_t = jnp.concatenate([-sin, zeros, jnp.zeros((seq, rest), F32)], axis=1)
    sb_t = jnp.concatenate([zeros, sin, jnp.zeros((seq, rest), F32)], axis=1)
    return cos_t, sa_t, sb_t


def _layer(x2, batch, seq, w_in, b_gate, w_attn_o, w_dw, b_dw, conv_ln_g, conv_ln_b, w_conv_o, w_out,
           ln1_g, ln1_b, w_router, router_bias, w_exp_gate, w_exp_up, w_exp_down,
           w_sh_gate, w_sh_up, w_sh_down, ln2_g, ln2_b):
    n = x2.shape[0]
    row = lambda v: v.reshape(1, -1)
    cos, sa, sb = _rope_tables(seq)
    q, k, v, u, gate_a, gate_c, k_mean, w_gate_b, w_up_b, w_down_b = _in_projection(
        x2, w_in.astype(BF16), row(b_gate), cos, sa, sb, w_exp_gate, w_exp_up, w_exp_down, seq)
    w_taps = jnp.broadcast_to(w_dw.reshape(CONV_WIDTH, 1, D_MODEL), (CONV_WIDTH, SUBLANES, D_MODEL))
    conv_gated = _conv_branch(u, w_taps, row(b_dw), row(conv_ln_g),
                              row(conv_ln_b), w_conv_o.astype(BF16), gate_c, batch, seq)
    attn = _moba_attention(q, k, v, k_mean, batch, seq)
    w_attn_o_b, w_out_b, w_router_t = w_attn_o.astype(BF16), w_out.astype(BF16), w_router.T
    shared_w = (w_sh_gate.astype(BF16), w_sh_up.astype(BF16), w_sh_down.astype(BF16))
    unit = SC_CORES * SC_SUBCORES * SC_CHUNK
    assert n % unit == 0 and n >= 2 * unit and unit % MERGE_TILE == 0 and unit % POSITION_TILE == 0
    first_size = min(max(unit, round(n * FIRST_CHUNK_SHARE / unit) * unit), n - unit)
    out = None
    for first, n_chunk in ((0, first_size), (first_size, n - first_size)):
        h, h_packed, top_idx, top_w, rank, totals = _merge_ln_router(
            attn, gate_a, conv_gated, x2, w_attn_o_b, w_out_b, row(ln1_g), row(ln1_b),
            w_router_t, router_bias.reshape(N_EXPERTS, 1), first, n_chunk)
        group_start, tile_expert, tile_valid, n_rows = _group_layout(totals, n_chunk)
        pos = _positions(top_idx, rank, group_start).reshape(TOP_K * n_chunk)
        xs = _sc_scatter_rows(h_packed, pos, n_rows)
        ys = _grouped_experts(tile_expert, tile_valid, xs, w_gate_b, w_up_b, w_down_b)
        yg = _sc_gather_rows(ys, pos).reshape(TOP_K, n_chunk, HALF)
        out = _combine_ln(h, yg, top_w.T, *shared_w, row(ln2_g), row(ln2_b), out, first, n)
    return out


def kernel(x, w_in, b_gate, w_attn_o, w_dw, b_dw, conv_ln_g, conv_ln_b, w_conv_o, w_out, ln1_g, ln1_b,
           w_router, router_bias, w_exp_gate, w_exp_up, w_exp_down, w_sh_gate, w_sh_up, w_sh_down,
           ln2_g, ln2_b):
    batch, seq, d = x.shape
    assert d == D_MODEL and seq % MOBA_BLOCK == 0 and seq % TOKEN_TILE == 0
    assert w_in.shape[0] == DEPTH
    x2 = x.reshape(batch * seq, d)
    for l in range(DEPTH):
        x2 = _layer(x2, batch, seq, w_in[l], b_gate[l], w_attn_o[l], w_dw[l], b_dw[l], conv_ln_g[l],
                    conv_ln_b[l], w_conv_o[l], w_out[l], ln1_g[l], ln1_b[l], w_router[l], router_bias[l],
                    w_exp_gate[l], w_exp_up[l], w_exp_down[l], w_sh_gate[l], w_sh_up[l], w_sh_down[l],
                    ln2_g[l], ln2_b[l])
    return x2.reshape(batch, seq, d)
```

```python
import functools

import jax
import jax.numpy as jnp
from jax import lax
from jax.experimental import pallas as pl
from jax.experimental.pallas import tpu as pltpu
from jax.experimental.pallas import tpu_sc as plsc

F32 = jnp.float32
BF16 = jnp.bfloat16
NEG_INF = float("-inf")
MASKED = -1e30
LOG2_E = 1.4426950408889634

D_MODEL = 1024
N_HEADS = 8
HEAD_DIM = 128
ROPE_THETA = 500000.0
ROPE_DIM = HEAD_DIM // 4
ROPE_HALF = ROPE_DIM // 2
MOBA_BLOCK = 256
MOBA_TOPK = 3
CONV_WIDTH = 31
SUBLANES = 8
CONV_HALO = 32
CONV_ROWS = 32
N_EXPERTS = 256
TOP_K = 8
N_GROUPS = 8
GROUP_SIZE = N_EXPERTS // N_GROUPS
TOPK_GROUPS = 4
EXPERT_HIDDEN = 256
ROUTED_SCALE = 2.5
LN_EPS = 1e-5
DEPTH = 1
DEEPNORM_ALPHA = (2 * DEPTH) ** 0.25
HALF = D_MODEL // 2

TOKEN_TILE = 256
MERGE_TILE = 512
EXPERT_TILE = 512
EXPERT_CHAIN = 256
XS_SLOTS = 4
POSITION_TILE = 1024
FIRST_CHUNK_SHARE = 9 / 16
HEADS_PER_STEP = 4
SC_CORES = 2
SC_SUBCORES = 16
SC_CHUNK = 64
VMEM_LIMIT = 56 * 1024 * 1024


def _params(*semantics):
    return pltpu.CompilerParams(dimension_semantics=semantics, vmem_limit_bytes=VMEM_LIMIT)


def _sigmoid(x):
    return 1.0 / (1.0 + jnp.exp(-x))


def _layer_norm(x, g, b):
    mu = jnp.mean(x, axis=-1, keepdims=True)
    xc = x - mu
    var = jnp.mean(xc * xc, axis=-1, keepdims=True)
    return xc * lax.rsqrt(var + LN_EPS) * g + b


def _pack_bf16_halves(y):
    lo = lax.bitcast_convert_type(y[:, :HALF].astype(BF16).astype(F32), jnp.uint32)
    hi = lax.bitcast_convert_type(y[:, HALF:].astype(BF16).astype(F32), jnp.uint32)
    return (hi & jnp.uint32(0xFFFF0000)) | (lo >> 16)


def _unpack_bf16_halves(p):
    lo = lax.bitcast_convert_type(p << 16, F32)
    hi = lax.bitcast_convert_type(p & jnp.uint32(0xFFFF0000), F32)
    return lo, hi


def _inproj_kernel(x_ref, w_ref, bg_ref, cos_ref, sa_ref, sb_ref, wg_ref, wu_ref, wd_ref,
                   q_ref, k_ref, v_ref, u_ref, ga_ref, gc_ref, km_ref, wgb_ref, wub_ref, wdb_ref):
    wgb_ref[...] = wg_ref[...].astype(BF16)
    wub_ref[...] = wu_ref[...].astype(BF16)
    wdb_ref[...] = wd_ref[...].astype(BF16)

    tm = x_ref.shape[0]
    xb = x_ref[...].astype(BF16)

    def proj(c):
        return jnp.dot(xb, w_ref[:, c * D_MODEL:(c + 1) * D_MODEL], preferred_element_type=F32)

    cos = cos_ref[...]
    sa = sa_ref[...]
    sb = sb_ref[...]

    def rope_head(t):
        return (t * cos + pltpu.roll(t, HEAD_DIM - ROPE_HALF, 1) * sa
                + pltpu.roll(t, ROPE_HALF, 1) * sb)

    q = proj(0)
    for h in range(N_HEADS):
        sl = slice(h * HEAD_DIM, (h + 1) * HEAD_DIM)
        q_ref[:, sl] = rope_head(q[:, sl]).astype(BF16)
    k = proj(1)
    for h in range(N_HEADS):
        sl = slice(h * HEAD_DIM, (h + 1) * HEAD_DIM)
        kr = rope_head(k[:, sl])
        k_ref[:, sl] = kr.astype(BF16)
        for g in range(tm // MOBA_BLOCK):
            km_ref[g, :, sl] = jnp.mean(kr[g * MOBA_BLOCK:(g + 1) * MOBA_BLOCK], axis=0, keepdims=True)
    v_ref[...] = proj(2).astype(BF16)
    u_ref[...] = proj(3) * _sigmoid(proj(4))
    ga_ref[...] = _sigmoid(proj(5) + bg_ref[:, :D_MODEL]).astype(BF16)
    gc_ref[...] = _sigmoid(proj(6) + bg_ref[:, D_MODEL:]).astype(BF16)


def _in_projection(x2, w_in_b, b_gate, cos, sa, sb, w_exp_gate, w_exp_up, w_exp_down, seq):
    n = x2.shape[0]
    tm = TOKEN_TILE
    n_cols = w_in_b.shape[1]
    tiles_per_seq = seq // tm
    steps = n // tm
    per_step = -(-N_EXPERTS // steps)
    assert N_EXPERTS % per_step == 0
    row = lambda i: (i, 0)
    const = lambda i: (0, 0)
    pos = lambda i: (i % tiles_per_seq, 0)
    experts = lambda i: (jnp.minimum(i, N_EXPERTS // per_step - 1), 0, 0)
    tok_bf16 = jax.ShapeDtypeStruct((n, D_MODEL), BF16)
    up_block = (per_step, D_MODEL, EXPERT_HIDDEN)
    down_block = (per_step, EXPERT_HIDDEN, D_MODEL)
    return pl.pallas_call(
        _inproj_kernel,
        grid=(steps,),
        in_specs=[
            pl.BlockSpec((tm, D_MODEL), row),
            pl.BlockSpec((D_MODEL, n_cols), const),
            pl.BlockSpec((1, 2 * D_MODEL), const),
            pl.BlockSpec((tm, HEAD_DIM), pos),
            pl.BlockSpec((tm, HEAD_DIM), pos),
            pl.BlockSpec((tm, HEAD_DIM), pos),
            pl.BlockSpec(up_block, experts),
            pl.BlockSpec(up_block, experts),
            pl.BlockSpec(down_block, experts),
        ],
        out_specs=[
            pl.BlockSpec((tm, D_MODEL), row),
            pl.BlockSpec((tm, D_MODEL), row),
            pl.BlockSpec((tm, D_MODEL), row),
            pl.BlockSpec((tm, D_MODEL), row),
            pl.BlockSpec((tm, D_MODEL), row),
            pl.BlockSpec((tm, D_MODEL), row),
            pl.BlockSpec((tm // MOBA_BLOCK, 1, D_MODEL), lambda i: (i, 0, 0)),
            pl.BlockSpec(up_block, experts),
            pl.BlockSpec(up_block, experts),
            pl.BlockSpec(down_block, experts),
        ],
        out_shape=[tok_bf16, tok_bf16, tok_bf16,
                   jax.ShapeDtypeStruct((n, D_MODEL), F32),
                   tok_bf16, tok_bf16,
                   jax.ShapeDtypeStruct((n // MOBA_BLOCK, 1, D_MODEL), F32),
                   jax.ShapeDtypeStruct(w_exp_gate.shape, BF16),
                   jax.ShapeDtypeStruct(w_exp_up.shape, BF16),
                   jax.ShapeDtypeStruct(w_exp_down.shape, BF16)],
        compiler_params=_params("arbitrary"),
        name="in_projection",
    )(x2, w_in_b, b_gate, cos, sa, sb, w_exp_gate, w_exp_up, w_exp_down)


def _conv_kernel(u_ref, wdw_ref, bdw_ref, lng_ref, lnb_ref, wo_ref, gc_ref, o_ref, buf_ref, sh_ref, y_ref):
    ts = u_ref.shape[0]
    s = pl.program_id(1)

    @pl.when(s == 0)
    def _():
        buf_ref[0:CONV_HALO, :] = jnp.zeros((CONV_HALO, D_MODEL), F32)

    @pl.when(s > 0)
    def _():
        buf_ref[0:CONV_HALO, :] = buf_ref[ts:ts + CONV_HALO, :]

    buf_ref[CONV_HALO:CONV_HALO + ts, :] = u_ref[...]

    span = ts + CONV_HALO - SUBLANES
    for b in range(1, SUBLANES):
        sh_ref[b - 1, 0:span, :] = buf_ref[b:b + span, :]

    base = CONV_HALO - (CONV_WIDTH - 1)
    for c in range(ts // CONV_ROWS):
        r0 = c * CONV_ROWS
        acc = jnp.zeros((CONV_ROWS // SUBLANES, SUBLANES, D_MODEL), F32)
        for j in range(CONV_WIDTH):
            shift = (base + j) % SUBLANES
            row = r0 + base + j - shift
            src = buf_ref if shift == 0 else sh_ref.at[shift - 1]
            tap = src[row:row + CONV_ROWS, :].reshape(CONV_ROWS // SUBLANES, SUBLANES, D_MODEL)
            acc = acc + tap * wdw_ref[j]
        acc = acc.reshape(CONV_ROWS, D_MODEL)
        y = _layer_norm(acc + bdw_ref[...], lng_ref[...], lnb_ref[...])
        y_ref[r0:r0 + CONV_ROWS, :] = (y * _sigmoid(y)).astype(BF16)
    z = jnp.dot(y_ref[...], wo_ref[...], preferred_element_type=F32)
    o_ref[...] = (z * gc_ref[...].astype(F32)).astype(BF16)


def _conv_branch(u, w_dw, b_dw, ln_g, ln_b, w_o_b, gate_c, batch, seq):
    n = u.shape[0]
    ts = TOKEN_TILE
    tiles_per_seq = seq // ts
    row = lambda b, s: (b * tiles_per_seq + s, 0)
    const = lambda b, s: (0, 0)
    return pl.pallas_call(
        _conv_kernel,
        grid=(batch, tiles_per_seq),
        in_specs=[
            pl.BlockSpec((ts, D_MODEL), row),
            pl.BlockSpec((CONV_WIDTH, SUBLANES, D_MODEL), lambda b, s: (0, 0, 0)),
            pl.BlockSpec((1, D_MODEL), const),
            pl.BlockSpec((1, D_MODEL), const),
            pl.BlockSpec((1, D_MODEL), const),
            pl.BlockSpec((D_MODEL, D_MODEL), const),
            pl.BlockSpec((ts, D_MODEL), row),
        ],
        out_specs=pl.BlockSpec((ts, D_MODEL), row),
        out_shape=jax.ShapeDtypeStruct((n, D_MODEL), BF16),
        scratch_shapes=[pltpu.VMEM((ts + CONV_HALO, D_MODEL), F32),
                        pltpu.VMEM((SUBLANES - 1, ts + CONV_HALO - SUBLANES, D_MODEL), F32),
                        pltpu.VMEM((ts, D_MODEL), BF16)],
        compiler_params=_params("parallel", "arbitrary"),
        name="conv_branch",
    )(u, w_dw, b_dw, ln_g, ln_b, w_o_b, gate_c)


def _attn_kernel(q_ref, k_ref, v_ref, km_ref, o_ref, *, n_blk):
    blk = MOBA_BLOCK
    seq = n_blk * blk
    k_sel = min(MOBA_TOPK, n_blk)
    exp2_scale = HEAD_DIM ** -0.5 * LOG2_E
    nt_dims = (((1,), (1,)), ((), ()))
    n_lane = HEAD_DIM

    assert blk & (blk - 1) == 0
    blk_shift = blk.bit_length() - 1
    n_sub = -(-n_blk // SUBLANES) * SUBLANES

    blk_t = lax.broadcasted_iota(jnp.int32, (n_sub, seq), 0)
    own_t = lax.broadcasted_iota(jnp.int32, (n_sub, seq), 1) >> blk_shift
    past = blk_t < own_t
    blk_f = blk_t.astype(F32)
    blk_id = lax.broadcasted_iota(jnp.int32, (seq, n_lane), 1)
    own_id = lax.broadcasted_iota(jnp.int32, (seq, n_lane), 0) >> blk_shift
    k_blk = jnp.where(blk_id == own_id, 1.0, 0.0).astype(BF16)
    ones = jnp.ones((seq, n_lane), BF16)
    causal = (lax.broadcasted_iota(jnp.int32, (blk, blk), 1)
              <= lax.broadcasted_iota(jnp.int32, (blk, blk), 0))

    def prepare(lanes):
        q_all = q_ref[:, lanes]
        km = km_ref[:, 0, lanes]
        if n_sub > n_blk:
            km = jnp.concatenate([km, jnp.zeros((n_sub - n_blk, HEAD_DIM), F32)], axis=0)
        km_hi = km.astype(BF16)
        km_lo = (km - km_hi.astype(F32)).astype(BF16)
        gate = (lax.dot_general(km_hi, q_all, nt_dims, preferred_element_type=F32)
                + lax.dot_general(km_lo, q_all, nt_dims, preferred_element_type=F32))
        g = jnp.where(past, gate, NEG_INF)
        sel = jnp.zeros((n_sub, seq), F32)
        for _ in range(k_sel):
            mx = jnp.max(g, axis=0, keepdims=True)
            first = jnp.min(jnp.where(g == mx, blk_f, float(n_sub)), axis=0, keepdims=True)
            pick = blk_f == first
            sel = jnp.where(pick, 1.0, sel)
            g = jnp.where(pick, NEG_INF, g)
        visible = ((sel > 0.0) & past) | (blk_t == own_t)
        bias_t = jnp.concatenate([jnp.where(visible, 0.0, MASKED), jnp.zeros((n_lane - n_sub, seq), F32)],
                                 axis=0)
        q_bias = bias_t.T.astype(BF16)
        k_aug = jnp.concatenate([k_ref[:, lanes], k_blk], axis=1)
        v_aug = jnp.concatenate([v_ref[:, lanes], ones], axis=1)
        return q_all, q_bias, k_aug, v_aug, lanes

    def scores(head, i):
        q_all, q_bias, k_aug, _, _ = head
        rows = slice(i * blk, (i + 1) * blk)
        q_aug = jnp.concatenate([q_all[rows], q_bias[rows]], axis=1)
        return lax.dot_general(q_aug, k_aug[:(i + 1) * blk], nt_dims, preferred_element_type=F32)

    def finish(head, i, p):
        pv = jnp.dot(p, head[3][:(i + 1) * blk], preferred_element_type=F32)
        o_ref[i * blk:(i + 1) * blk, head[4]] = (pv[:, :HEAD_DIM] / pv[:, HEAD_DIM:HEAD_DIM + 1]).astype(BF16)

    def softmax_numerator(raw, i):
        own = jnp.where(causal, raw[:, i * blk:], MASKED)
        parts = [raw[:, :i * blk], own] if i else [own]
        m = jnp.max(own, axis=1, keepdims=True)
        if i:
            m = jnp.maximum(m, jnp.max(parts[0], axis=1, keepdims=True))
        return jnp.concatenate([jnp.exp2((t - m) * exp2_scale) for t in parts], axis=1).astype(BF16)

    heads = [prepare(slice(hh * HEAD_DIM, (hh + 1) * HEAD_DIM)) for hh in range(HEADS_PER_STEP)]
    raw_next = [scores(head, 0) for head in heads]
    p_prev = None
    for i in range(n_blk):
        raws = raw_next
        if i + 1 < n_blk:
            raw_next = [scores(head, i + 1) for head in heads]
        if p_prev is not None:
            for head, p in zip(heads, p_prev):
                finish(head, i - 1, p)
        p_prev = [softmax_numerator(raw, i) for raw in raws]
    for head, p in zip(heads, p_prev):
        finish(head, n_blk - 1, p)


def _moba_attention(q, k, v, k_mean, batch, seq):
    n = q.shape[0]
    n_blk = seq // MOBA_BLOCK
    width = HEADS_PER_STEP * HEAD_DIM
    seq_head = lambda b, h: (b, h)
    return pl.pallas_call(
        functools.partial(_attn_kernel, n_blk=n_blk),
        grid=(batch, N_HEADS // HEADS_PER_STEP),
        in_specs=[
            pl.BlockSpec((seq, width), seq_head),
            pl.BlockSpec((seq, width), seq_head),
            pl.BlockSpec((seq, width), seq_head),
            pl.BlockSpec((n_blk, 1, width), lambda b, h: (b, 0, h)),
        ],
        out_specs=pl.BlockSpec((seq, width), seq_head),
        out_shape=jax.ShapeDtypeStruct((n, D_MODEL), BF16),
        compiler_params=_params("parallel", "parallel"),
        name="moba_attention",
    )(q, k, v, k_mean)


def _merge_kernel(attn_ref, ga_ref, cg_ref, x_ref, wao_ref, wout_ref, g1_ref, b1_ref,
                  wr_ref, rb_ref, h_ref, hp_ref, idx_ref, wgt_ref, rank_ref, total_ref, count_ref, hprev_ref):
    tm = x_ref.shape[0]
    step = pl.program_id(0)

    @pl.when(step == 0)
    def _():
        count_ref[...] = jnp.zeros((N_EXPERTS, 1), F32)
        hprev_ref[...] = jnp.zeros(hprev_ref.shape, F32)

    h = hprev_ref[...]

    a = jnp.dot(attn_ref[...], wao_ref[...], preferred_element_type=F32)
    merged = ga_ref[...].astype(F32) * a + cg_ref[...].astype(F32)
    y = jnp.dot(merged.astype(BF16), wout_ref[...], preferred_element_type=F32)
    h_new = _layer_norm(DEEPNORM_ALPHA * x_ref[...] + y, g1_ref[...], b1_ref[...])
    h_ref[...] = h_new
    hp_ref[...] = _pack_bf16_halves(h_new)
    hprev_ref[...] = h_new

    nt_dims = (((1,), (1,)), ((), ()))
    h_hi = h.astype(BF16)
    h_lo = (h - h_hi.astype(F32)).astype(BF16)
    w = wr_ref[...]
    w_hi = w.astype(BF16)
    w_lo = (w - w_hi.astype(F32)).astype(BF16)
    logits = (lax.dot_general(w_hi, h_hi, nt_dims, preferred_element_type=F32)
              + lax.dot_general(w_hi, h_lo, nt_dims, preferred_element_type=F32)
              + lax.dot_general(w_lo, h_hi, nt_dims, preferred_element_type=F32))
    scores = _sigmoid(logits)
    biased = scores + rb_ref[...]

    g3 = biased.reshape(N_GROUPS, GROUP_SIZE, tm)
    m1 = jnp.max(g3, axis=1, keepdims=True)
    is_max = g3 == m1
    n_max = jnp.sum(jnp.where(is_max, 1.0, 0.0), axis=1, keepdims=True)
    m2 = jnp.max(jnp.where(is_max, NEG_INF, g3), axis=1, keepdims=True)
    grp = (m1 + jnp.where(n_max >= 2.0, m1, m2)).reshape(N_GROUPS, tm)

    gid = lax.broadcasted_iota(jnp.int32, (N_GROUPS, tm), 0)
    rank = jnp.zeros((N_GROUPS, tm), F32)
    for o in range(N_GROUPS):
        other = grp[o:o + 1, :]
        ahead = (other > grp) | ((other == grp) & (o < gid))
        rank = rank + jnp.where(ahead, 1.0, 0.0)
    grp_keep = jnp.where(rank < float(TOPK_GROUPS), 1.0, 0.0)
    keep = jnp.broadcast_to(grp_keep.reshape(N_GROUPS, 1, tm),
                            (N_GROUPS, GROUP_SIZE, tm)).reshape(N_EXPERTS, tm)
    cand = jnp.where(keep > 0.0, biased, NEG_INF)

    eid = lax.broadcasted_iota(jnp.int32, (N_EXPERTS, tm), 0).astype(F32)
    chosen = jnp.zeros((N_EXPERTS, tm), F32)
    firsts = []
    for r in range(TOP_K):
        mx = jnp.max(cand, axis=0, keepdims=True)
        first = jnp.min(jnp.where(cand == mx, eid, float(N_EXPERTS)), axis=0, keepdims=True)
        pick = eid == first
        firsts.append(first)
        idx_ref[r:r + 1, :] = first.astype(jnp.int32)
        wgt_ref[r:r + 1, :] = jnp.sum(jnp.where(pick, scores, 0.0), axis=0, keepdims=True)
        cand = jnp.where(pick, NEG_INF, cand)
        chosen = jnp.where(pick, 1.0, chosen)
    top_s = wgt_ref[...]
    wgt_ref[...] = top_s / (jnp.sum(top_s, axis=0, keepdims=True) + 1e-20) * ROUTED_SCALE

    earlier = (lax.broadcasted_iota(jnp.int32, (tm, tm), 0)
               < lax.broadcasted_iota(jnp.int32, (tm, tm), 1))
    before = jnp.dot(chosen.astype(BF16), jnp.where(earlier, 1.0, 0.0).astype(BF16),
                     preferred_element_type=F32) + count_ref[...]
    for r in range(TOP_K):
        rank_ref[r:r + 1, :] = jnp.sum(jnp.where(eid == firsts[r], before, 0.0),
                                       axis=0, keepdims=True).astype(jnp.int32)
    counted = jnp.where(step > 0, 1.0, 0.0)
    total = count_ref[...] + counted * jnp.sum(chosen, axis=1, keepdims=True)
    count_ref[...] = total
    total_ref[...] = total


def _merge_ln_router(attn, gate_a, conv_gated, x2, w_attn_o_b, w_out_b, ln_g, ln_b, w_router_t, router_bias,
                     first_token, n):
    tm = MERGE_TILE
    first_tile = first_token // tm
    n_tiles = n // tm
    row_in = lambda i: (jnp.minimum(i, n_tiles - 1) + first_tile, 0)
    row = lambda i: (jnp.minimum(i, n_tiles - 1), 0)
    const = lambda i: (0, 0)
    col = lambda i: (0, jnp.maximum(i - 1, 0))
    return pl.pallas_call(
        _merge_kernel,
        grid=(n_tiles + 1,),
        in_specs=[
            pl.BlockSpec((tm, D_MODEL), row_in),
            pl.BlockSpec((tm, D_MODEL), row_in),
            pl.BlockSpec((tm, D_MODEL), row_in),
            pl.BlockSpec((tm, D_MODEL), row_in),
            pl.BlockSpec((D_MODEL, D_MODEL), const),
            pl.BlockSpec((D_MODEL, D_MODEL), const),
            pl.BlockSpec((1, D_MODEL), const),
            pl.BlockSpec((1, D_MODEL), const),
            pl.BlockSpec((N_EXPERTS, D_MODEL), const),
            pl.BlockSpec((N_EXPERTS, 1), const),
        ],
        out_specs=[
            pl.BlockSpec((tm, D_MODEL), row),
            pl.BlockSpec((tm, HALF), row),
            pl.BlockSpec((TOP_K, tm), col),
            pl.BlockSpec((TOP_K, tm), col),
            pl.BlockSpec((TOP_K, tm), col),
            pl.BlockSpec((N_EXPERTS, 1), const),
        ],
        out_shape=[
            jax.ShapeDtypeStruct((n, D_MODEL), F32),
            jax.ShapeDtypeStruct((n, HALF), jnp.uint32),
            jax.ShapeDtypeStruct((TOP_K, n), jnp.int32),
            jax.ShapeDtypeStruct((TOP_K, n), F32),
            jax.ShapeDtypeStruct((TOP_K, n), jnp.int32),
            jax.ShapeDtypeStruct((N_EXPERTS, 1), F32),
        ],
        scratch_shapes=[pltpu.VMEM((N_EXPERTS, 1), F32), pltpu.VMEM((tm, D_MODEL), F32)],
        compiler_params=_params("arbitrary"),
        name="merge_ln_router",
    )(attn, gate_a, conv_gated, x2, w_attn_o_b, w_out_b, ln_g, ln_b, w_router_t, router_bias)


def _sc_mesh():
    return plsc.VectorSubcoreMesh(core_axis_name="c", subcore_axis_name="s",
                                  num_cores=SC_CORES, num_subcores=SC_SUBCORES)


def _sc_worker_base(rows_per_worker):
    return (lax.axis_index("s") * SC_CORES + lax.axis_index("c")) * rows_per_worker


def _sc_gather_rows(table, idx):
    m = idx.shape[0]
    width = table.shape[1]
    per_worker = m // (SC_CORES * SC_SUBCORES)
    assert per_worker * SC_CORES * SC_SUBCORES == m and per_worker % SC_CHUNK == 0

    @functools.partial(
        pl.kernel, mesh=_sc_mesh(),
        out_type=jax.ShapeDtypeStruct((m, width), table.dtype),
        scratch_types=[pltpu.VMEM((SC_CHUNK,), jnp.int32),
                       pltpu.VMEM((SC_CHUNK, width), table.dtype),
                       pltpu.SemaphoreType.DMA],
        name="sc_gather_rows")
    def gather(table_hbm, idx_hbm, out_hbm, idx_v, rows_v, sem):
        base = _sc_worker_base(per_worker)

        @pl.loop(0, per_worker // SC_CHUNK)
        def _(c):
            off = pl.multiple_of(base + c * SC_CHUNK, SC_CHUNK)
            pltpu.sync_copy(idx_hbm.at[pl.ds(off, SC_CHUNK)], idx_v)
            pltpu.async_copy(table_hbm.at[idx_v], rows_v, sem).wait()
            pltpu.sync_copy(rows_v, out_hbm.at[pl.ds(off, SC_CHUNK)])

    return gather(table, idx)


def _sc_scatter_rows(rows, pos, n_out):
    n, width = rows.shape
    per_worker = n // (SC_CORES * SC_SUBCORES)
    assert per_worker * SC_CORES * SC_SUBCORES == n and per_worker % SC_CHUNK == 0

    @functools.partial(
        pl.kernel, mesh=_sc_mesh(),
        out_type=jax.ShapeDtypeStruct((n_out, width), rows.dtype),
        scratch_types=[pltpu.VMEM((SC_CHUNK,), jnp.int32),
                       pltpu.VMEM((SC_CHUNK, width), rows.dtype),
                       pltpu.SemaphoreType.DMA],
        name="sc_scatter_rows")
    def scatter(rows_hbm, pos_hbm, out_hbm, idx_v, rows_v, sem):
        base = _sc_worker_base(per_worker)

        @pl.loop(0, per_worker // SC_CHUNK)
        def _(c):
            off = pl.multiple_of(base + c * SC_CHUNK, SC_CHUNK)
            pltpu.sync_copy(rows_hbm.at[pl.ds(off, SC_CHUNK)], rows_v)
            for r in range(TOP_K):
                pltpu.sync_copy(pos_hbm.at[pl.ds(r * n + off, SC_CHUNK)], idx_v)
                pltpu.async_copy(rows_v, out_hbm.at[idx_v], sem).wait()

    return scatter(rows, pos)


def _position_kernel(idx_ref, rank_ref, start_ref, pos_ref):
    tl = idx_ref.shape[1]
    eid = lax.broadcasted_iota(jnp.int32, (N_EXPERTS, tl), 0)
    start = start_ref[...]
    for r in range(TOP_K):
        here = jnp.sum(jnp.where(eid == idx_ref[r:r + 1, :], start, 0.0), axis=0, keepdims=True)
        pos_ref[r:r + 1, :] = here.astype(jnp.int32) + rank_ref[r:r + 1, :]


def _positions(top_idx, rank, group_start):
    n = top_idx.shape[1]
    tl = POSITION_TILE
    col = lambda i: (0, i)
    return pl.pallas_call(
        _position_kernel,
        grid=(n // tl,),
        in_specs=[pl.BlockSpec((TOP_K, tl), col), pl.BlockSpec((TOP_K, tl), col),
                  pl.BlockSpec((N_EXPERTS, 1), lambda i: (0, 0))],
        out_specs=pl.BlockSpec((TOP_K, tl), col),
        out_shape=jax.ShapeDtypeStruct((TOP_K, n), jnp.int32),
        compiler_params=_params("parallel"),
        name="positions",
    )(top_idx, rank, group_start)


def _expert_kernel(te_ref, tv_ref, nu_ref, xs_hbm, wg_ref, wu_ref, wd_ref, y_ref, act_ref, xs_ring, ring_sems):
    i = pl.program_id(0)
    n_used = nu_ref[0]
    valid = tv_ref[i]
    valid_prev = tv_ref[jnp.maximum(i - 1, 0)]

    def tile_copy(t):
        slot = t % XS_SLOTS
        first_row = t * EXPERT_TILE if isinstance(t, int) else pl.multiple_of(t * EXPERT_TILE, EXPERT_TILE)
        return pltpu.make_async_copy(xs_hbm.at[pl.ds(first_row, EXPERT_TILE), :],
                                     xs_ring.at[slot], ring_sems.at[slot])

    @pl.when(i == 0)
    def _():
        act_ref[...] = jnp.zeros(act_ref.shape, BF16)

    for t in range(XS_SLOTS - 1):
        @pl.when((i == 0) & (t < n_used))
        def _():
            tile_copy(t).start()

    @pl.when(i + (XS_SLOTS - 1) < n_used)
    def _():
        tile_copy(i + (XS_SLOTS - 1)).start()

    @pl.when(i < n_used)
    def _():
        tile_copy(i).wait()

    xs_ref = xs_ring.at[i % XS_SLOTS]

    @pl.when((valid > 0) | (valid_prev > 0))
    def _():
        act_prev = act_ref[...]
        wd = wd_ref[0]
        for c in range(EXPERT_TILE // EXPERT_CHAIN):
            rows = slice(c * EXPERT_CHAIN, (c + 1) * EXPERT_CHAIN)
            y_ref[rows, :] = _pack_bf16_halves(jnp.dot(act_prev[rows], wd, preferred_element_type=F32))
        wg = wg_ref[0]
        wu = wu_ref[0]
        for c in range(EXPERT_TILE // EXPERT_CHAIN):
            rows = slice(c * EXPERT_CHAIN, (c + 1) * EXPERT_CHAIN)
            live = lax.broadcasted_iota(jnp.int32, (EXPERT_CHAIN, HALF), 0) < valid - c * EXPERT_CHAIN
            lo, hi = _unpack_bf16_halves(jnp.where(live, xs_ref[rows, :], jnp.uint32(0)))
            x = jnp.concatenate([lo, hi], axis=1).astype(BF16)
            g = jnp.dot(x, wg, preferred_element_type=F32)
            u = jnp.dot(x, wu, preferred_element_type=F32)
            act_ref[rows, :] = (g * _sigmoid(g) * u).astype(BF16)


def _grouped_experts(tile_expert, tile_valid, xs, wg, wu, wd):
    p = xs.shape[0]
    t = EXPERT_TILE
    n_tiles = p // t
    n_used = jnp.sum((tile_valid > 0).astype(jnp.int32)).reshape(1)
    tile_valid = jnp.concatenate([tile_valid, jnp.zeros((1,), jnp.int32)])
    cur = lambda i, nu: jnp.minimum(i, nu[0] - 1)
    prev = lambda i, nu: jnp.minimum(jnp.maximum(i - 1, 0), nu[0] - 1)
    row_in = lambda i, te, tv, nu: (cur(i, nu), 0)
    row_out = lambda i, te, tv, nu: (prev(i, nu), 0)
    expert = lambda i, te, tv, nu: (te[cur(i, nu)], 0, 0)
    expert_prev = lambda i, te, tv, nu: (te[prev(i, nu)], 0, 0)
    return pl.pallas_call(
        _expert_kernel,
        grid_spec=pltpu.PrefetchScalarGridSpec(
            num_scalar_prefetch=3,
            grid=(n_tiles + 1,),
            in_specs=[
                pl.BlockSpec(memory_space=pl.ANY),
                pl.BlockSpec((1, D_MODEL, EXPERT_HIDDEN), expert),
                pl.BlockSpec((1, D_MODEL, EXPERT_HIDDEN), expert),
                pl.BlockSpec((1, EXPERT_HIDDEN, D_MODEL), expert_prev),
            ],
            out_specs=pl.BlockSpec((t, HALF), row_out),
            scratch_shapes=[pltpu.VMEM((t, EXPERT_HIDDEN), BF16),
                            pltpu.VMEM((XS_SLOTS, t, HALF), jnp.uint32),
                            pltpu.SemaphoreType.DMA((XS_SLOTS,))],
        ),
        out_shape=jax.ShapeDtypeStruct((p, HALF), jnp.uint32),
        compiler_params=_params("arbitrary"),
        name="grouped_experts",
    )(tile_expert, tile_valid, n_used, xs, wg, wu, wd)


def _combine_kernel(h_ref, yg_ref, wt_ref, wsg_ref, wsu_ref, wsd_ref, g2_ref, b2_ref, *out_refs):
    o_ref = out_refs[-1]
    h = h_ref[...]
    hb = h.astype(BF16)
    g = jnp.dot(hb, wsg_ref[...], preferred_element_type=F32)
    u = jnp.dot(hb, wsu_ref[...], preferred_element_type=F32)
    shared = jnp.dot((g * _sigmoid(g) * u).astype(BF16), wsd_ref[...], preferred_element_type=F32)
    wt = wt_ref[...]
    r_lo = jnp.zeros((h.shape[0], HALF), F32)
    r_hi = jnp.zeros((h.shape[0], HALF), F32)
    for r in range(TOP_K):
        lo, hi = _unpack_bf16_halves(yg_ref[r])
        w = wt[:, r:r + 1]
        r_lo = r_lo + lo * w
        r_hi = r_hi + hi * w
    routed = jnp.concatenate([r_lo, r_hi], axis=1)
    o_ref[...] = _layer_norm(DEEPNORM_ALPHA * h + (shared + routed), g2_ref[...], b2_ref[...])


def _combine_ln(h, yg, w_tok, wsg_b, wsu_b, wsd_b, ln_g, ln_b, out_so_far, first_token, n_total):
    n = h.shape[0]
    tm = MERGE_TILE
    first_tile = first_token // tm
    row = lambda i: (i, 0)
    const = lambda i: (0, 0)
    hidden = wsg_b.shape[1]
    in_specs = [
        pl.BlockSpec((tm, D_MODEL), row),
        pl.BlockSpec((TOP_K, tm, HALF), lambda i: (0, i, 0)),
        pl.BlockSpec((tm, TOP_K), row),
        pl.BlockSpec((D_MODEL, hidden), const),
        pl.BlockSpec((D_MODEL, hidden), const),
        pl.BlockSpec((hidden, D_MODEL), const),
        pl.BlockSpec((1, D_MODEL), const),
        pl.BlockSpec((1, D_MODEL), const),
    ]
    args = [h, yg, w_tok, wsg_b, wsu_b, wsd_b, ln_g, ln_b]
    aliases = {}
    if out_so_far is not None:
        in_specs.append(pl.BlockSpec(memory_space=pl.ANY))
        args.append(out_so_far)
        aliases = {len(args) - 1: 0}
    return pl.pallas_call(
        _combine_kernel,
        grid=(n // tm,),
        in_specs=in_specs,
        out_specs=pl.BlockSpec((tm, D_MODEL), lambda i: (i + first_tile, 0)),
        out_shape=jax.ShapeDtypeStruct((n_total, D_MODEL), F32),
        input_output_aliases=aliases,
        compiler_params=_params("parallel"),
        name="combine_ln",
    )(*args)


def _group_layout(totals, n_tokens):
    t = EXPERT_TILE
    n_tiles = (TOP_K * n_tokens + N_EXPERTS * (t - 1)) // t
    counts = totals[:, 0].astype(jnp.int32)
    padded = ((counts + t - 1) // t) * t
    group_end = jnp.cumsum(padded)
    group_start = group_end - padded
    tile_start = jnp.arange(n_tiles, dtype=jnp.int32) * t
    tile_expert = jnp.minimum(jnp.sum((group_end[None, :] <= tile_start[:, None]).astype(jnp.int32), axis=1),
                              N_EXPERTS - 1)
    of_tile = tile_expert[:, None] == jnp.arange(N_EXPERTS, dtype=jnp.int32)[None, :]
    real_end = jnp.sum(jnp.where(of_tile, (group_start + counts)[None, :], 0), axis=1)
    tile_valid = jnp.clip(real_end - tile_start, 0, t).astype(jnp.int32)
    return group_start.astype(F32).reshape(N_EXPERTS, 1), tile_expert, tile_valid, n_tiles * t


def _rope_tables(seq):
    inv_freq = ROPE_THETA ** (-jnp.arange(0, ROPE_DIM, 2, dtype=F32) / ROPE_DIM)
    ang = jnp.arange(seq).astype(F32)[:, None] * inv_freq[None, :]
    cos, sin = jnp.cos(ang), jnp.sin(ang)
    rest = HEAD_DIM - ROPE_DIM
    zeros = jnp.zeros((seq, ROPE_HALF), F32)
    cos_t = jnp.concatenate([cos, cos, jnp.ones((seq, rest), F32)], axis=1)
    sa_t = jnp.concatenate([-sin, zeros, jnp.zeros((seq, rest), F32)], axis=1)
    sb_t = jnp.concatenate([zeros, sin, jnp.zeros((seq, rest), F32)], axis=1)
    return cos_t, sa_t, sb_t


def _layer(x2, batch, seq, w_in, b_gate, w_attn_o, w_dw, b_dw, conv_ln_g, conv_ln_b, w_conv_o, w_out,
           ln1_g, ln1_b, w_router, router_bias, w_exp_gate, w_exp_up, w_exp_down,
           w_sh_gate, w_sh_up, w_sh_down, ln2_g, ln2_b):
    n = x2.shape[0]
    row = lambda v: v.reshape(1, -1)
    cos, sa, sb = _rope_tables(seq)
    q, k, v, u, gate_a, gate_c, k_mean, w_gate_b, w_up_b, w_down_b = _in_projection(
        x2, w_in.astype(BF16), row(b_gate), cos, sa, sb, w_exp_gate, w_exp_up, w_exp_down, seq)
    w_taps = jnp.broadcast_to(w_dw.reshape(CONV_WIDTH, 1, D_MODEL), (CONV_WIDTH, SUBLANES, D_MODEL))
    conv_gated = _conv_branch(u, w_taps, row(b_dw), row(conv_ln_g),
                              row(conv_ln_b), w_conv_o.astype(BF16), gate_c, batch, seq)
    attn = _moba_attention(q, k, v, k_mean, batch, seq)
    w_attn_o_b, w_out_b, w_router_t = w_attn_o.astype(BF16), w_out.astype(BF16), w_router.T
    shared_w = (w_sh_gate.astype(BF16), w_sh_up.astype(BF16), w_sh_down.astype(BF16))
    unit = SC_CORES * SC_SUBCORES * SC_CHUNK
    assert n % unit == 0 and n >= 2 * unit and unit % MERGE_TILE == 0 and unit % POSITION_TILE == 0
    first_size = min(max(unit, round(n * FIRST_CHUNK_SHARE / unit) * unit), n - unit)
    out = None
    for first, n_chunk in ((0, first_size), (first_size, n - first_size)):
        h, h_packed, top_idx, top_w, rank, totals = _merge_ln_router(
            attn, gate_a, conv_gated, x2, w_attn_o_b, w_out_b, row(ln1_g), row(ln1_b),
            w_router_t, router_bias.reshape(N_EXPERTS, 1), first, n_chunk)
        group_start, tile_expert, tile_valid, n_rows = _group_layout(totals, n_chunk)
        pos = _positions(top_idx, rank, group_start).reshape(TOP_K * n_chunk)
        xs = _sc_scatter_rows(h_packed, pos, n_rows)
        ys = _grouped_experts(tile_expert, tile_valid, xs, w_gate_b, w_up_b, w_down_b)
        yg = _sc_gather_rows(ys, pos).reshape(TOP_K, n_chunk, HALF)
        out = _combine_ln(h, yg, top_w.T, *shared_w, row(ln2_g), row(ln2_b), out, first, n)
    return out


def kernel(x, w_in, b_gate, w_attn_o, w_dw, b_dw, conv_ln_g, conv_ln_b, w_conv_o, w_out, ln1_g, ln1_b,
           w_router, router_bias, w_exp_gate, w_exp_up, w_exp_down, w_sh_gate, w_sh_up, w_sh_down,
           ln2_g, ln2_b):
    batch, seq, d = x.shape
    assert d == D_MODEL and seq % MOBA_BLOCK == 0 and seq % TOKEN_TILE == 0
    assert w_in.shape[0] == DEPTH
    x2 = x.reshape(batch * seq, d)
    for l in range(DEPTH):
        x2 = _layer(x2, batch, seq, w_in[l], b_gate[l], w_attn_o[l], w_dw[l], b_dw[l], conv_ln_g[l],
                    conv_ln_b[l], w_conv_o[l], w_out[l], ln1_g[l], ln1_b[l], w_router[l], router_bias[l],
                    w_exp_gate[l], w_exp_up[l], w_exp_down[l], w_sh_gate[l], w_sh_up[l], w_sh_down[l],
                    ln2_g[l], ln2_b[l])
    return x2.reshape(batch, seq, d)
```
